```python
import jax, jax.numpy as jnp
from jax import lax
import numpy as np

D_MODEL = 2048
BATCH = 2
SEQ = 4096
DEPTH = 1
DEC_BATCH = 32
DEC_SEQ = 4
PAST_LEN = 8192
PAGE_SIZE = 128

CHUNK = 128
A_GROUPS = 16
A_GROUP_DIM = 128
A_WIDTH = A_GROUPS * A_GROUP_DIM
HEAD_DIM = 128
B_HEADS_PER_GROUP = 4
DILATED_GROUPS = ((128, 1), (512, 4), (2048, 16))
N_DIL = len(DILATED_GROUPS)
B_HEADS = N_DIL * B_HEADS_PER_GROUP
B_WIDTH = B_HEADS * HEAD_DIM
B_OUT = B_HEADS_PER_GROUP * HEAD_DIM
IN_COLS = 2 * A_WIDTH + 3 * B_WIDTH + 2 * D_MODEL
SPLITS = (A_WIDTH, 2 * A_WIDTH, 2 * A_WIDTH + B_WIDTH, 2 * A_WIDTH + 2 * B_WIDTH,
          2 * A_WIDTH + 3 * B_WIDTH, 2 * A_WIDTH + 3 * B_WIDTH + D_MODEL)
N_EXPERT_GROUPS = 4
EXPERTS_PER_GROUP = 8
N_EXPERTS = N_EXPERT_GROUPS * EXPERTS_PER_GROUP
TOP_K = 2
D_EXPERT = 256
RMS_EPS = 1e-6
LN_EPS = 1e-5
NEG = -1e30

kernel_name = "hybrid_gmlp_dilated_attn_hmoe_step"


def rmsnorm(x, g):
    xf = x.astype(jnp.float32)
    r = xf * lax.rsqrt(jnp.mean(xf * xf, axis=-1, keepdims=True) + RMS_EPS)
    return (r * g.astype(jnp.float32)).astype(x.dtype)


def layernorm(x, g, b):
    xf = x.astype(jnp.float32)
    mu = jnp.mean(xf, axis=-1, keepdims=True)
    xc = xf - mu
    r = xc * lax.rsqrt(jnp.mean(xc * xc, axis=-1, keepdims=True) + LN_EPS)
    return (r * g.astype(jnp.float32) + b.astype(jnp.float32)).astype(x.dtype)


def mixer_inputs(xn, w_in, ln_g, ln_b):
    z = xn @ w_in
    u, v, q, k, vb, ga, gb = jnp.split(z, SPLITS, axis=-1)
    u = jax.nn.gelu(u)
    vn = layernorm(jax.nn.gelu(v), ln_g, ln_b)
    hs = q.shape[:-1] + (B_HEADS, HEAD_DIM)
    return u, vn, q.reshape(hs), k.reshape(hs), vb.reshape(hs), ga, gb


def spatial_gate(u, vn, w_s, b_s):
    B, T, _ = u.shape
    n = min(T, CHUNK)
    nc = T // n
    tri = jnp.tril(jnp.ones((n, n), dtype=bool))
    ws = jnp.where(tri[None], w_s[:, :n, :n], jnp.zeros((), w_s.dtype))
    vc = vn.reshape(B, nc, n, A_GROUPS, A_GROUP_DIM)
    s = jnp.einsum('gts,bcsgd->bctgd', ws, vc) + b_s[:, :n].T[None, None, :, :, None]
    return u * s.reshape(B, T, A_WIDTH)


def dilated_window_prompt(q, k, v, dil, n_back):
    B, S, H, Dh = q.shape
    L = S // dil
    blk = n_back
    nb = -(-L // blk)
    Lp = nb * blk

    def split(a):
        a = a.astype(jnp.float32).reshape(B, L, dil, H, Dh).transpose(0, 2, 1, 3, 4)
        a = jnp.pad(a, ((0, 0), (0, 0), (0, Lp - L), (0, 0), (0, 0)))
        return a.reshape(B, dil, nb, blk, H, Dh)

    def with_prev(a):
        prev = jnp.pad(a, ((0, 0), (0, 0), (1, 0), (0, 0), (0, 0), (0, 0)))[:, :, :nb]
        return jnp.concatenate([prev, a], axis=3)

    qb = split(q)
    kk = with_prev(split(k))
    vv = with_prev(split(v))
    s = jnp.einsum('brnqhd,brnkhd->brnhqk', qb, kk) * (HEAD_DIM ** -0.5)
    qi = jnp.arange(blk)[:, None]
    ki = jnp.arange(2 * blk)[None, :]
    dist = qi + blk - ki
    band = (dist >= 0) & (dist <= n_back)
    exists = (jnp.arange(nb)[:, None, None] > 0) | (ki >= blk)[None]
    mask = band[None] & exists
    s = jnp.where(mask[None, None, :, None], s, NEG)
    lse = jax.nn.logsumexp(s, axis=-1)
    p = jnp.exp(s - lse[..., None])
    o = jnp.einsum('brnhqk,brnkhd->brnqhd', p, vv)
    o = o.reshape(B, dil, Lp, H, Dh)[:, :, :L].transpose(0, 2, 1, 3, 4).reshape(B, S, H, Dh)
    lse = lse.transpose(0, 1, 2, 4, 3).reshape(B, dil, Lp, H)[:, :, :L]
    lse = lse.transpose(0, 2, 1, 3).reshape(B, S, H)
    return o, lse


def dilated_window_sample(q, k_full, v_full, n_cache, dil, n_back):
    T = q.shape[1]
    j = jnp.arange(n_back + 1)
    idx = n_cache + jnp.arange(T)[:, None] - j[None, :] * dil
    valid = idx >= 0
    idxc = jnp.maximum(idx, 0)
    kg = k_full[:, idxc].astype(jnp.float32)
    vg = v_full[:, idxc].astype(jnp.float32)
    s = jnp.einsum('bthd,btjhd->bthj', q.astype(jnp.float32), kg) * (HEAD_DIM ** -0.5)
    s = jnp.where(valid[None, :, None, :], s, NEG)
    lse = jax.nn.logsumexp(s, axis=-1)
    p = jnp.exp(s - lse[..., None])
    o = jnp.einsum('bthj,btjhd->bthd', p, vg)
    return o, lse


def merge_dilations(outs, lses):
    o = jnp.stack(outs, 0)
    w = jax.nn.softmax(jnp.stack(lses, 0), axis=0)
    return jnp.einsum('g...h,g...hd->...hd', w, o)


def hier_moe(xn, w_rg, b_rg, w_re, b_re, w_gate, w_up, w_down):
    T = xn.shape[0]
    xf = xn.astype(jnp.float32)
    lg = xf @ w_rg.astype(jnp.float32) + b_rg.astype(jnp.float32)
    pg = jax.nn.softmax(lg, axis=-1)
    _, g_sel = lax.top_k(lg, 1)
    p_sel = jnp.take_along_axis(pg, g_sel, axis=-1)
    le = (xf @ w_re.astype(jnp.float32) + b_re.astype(jnp.float32)).reshape(T, N_EXPERT_GROUPS, EXPERTS_PER_GROUP)
    le_sel = jnp.take_along_axis(le, g_sel[:, :, None], axis=1)[:, 0]
    top_v, top_i = lax.top_k(le_sel, TOP_K)
    ge = jax.nn.softmax(top_v, axis=-1) * p_sel
    eidx = g_sel * EXPERTS_PER_GROUP + top_i
    cw = jnp.sum(jax.nn.one_hot(eidx, N_EXPERTS, dtype=jnp.float32) * ge[..., None], axis=1)
    h = jax.nn.silu(jnp.einsum('td,edf->tef', xn, w_gate)) * jnp.einsum('td,edf->tef', xn, w_up)
    return jnp.einsum('tef,efd->td', h * cw[..., None].astype(h.dtype), w_down)


def branch_merge_and_ffn(x, sg, ob, ga, gb, w_a_out, w_b_out, w_o, norm_ffn,
                         w_rg, b_rg, w_re, b_re, w_gate, w_up, w_down):
    y_a = sg @ w_a_out
    y_b = ob.astype(x.dtype).reshape(ob.shape[:-2] + (B_OUT,)) @ w_b_out
    m = jax.nn.sigmoid(ga) * y_a + jax.nn.sigmoid(gb) * y_b
    h = x + m @ w_o
    hn = rmsnorm(h, norm_ffn)
    lead = hn.shape[:-1]
    f = hier_moe(hn.reshape(-1, D_MODEL), w_rg, b_rg, w_re, b_re, w_gate, w_up, w_down)
    return h + f.reshape(lead + (D_MODEL,))


def setup_inputs(seed: int = 0) -> dict:
    key = jax.random.key(seed)
    ks = jax.random.split(key, 32)
    f32 = jnp.float32

    def nrm(k, shape, scale):
        return jax.random.normal(k, shape, f32) * scale

    G = B_HEADS_PER_GROUP
    inp = {}
    inp["x_prompt"] = nrm(ks[0], (BATCH, SEQ, D_MODEL), 1.0)
    inp["x_sample"] = nrm(ks[1], (DEC_BATCH, DEC_SEQ, D_MODEL), 1.0)
    inp["cache_kv_w128"] = nrm(ks[2], (DEPTH, DEC_BATCH, min(128, PAST_LEN), 2, G, HEAD_DIM), 1.0)
    inp["cache_kv_w512"] = nrm(ks[3], (DEPTH, DEC_BATCH, min(512, PAST_LEN), 2, G, HEAD_DIM), 1.0)
    inp["cache_kv_w2048"] = nrm(ks[4], (DEPTH, DEC_BATCH, min(2048, PAST_LEN), 2, G, HEAD_DIM), 1.0)
    inp["norm_mix"] = 1.0 + nrm(ks[5], (DEPTH, D_MODEL), 0.01)
    inp["w_in"] = nrm(ks[6], (DEPTH, D_MODEL, IN_COLS), D_MODEL ** -0.5)
    inp["ln_v_g"] = 1.0 + nrm(ks[7], (DEPTH, A_WIDTH), 0.01)
    inp["ln_v_b"] = nrm(ks[8], (DEPTH, A_WIDTH), 0.01)
    inp["w_s"] = nrm(ks[9], (DEPTH, A_GROUPS, CHUNK, CHUNK), 0.5 * CHUNK ** -0.5)
    inp["b_s"] = 1.0 + nrm(ks[10], (DEPTH, A_GROUPS, CHUNK), 0.01)
    inp["w_a_out"] = nrm(ks[11], (DEPTH, A_WIDTH, D_MODEL), A_WIDTH ** -0.5)
    inp["w_b_out"] = nrm(ks[12], (DEPTH, B_OUT, D_MODEL), B_OUT ** -0.5)
    inp["w_o"] = nrm(ks[13], (DEPTH, D_MODEL, D_MODEL), D_MODEL ** -0.5)
    inp["norm_ffn"] = 1.0 + nrm(ks[14], (DEPTH, D_MODEL), 0.01)
    inp["w_route_group"] = nrm(ks[15], (DEPTH, D_MODEL, N_EXPERT_GROUPS), D_MODEL ** -0.5)
    inp["b_route_group"] = nrm(ks[16], (DEPTH, N_EXPERT_GROUPS), 0.01)
    inp["w_route_expert"] = nrm(ks[17], (DEPTH, D_MODEL, N_EXPERTS), D_MODEL ** -0.5)
    inp["b_route_expert"] = nrm(ks[18], (DEPTH, N_EXPERTS), 0.01)
    inp["w_gate_e"] = nrm(ks[19], (DEPTH, N_EXPERTS, D_MODEL, D_EXPERT), D_MODEL ** -0.5)
    inp["w_up_e"] = nrm(ks[20], (DEPTH, N_EXPERTS, D_MODEL, D_EXPERT), D_MODEL ** -0.5)
    inp["w_down_e"] = nrm(ks[21], (DEPTH, N_EXPERTS, D_EXPERT, D_MODEL), D_EXPERT ** -0.5)
    inp["norm_final"] = 1.0 + nrm(ks[22], (D_MODEL,), 0.01)
    return inp


def reference(x_prompt, x_sample, cache_kv_w128, cache_kv_w512, cache_kv_w2048,
              norm_mix, w_in, ln_v_g, ln_v_b, w_s, b_s, w_a_out, w_b_out, w_o,
              norm_ffn, w_route_group, b_route_group, w_route_expert, b_route_expert,
              w_gate_e, w_up_e, w_down_e, norm_final):
    caches = (cache_kv_w128, cache_kv_w512, cache_kv_w2048)
    G = B_HEADS_PER_GROUP
    S = x_prompt.shape[1]
    hp, hs = x_prompt, x_sample
    kv_p = [[] for _ in range(N_DIL)]
    kv_s = [[] for _ in range(N_DIL)]
    chunk_v = []
    for l in range(DEPTH):
        ffn_args = (w_route_group[l], b_route_group[l], w_route_expert[l], b_route_expert[l],
                    w_gate_e[l], w_up_e[l], w_down_e[l])
        u, vn, q, k, v, ga, gb = mixer_inputs(rmsnorm(hp, norm_mix[l]), w_in[l], ln_v_g[l], ln_v_b[l])
        sg = spatial_gate(u, vn, w_s[l], b_s[l])
        outs, lses = [], []
        for g, (win, dil) in enumerate(DILATED_GROUPS):
            hsl = slice(g * G, (g + 1) * G)
            o, lse = dilated_window_prompt(q[:, :, hsl], k[:, :, hsl], v[:, :, hsl], dil, win // dil)
            outs.append(o)
            lses.append(lse)
            keep = min(win, S)
            kv_p[g].append(jnp.stack([k[:, S - keep:, hsl], v[:, S - keep:, hsl]], axis=2))
        ob = merge_dilations(outs, lses)
        hp = branch_merge_and_ffn(hp, sg, ob, ga, gb, w_a_out[l], w_b_out[l], w_o[l], norm_ffn[l], *ffn_args)
        u, vn, q, k, v, ga, gb = mixer_inputs(rmsnorm(hs, norm_mix[l]), w_in[l], ln_v_g[l], ln_v_b[l])
        sg = spatial_gate(u, vn, w_s[l], b_s[l])
        chunk_v.append(vn)
        outs, lses = [], []
        for g, (win, dil) in enumerate(DILATED_GROUPS):
            hsl = slice(g * G, (g + 1) * G)
            cache = caches[g][l]
            n_cache = cache.shape[1]
            kg, vg = k[:, :, hsl], v[:, :, hsl]
            k_full = jnp.concatenate([cache[:, :, 0].astype(kg.dtype), kg], axis=1)
            v_full = jnp.concatenate([cache[:, :, 1].astype(vg.dtype), vg], axis=1)
            o, lse = dilated_window_sample(q[:, :, hsl], k_full, v_full, n_cache, dil, win // dil)
            outs.append(o)
            lses.append(lse)
            kv_s[g].append(jnp.stack([kg, vg], axis=2))
        ob = merge_dilations(outs, lses)
        hs = branch_merge_and_ffn(hs, sg, ob, ga, gb, w_a_out[l], w_b_out[l], w_o[l], norm_ffn[l], *ffn_args)
    y_prompt = rmsnorm(hp, norm_final)
    y_sample = rmsnorm(hs, norm_final)
    return (y_prompt, y_sample,
            jnp.stack(kv_p[0]), jnp.stack(kv_p[1]), jnp.stack(kv_p[2]),
            jnp.stack(kv_s[0]), jnp.stack(kv_s[1]), jnp.stack(kv_s[2]),
            jnp.stack(chunk_v))
```

```python
import functools

import jax
import jax.numpy as jnp
from jax import lax
from jax.experimental import pallas as pl
from jax.experimental.pallas import tpu as pltpu

F32 = jnp.float32
BF16 = jnp.bfloat16

D_MODEL = 2048
CHUNK = 128
A_GROUPS = 16
A_GROUP_DIM = 128
A_WIDTH = A_GROUPS * A_GROUP_DIM
HEAD_DIM = 128
HEADS_PER_GROUP = 4
GROUP_COLS = HEADS_PER_GROUP * HEAD_DIM
DILATED_GROUPS = ((128, 1), (512, 4), (2048, 16))
N_DIL = len(DILATED_GROUPS)
B_WIDTH = N_DIL * GROUP_COLS
IN_COLS = 2 * A_WIDTH + 3 * B_WIDTH + 2 * D_MODEL
COL_U, COL_V = 0, A_WIDTH
COL_Q = 2 * A_WIDTH
COL_K = COL_Q + B_WIDTH
COL_VB = COL_K + B_WIDTH
COL_GA = COL_VB + B_WIDTH
COL_GB = COL_GA + D_MODEL
N_EXPERT_GROUPS = 4
EXPERTS_PER_GROUP = 8
N_EXPERTS = N_EXPERT_GROUPS * EXPERTS_PER_GROUP
TOP_K = 2
D_EXPERT = 256
RMS_EPS = 1e-6
LN_EPS = 1e-5
NEG = -1e30
ATTN_SCALE = HEAD_DIM ** -0.5

LANES = 128
VMEM_LIMIT_BYTES = 56 * 1024 * 1024
IN_TN = 512
MOE_TM = 256


def _cparams(*sem):
    return pltpu.CompilerParams(dimension_semantics=sem, vmem_limit_bytes=VMEM_LIMIT_BYTES)


def _gelu_tanh(x):
    return 0.5 * x * (1.0 + jnp.tanh(0.7978845608028654 * (x + 0.044715 * (x * x * x))))


def _sigmoid(x):
    return 1.0 / (1.0 + jnp.exp(-x))


def _inproj_kernel(x_ref, g_ref, w_ref, z_ref, xn_ref):
    j = pl.program_id(1)

    @pl.when(j == 0)
    def _():
        x = x_ref[...]
        ms = jnp.mean(x * x, axis=-1, keepdims=True)
        xn_ref[...] = (x * lax.rsqrt(ms + RMS_EPS) * g_ref[...]).astype(BF16)

    z = jnp.dot(xn_ref[...], w_ref[...], preferred_element_type=F32)
    is_gelu = j < COL_Q // IN_TN
    is_gate = j >= COL_GA // IN_TN

    @pl.when(is_gelu)
    def _():
        z_ref[...] = _gelu_tanh(z).astype(BF16)

    @pl.when(is_gate)
    def _():
        z_ref[...] = _sigmoid(z).astype(BF16)

    @pl.when(jnp.logical_not(jnp.logical_or(is_gelu, is_gate)))
    def _():
        z_ref[...] = z.astype(BF16)


def _inproj(x, norm_g, w_in_bf16, tm):
    m = x.shape[0]
    return pl.pallas_call(
        _inproj_kernel,
        out_shape=jax.ShapeDtypeStruct((m, IN_COLS), BF16),
        grid=(m // tm, IN_COLS // IN_TN),
        in_specs=[
            pl.BlockSpec((tm, D_MODEL), lambda i, j: (i, 0)),
            pl.BlockSpec((1, D_MODEL), lambda i, j: (0, 0)),
            pl.BlockSpec((D_MODEL, IN_TN), lambda i, j: (0, j)),
        ],
        out_specs=pl.BlockSpec((tm, IN_TN), lambda i, j: (i, j)),
        scratch_shapes=[pltpu.VMEM((tm, D_MODEL), BF16)],
        compiler_params=_cparams("parallel", "arbitrary"),
        name="inproj",
    )(x, norm_g, w_in_bf16)


def _sgu_kernel(u_ref, v_ref, lng_ref, lnb_ref, ws_ref, bst_ref, sg_ref, *vn_out, chunks):
    row = lax.broadcasted_iota(jnp.int32, (CHUNK, CHUNK), 0)
    col = lax.broadcasted_iota(jnp.int32, (CHUNK, CHUNK), 1)
    tri = row >= col
    ws = [jnp.where(tri, ws_ref[g], 0.0).astype(BF16) for g in range(A_GROUPS)]
    for c in range(chunks):
        rows = slice(c * CHUNK, (c + 1) * CHUNK)
        v = v_ref[rows, :].astype(F32)
        mu = jnp.mean(v, axis=-1, keepdims=True)
        vc = v - mu
        var = jnp.mean(vc * vc, axis=-1, keepdims=True)
        vn = vc * lax.rsqrt(var + LN_EPS) * lng_ref[...] + lnb_ref[...]
        if vn_out:
            vn_out[0][rows, :] = vn
        vnb = vn.astype(BF16)
        for g in range(A_GROUPS):
            cols = slice(g * A_GROUP_DIM, (g + 1) * A_GROUP_DIM)
            s = jnp.dot(ws[g], vnb[:, cols], preferred_element_type=F32) + bst_ref[:, g:g + 1]
            sg_ref[rows, cols] = (u_ref[rows, cols].astype(F32) * s).astype(BF16)


def _sgu(z, ln_g, ln_b, ws, bst, chunks, emit_vn):
    m = z.shape[0]
    tm = chunks * CHUNK
    out_shape = [jax.ShapeDtypeStruct((m, A_WIDTH), BF16)]
    out_specs = [pl.BlockSpec((tm, A_WIDTH), lambda i: (i, 0))]
    if emit_vn:
        out_shape.append(jax.ShapeDtypeStruct((m, A_WIDTH), F32))
        out_specs.append(pl.BlockSpec((tm, A_WIDTH), lambda i: (i, 0)))
    return pl.pallas_call(
        functools.partial(_sgu_kernel, chunks=chunks),
        out_shape=out_shape,
        grid=(m // tm,),
        in_specs=[
            pl.BlockSpec((tm, A_WIDTH), lambda i: (i, COL_U // A_WIDTH)),
            pl.BlockSpec((tm, A_WIDTH), lambda i: (i, COL_V // A_WIDTH)),
            pl.BlockSpec((1, A_WIDTH), lambda i: (0, 0)),
            pl.BlockSpec((1, A_WIDTH), lambda i: (0, 0)),
            pl.BlockSpec((A_GROUPS, CHUNK, CHUNK), lambda i: (0, 0, 0)),
            pl.BlockSpec((CHUNK, A_GROUPS), lambda i: (0, 0)),
        ],
        out_specs=out_specs,
        compiler_params=_cparams("parallel"),
        name="sgu",
    )(z, z, ln_g, ln_b, ws, bst)


def _attn_kernel(q_ref, kc_ref, kp_ref, vc_ref, vp_ref, o_ref, lse_ref, *, qblocks):
    i = pl.program_id(2)
    qi = lax.broadcasted_iota(jnp.int32, (CHUNK, CHUNK), 0)
    ki = lax.broadcasted_iota(jnp.int32, (CHUNK, CHUNK), 1)
    cur_mask = ki <= qi
    no_prev = jnp.where(i == 0, CHUNK, 0)
    dn = (((1,), (1,)), ((), ()))
    for a in range(qblocks):
        rows = slice(a * CHUNK, (a + 1) * CHUNK)
        prows = slice((a - 1) * CHUNK, a * CHUNK)
        for h in range(HEADS_PER_GROUP):
            cols = slice(h * HEAD_DIM, (h + 1) * HEAD_DIM)
            q = q_ref[rows, cols]
            kc = kc_ref[rows, cols]
            vc = vc_ref[rows, cols]
            if a == 0:
                kp, vp = kp_ref[:, cols], vp_ref[:, cols]
                prev_mask = ki >= qi + no_prev
            else:
                kp, vp = kc_ref[prows, cols], vc_ref[prows, cols]
                prev_mask = ki >= qi
            s_c = lax.dot_general(q, kc, dn, preferred_element_type=F32) * ATTN_SCALE
            s_p = lax.dot_general(q, kp, dn, preferred_element_type=F32) * ATTN_SCALE
            s_c = jnp.where(cur_mask, s_c, NEG)
            s_p = jnp.where(prev_mask, s_p, NEG)
            mx = jnp.maximum(jnp.max(s_c, axis=-1, keepdims=True), jnp.max(s_p, axis=-1, keepdims=True))
            p_c = jnp.exp(s_c - mx)
            p_p = jnp.exp(s_p - mx)
            l = jnp.sum(p_c, axis=-1, keepdims=True) + jnp.sum(p_p, axis=-1, keepdims=True)
            acc = jnp.dot(p_c.astype(BF16), vc, preferred_element_type=F32)
            acc = acc + jnp.dot(p_p.astype(BF16), vp, preferred_element_type=F32)
            o_ref[rows, cols] = (acc / l).astype(BF16)
            lse_ref[rows, h:h + 1] = mx + jnp.log(l)


def _attn_group(z, batch, seq, group):
    _, dil = DILATED_GROUPS[group]
    sub = seq // dil
    qblocks = 2 if sub % (2 * CHUNK) == 0 else 1
    tq = qblocks * CHUNK
    nblk = IN_COLS // GROUP_COLS
    cq = COL_Q // GROUP_COLS + group
    ck = COL_K // GROUP_COLS + group
    cv = COL_VB // GROUP_COLS + group
    zv = z.reshape(batch, sub, dil * IN_COLS)

    def cur(cb):
        return pl.BlockSpec((None, tq, GROUP_COLS), lambda b, r, i: (b, i, r * nblk + cb))

    def prev(cb):
        return pl.BlockSpec((None, CHUNK, GROUP_COLS),
                            lambda b, r, i: (b, jnp.maximum(i * qblocks - 1, 0), r * nblk + cb))

    o, lse = pl.pallas_call(
        functools.partial(_attn_kernel, qblocks=qblocks),
        out_shape=[jax.ShapeDtypeStruct((batch, sub, dil * GROUP_COLS), BF16),
                   jax.ShapeDtypeStruct((batch, dil, sub, HEADS_PER_GROUP), F32)],
        grid=(batch, dil, sub // tq),
        in_specs=[cur(cq), cur(ck), prev(ck), cur(cv), prev(cv)],
        out_specs=[pl.BlockSpec((None, tq, GROUP_COLS), lambda b, r, i: (b, i, r)),
                   pl.BlockSpec((None, None, tq, HEADS_PER_GROUP), lambda b, r, i: (b, r, i, 0))],
        compiler_params=_cparams("parallel", "parallel", "arbitrary"),
        name=f"attn_prompt_g{group}",
    )(zv, zv, zv, zv, zv)
    o = o.reshape(batch * seq, GROUP_COLS)
    lse = lse.transpose(0, 2, 1, 3).reshape(batch * seq, HEADS_PER_GROUP)
    return o, lse


def _attn_sample_kernel(q_ref, k_ref, v_ref, c0_ref, c1_ref, c2_ref, o_ref, lse_ref, *, n_new):
    caches = (c0_ref, c1_ref, c2_ref)
    rowid = lax.broadcasted_iota(jnp.int32, (CHUNK, 1), 0)
    kv_cols = 2 * GROUP_COLS
    for g, (win, dil) in enumerate(DILATED_GROUPS):
        cref = caches[g]
        gcols = slice(g * GROUP_COLS, (g + 1) * GROUP_COLS)
        for t in range(n_new):
            res, first = t % dil, t // dil
            kc = cref[:, res * kv_cols: res * kv_cols + GROUP_COLS]
            vc = cref[:, res * kv_cols + GROUP_COLS: (res + 1) * kv_cols]
            q = q_ref[t:t + 1, gcols]
            prod = kc * q
            new_rows = [tn for tn in range(t + 1) if (t - tn) % dil == 0 and (t - tn) // dil <= win // dil]
            for h in range(HEADS_PER_GROUP):
                cols = slice(h * HEAD_DIM, (h + 1) * HEAD_DIM)
                s = jnp.sum(prod[:, cols], axis=-1, keepdims=True) * ATTN_SCALE
                if first > 0:
                    s = jnp.where(rowid >= first, s, NEG)
                qh = q[:, cols]
                s_new = [jnp.sum(qh * k_ref[tn:tn + 1, g * GROUP_COLS + h * HEAD_DIM:
                                            g * GROUP_COLS + (h + 1) * HEAD_DIM],
                                 axis=-1, keepdims=True) * ATTN_SCALE for tn in new_rows]
                mx = jnp.max(s, axis=0, keepdims=True)
                for sn in s_new:
                    mx = jnp.maximum(mx, sn)
                p = jnp.exp(s - mx)
                l = jnp.sum(p, axis=0, keepdims=True)
                acc = jnp.sum(p * vc[:, cols], axis=0, keepdims=True)
                for tn, sn in zip(new_rows, s_new):
                    pn = jnp.exp(sn - mx)
                    l = l + pn
                    acc = acc + pn * v_ref[tn:tn + 1, g * GROUP_COLS + h * HEAD_DIM:
                                           g * GROUP_COLS + (h + 1) * HEAD_DIM]
                o_ref[g, t:t + 1, cols] = acc / l
                lse_ref[g, t:t + 1, h:h + 1] = mx + jnp.log(l)


def _attn_sample(q, k, v, caches):
    bd, n_new, _ = q.shape
    kv_cols = 2 * GROUP_COLS
    views, specs = [], []
    for (win, dil), c in zip(DILATED_GROUPS, caches):
        assert c.shape[1] == win and win == CHUNK * dil, "cache must hold exactly one full window"
        views.append(c.reshape(bd, CHUNK, dil * kv_cols))
        used = min(dil, n_new) * kv_cols
        specs.append(pl.BlockSpec((None, CHUNK, used), lambda b: (b, 0, 0)))
    new_spec = pl.BlockSpec((None, n_new, B_WIDTH), lambda b: (b, 0, 0))
    return pl.pallas_call(
        functools.partial(_attn_sample_kernel, n_new=n_new),
        out_shape=[jax.ShapeDtypeStruct((N_DIL, bd, n_new, GROUP_COLS), F32),
                   jax.ShapeDtypeStruct((N_DIL, bd, n_new, HEADS_PER_GROUP), F32)],
        grid=(bd,),
        in_specs=[new_spec, new_spec, new_spec] + specs,
        out_specs=[pl.BlockSpec((N_DIL, None, n_new, GROUP_COLS), lambda b: (0, b, 0, 0)),
                   pl.BlockSpec((N_DIL, None, n_new, HEADS_PER_GROUP), lambda b: (0, b, 0, 0))],
        compiler_params=_cparams("parallel"),
        name="attn_sample",
    )(q, k, v, *views)


def _merge_kernel(sg_ref, o0_ref, o1_ref, o2_ref, l0_ref, l1_ref, l2_ref, ga_ref, gb_ref,
                  wa_ref, wb_ref, m_ref, ob_ref):
    j = pl.program_id(1)

    @pl.when(j == 0)
    def _():
        l0, l1, l2 = l0_ref[...], l1_ref[...], l2_ref[...]
        mx = jnp.maximum(jnp.maximum(l0, l1), l2)
        e0, e1, e2 = jnp.exp(l0 - mx), jnp.exp(l1 - mx), jnp.exp(l2 - mx)
        inv = 1.0 / (e0 + e1 + e2)
        w0, w1, w2 = e0 * inv, e1 * inv, e2 * inv
        for h in range(HEADS_PER_GROUP):
            cols = slice(h * HEAD_DIM, (h + 1) * HEAD_DIM)
            ob = (w0[:, h:h + 1] * o0_ref[:, cols].astype(F32)
                  + w1[:, h:h + 1] * o1_ref[:, cols].astype(F32)
                  + w2[:, h:h + 1] * o2_ref[:, cols].astype(F32))
            ob_ref[:, cols] = ob.astype(BF16)

    ya = jnp.dot(sg_ref[...], wa_ref[...], preferred_element_type=F32)
    yb = jnp.dot(ob_ref[...], wb_ref[...], preferred_element_type=F32)
    m_ref[...] = (ga_ref[...].astype(F32) * ya + gb_ref[...].astype(F32) * yb).astype(BF16)


def _merge(z, sg, os_, lses, wa, wb, tm, tn):
    m = z.shape[0]
    row = lambda width: pl.BlockSpec((tm, width), lambda i, j: (i, 0))
    return pl.pallas_call(
        _merge_kernel,
        out_shape=jax.ShapeDtypeStruct((m, D_MODEL), BF16),
        grid=(m // tm, D_MODEL // tn),
        in_specs=[row(A_WIDTH)] + [row(GROUP_COLS)] * 3 + [row(HEADS_PER_GROUP)] * 3 + [
            pl.BlockSpec((tm, tn), lambda i, j: (i, COL_GA // tn + j)),
            pl.BlockSpec((tm, tn), lambda i, j: (i, COL_GB // tn + j)),
            pl.BlockSpec((A_WIDTH, tn), lambda i, j: (0, j)),
            pl.BlockSpec((GROUP_COLS, tn), lambda i, j: (0, j)),
        ],
        out_specs=pl.BlockSpec((tm, tn), lambda i, j: (i, j)),
        scratch_shapes=[pltpu.VMEM((tm, GROUP_COLS), BF16)],
        compiler_params=_cparams("parallel", "arbitrary"),
        name="branch_merge",
    )(sg, *os_, *lses, z, z, wa, wb)


def _oproj_kernel(m_ref, x_ref, wo_ref, g_ref, wr_ref, br_ref, h_ref, hn_ref, ri_ref, rw_ref):
    h = x_ref[...] + jnp.dot(m_ref[...], wo_ref[...], preferred_element_type=F32)
    h_ref[...] = h
    ms = jnp.mean(h * h, axis=-1, keepdims=True)
    hn = h * lax.rsqrt(ms + RMS_EPS) * g_ref[...]
    hn_ref[...] = hn
    logits = jnp.dot(hn.astype(BF16), wr_ref[...], preferred_element_type=F32) + br_ref[...]
    lane = lax.broadcasted_iota(jnp.int32, logits.shape, 1).astype(F32)
    big = float(LANES)

    def first_argmax(vals, vmax):
        return jnp.min(jnp.where(vals == vmax, lane, big), axis=-1, keepdims=True)

    lg = jnp.where(lane < N_EXPERT_GROUPS, logits, NEG)
    gmax = jnp.max(lg, axis=-1, keepdims=True)
    gsel = first_argmax(lg, gmax)
    p_sel = 1.0 / jnp.sum(jnp.exp(lg - gmax), axis=-1, keepdims=True)
    lo = N_EXPERT_GROUPS + EXPERTS_PER_GROUP * gsel
    le = jnp.where(jnp.logical_and(lane >= lo, lane < lo + EXPERTS_PER_GROUP), logits, NEG)
    v1 = jnp.max(le, axis=-1, keepdims=True)
    i1 = first_argmax(le, v1)
    le2 = jnp.where(lane == i1, NEG, le)
    v2 = jnp.max(le2, axis=-1, keepdims=True)
    i2 = first_argmax(le2, v2)
    e2 = jnp.exp(v2 - v1)
    w1 = p_sel / (1.0 + e2)
    w2 = p_sel * e2 / (1.0 + e2)
    ri = jnp.where(lane == 0, i1 - N_EXPERT_GROUPS, jnp.where(lane == 1, i2 - N_EXPERT_GROUPS, 0.0))
    ri_ref[...] = ri.astype(jnp.int32)
    rw_ref[...] = jnp.where(lane == 0, w1, jnp.where(lane == 1, w2, 0.0))


def _oproj(mm, x, wo, norm_g, wr, br, tm):
    m = x.shape[0]
    row = lambda width: pl.BlockSpec((tm, width), lambda i: (i, 0))
    full = lambda a, b: pl.BlockSpec((a, b), lambda i: (0, 0))
    return pl.pallas_call(
        _oproj_kernel,
        out_shape=[jax.ShapeDtypeStruct((m, D_MODEL), F32), jax.ShapeDtypeStruct((m, D_MODEL), F32),
                   jax.ShapeDtypeStruct((m, LANES), jnp.int32), jax.ShapeDtypeStruct((m, LANES), F32)],
        grid=(m // tm,),
        in_specs=[row(D_MODEL), row(D_MODEL), full(D_MODEL, D_MODEL), full(1, D_MODEL),
                  full(D_MODEL, LANES), full(1, LANES)],
        out_specs=[row(D_MODEL), row(D_MODEL), row(LANES), row(LANES)],
        compiler_params=_cparams("parallel"),
        name="oproj_router",
    )(mm, x, wo, norm_g, wr, br)


def _moe_kernel(texp_ref, nused_ref, tok_ref, hn_hbm, wg_ref, wu_ref, wd_ref, ys_ref, xbuf, sem):
    i = pl.program_id(0)
    tm = xbuf.shape[0]

    @pl.when(i < nused_ref[0])
    def _():
        def issue(r, carry):
            tok = tok_ref[i * tm + r]
            pltpu.make_async_copy(hn_hbm.at[pl.ds(tok, 1)], xbuf.at[pl.ds(r, 1)], sem).start()
            return carry

        lax.fori_loop(0, tm, issue, 0)
        pltpu.make_async_copy(xbuf, xbuf, sem).wait()
        x = xbuf[...].astype(BF16)
        gate = jnp.dot(x, wg_ref[...], preferred_element_type=F32)
        up = jnp.dot(x, wu_ref[...], preferred_element_type=F32)
        hid = (gate * _sigmoid(gate) * up).astype(BF16)
        ys_ref[...] = jnp.dot(hid, wd_ref[...], preferred_element_type=F32)

    @pl.when(i >= nused_ref[0])
    def _():
        ys_ref[...] = jnp.zeros_like(ys_ref)


def _moe(hn, wg, wu, wd, tile_expert, n_used, tok_of_slot, tm):
    n_tiles = tile_expert.shape[0]
    grid_spec = pltpu.PrefetchScalarGridSpec(
        num_scalar_prefetch=3,
        grid=(n_tiles,),
        in_specs=[
            pl.BlockSpec(memory_space=pl.ANY),
            pl.BlockSpec((None, D_MODEL, D_EXPERT), lambda i, te, nu, tk: (te[i], 0, 0)),
            pl.BlockSpec((None, D_MODEL, D_EXPERT), lambda i, te, nu, tk: (te[i], 0, 0)),
            pl.BlockSpec((None, D_EXPERT, D_MODEL), lambda i, te, nu, tk: (te[i], 0, 0)),
        ],
        out_specs=pl.BlockSpec((tm, D_MODEL), lambda i, te, nu, tk: (i, 0)),
        scratch_shapes=[pltpu.VMEM((tm, D_MODEL), F32), pltpu.SemaphoreType.DMA(())],
    )
    return pl.pallas_call(
        _moe_kernel,
        out_shape=jax.ShapeDtypeStruct((n_tiles * tm, D_MODEL), F32),
        grid_spec=grid_spec,
        compiler_params=_cparams("arbitrary"),
        name="moe_experts",
    )(tile_expert, n_used, tok_of_slot, hn, wg, wu, wd)


def _combine_kernel(slot_ref, h_ref, rw_ref, g_ref, ys_hbm, y_ref, ybuf, sem, *, tm):
    i = pl.program_id(0)

    def issue(r, carry):
        for k in range(TOP_K):
            slot = slot_ref[(i * tm + r) * TOP_K + k]
            pltpu.make_async_copy(ys_hbm.at[pl.ds(slot, 1)], ybuf.at[k, pl.ds(r, 1)], sem).start()
        return carry

    lax.fori_loop(0, tm, issue, 0)
    for k in range(TOP_K):
        pltpu.make_async_copy(ybuf.at[k], ybuf.at[k], sem).wait()
    y = h_ref[...] + rw_ref[:, 0:1] * ybuf[0] + rw_ref[:, 1:2] * ybuf[1]
    ms = jnp.mean(y * y, axis=-1, keepdims=True)
    y_ref[...] = y * lax.rsqrt(ms + RMS_EPS) * g_ref[...]


def _combine(h, route_w, norm_g, ys, slot, tm):
    m = h.shape[0]
    grid_spec = pltpu.PrefetchScalarGridSpec(
        num_scalar_prefetch=1,
        grid=(m // tm,),
        in_specs=[
            pl.BlockSpec((tm, D_MODEL), lambda i, s: (i, 0)),
            pl.BlockSpec((tm, LANES), lambda i, s: (i, 0)),
            pl.BlockSpec((1, D_MODEL), lambda i, s: (0, 0)),
            pl.BlockSpec(memory_space=pl.ANY),
        ],
        out_specs=pl.BlockSpec((tm, D_MODEL), lambda i, s: (i, 0)),
        scratch_shapes=[pltpu.VMEM((TOP_K, tm, D_MODEL), F32), pltpu.SemaphoreType.DMA(())],
    )
    return pl.pallas_call(
        functools.partial(_combine_kernel, tm=tm),
        out_shape=jax.ShapeDtypeStruct((m, D_MODEL), F32),
        grid_spec=grid_spec,
        compiler_params=_cparams("arbitrary"),
        name="moe_combine",
    )(slot, h, route_w, norm_g, ys)


def _routing_tables(route_i, m, tm):
    n = m * TOP_K
    n_tiles = pl.cdiv(n, tm) + N_EXPERTS - 1
    e_flat = route_i[:, :TOP_K].reshape(n)
    onehot = (e_flat[:, None] == jnp.arange(N_EXPERTS, dtype=jnp.int32)[None, :]).astype(jnp.int32)
    csum = jnp.cumsum(onehot, axis=0)
    rank = jnp.take_along_axis(csum, e_flat[:, None], axis=1)[:, 0] - 1
    counts = csum[-1]
    tiles_e = (counts + tm - 1) // tm
    tile_end = jnp.cumsum(tiles_e)
    tile_start = tile_end - tiles_e
    slot = (tile_start[e_flat] * tm + rank).astype(jnp.int32)
    n_used = tile_end[-1:].astype(jnp.int32)
    tile_ids = jnp.arange(n_tiles, dtype=jnp.int32)
    tile_expert = jnp.minimum(jnp.searchsorted(tile_end, tile_ids, side="right"), N_EXPERTS - 1).astype(jnp.int32)
    tok_of_slot = jnp.zeros((n_tiles * tm,), jnp.int32).at[slot].set(
        jnp.arange(n, dtype=jnp.int32) // TOP_K, unique_indices=True)
    return slot, tile_expert, n_used, tok_of_slot


def _token_tail(x, z, sg, os_, lses, wts, tm):
    mm = _merge(z, sg, os_, lses, wts["wa"], wts["wb"], tm=min(tm, 512), tn=512)
    h, hn, route_i, route_w = _oproj(mm, x, wts["wo"], wts["norm_ffn"], wts["wr"], wts["br"], tm=min(tm, 256))
    moe_tm = min(tm, MOE_TM)
    slot, tile_expert, n_used, tok_of_slot = _routing_tables(route_i, x.shape[0], moe_tm)
    ys = _moe(hn, wts["wg"], wts["wu"], wts["wd"], tile_expert, n_used, tok_of_slot, moe_tm)
    return _combine(h, route_w, wts["norm_final"], ys, slot, tm=min(tm, 256))


def _kv_rows(z, lead, keep_from, group):
    zz = z.reshape(lead + (IN_COLS,))[:, keep_from:]
    k = zz[..., COL_K + group * GROUP_COLS: COL_K + (group + 1) * GROUP_COLS]
    v = zz[..., COL_VB + group * GROUP_COLS: COL_VB + (group + 1) * GROUP_COLS]
    kv = jnp.stack([k, v], axis=2).astype(F32)
    return kv.reshape(kv.shape[:3] + (HEADS_PER_GROUP, HEAD_DIM))


def kernel(x_prompt, x_sample, cache_kv_w128, cache_kv_w512, cache_kv_w2048, norm_mix, w_in, ln_v_g, ln_v_b, w_s, b_s, w_a_out, w_b_out, w_o, norm_ffn, w_route_group, b_route_group, w_route_expert, b_route_expert, w_gate_e, w_up_e, w_down_e, norm_final):
    assert norm_mix.shape[0] == 1, "single-layer trunk"
    batch, seq, _ = x_prompt.shape
    bd, n_new, _ = x_sample.shape
    caches = (cache_kv_w128[0], cache_kv_w512[0], cache_kv_w2048[0])

    pad = LANES - N_EXPERT_GROUPS - N_EXPERTS
    wr = jnp.concatenate([w_route_group[0], w_route_expert[0], jnp.zeros((D_MODEL, pad), F32)], axis=1)
    br = jnp.concatenate([b_route_group[0], b_route_expert[0], jnp.zeros((pad,), F32)])[None, :]
    wts = dict(
        wa=w_a_out[0].astype(BF16), wb=w_b_out[0].astype(BF16), wo=w_o[0].astype(BF16),
        wg=w_gate_e[0].astype(BF16), wu=w_up_e[0].astype(BF16), wd=w_down_e[0].astype(BF16),
        wr=wr.astype(BF16), br=br, norm_ffn=norm_ffn[0][None, :], norm_final=norm_final[None, :],
    )
    w_in_b = w_in[0].astype(BF16)
    norm_g = norm_mix[0][None, :]
    ln_g, ln_b = ln_v_g[0][None, :], ln_v_b[0][None, :]

    xp = x_prompt.reshape(batch * seq, D_MODEL)
    zp = _inproj(xp, norm_g, w_in_b, tm=1024)
    (sgp,) = _sgu(zp, ln_g, ln_b, w_s[0], b_s[0].T, chunks=4, emit_vn=False)
    os_p, lses_p = zip(*[_attn_group(zp, batch, seq, g) for g in range(N_DIL)])
    y_prompt = _token_tail(xp, zp, sgp, os_p, lses_p, wts, tm=512).reshape(batch, seq, D_MODEL)
    kv_prompt = [_kv_rows(zp, (batch, seq), seq - min(win, seq), g)[None] for g, (win, _) in enumerate(DILATED_GROUPS)]

    ms = bd * n_new
    assert ms == CHUNK and n_new <= CHUNK
    xs = x_sample.reshape(ms, D_MODEL)
    zs = _inproj(xs, norm_g, w_in_b, tm=ms)
    eye = jnp.eye(bd, dtype=F32)
    ws_s = jnp.einsum("ab,gts->gatbs", eye, w_s[0][:, :n_new, :n_new]).reshape(A_GROUPS, ms, ms)
    bst_s = jnp.tile(b_s[0][:, :n_new].T, (bd, 1))
    sgs, vns = _sgu(zs, ln_g, ln_b, ws_s, bst_s, chunks=1, emit_vn=True)
    zs3 = zs.reshape(bd, n_new, IN_COLS)
    qkv = [zs3[..., c:c + B_WIDTH].astype(F32) for c in (COL_Q, COL_K, COL_VB)]
    o_s, lse_s = _attn_sample(*qkv, caches)
    os_s = [o_s[g].reshape(ms, GROUP_COLS).astype(BF16) for g in range(N_DIL)]
    lses_s = [lse_s[g].reshape(ms, HEADS_PER_GROUP) for g in range(N_DIL)]
    y_sample = _token_tail(xs, zs, sgs, os_s, lses_s, wts, tm=ms).reshape(bd, n_new, D_MODEL)
    kv_sample = [_kv_rows(zs, (bd, n_new), 0, g)[None] for g in range(N_DIL)]
    chunk_v = vns.reshape(1, bd, n_new, A_WIDTH)

    return (y_prompt, y_sample, *kv_prompt, *kv_sample, chunk_v)
```

```python
import functools

import jax
import jax.numpy as jnp
from jax import lax
from jax.experimental import pallas as pl
from jax.experimental.pallas import tpu as pltpu

F32 = jnp.float32
BF16 = jnp.bfloat16

D_MODEL = 2048
CHUNK = 128
A_GROUPS = 16
A_GROUP_DIM = 128
A_WIDTH = A_GROUPS * A_GROUP_DIM
HEAD_DIM = 128
HEADS_PER_GROUP = 4
GROUP_COLS = HEADS_PER_GROUP * HEAD_DIM
DILATED_GROUPS = ((128, 1), (512, 4), (2048, 16))
N_DIL = len(DILATED_GROUPS)
B_WIDTH = N_DIL * GROUP_COLS
IN_COLS = 2 * A_WIDTH + 3 * B_WIDTH + 2 * D_MODEL
COL_U, COL_V = 0, A_WIDTH
COL_Q = 2 * A_WIDTH
COL_K = COL_Q + B_WIDTH
COL_VB = COL_K + B_WIDTH
COL_GA = COL_VB + B_WIDTH
COL_GB = COL_GA + D_MODEL
N_EXPERT_GROUPS = 4
EXPERTS_PER_GROUP = 8
N_EXPERTS = N_EXPERT_GROUPS * EXPERTS_PER_GROUP
TOP_K = 2
D_EXPERT = 256
RMS_EPS = 1e-6
LN_EPS = 1e-5
NEG = -1e30
ATTN_SCALE = HEAD_DIM ** -0.5

LANES = 128
VMEM_LIMIT_BYTES = 56 * 1024 * 1024
IN_TN = 512
MOE_TM = 256
PERM_BLOCK = 256


def _cparams(*sem):
    return pltpu.CompilerParams(dimension_semantics=sem, vmem_limit_bytes=VMEM_LIMIT_BYTES)


def _gelu_tanh(x):
    return 0.5 * x * (1.0 + jnp.tanh(0.7978845608028654 * (x + 0.044715 * (x * x * x))))


def _sigmoid(x):
    return 1.0 / (1.0 + jnp.exp(-x))


def _class_major_perm(dil):
    out_row = jnp.arange(PERM_BLOCK, dtype=jnp.int32)
    src = (out_row % (PERM_BLOCK // dil)) * dil + out_row // (PERM_BLOCK // dil)
    return (src[:, None] == jnp.arange(PERM_BLOCK, dtype=jnp.int32)[None, :]).astype(BF16)


def _inproj_kernel(x_ref, g_ref, w_ref, *rest, permute):
    if permute:
        perm_ref, z_ref, xn_ref = rest
    else:
        z_ref, xn_ref = rest
    j = pl.program_id(1)

    @pl.when(j == 0)
    def _():
        x = x_ref[...]
        ms = jnp.mean(x * x, axis=-1, keepdims=True)
        xn_ref[...] = (x * lax.rsqrt(ms + RMS_EPS) * g_ref[...]).astype(BF16)

    z = jnp.dot(xn_ref[...], w_ref[...], preferred_element_type=F32)
    is_gelu = j < COL_Q // IN_TN
    is_gate = j >= COL_GA // IN_TN
    is_plain = jnp.logical_not(jnp.logical_or(is_gelu, is_gate))
    group = (j - COL_Q // IN_TN) % N_DIL

    @pl.when(is_gelu)
    def _():
        z_ref[...] = _gelu_tanh(z).astype(BF16)

    @pl.when(is_gate)
    def _():
        z_ref[...] = _sigmoid(z).astype(BF16)

    if not permute:
        @pl.when(is_plain)
        def _():
            z_ref[...] = z.astype(BF16)
        return

    @pl.when(jnp.logical_and(is_plain, group == 0))
    def _():
        z_ref[...] = z.astype(BF16)

    for g in range(1, N_DIL):
        @pl.when(jnp.logical_and(is_plain, group == g))
        def _():
            zb = z.astype(BF16)
            for blk in range(z.shape[0] // PERM_BLOCK):
                rows = slice(blk * PERM_BLOCK, (blk + 1) * PERM_BLOCK)
                z_ref[rows, :] = jnp.dot(perm_ref[g - 1], zb[rows, :], preferred_element_type=F32).astype(BF16)


def _inproj(x, norm_g, w_in_bf16, tm, perms=None):
    m = x.shape[0]
    in_specs = [
        pl.BlockSpec((tm, D_MODEL), lambda i, j: (i, 0)),
        pl.BlockSpec((1, D_MODEL), lambda i, j: (0, 0)),
        pl.BlockSpec((D_MODEL, IN_TN), lambda i, j: (0, j)),
    ]
    args = [x, norm_g, w_in_bf16]
    if perms is not None:
        in_specs.append(pl.BlockSpec(perms.shape, lambda i, j: (0, 0, 0)))
        args.append(perms)
    return pl.pallas_call(
        functools.partial(_inproj_kernel, permute=perms is not None),
        out_shape=jax.ShapeDtypeStruct((m, IN_COLS), BF16),
        grid=(m // tm, IN_COLS // IN_TN),
        in_specs=in_specs,
        out_specs=pl.BlockSpec((tm, IN_TN), lambda i, j: (i, j)),
        scratch_shapes=[pltpu.VMEM((tm, D_MODEL), BF16)],
        compiler_params=_cparams("parallel", "arbitrary"),
        name="inproj",
    )(*args)


def _sgu_kernel(u_ref, v_ref, lng_ref, lnb_ref, ws_ref, bst_ref, sg_ref, *vn_out, chunks):
    row = lax.broadcasted_iota(jnp.int32, (CHUNK, CHUNK), 0)
    col = lax.broadcasted_iota(jnp.int32, (CHUNK, CHUNK), 1)
    tri = row >= col
    ws = [jnp.where(tri, ws_ref[g], 0.0).astype(BF16) for g in range(A_GROUPS)]
    for c in range(chunks):
        rows = slice(c * CHUNK, (c + 1) * CHUNK)
        v = v_ref[rows, :].astype(F32)
        mu = jnp.mean(v, axis=-1, keepdims=True)
        vc = v - mu
        var = jnp.mean(vc * vc, axis=-1, keepdims=True)
        vn = vc * lax.rsqrt(var + LN_EPS) * lng_ref[...] + lnb_ref[...]
        if vn_out:
            vn_out[0][rows, :] = vn
        vnb = vn.astype(BF16)
        for g in range(A_GROUPS):
            cols = slice(g * A_GROUP_DIM, (g + 1) * A_GROUP_DIM)
            s = jnp.dot(ws[g], vnb[:, cols], preferred_element_type=F32) + bst_ref[:, g:g + 1]
            sg_ref[rows, cols] = (u_ref[rows, cols].astype(F32) * s).astype(BF16)


def _sgu(z, ln_g, ln_b, ws, bst, chunks, emit_vn):
    m = z.shape[0]
    tm = chunks * CHUNK
    out_shape = [jax.ShapeDtypeStruct((m, A_WIDTH), BF16)]
    out_specs = [pl.BlockSpec((tm, A_WIDTH), lambda i: (i, 0))]
    if emit_vn:
        out_shape.append(jax.ShapeDtypeStruct((m, A_WIDTH), F32))
        out_specs.append(pl.BlockSpec((tm, A_WIDTH), lambda i: (i, 0)))
    return pl.pallas_call(
        functools.partial(_sgu_kernel, chunks=chunks),
        out_shape=out_shape,
        grid=(m // tm,),
        in_specs=[
            pl.BlockSpec((tm, A_WIDTH), lambda i: (i, COL_U // A_WIDTH)),
            pl.BlockSpec((tm, A_WIDTH), lambda i: (i, COL_V // A_WIDTH)),
            pl.BlockSpec((1, A_WIDTH), lambda i: (0, 0)),
            pl.BlockSpec((1, A_WIDTH), lambda i: (0, 0)),
            pl.BlockSpec((A_GROUPS, CHUNK, CHUNK), lambda i: (0, 0, 0)),
            pl.BlockSpec((CHUNK, A_GROUPS), lambda i: (0, 0)),
        ],
        out_specs=out_specs,
        compiler_params=_cparams("parallel"),
        name="sgu",
    )(z, z, ln_g, ln_b, ws, bst)


def _attn_kernel(q_ref, kc_ref, kp_ref, vc_ref, vp_ref, o_ref, lse_ref, *, qblocks):
    i = pl.program_id(2)
    qi = lax.broadcasted_iota(jnp.int32, (CHUNK, CHUNK), 0)
    ki = lax.broadcasted_iota(jnp.int32, (CHUNK, CHUNK), 1)
    cur_mask = ki <= qi
    no_prev = jnp.where(i == 0, CHUNK, 0)
    dn = (((1,), (1,)), ((), ()))
    rpc = q_ref.shape[1]
    nbq = CHUNK // rpc

    def load(ref, a, cols):
        return ref[a * nbq:(a + 1) * nbq, :, cols].reshape(CHUNK, HEAD_DIM)

    for a in range(qblocks):
        rows = slice(a * CHUNK, (a + 1) * CHUNK)
        for h in range(HEADS_PER_GROUP):
            cols = slice(h * HEAD_DIM, (h + 1) * HEAD_DIM)
            q = load(q_ref, a, cols)
            kc = load(kc_ref, a, cols)
            vc = load(vc_ref, a, cols)
            if a == 0:
                kp, vp = load(kp_ref, 0, cols), load(vp_ref, 0, cols)
                prev_mask = ki >= qi + no_prev
            else:
                kp, vp = load(kc_ref, a - 1, cols), load(vc_ref, a - 1, cols)
                prev_mask = ki >= qi
            s_c = lax.dot_general(q, kc, dn, preferred_element_type=F32) * ATTN_SCALE
            s_p = lax.dot_general(q, kp, dn, preferred_element_type=F32) * ATTN_SCALE
            s_c = jnp.where(cur_mask, s_c, NEG)
            s_p = jnp.where(prev_mask, s_p, NEG)
            mx = jnp.maximum(jnp.max(s_c, axis=-1, keepdims=True), jnp.max(s_p, axis=-1, keepdims=True))
            p_c = jnp.exp(s_c - mx)
            p_p = jnp.exp(s_p - mx)
            l = jnp.sum(p_c, axis=-1, keepdims=True) + jnp.sum(p_p, axis=-1, keepdims=True)
            acc = jnp.dot(p_c.astype(BF16), vc, preferred_element_type=F32)
            acc = acc + jnp.dot(p_p.astype(BF16), vp, preferred_element_type=F32)
            o_ref[a * nbq:(a + 1) * nbq, :, cols] = (acc / l).astype(BF16).reshape(nbq, rpc, HEAD_DIM)
            lse_ref[rows, h:h + 1] = mx + jnp.log(l)


def _rows_per_class(dil):
    return min(PERM_BLOCK // dil, CHUNK)


def _attn_group(z, batch, seq, group):
    _, dil = DILATED_GROUPS[group]
    sub = seq // dil
    rpc = _rows_per_class(dil)
    qblocks = 2 if sub % (2 * CHUNK) == 0 else 1
    tq = qblocks * CHUNK
    cq = COL_Q // GROUP_COLS + group
    ck = COL_K // GROUP_COLS + group
    cv = COL_VB // GROUP_COLS + group
    zv = z.reshape(batch, sub // rpc, dil, rpc, IN_COLS)

    def cur(cb):
        return pl.BlockSpec((None, tq // rpc, None, rpc, GROUP_COLS), lambda b, r, i: (b, i, r, 0, cb))

    def prev(cb):
        return pl.BlockSpec((None, CHUNK // rpc, None, rpc, GROUP_COLS),
                            lambda b, r, i: (b, jnp.maximum(i * qblocks - 1, 0), r, 0, cb))

    o, lse = pl.pallas_call(
        functools.partial(_attn_kernel, qblocks=qblocks),
        out_shape=[jax.ShapeDtypeStruct((batch, sub // rpc, dil, rpc, GROUP_COLS), BF16),
                   jax.ShapeDtypeStruct((batch, dil, sub, HEADS_PER_GROUP), F32)],
        grid=(batch, dil, sub // tq),
        in_specs=[cur(cq), cur(ck), prev(ck), cur(cv), prev(cv)],
        out_specs=[pl.BlockSpec((None, tq // rpc, None, rpc, GROUP_COLS), lambda b, r, i: (b, i, r, 0, 0)),
                   pl.BlockSpec((None, None, tq, HEADS_PER_GROUP), lambda b, r, i: (b, r, i, 0))],
        compiler_params=_cparams("parallel", "parallel", "arbitrary"),
        name=f"attn_prompt_g{group}",
    )(zv, zv, zv, zv, zv)
    o = o.reshape(batch * seq, GROUP_COLS)
    lse = lse.transpose(0, 2, 1, 3).reshape(batch * seq, HEADS_PER_GROUP)
    return o, lse


def _attn_sample_kernel(q_ref, k_ref, v_ref, c0_ref, c1_ref, c2_ref, o_ref, lse_ref, *, n_new):
    caches = (c0_ref, c1_ref, c2_ref)
    rowid = lax.broadcasted_iota(jnp.int32, (CHUNK, 1), 0)
    kv_cols = 2 * GROUP_COLS
    for g, (win, dil) in enumerate(DILATED_GROUPS):
        cref = caches[g]
        gcols = slice(g * GROUP_COLS, (g + 1) * GROUP_COLS)
        for t in range(n_new):
            res, first = t % dil, t // dil
            kc = cref[:, res * kv_cols: res * kv_cols + GROUP_COLS]
            vc = cref[:, res * kv_cols + GROUP_COLS: (res + 1) * kv_cols]
            q = q_ref[t:t + 1, gcols]
            prod = kc * q
            new_rows = [tn for tn in range(t + 1) if (t - tn) % dil == 0 and (t - tn) // dil <= win // dil]
            for h in range(HEADS_PER_GROUP):
                cols = slice(h * HEAD_DIM, (h + 1) * HEAD_DIM)
                s = jnp.sum(prod[:, cols], axis=-1, keepdims=True) * ATTN_SCALE
                if first > 0:
                    s = jnp.where(rowid >= first, s, NEG)
                qh = q[:, cols]
                s_new = [jnp.sum(qh * k_ref[tn:tn + 1, g * GROUP_COLS + h * HEAD_DIM:
                                            g * GROUP_COLS + (h + 1) * HEAD_DIM],
                                 axis=-1, keepdims=True) * ATTN_SCALE for tn in new_rows]
                mx = jnp.max(s, axis=0, keepdims=True)
                for sn in s_new:
                    mx = jnp.maximum(mx, sn)
                p = jnp.exp(s - mx)
                l = jnp.sum(p, axis=0, keepdims=True)
                acc = jnp.sum(p * vc[:, cols], axis=0, keepdims=True)
                for tn, sn in zip(new_rows, s_new):
                    pn = jnp.exp(sn - mx)
                    l = l + pn
                    acc = acc + pn * v_ref[tn:tn + 1, g * GROUP_COLS + h * HEAD_DIM:
                                           g * GROUP_COLS + (h + 1) * HEAD_DIM]
                o_ref[g, t:t + 1, cols] = acc / l
                lse_ref[g, t:t + 1, h:h + 1] = mx + jnp.log(l)


def _attn_sample(q, k, v, caches):
    bd, n_new, _ = q.shape
    kv_cols = 2 * GROUP_COLS
    views, specs = [], []
    for (win, dil), c in zip(DILATED_GROUPS, caches):
        assert c.shape[1] == win and win == CHUNK * dil, "cache must hold exactly one full window"
        used = min(dil, n_new)
        views.append(c.reshape(bd, CHUNK, dil, kv_cols)[:, :, :used].reshape(bd, CHUNK, used * kv_cols))
        specs.append(pl.BlockSpec((None, CHUNK, used * kv_cols), lambda b: (b, 0, 0)))
    new_spec = pl.BlockSpec((None, n_new, B_WIDTH), lambda b: (b, 0, 0))
    return pl.pallas_call(
        functools.partial(_attn_sample_kernel, n_new=n_new),
        out_shape=[jax.ShapeDtypeStruct((N_DIL, bd, n_new, GROUP_COLS), F32),
                   jax.ShapeDtypeStruct((N_DIL, bd, n_new, HEADS_PER_GROUP), F32)],
        grid=(bd,),
        in_specs=[new_spec, new_spec, new_spec] + specs,
        out_specs=[pl.BlockSpec((N_DIL, None, n_new, GROUP_COLS), lambda b: (0, b, 0, 0)),
                   pl.BlockSpec((N_DIL, None, n_new, HEADS_PER_GROUP), lambda b: (0, b, 0, 0))],
        compiler_params=_cparams("parallel"),
        name="attn_sample",
    )(q, k, v, *views)


def _merge_kernel(sg_ref, o0_ref, o1_ref, o2_ref, l0_ref, l1_ref, l2_ref, ga_ref, gb_ref,
                  wa_ref, wb_ref, *rest, permuted):
    if permuted:
        pt_ref, m_ref, ob_ref = rest
    else:
        m_ref, ob_ref = rest
    j = pl.program_id(1)

    @pl.when(j == 0)
    def _():
        l0, l1, l2 = l0_ref[...], l1_ref[...], l2_ref[...]
        mx = jnp.maximum(jnp.maximum(l0, l1), l2)
        e0, e1, e2 = jnp.exp(l0 - mx), jnp.exp(l1 - mx), jnp.exp(l2 - mx)
        inv = 1.0 / (e0 + e1 + e2)
        ws = (e0 * inv, e1 * inv, e2 * inv)
        o_refs = (o0_ref, o1_ref, o2_ref)
        tm = ob_ref.shape[0]
        for blk in range(max(tm // PERM_BLOCK, 1)):
            rows = slice(blk * PERM_BLOCK, min((blk + 1) * PERM_BLOCK, tm))
            og = []
            for g in range(N_DIL):
                if permuted and g > 0:
                    og.append(jnp.dot(pt_ref[g - 1], o_refs[g][rows, :], preferred_element_type=F32))
                else:
                    og.append(o_refs[g][rows, :].astype(F32))
            for h in range(HEADS_PER_GROUP):
                cols = slice(h * HEAD_DIM, (h + 1) * HEAD_DIM)
                ob = sum(ws[g][rows, h:h + 1] * og[g][:, cols] for g in range(N_DIL))
                ob_ref[rows, cols] = ob.astype(BF16)

    ya = jnp.dot(sg_ref[...], wa_ref[...], preferred_element_type=F32)
    yb = jnp.dot(ob_ref[...], wb_ref[...], preferred_element_type=F32)
    m_ref[...] = (ga_ref[...].astype(F32) * ya + gb_ref[...].astype(F32) * yb).astype(BF16)


def _merge(z, sg, os_, lses, wa, wb, tm, tn, perms_t=None):
    m = z.shape[0]
    row = lambda width: pl.BlockSpec((tm, width), lambda i, j: (i, 0))
    in_specs = [row(A_WIDTH)] + [row(GROUP_COLS)] * 3 + [row(HEADS_PER_GROUP)] * 3 + [
        pl.BlockSpec((tm, tn), lambda i, j: (i, COL_GA // tn + j)),
        pl.BlockSpec((tm, tn), lambda i, j: (i, COL_GB // tn + j)),
        pl.BlockSpec((A_WIDTH, tn), lambda i, j: (0, j)),
        pl.BlockSpec((GROUP_COLS, tn), lambda i, j: (0, j)),
    ]
    args = [sg, *os_, *lses, z, z, wa, wb]
    if perms_t is not None:
        assert tm % PERM_BLOCK == 0
        in_specs.append(pl.BlockSpec(perms_t.shape, lambda i, j: (0, 0, 0)))
        args.append(perms_t)
    return pl.pallas_call(
        functools.partial(_merge_kernel, permuted=perms_t is not None),
        out_shape=jax.ShapeDtypeStruct((m, D_MODEL), BF16),
        grid=(m // tm, D_MODEL // tn),
        in_specs=in_specs,
        out_specs=pl.BlockSpec((tm, tn), lambda i, j: (i, j)),
        scratch_shapes=[pltpu.VMEM((tm, GROUP_COLS), BF16)],
        compiler_params=_cparams("parallel", "arbitrary"),
        name="branch_merge",
    )(*args)


def _oproj_kernel(m_ref, x_ref, wo_ref, g_ref, wr_ref, br_ref, h_ref, hn_ref, ri_ref, rw_ref):
    h = x_ref[...] + jnp.dot(m_ref[...], wo_ref[...], preferred_element_type=F32)
    h_ref[...] = h
    ms = jnp.mean(h * h, axis=-1, keepdims=True)
    hn = h * lax.rsqrt(ms + RMS_EPS) * g_ref[...]
    hn_ref[...] = hn
    logits = jnp.dot(hn.astype(BF16), wr_ref[...], preferred_element_type=F32) + br_ref[...]
    lane = lax.broadcasted_iota(jnp.int32, logits.shape, 1).astype(F32)
    big = float(LANES)

    def first_argmax(vals, vmax):
        return jnp.min(jnp.where(vals == vmax, lane, big), axis=-1, keepdims=True)

    lg = jnp.where(lane < N_EXPERT_GROUPS, logits, NEG)
    gmax = jnp.max(lg, axis=-1, keepdims=True)
    gsel = first_argmax(lg, gmax)
    p_sel = 1.0 / jnp.sum(jnp.exp(lg - gmax), axis=-1, keepdims=True)
    lo = N_EXPERT_GROUPS + EXPERTS_PER_GROUP * gsel
    le = jnp.where(jnp.logical_and(lane >= lo, lane < lo + EXPERTS_PER_GROUP), logits, NEG)
    v1 = jnp.max(le, axis=-1, keepdims=True)
    i1 = first_argmax(le, v1)
    le2 = jnp.where(lane == i1, NEG, le)
    v2 = jnp.max(le2, axis=-1, keepdims=True)
    i2 = first_argmax(le2, v2)
    e2 = jnp.exp(v2 - v1)
    w1 = p_sel / (1.0 + e2)
    w2 = p_sel * e2 / (1.0 + e2)
    ri = jnp.where(lane == 0, i1 - N_EXPERT_GROUPS, jnp.where(lane == 1, i2 - N_EXPERT_GROUPS, 0.0))
    ri_ref[...] = ri.astype(jnp.int32)
    rw_ref[...] = jnp.where(lane == 0, w1, jnp.where(lane == 1, w2, 0.0))


def _oproj(mm, x, wo, norm_g, wr, br, tm):
    m = x.shape[0]
    row = lambda width: pl.BlockSpec((tm, width), lambda i: (i, 0))
    full = lambda a, b: pl.BlockSpec((a, b), lambda i: (0, 0))
    return pl.pallas_call(
        _oproj_kernel,
        out_shape=[jax.ShapeDtypeStruct((m, D_MODEL), F32), jax.ShapeDtypeStruct((m, D_MODEL), F32),
                   jax.ShapeDtypeStruct((m, LANES), jnp.int32), jax.ShapeDtypeStruct((m, LANES), F32)],
        grid=(m // tm,),
        in_specs=[row(D_MODEL), row(D_MODEL), full(D_MODEL, D_MODEL), full(1, D_MODEL),
                  full(D_MODEL, LANES), full(1, LANES)],
        out_specs=[row(D_MODEL), row(D_MODEL), row(LANES), row(LANES)],
        compiler_params=_cparams("parallel"),
        name="oproj_router",
    )(mm, x, wo, norm_g, wr, br)


def _moe_kernel(texp_ref, nused_ref, tok_ref, hn_hbm, wg_ref, wu_ref, wd_ref, ys_ref, xbuf, sem):
    i = pl.program_id(0)
    tm = xbuf.shape[0]

    @pl.when(i < nused_ref[0])
    def _():
        def issue(r, carry):
            tok = tok_ref[i * tm + r]
            pltpu.make_async_copy(hn_hbm.at[pl.ds(tok, 1)], xbuf.at[pl.ds(r, 1)], sem).start()
            return carry

        lax.fori_loop(0, tm, issue, 0)
        pltpu.make_async_copy(xbuf, xbuf, sem).wait()
        x = xbuf[...].astype(BF16)
        gate = jnp.dot(x, wg_ref[...], preferred_element_type=F32)
        up = jnp.dot(x, wu_ref[...], preferred_element_type=F32)
        hid = (gate * _sigmoid(gate) * up).astype(BF16)
        ys_ref[...] = jnp.dot(hid, wd_ref[...], preferred_element_type=F32)

    @pl.when(i >= nused_ref[0])
    def _():
        ys_ref[...] = jnp.zeros_like(ys_ref)


def _moe(hn, wg, wu, wd, tile_expert, n_used, tok_of_slot, tm):
    n_tiles = tile_expert.shape[0]
    grid_spec = pltpu.PrefetchScalarGridSpec(
        num_scalar_prefetch=3,
        grid=(n_tiles,),
        in_specs=[
            pl.BlockSpec(memory_space=pl.ANY),
            pl.BlockSpec((None, D_MODEL, D_EXPERT), lambda i, te, nu, tk: (te[i], 0, 0)),
            pl.BlockSpec((None, D_MODEL, D_EXPERT), lambda i, te, nu, tk: (te[i], 0, 0)),
            pl.BlockSpec((None, D_EXPERT, D_MODEL), lambda i, te, nu, tk: (te[i], 0, 0)),
        ],
        out_specs=pl.BlockSpec((tm, D_MODEL), lambda i, te, nu, tk: (i, 0)),
        scratch_shapes=[pltpu.VMEM((tm, D_MODEL), F32), pltpu.SemaphoreType.DMA(())],
    )
    return pl.pallas_call(
        _moe_kernel,
        out_shape=jax.ShapeDtypeStruct((n_tiles * tm, D_MODEL), F32),
        grid_spec=grid_spec,
        compiler_params=_cparams("arbitrary"),
        name="moe_experts",
    )(tile_expert, n_used, tok_of_slot, hn, wg, wu, wd)


def _combine_kernel(slot_ref, h_ref, rw_ref, g_ref, ys_hbm, y_ref, ybuf, sem, *, tm):
    i = pl.program_id(0)

    def issue(r, carry):
        for k in range(TOP_K):
            slot = slot_ref[(i * tm + r) * TOP_K + k]
            pltpu.make_async_copy(ys_hbm.at[pl.ds(slot, 1)], ybuf.at[k, pl.ds(r, 1)], sem).start()
        return carry

    lax.fori_loop(0, tm, issue, 0)
    for k in range(TOP_K):
        pltpu.make_async_copy(ybuf.at[k], ybuf.at[k], sem).wait()
    y = h_ref[...] + rw_ref[:, 0:1] * ybuf[0] + rw_ref[:, 1:2] * ybuf[1]
    ms = jnp.mean(y * y, axis=-1, keepdims=True)
    y_ref[...] = y * lax.rsqrt(ms + RMS_EPS) * g_ref[...]


def _combine(h, route_w, norm_g, ys, slot, tm):
    m = h.shape[0]
    grid_spec = pltpu.PrefetchScalarGridSpec(
        num_scalar_prefetch=1,
        grid=(m // tm,),
        in_specs=[
            pl.BlockSpec((tm, D_MODEL), lambda i, s: (i, 0)),
            pl.BlockSpec((tm, LANES), lambda i, s: (i, 0)),
            pl.BlockSpec((1, D_MODEL), lambda i, s: (0, 0)),
            pl.BlockSpec(memory_space=pl.ANY),
        ],
        out_specs=pl.BlockSpec((tm, D_MODEL), lambda i, s: (i, 0)),
        scratch_shapes=[pltpu.VMEM((TOP_K, tm, D_MODEL), F32), pltpu.SemaphoreType.DMA(())],
    )
    return pl.pallas_call(
        functools.partial(_combine_kernel, tm=tm),
        out_shape=jax.ShapeDtypeStruct((m, D_MODEL), F32),
        grid_spec=grid_spec,
        compiler_params=_cparams("arbitrary"),
        name="moe_combine",
    )(slot, h, route_w, norm_g, ys)


def _routing_tables(route_i, m, tm):
    n = m * TOP_K
    n_tiles = pl.cdiv(n, tm) + N_EXPERTS - 1
    e_flat = route_i[:, :TOP_K].reshape(n)
    onehot = (e_flat[:, None] == jnp.arange(N_EXPERTS, dtype=jnp.int32)[None, :]).astype(jnp.int32)
    csum = jnp.cumsum(onehot, axis=0)
    rank = jnp.take_along_axis(csum, e_flat[:, None], axis=1)[:, 0] - 1
    counts = csum[-1]
    tiles_e = (counts + tm - 1) // tm
    tile_end = jnp.cumsum(tiles_e)
    tile_start = tile_end - tiles_e
    slot = (tile_start[e_flat] * tm + rank).astype(jnp.int32)
    n_used = tile_end[-1:].astype(jnp.int32)
    tile_ids = jnp.arange(n_tiles, dtype=jnp.int32)
    tile_expert = jnp.sum((tile_end[None, :] <= tile_ids[:, None]).astype(jnp.int32), axis=1)
    tile_expert = jnp.minimum(tile_expert, N_EXPERTS - 1)
    tok_of_slot = jnp.zeros((n_tiles * tm,), jnp.int32).at[slot].set(
        jnp.arange(n, dtype=jnp.int32) // TOP_K, unique_indices=True)
    return slot, tile_expert, n_used, tok_of_slot


def _token_tail(x, z, sg, os_, lses, wts, tm, perms_t=None):
    mm = _merge(z, sg, os_, lses, wts["wa"], wts["wb"], tm=min(tm, 512), tn=512, perms_t=perms_t)
    h, hn, route_i, route_w = _oproj(mm, x, wts["wo"], wts["norm_ffn"], wts["wr"], wts["br"], tm=min(tm, 256))
    moe_tm = min(tm, MOE_TM)
    slot, tile_expert, n_used, tok_of_slot = _routing_tables(route_i, x.shape[0], moe_tm)
    ys = _moe(hn, wts["wg"], wts["wu"], wts["wd"], tile_expert, n_used, tok_of_slot, moe_tm)
    return _combine(h, route_w, wts["norm_final"], ys, slot, tm=min(tm, 256))


def _kv_rows(z, lead, keep_from, group, permuted):
    zz = z.reshape(lead + (IN_COLS,))[:, keep_from:]
    k = zz[..., COL_K + group * GROUP_COLS: COL_K + (group + 1) * GROUP_COLS]
    v = zz[..., COL_VB + group * GROUP_COLS: COL_VB + (group + 1) * GROUP_COLS]
    kv = jnp.stack([k, v], axis=2).astype(F32)
    dil = DILATED_GROUPS[group][1]
    if permuted and dil > 1:
        assert keep_from % PERM_BLOCK == 0 and kv.shape[1] % PERM_BLOCK == 0
        b, rows = kv.shape[:2]
        kv = kv.reshape(b, rows // PERM_BLOCK, dil, PERM_BLOCK // dil, 2, GROUP_COLS)
        kv = kv.transpose(0, 1, 3, 2, 4, 5).reshape(b, rows, 2, GROUP_COLS)
    return kv.reshape(kv.shape[:3] + (HEADS_PER_GROUP, HEAD_DIM))


def kernel(x_prompt, x_sample, cache_kv_w128, cache_kv_w512, cache_kv_w2048, norm_mix, w_in, ln_v_g, ln_v_b, w_s, b_s, w_a_out, w_b_out, w_o, norm_ffn, w_route_group, b_route_group, w_route_expert, b_route_expert, w_gate_e, w_up_e, w_down_e, norm_final):
    assert norm_mix.shape[0] == 1, "single-layer trunk"
    batch, seq, _ = x_prompt.shape
    bd, n_new, _ = x_sample.shape
    caches = (cache_kv_w128[0], cache_kv_w512[0], cache_kv_w2048[0])

    pad = LANES - N_EXPERT_GROUPS - N_EXPERTS
    wr = jnp.concatenate([w_route_group[0], w_route_expert[0], jnp.zeros((D_MODEL, pad), F32)], axis=1)
    br = jnp.concatenate([b_route_group[0], b_route_expert[0], jnp.zeros((pad,), F32)])[None, :]
    wts = dict(
        wa=w_a_out[0].astype(BF16), wb=w_b_out[0].astype(BF16), wo=w_o[0].astype(BF16),
        wg=w_gate_e[0].astype(BF16), wu=w_up_e[0].astype(BF16), wd=w_down_e[0].astype(BF16),
        wr=wr.astype(BF16), br=br, norm_ffn=norm_ffn[0][None, :], norm_final=norm_final[None, :],
    )
    w_in_b = w_in[0].astype(BF16)
    norm_g = norm_mix[0][None, :]
    ln_g, ln_b = ln_v_g[0][None, :], ln_v_b[0][None, :]

    xp = x_prompt.reshape(batch * seq, D_MODEL)
    perms = jnp.stack([_class_major_perm(dil) for _, dil in DILATED_GROUPS[1:]])
    zp = _inproj(xp, norm_g, w_in_b, tm=1024, perms=perms)
    (sgp,) = _sgu(zp, ln_g, ln_b, w_s[0], b_s[0].T, chunks=4, emit_vn=False)
    os_p, lses_p = zip(*[_attn_group(zp, batch, seq, g) for g in range(N_DIL)])
    y_prompt = _token_tail(xp, zp, sgp, os_p, lses_p, wts, tm=512,
                           perms_t=perms.transpose(0, 2, 1)).reshape(batch, seq, D_MODEL)
    kv_prompt = [_kv_rows(zp, (batch, seq), seq - min(win, seq), g, permuted=True)[None]
                 for g, (win, _) in enumerate(DILATED_GROUPS)]

    ms = bd * n_new
    assert ms == CHUNK and n_new <= CHUNK
    xs = x_sample.reshape(ms, D_MODEL)
    zs = _inproj(xs, norm_g, w_in_b, tm=ms)
    eye = jnp.eye(bd, dtype=F32)
    ws_s = jnp.einsum("ab,gts->gatbs", eye, w_s[0][:, :n_new, :n_new]).reshape(A_GROUPS, ms, ms)
    bst_s = jnp.tile(b_s[0][:, :n_new].T, (bd, 1))
    sgs, vns = _sgu(zs, ln_g, ln_b, ws_s, bst_s, chunks=1, emit_vn=True)
    zs3 = zs.reshape(bd, n_new, IN_COLS)
    qkv = [zs3[..., c:c + B_WIDTH].astype(F32) for c in (COL_Q, COL_K, COL_VB)]
    o_s, lse_s = _attn_sample(*qkv, caches)
    os_s = [o_s[g].reshape(ms, GROUP_COLS).astype(BF16) for g in range(N_DIL)]
    lses_s = [lse_s[g].reshape(ms, HEADS_PER_GROUP) for g in range(N_DIL)]
    y_sample = _token_tail(xs, zs, sgs, os_s, lses_s, wts, tm=ms).reshape(bd, n_new, D_MODEL)
    kv_sample = [_kv_rows(zs, (bd, n_new), 0, g, permuted=False)[None] for g in range(N_DIL)]
    chunk_v = vns.reshape(1, bd, n_new, A_WIDTH)

    return (y_prompt, y_sample, *kv_prompt, *kv_sample, chunk_v)
```

```python
import functools

import jax
import jax.numpy as jnp
from jax import lax
from jax.experimental import pallas as pl
from jax.experimental.pallas import tpu as pltpu

F32 = jnp.float32
BF16 = jnp.bfloat16

D_MODEL = 2048
CHUNK = 128
A_GROUPS = 16
A_GROUP_DIM = 128
A_WIDTH = A_GROUPS * A_GROUP_DIM
HEAD_DIM = 128
HEADS_PER_GROUP = 4
GROUP_COLS = HEADS_PER_GROUP * HEAD_DIM
DILATED_GROUPS = ((128, 1), (512, 4), (2048, 16))
N_DIL = len(DILATED_GROUPS)
B_WIDTH = N_DIL * GROUP_COLS
IN_COLS = 2 * A_WIDTH + 3 * B_WIDTH + 2 * D_MODEL
COL_U, COL_V = 0, A_WIDTH
COL_Q = 2 * A_WIDTH
COL_K = COL_Q + B_WIDTH
COL_VB = COL_K + B_WIDTH
COL_GA = COL_VB + B_WIDTH
COL_GB = COL_GA + D_MODEL
N_EXPERT_GROUPS = 4
EXPERTS_PER_GROUP = 8
N_EXPERTS = N_EXPERT_GROUPS * EXPERTS_PER_GROUP
TOP_K = 2
D_EXPERT = 256
RMS_EPS = 1e-6
LN_EPS = 1e-5
NEG = -1e30
ATTN_SCALE = HEAD_DIM ** -0.5

LANES = 128
VMEM_LIMIT_BYTES = 56 * 1024 * 1024
IN_TN = 512
MOE_TM = 256
PERM_BLOCK = 256


def _cparams(*sem):
    return pltpu.CompilerParams(dimension_semantics=sem, vmem_limit_bytes=VMEM_LIMIT_BYTES)


def _gelu_tanh(x):
    return 0.5 * x * (1.0 + jnp.tanh(0.7978845608028654 * (x + 0.044715 * (x * x * x))))


def _sigmoid(x):
    return 1.0 / (1.0 + jnp.exp(-x))


def _class_major_perm(dil):
    out_row = jnp.arange(PERM_BLOCK, dtype=jnp.int32)
    src = (out_row % (PERM_BLOCK // dil)) * dil + out_row // (PERM_BLOCK // dil)
    return (src[:, None] == jnp.arange(PERM_BLOCK, dtype=jnp.int32)[None, :]).astype(BF16)


def _inproj_kernel(x_ref, g_ref, w_ref, *rest, permute):
    if permute:
        perm_ref, z_ref, xn_ref = rest
    else:
        z_ref, xn_ref = rest
    j = pl.program_id(1)

    @pl.when(j == 0)
    def _():
        x = x_ref[...]
        ms = jnp.mean(x * x, axis=-1, keepdims=True)
        xn_ref[...] = (x * lax.rsqrt(ms + RMS_EPS) * g_ref[...]).astype(BF16)

    z = jnp.dot(xn_ref[...], w_ref[...], preferred_element_type=F32)
    is_gelu = j < COL_Q // IN_TN
    is_gate = j >= COL_GA // IN_TN
    is_plain = jnp.logical_not(jnp.logical_or(is_gelu, is_gate))
    group = (j - COL_Q // IN_TN) % N_DIL

    @pl.when(is_gelu)
    def _():
        z_ref[...] = _gelu_tanh(z).astype(BF16)

    @pl.when(is_gate)
    def _():
        z_ref[...] = _sigmoid(z).astype(BF16)

    if not permute:
        @pl.when(is_plain)
        def _():
            z_ref[...] = z.astype(BF16)
        return

    @pl.when(jnp.logical_and(is_plain, group == 0))
    def _():
        z_ref[...] = z.astype(BF16)

    for g in range(1, N_DIL):
        @pl.when(jnp.logical_and(is_plain, group == g))
        def _():
            zb = z.astype(BF16)
            for blk in range(z.shape[0] // PERM_BLOCK):
                rows = slice(blk * PERM_BLOCK, (blk + 1) * PERM_BLOCK)
                z_ref[rows, :] = jnp.dot(perm_ref[g - 1], zb[rows, :], preferred_element_type=F32).astype(BF16)


def _inproj(x, norm_g, w_in_bf16, tm, perms=None):
    m = x.shape[0]
    in_specs = [
        pl.BlockSpec((tm, D_MODEL), lambda i, j: (i, 0)),
        pl.BlockSpec((1, D_MODEL), lambda i, j: (0, 0)),
        pl.BlockSpec((D_MODEL, IN_TN), lambda i, j: (0, j)),
    ]
    args = [x, norm_g, w_in_bf16]
    if perms is not None:
        in_specs.append(pl.BlockSpec(perms.shape, lambda i, j: (0, 0, 0)))
        args.append(perms)
    return pl.pallas_call(
        functools.partial(_inproj_kernel, permute=perms is not None),
        out_shape=jax.ShapeDtypeStruct((m, IN_COLS), BF16),
        grid=(m // tm, IN_COLS // IN_TN),
        in_specs=in_specs,
        out_specs=pl.BlockSpec((tm, IN_TN), lambda i, j: (i, j)),
        scratch_shapes=[pltpu.VMEM((tm, D_MODEL), BF16)],
        compiler_params=_cparams("parallel", "arbitrary"),
        name="inproj",
    )(*args)


def _sgu_kernel(u_ref, v_ref, lng_ref, lnb_ref, ws_ref, bst_ref, sg_ref, *vn_out, chunks):
    row = lax.broadcasted_iota(jnp.int32, (CHUNK, CHUNK), 0)
    col = lax.broadcasted_iota(jnp.int32, (CHUNK, CHUNK), 1)
    tri = row >= col
    ws = [jnp.where(tri, ws_ref[g], 0.0).astype(BF16) for g in range(A_GROUPS)]
    for c in range(chunks):
        rows = slice(c * CHUNK, (c + 1) * CHUNK)
        v = v_ref[rows, :].astype(F32)
        mu = jnp.mean(v, axis=-1, keepdims=True)
        vc = v - mu
        var = jnp.mean(vc * vc, axis=-1, keepdims=True)
        vn = vc * lax.rsqrt(var + LN_EPS) * lng_ref[...] + lnb_ref[...]
        if vn_out:
            vn_out[0][rows, :] = vn
        vnb = vn.astype(BF16)
        for g in range(A_GROUPS):
            cols = slice(g * A_GROUP_DIM, (g + 1) * A_GROUP_DIM)
            s = jnp.dot(ws[g], vnb[:, cols], preferred_element_type=F32) + bst_ref[:, g:g + 1]
            sg_ref[rows, cols] = (u_ref[rows, cols].astype(F32) * s).astype(BF16)


def _sgu(z, ln_g, ln_b, ws, bst, chunks, emit_vn):
    m = z.shape[0]
    tm = chunks * CHUNK
    out_shape = [jax.ShapeDtypeStruct((m, A_WIDTH), BF16)]
    out_specs = [pl.BlockSpec((tm, A_WIDTH), lambda i: (i, 0))]
    if emit_vn:
        out_shape.append(jax.ShapeDtypeStruct((m, A_WIDTH), F32))
        out_specs.append(pl.BlockSpec((tm, A_WIDTH), lambda i: (i, 0)))
    return pl.pallas_call(
        functools.partial(_sgu_kernel, chunks=chunks),
        out_shape=out_shape,
        grid=(m // tm,),
        in_specs=[
            pl.BlockSpec((tm, A_WIDTH), lambda i: (i, COL_U // A_WIDTH)),
            pl.BlockSpec((tm, A_WIDTH), lambda i: (i, COL_V // A_WIDTH)),
            pl.BlockSpec((1, A_WIDTH), lambda i: (0, 0)),
            pl.BlockSpec((1, A_WIDTH), lambda i: (0, 0)),
            pl.BlockSpec((A_GROUPS, CHUNK, CHUNK), lambda i: (0, 0, 0)),
            pl.BlockSpec((CHUNK, A_GROUPS), lambda i: (0, 0)),
        ],
        out_specs=out_specs,
        compiler_params=_cparams("parallel"),
        name="sgu",
    )(z, z, ln_g, ln_b, ws, bst)


def _attn_kernel(q_ref, kc_ref, kp_ref, vc_ref, vp_ref, o_ref, lse_ref, *, qblocks):
    i = pl.program_id(2)
    qi = lax.broadcasted_iota(jnp.int32, (CHUNK, CHUNK), 0)
    ki = lax.broadcasted_iota(jnp.int32, (CHUNK, CHUNK), 1)
    cur_mask = ki <= qi
    no_prev = jnp.where(i == 0, CHUNK, 0)
    dn = (((1,), (1,)), ((), ()))
    rpc = q_ref.shape[1]
    nbq = CHUNK // rpc

    def load(ref, a, cols):
        return ref[a * nbq:(a + 1) * nbq, :, cols].reshape(CHUNK, HEAD_DIM)

    for a in range(qblocks):
        rows = slice(a * CHUNK, (a + 1) * CHUNK)
        for h in range(HEADS_PER_GROUP):
            cols = slice(h * HEAD_DIM, (h + 1) * HEAD_DIM)
            q = load(q_ref, a, cols)
            kc = load(kc_ref, a, cols)
            vc = load(vc_ref, a, cols)
            if a == 0:
                kp, vp = load(kp_ref, 0, cols), load(vp_ref, 0, cols)
                prev_mask = ki >= qi + no_prev
            else:
                kp, vp = load(kc_ref, a - 1, cols), load(vc_ref, a - 1, cols)
                prev_mask = ki >= qi
            s_c = lax.dot_general(q, kc, dn, preferred_element_type=F32) * ATTN_SCALE
            s_p = lax.dot_general(q, kp, dn, preferred_element_type=F32) * ATTN_SCALE
            s_c = jnp.where(cur_mask, s_c, NEG)
            s_p = jnp.where(prev_mask, s_p, NEG)
            mx = jnp.maximum(jnp.max(s_c, axis=-1, keepdims=True), jnp.max(s_p, axis=-1, keepdims=True))
            p_c = jnp.exp(s_c - mx)
            p_p = jnp.exp(s_p - mx)
            l = jnp.sum(p_c, axis=-1, keepdims=True) + jnp.sum(p_p, axis=-1, keepdims=True)
            acc = jnp.dot(p_c.astype(BF16), vc, preferred_element_type=F32)
            acc = acc + jnp.dot(p_p.astype(BF16), vp, preferred_element_type=F32)
            o_ref[a * nbq:(a + 1) * nbq, :, cols] = (acc / l).astype(BF16).reshape(nbq, rpc, HEAD_DIM)
            lse_ref[rows, h:h + 1] = mx + jnp.log(l)


def _rows_per_class(dil):
    return min(PERM_BLOCK // dil, CHUNK)


def _attn_group(z, batch, seq, group):
    _, dil = DILATED_GROUPS[group]
    sub = seq // dil
    rpc = _rows_per_class(dil)
    qblocks = 2 if sub % (2 * CHUNK) == 0 else 1
    tq = qblocks * CHUNK
    cq = COL_Q // GROUP_COLS + group
    ck = COL_K // GROUP_COLS + group
    cv = COL_VB // GROUP_COLS + group
    zv = z.reshape(batch, sub // rpc, dil, rpc, IN_COLS)

    def cur(cb):
        return pl.BlockSpec((None, tq // rpc, None, rpc, GROUP_COLS), lambda b, r, i: (b, i, r, 0, cb))

    def prev(cb):
        return pl.BlockSpec((None, CHUNK // rpc, None, rpc, GROUP_COLS),
                            lambda b, r, i: (b, jnp.maximum(i * qblocks - 1, 0), r, 0, cb))

    o, lse = pl.pallas_call(
        functools.partial(_attn_kernel, qblocks=qblocks),
        out_shape=[jax.ShapeDtypeStruct((batch, sub // rpc, dil, rpc, GROUP_COLS), BF16),
                   jax.ShapeDtypeStruct((batch, dil, sub, HEADS_PER_GROUP), F32)],
        grid=(batch, dil, sub // tq),
        in_specs=[cur(cq), cur(ck), prev(ck), cur(cv), prev(cv)],
        out_specs=[pl.BlockSpec((None, tq // rpc, None, rpc, GROUP_COLS), lambda b, r, i: (b, i, r, 0, 0)),
                   pl.BlockSpec((None, None, tq, HEADS_PER_GROUP), lambda b, r, i: (b, r, i, 0))],
        compiler_params=_cparams("parallel", "parallel", "arbitrary"),
        name=f"attn_prompt_g{group}",
    )(zv, zv, zv, zv, zv)
    o = o.reshape(batch * seq, GROUP_COLS)
    lse = lse.transpose(0, 2, 1, 3).reshape(batch * seq, HEADS_PER_GROUP)
    return o, lse


def _attn_sample_kernel(q_ref, k_ref, v_ref, c0_ref, c1_ref, c2_ref, o_ref, lse_ref, *, n_new):
    caches = (c0_ref, c1_ref, c2_ref)
    rowid = lax.broadcasted_iota(jnp.int32, (CHUNK, HEADS_PER_GROUP, 1), 0)
    for g, (win, dil) in enumerate(DILATED_GROUPS):
        cref = caches[g]
        hs = slice(g * HEADS_PER_GROUP, (g + 1) * HEADS_PER_GROUP)
        for t in range(n_new):
            res, first = t % dil, t // dil
            kc = cref[:, res, 0]
            vc = cref[:, res, 1]
            q = q_ref[t, hs, :]
            s = jnp.sum(kc * q[None], axis=-1, keepdims=True) * ATTN_SCALE
            if first > 0:
                s = jnp.where(rowid >= first, s, NEG)
            new_rows = [tn for tn in range(t + 1) if (t - tn) % dil == 0 and (t - tn) // dil <= win // dil]
            s_new = [jnp.sum(q * k_ref[tn, hs, :], axis=-1, keepdims=True) * ATTN_SCALE for tn in new_rows]
            mx = jnp.max(s, axis=0)
            for sn in s_new:
                mx = jnp.maximum(mx, sn)
            p = jnp.exp(s - mx[None])
            l = jnp.sum(p, axis=0)
            acc = jnp.sum(p * vc, axis=0)
            for tn, sn in zip(new_rows, s_new):
                pn = jnp.exp(sn - mx)
                l = l + pn
                acc = acc + pn * v_ref[tn, hs, :]
            o_ref[g, t] = acc / l
            lse_ref[g, t] = jnp.broadcast_to(mx + jnp.log(l), (HEADS_PER_GROUP, HEAD_DIM))


def _attn_sample(q, k, v, caches):
    bd, n_new = q.shape[:2]
    views, specs = [], []
    for (win, dil), c in zip(DILATED_GROUPS, caches):
        assert c.shape[2] == win and win == CHUNK * dil, "cache must hold exactly one full window"
        used = min(dil, n_new)
        views.append(c.reshape(1, bd, CHUNK, dil, 2, HEADS_PER_GROUP, HEAD_DIM))
        specs.append(pl.BlockSpec((None, None, CHUNK, used, 2, HEADS_PER_GROUP, HEAD_DIM),
                                  lambda b: (0, b, 0, 0, 0, 0, 0)))
    new_spec = pl.BlockSpec((None, n_new, N_DIL * HEADS_PER_GROUP, HEAD_DIM), lambda b: (b, 0, 0, 0))
    out_sds = jax.ShapeDtypeStruct((N_DIL, bd, n_new, HEADS_PER_GROUP, HEAD_DIM), F32)
    out_spec = pl.BlockSpec((N_DIL, None, n_new, HEADS_PER_GROUP, HEAD_DIM), lambda b: (0, b, 0, 0, 0))
    return pl.pallas_call(
        functools.partial(_attn_sample_kernel, n_new=n_new),
        out_shape=[out_sds, out_sds],
        grid=(bd,),
        in_specs=[new_spec, new_spec, new_spec] + specs,
        out_specs=[out_spec, out_spec],
        compiler_params=_cparams("parallel"),
        name="attn_sample",
    )(q, k, v, *views)


def _merge_kernel(sg_ref, o0_ref, o1_ref, o2_ref, l0_ref, l1_ref, l2_ref, ga_ref, gb_ref,
                  wa_ref, wb_ref, *rest, permuted):
    if permuted:
        pt_ref, m_ref, ob_ref = rest
    else:
        m_ref, ob_ref = rest
    j = pl.program_id(1)

    @pl.when(j == 0)
    def _():
        l0, l1, l2 = l0_ref[...], l1_ref[...], l2_ref[...]
        mx = jnp.maximum(jnp.maximum(l0, l1), l2)
        e0, e1, e2 = jnp.exp(l0 - mx), jnp.exp(l1 - mx), jnp.exp(l2 - mx)
        inv = 1.0 / (e0 + e1 + e2)
        ws = (e0 * inv, e1 * inv, e2 * inv)
        o_refs = (o0_ref, o1_ref, o2_ref)
        tm = ob_ref.shape[0]
        for blk in range(max(tm // PERM_BLOCK, 1)):
            rows = slice(blk * PERM_BLOCK, min((blk + 1) * PERM_BLOCK, tm))
            og = []
            for g in range(N_DIL):
                if permuted and g > 0:
                    og.append(jnp.dot(pt_ref[g - 1], o_refs[g][rows, :], preferred_element_type=F32))
                else:
                    og.append(o_refs[g][rows, :].astype(F32))
            for h in range(HEADS_PER_GROUP):
                cols = slice(h * HEAD_DIM, (h + 1) * HEAD_DIM)
                ob = sum(ws[g][rows, h:h + 1] * og[g][:, cols] for g in range(N_DIL))
                ob_ref[rows, cols] = ob.astype(BF16)

    ya = jnp.dot(sg_ref[...], wa_ref[...], preferred_element_type=F32)
    yb = jnp.dot(ob_ref[...], wb_ref[...], preferred_element_type=F32)
    m_ref[...] = (ga_ref[...].astype(F32) * ya + gb_ref[...].astype(F32) * yb).astype(BF16)


def _merge(z, sg, os_, lses, wa, wb, tm, tn, perms_t=None):
    m = z.shape[0]
    row = lambda width: pl.BlockSpec((tm, width), lambda i, j: (i, 0))
    in_specs = [row(A_WIDTH)] + [row(GROUP_COLS)] * 3 + [row(HEADS_PER_GROUP)] * 3 + [
        pl.BlockSpec((tm, tn), lambda i, j: (i, COL_GA // tn + j)),
        pl.BlockSpec((tm, tn), lambda i, j: (i, COL_GB // tn + j)),
        pl.BlockSpec((A_WIDTH, tn), lambda i, j: (0, j)),
        pl.BlockSpec((GROUP_COLS, tn), lambda i, j: (0, j)),
    ]
    args = [sg, *os_, *lses, z, z, wa, wb]
    if perms_t is not None:
        assert tm % PERM_BLOCK == 0
        in_specs.append(pl.BlockSpec(perms_t.shape, lambda i, j: (0, 0, 0)))
        args.append(perms_t)
    return pl.pallas_call(
        functools.partial(_merge_kernel, permuted=perms_t is not None),
        out_shape=jax.ShapeDtypeStruct((m, D_MODEL), BF16),
        grid=(m // tm, D_MODEL // tn),
        in_specs=in_specs,
        out_specs=pl.BlockSpec((tm, tn), lambda i, j: (i, j)),
        scratch_shapes=[pltpu.VMEM((tm, GROUP_COLS), BF16)],
        compiler_params=_cparams("parallel", "arbitrary"),
        name="branch_merge",
    )(*args)


def _oproj_kernel(m_ref, x_ref, wo_ref, g_ref, wr_ref, br_ref, h_ref, hn_ref, ri_ref, rw_ref):
    h = x_ref[...] + jnp.dot(m_ref[...], wo_ref[...], preferred_element_type=F32)
    h_ref[...] = h
    ms = jnp.mean(h * h, axis=-1, keepdims=True)
    hn = h * lax.rsqrt(ms + RMS_EPS) * g_ref[...]
    hn_ref[...] = hn
    logits = jnp.dot(hn.astype(BF16), wr_ref[...], preferred_element_type=F32) + br_ref[...]
    lane = lax.broadcasted_iota(jnp.int32, logits.shape, 1).astype(F32)
    big = float(LANES)

    def first_argmax(vals, vmax):
        return jnp.min(jnp.where(vals == vmax, lane, big), axis=-1, keepdims=True)

    lg = jnp.where(lane < N_EXPERT_GROUPS, logits, NEG)
    gmax = jnp.max(lg, axis=-1, keepdims=True)
    gsel = first_argmax(lg, gmax)
    p_sel = 1.0 / jnp.sum(jnp.exp(lg - gmax), axis=-1, keepdims=True)
    lo = N_EXPERT_GROUPS + EXPERTS_PER_GROUP * gsel
    le = jnp.where(jnp.logical_and(lane >= lo, lane < lo + EXPERTS_PER_GROUP), logits, NEG)
    v1 = jnp.max(le, axis=-1, keepdims=True)
    i1 = first_argmax(le, v1)
    le2 = jnp.where(lane == i1, NEG, le)
    v2 = jnp.max(le2, axis=-1, keepdims=True)
    i2 = first_argmax(le2, v2)
    e2 = jnp.exp(v2 - v1)
    w1 = p_sel / (1.0 + e2)
    w2 = p_sel * e2 / (1.0 + e2)
    ri = jnp.where(lane == 0, i1 - N_EXPERT_GROUPS, jnp.where(lane == 1, i2 - N_EXPERT_GROUPS, 0.0))
    ri_ref[...] = ri.astype(jnp.int32)
    rw_ref[...] = jnp.where(lane == 0, w1, jnp.where(lane == 1, w2, 0.0))


def _oproj(mm, x, wo, norm_g, wr, br, tm):
    m = x.shape[0]
    row = lambda width: pl.BlockSpec((tm, width), lambda i: (i, 0))
    full = lambda a, b: pl.BlockSpec((a, b), lambda i: (0, 0))
    return pl.pallas_call(
        _oproj_kernel,
        out_shape=[jax.ShapeDtypeStruct((m, D_MODEL), F32), jax.ShapeDtypeStruct((m, D_MODEL), F32),
                   jax.ShapeDtypeStruct((m, LANES), jnp.int32), jax.ShapeDtypeStruct((m, LANES), F32)],
        grid=(m // tm,),
        in_specs=[row(D_MODEL), row(D_MODEL), full(D_MODEL, D_MODEL), full(1, D_MODEL),
                  full(D_MODEL, LANES), full(1, LANES)],
        out_specs=[row(D_MODEL), row(D_MODEL), row(LANES), row(LANES)],
        compiler_params=_cparams("parallel"),
        name="oproj_router",
    )(mm, x, wo, norm_g, wr, br)


def _moe_kernel(texp_ref, nused_ref, tok_ref, hn_hbm, wg_ref, wu_ref, wd_ref, ys_ref, xbuf, sem):
    i = pl.program_id(0)
    tm = xbuf.shape[0]

    @pl.when(i < nused_ref[0])
    def _():
        def issue(r, carry):
            tok = tok_ref[i * tm + r]
            pltpu.make_async_copy(hn_hbm.at[pl.ds(tok, 1)], xbuf.at[pl.ds(r, 1)], sem).start()
            return carry

        lax.fori_loop(0, tm, issue, 0)
        pltpu.make_async_copy(xbuf, xbuf, sem).wait()
        x = xbuf[...].astype(BF16)
        gate = jnp.dot(x, wg_ref[...], preferred_element_type=F32)
        up = jnp.dot(x, wu_ref[...], preferred_element_type=F32)
        hid = (gate * _sigmoid(gate) * up).astype(BF16)
        ys_ref[...] = jnp.dot(hid, wd_ref[...], preferred_element_type=F32)

    @pl.when(i >= nused_ref[0])
    def _():
        ys_ref[...] = jnp.zeros_like(ys_ref)


def _moe(hn, wg, wu, wd, tile_expert, n_used, tok_of_slot, tm):
    n_tiles = tile_expert.shape[0]
    grid_spec = pltpu.PrefetchScalarGridSpec(
        num_scalar_prefetch=3,
        grid=(n_tiles,),
        in_specs=[
            pl.BlockSpec(memory_space=pl.ANY),
            pl.BlockSpec((None, D_MODEL, D_EXPERT), lambda i, te, nu, tk: (te[i], 0, 0)),
            pl.BlockSpec((None, D_MODEL, D_EXPERT), lambda i, te, nu, tk: (te[i], 0, 0)),
            pl.BlockSpec((None, D_EXPERT, D_MODEL), lambda i, te, nu, tk: (te[i], 0, 0)),
        ],
        out_specs=pl.BlockSpec((tm, D_MODEL), lambda i, te, nu, tk: (i, 0)),
        scratch_shapes=[pltpu.VMEM((tm, D_MODEL), F32), pltpu.SemaphoreType.DMA(())],
    )
    return pl.pallas_call(
        _moe_kernel,
        out_shape=jax.ShapeDtypeStruct((n_tiles * tm, D_MODEL), F32),
        grid_spec=grid_spec,
        compiler_params=_cparams("arbitrary"),
        name="moe_experts",
    )(tile_expert, n_used, tok_of_slot, hn, wg, wu, wd)


def _combine_kernel(slot_ref, h_ref, rw_ref, g_ref, ys_hbm, y_ref, ybuf, sem, *, tm):
    i = pl.program_id(0)

    def issue(r, carry):
        for k in range(TOP_K):
            slot = slot_ref[(i * tm + r) * TOP_K + k]
            pltpu.make_async_copy(ys_hbm.at[pl.ds(slot, 1)], ybuf.at[k, pl.ds(r, 1)], sem).start()
        return carry

    lax.fori_loop(0, tm, issue, 0)
    for k in range(TOP_K):
        pltpu.make_async_copy(ybuf.at[k], ybuf.at[k], sem).wait()
    y = h_ref[...] + rw_ref[:, 0:1] * ybuf[0] + rw_ref[:, 1:2] * ybuf[1]
    ms = jnp.mean(y * y, axis=-1, keepdims=True)
    y_ref[...] = y * lax.rsqrt(ms + RMS_EPS) * g_ref[...]


def _combine(h, route_w, norm_g, ys, slot, tm):
    m = h.shape[0]
    grid_spec = pltpu.PrefetchScalarGridSpec(
        num_scalar_prefetch=1,
        grid=(m // tm,),
        in_specs=[
            pl.BlockSpec((tm, D_MODEL), lambda i, s: (i, 0)),
            pl.BlockSpec((tm, LANES), lambda i, s: (i, 0)),
            pl.BlockSpec((1, D_MODEL), lambda i, s: (0, 0)),
            pl.BlockSpec(memory_space=pl.ANY),
        ],
        out_specs=pl.BlockSpec((tm, D_MODEL), lambda i, s: (i, 0)),
        scratch_shapes=[pltpu.VMEM((TOP_K, tm, D_MODEL), F32), pltpu.SemaphoreType.DMA(())],
    )
    return pl.pallas_call(
        functools.partial(_combine_kernel, tm=tm),
        out_shape=jax.ShapeDtypeStruct((m, D_MODEL), F32),
        grid_spec=grid_spec,
        compiler_params=_cparams("arbitrary"),
        name="moe_combine",
    )(slot, h, route_w, norm_g, ys)


def _routing_tables(route_i, m, tm):
    n = m * TOP_K
    n_tiles = pl.cdiv(n, tm) + N_EXPERTS - 1
    e_flat = route_i[:, :TOP_K].reshape(n)
    onehot = (e_flat[:, None] == jnp.arange(N_EXPERTS, dtype=jnp.int32)[None, :]).astype(jnp.int32)
    csum = jnp.cumsum(onehot, axis=0)
    rank = jnp.take_along_axis(csum, e_flat[:, None], axis=1)[:, 0] - 1
    counts = csum[-1]
    tiles_e = (counts + tm - 1) // tm
    tile_end = jnp.cumsum(tiles_e)
    tile_start = tile_end - tiles_e
    slot = (tile_start[e_flat] * tm + rank).astype(jnp.int32)
    n_used = tile_end[-1:].astype(jnp.int32)
    tile_ids = jnp.arange(n_tiles, dtype=jnp.int32)
    tile_expert = jnp.sum((tile_end[None, :] <= tile_ids[:, None]).astype(jnp.int32), axis=1)
    tile_expert = jnp.minimum(tile_expert, N_EXPERTS - 1)
    tok_of_slot = (jnp.arange(n_tiles * tm, dtype=jnp.int32) % m).at[slot].set(
        jnp.arange(n, dtype=jnp.int32) // TOP_K, unique_indices=True)
    return slot, tile_expert, n_used, tok_of_slot


def _token_tail(x, z, sg, os_, lses, wts, tm, perms_t=None):
    mm = _merge(z, sg, os_, lses, wts["wa"], wts["wb"], tm=min(tm, 512), tn=512, perms_t=perms_t)
    h, hn, route_i, route_w = _oproj(mm, x, wts["wo"], wts["norm_ffn"], wts["wr"], wts["br"], tm=min(tm, 256))
    moe_tm = min(tm, MOE_TM)
    slot, tile_expert, n_used, tok_of_slot = _routing_tables(route_i, x.shape[0], moe_tm)
    ys = _moe(hn, wts["wg"], wts["wu"], wts["wd"], tile_expert, n_used, tok_of_slot, moe_tm)
    return _combine(h, route_w, wts["norm_final"], ys, slot, tm=min(tm, 256))


def _kv_rows(z, lead, keep_from, group, permuted):
    zz = z.reshape(lead + (IN_COLS,))[:, keep_from:]
    k = zz[..., COL_K + group * GROUP_COLS: COL_K + (group + 1) * GROUP_COLS]
    v = zz[..., COL_VB + group * GROUP_COLS: COL_VB + (group + 1) * GROUP_COLS]
    kv = jnp.stack([k, v], axis=2).astype(F32)
    dil = DILATED_GROUPS[group][1]
    if permuted and dil > 1:
        assert keep_from % PERM_BLOCK == 0 and kv.shape[1] % PERM_BLOCK == 0
        b, rows = kv.shape[:2]
        kv = kv.reshape(b, rows // PERM_BLOCK, dil, PERM_BLOCK // dil, 2, GROUP_COLS)
        kv = kv.transpose(0, 1, 3, 2, 4, 5).reshape(b, rows, 2, GROUP_COLS)
    return kv.reshape(kv.shape[:3] + (HEADS_PER_GROUP, HEAD_DIM))


def kernel(x_prompt, x_sample, cache_kv_w128, cache_kv_w512, cache_kv_w2048, norm_mix, w_in, ln_v_g, ln_v_b, w_s, b_s, w_a_out, w_b_out, w_o, norm_ffn, w_route_group, b_route_group, w_route_expert, b_route_expert, w_gate_e, w_up_e, w_down_e, norm_final):
    assert norm_mix.shape[0] == 1, "single-layer trunk"
    batch, seq, _ = x_prompt.shape
    bd, n_new, _ = x_sample.shape
    caches = (cache_kv_w128, cache_kv_w512, cache_kv_w2048)

    pad = LANES - N_EXPERT_GROUPS - N_EXPERTS
    wr = jnp.concatenate([w_route_group[0], w_route_expert[0], jnp.zeros((D_MODEL, pad), F32)], axis=1)
    br = jnp.concatenate([b_route_group[0], b_route_expert[0], jnp.zeros((pad,), F32)])[None, :]
    wts = dict(
        wa=w_a_out[0].astype(BF16), wb=w_b_out[0].astype(BF16), wo=w_o[0].astype(BF16),
        wg=w_gate_e[0].astype(BF16), wu=w_up_e[0].astype(BF16), wd=w_down_e[0].astype(BF16),
        wr=wr.astype(BF16), br=br, norm_ffn=norm_ffn[0][None, :], norm_final=norm_final[None, :],
    )
    w_in_b = w_in[0].astype(BF16)
    norm_g = norm_mix[0][None, :]
    ln_g, ln_b = ln_v_g[0][None, :], ln_v_b[0][None, :]

    xp = x_prompt.reshape(batch * seq, D_MODEL)
    perms = jnp.stack([_class_major_perm(dil) for _, dil in DILATED_GROUPS[1:]])
    zp = _inproj(xp, norm_g, w_in_b, tm=1024, perms=perms)
    (sgp,) = _sgu(zp, ln_g, ln_b, w_s[0], b_s[0].T, chunks=4, emit_vn=False)
    os_p, lses_p = zip(*[_attn_group(zp, batch, seq, g) for g in range(N_DIL)])
    y_prompt = _token_tail(xp, zp, sgp, os_p, lses_p, wts, tm=512,
                           perms_t=perms.transpose(0, 2, 1)).reshape(batch, seq, D_MODEL)
    kv_prompt = [_kv_rows(zp, (batch, seq), seq - min(win, seq), g, permuted=True)[None]
                 for g, (win, _) in enumerate(DILATED_GROUPS)]

    ms = bd * n_new
    assert ms == CHUNK and n_new <= CHUNK
    xs = x_sample.reshape(ms, D_MODEL)
    zs = _inproj(xs, norm_g, w_in_b, tm=ms)
    eye = jnp.eye(bd, dtype=F32)
    ws_s = jnp.einsum("ab,gts->gatbs", eye, w_s[0][:, :n_new, :n_new]).reshape(A_GROUPS, ms, ms)
    bst_s = jnp.tile(b_s[0][:, :n_new].T, (bd, 1))
    sgs, vns = _sgu(zs, ln_g, ln_b, ws_s, bst_s, chunks=1, emit_vn=True)
    zs3 = zs.reshape(bd, n_new, IN_COLS)
    qkv = [zs3[..., c:c + B_WIDTH].astype(F32).reshape(bd, n_new, N_DIL * HEADS_PER_GROUP, HEAD_DIM)
           for c in (COL_Q, COL_K, COL_VB)]
    o_s, lse_s = _attn_sample(*qkv, caches)
    os_s = [o_s[g].reshape(ms, GROUP_COLS).astype(BF16) for g in range(N_DIL)]
    lses_s = [lse_s[g, ..., 0].reshape(ms, HEADS_PER_GROUP) for g in range(N_DIL)]
    y_sample = _token_tail(xs, zs, sgs, os_s, lses_s, wts, tm=ms).reshape(bd, n_new, D_MODEL)
    kv_sample = [_kv_rows(zs, (bd, n_new), 0, g, permuted=False)[None] for g in range(N_DIL)]
    chunk_v = vns.reshape(1, bd, n_new, A_WIDTH)

    return (y_prompt, y_sample, *kv_prompt, *kv_sample, chunk_v)
```

```python
import functools

import jax
import jax.numpy as jnp
from jax import lax
from jax.experimental import pallas as pl
from jax.experimental.pallas import tpu as pltpu

F32 = jnp.float32
BF16 = jnp.bfloat16

D_MODEL = 2048
CHUNK = 128
A_GROUPS = 16
A_GROUP_DIM = 128
A_WIDTH = A_GROUPS * A_GROUP_DIM
HEAD_DIM = 128
HEADS_PER_GROUP = 4
GROUP_COLS = HEADS_PER_GROUP * HEAD_DIM
DILATED_GROUPS = ((128, 1), (512, 4), (2048, 16))
N_DIL = len(DILATED_GROUPS)
B_WIDTH = N_DIL * GROUP_COLS
IN_COLS = 2 * A_WIDTH + 3 * B_WIDTH + 2 * D_MODEL
COL_U, COL_V = 0, A_WIDTH
COL_Q = 2 * A_WIDTH
COL_K = COL_Q + B_WIDTH
COL_VB = COL_K + B_WIDTH
COL_GA = COL_VB + B_WIDTH
COL_GB = COL_GA + D_MODEL
N_EXPERT_GROUPS = 4
EXPERTS_PER_GROUP = 8
N_EXPERTS = N_EXPERT_GROUPS * EXPERTS_PER_GROUP
TOP_K = 2
D_EXPERT = 256
RMS_EPS = 1e-6
LN_EPS = 1e-5
NEG = -1e30
ATTN_SCALE = HEAD_DIM ** -0.5

LANES = 128
VMEM_LIMIT_BYTES = 56 * 1024 * 1024
IN_TN = 512
MOE_TM = 256
PERM_BLOCK = 256
ROW_TILE = D_MODEL // LANES
DISPATCH_CHUNK = 256


def _cparams(*sem):
    return pltpu.CompilerParams(dimension_semantics=sem, vmem_limit_bytes=VMEM_LIMIT_BYTES)


def _gelu_tanh(x):
    return 0.5 * x * (1.0 + jnp.tanh(0.7978845608028654 * (x + 0.044715 * (x * x * x))))


def _sigmoid(x):
    return 1.0 / (1.0 + jnp.exp(-x))


def _store_token_tiles(ref, val):
    rows = val.shape[0]
    for s in range(ROW_TILE):
        ref[pl.ds(s, rows, stride=ROW_TILE), :] = val[:, s * LANES:(s + 1) * LANES]


def _load_token_tile_chunk(ref, lead, rows, s):
    return ref[lead + (pl.ds(s, rows, stride=ROW_TILE), slice(None))]


def _class_major_perm(dil):
    out_row = jnp.arange(PERM_BLOCK, dtype=jnp.int32)
    src = (out_row % (PERM_BLOCK // dil)) * dil + out_row // (PERM_BLOCK // dil)
    return (src[:, None] == jnp.arange(PERM_BLOCK, dtype=jnp.int32)[None, :]).astype(BF16)


def _inproj_kernel(x_ref, g_ref, w_ref, *rest, permute):
    if permute:
        perm_ref, z_ref, xn_ref = rest
    else:
        z_ref, xn_ref = rest
    j = pl.program_id(1)

    @pl.when(j == 0)
    def _():
        x = x_ref[...]
        ms = jnp.mean(x * x, axis=-1, keepdims=True)
        xn_ref[...] = (x * lax.rsqrt(ms + RMS_EPS) * g_ref[...]).astype(BF16)

    z = jnp.dot(xn_ref[...], w_ref[...], preferred_element_type=F32)
    is_gelu = j < COL_Q // IN_TN
    is_gate = j >= COL_GA // IN_TN
    is_plain = jnp.logical_not(jnp.logical_or(is_gelu, is_gate))
    group = (j - COL_Q // IN_TN) % N_DIL

    @pl.when(is_gelu)
    def _():
        z_ref[...] = _gelu_tanh(z).astype(BF16)

    @pl.when(is_gate)
    def _():
        z_ref[...] = _sigmoid(z).astype(BF16)

    if not permute:
        @pl.when(is_plain)
        def _():
            z_ref[...] = z.astype(BF16)
        return

    @pl.when(jnp.logical_and(is_plain, group == 0))
    def _():
        z_ref[...] = z.astype(BF16)

    for g in range(1, N_DIL):
        @pl.when(jnp.logical_and(is_plain, group == g))
        def _():
            zb = z.astype(BF16)
            for blk in range(z.shape[0] // PERM_BLOCK):
                rows = slice(blk * PERM_BLOCK, (blk + 1) * PERM_BLOCK)
                z_ref[rows, :] = jnp.dot(perm_ref[g - 1], zb[rows, :], preferred_element_type=F32).astype(BF16)


def _inproj(x, norm_g, w_in_bf16, tm, perms=None):
    m = x.shape[0]
    in_specs = [
        pl.BlockSpec((tm, D_MODEL), lambda i, j: (i, 0)),
        pl.BlockSpec((1, D_MODEL), lambda i, j: (0, 0)),
        pl.BlockSpec((D_MODEL, IN_TN), lambda i, j: (0, j)),
    ]
    args = [x, norm_g, w_in_bf16]
    if perms is not None:
        in_specs.append(pl.BlockSpec(perms.shape, lambda i, j: (0, 0, 0)))
        args.append(perms)
    return pl.pallas_call(
        functools.partial(_inproj_kernel, permute=perms is not None),
        out_shape=jax.ShapeDtypeStruct((m, IN_COLS), BF16),
        grid=(m // tm, IN_COLS // IN_TN),
        in_specs=in_specs,
        out_specs=pl.BlockSpec((tm, IN_TN), lambda i, j: (i, j)),
        scratch_shapes=[pltpu.VMEM((tm, D_MODEL), BF16)],
        compiler_params=_cparams("parallel", "arbitrary"),
        name="inproj",
    )(*args)


def _sgu_kernel(u_ref, v_ref, lng_ref, lnb_ref, ws_ref, bst_ref, sg_ref, *vn_out, chunks):
    row = lax.broadcasted_iota(jnp.int32, (CHUNK, CHUNK), 0)
    col = lax.broadcasted_iota(jnp.int32, (CHUNK, CHUNK), 1)
    tri = row >= col
    ws = [jnp.where(tri, ws_ref[g], 0.0).astype(BF16) for g in range(A_GROUPS)]
    for c in range(chunks):
        rows = slice(c * CHUNK, (c + 1) * CHUNK)
        v = v_ref[rows, :].astype(F32)
        mu = jnp.mean(v, axis=-1, keepdims=True)
        vc = v - mu
        var = jnp.mean(vc * vc, axis=-1, keepdims=True)
        vn = vc * lax.rsqrt(var + LN_EPS) * lng_ref[...] + lnb_ref[...]
        if vn_out:
            vn_out[0][rows, :] = vn
        vnb = vn.astype(BF16)
        for g in range(A_GROUPS):
            cols = slice(g * A_GROUP_DIM, (g + 1) * A_GROUP_DIM)
            s = jnp.dot(ws[g], vnb[:, cols], preferred_element_type=F32) + bst_ref[:, g:g + 1]
            sg_ref[rows, cols] = (u_ref[rows, cols].astype(F32) * s).astype(BF16)


def _sgu(z, ln_g, ln_b, ws, bst, chunks, emit_vn):
    m = z.shape[0]
    tm = chunks * CHUNK
    out_shape = [jax.ShapeDtypeStruct((m, A_WIDTH), BF16)]
    out_specs = [pl.BlockSpec((tm, A_WIDTH), lambda i: (i, 0))]
    if emit_vn:
        out_shape.append(jax.ShapeDtypeStruct((m, A_WIDTH), F32))
        out_specs.append(pl.BlockSpec((tm, A_WIDTH), lambda i: (i, 0)))
    return pl.pallas_call(
        functools.partial(_sgu_kernel, chunks=chunks),
        out_shape=out_shape,
        grid=(m // tm,),
        in_specs=[
            pl.BlockSpec((tm, A_WIDTH), lambda i: (i, COL_U // A_WIDTH)),
            pl.BlockSpec((tm, A_WIDTH), lambda i: (i, COL_V // A_WIDTH)),
            pl.BlockSpec((1, A_WIDTH), lambda i: (0, 0)),
            pl.BlockSpec((1, A_WIDTH), lambda i: (0, 0)),
            pl.BlockSpec((A_GROUPS, CHUNK, CHUNK), lambda i: (0, 0, 0)),
            pl.BlockSpec((CHUNK, A_GROUPS), lambda i: (0, 0)),
        ],
        out_specs=out_specs,
        compiler_params=_cparams("parallel"),
        name="sgu",
    )(z, z, ln_g, ln_b, ws, bst)


def _attn_kernel(q_ref, kc_ref, kp_ref, vc_ref, vp_ref, o_ref, lse_ref, *, qblocks):
    i = pl.program_id(2)
    qi = lax.broadcasted_iota(jnp.int32, (CHUNK, CHUNK), 0)
    ki = lax.broadcasted_iota(jnp.int32, (CHUNK, CHUNK), 1)
    cur_mask = ki <= qi
    no_prev = jnp.where(i == 0, CHUNK, 0)
    dn = (((1,), (1,)), ((), ()))
    rpc = q_ref.shape[1]
    nbq = CHUNK // rpc

    def load(ref, a, cols):
        return ref[a * nbq:(a + 1) * nbq, :, cols].reshape(CHUNK, HEAD_DIM)

    for a in range(qblocks):
        rows = slice(a * CHUNK, (a + 1) * CHUNK)
        for h in range(HEADS_PER_GROUP):
            cols = slice(h * HEAD_DIM, (h + 1) * HEAD_DIM)
            q = load(q_ref, a, cols)
            kc = load(kc_ref, a, cols)
            vc = load(vc_ref, a, cols)
            if a == 0:
                kp, vp = load(kp_ref, 0, cols), load(vp_ref, 0, cols)
                prev_mask = ki >= qi + no_prev
            else:
                kp, vp = load(kc_ref, a - 1, cols), load(vc_ref, a - 1, cols)
                prev_mask = ki >= qi
            s_c = lax.dot_general(q, kc, dn, preferred_element_type=F32) * ATTN_SCALE
            s_p = lax.dot_general(q, kp, dn, preferred_element_type=F32) * ATTN_SCALE
            s_c = jnp.where(cur_mask, s_c, NEG)
            s_p = jnp.where(prev_mask, s_p, NEG)
            mx = jnp.maximum(jnp.max(s_c, axis=-1, keepdims=True), jnp.max(s_p, axis=-1, keepdims=True))
            p_c = jnp.exp(s_c - mx)
            p_p = jnp.exp(s_p - mx)
            l = jnp.sum(p_c, axis=-1, keepdims=True) + jnp.sum(p_p, axis=-1, keepdims=True)
            acc = jnp.dot(p_c.astype(BF16), vc, preferred_element_type=F32)
            acc = acc + jnp.dot(p_p.astype(BF16), vp, preferred_element_type=F32)
            o_ref[a * nbq:(a + 1) * nbq, :, cols] = (acc / l).astype(BF16).reshape(nbq, rpc, HEAD_DIM)
            lse_ref[rows, h:h + 1] = mx + jnp.log(l)


def _rows_per_class(dil):
    return min(PERM_BLOCK // dil, CHUNK)


def _attn_group(z, batch, seq, group):
    _, dil = DILATED_GROUPS[group]
    sub = seq // dil
    rpc = _rows_per_class(dil)
    qblocks = 2 if sub % (2 * CHUNK) == 0 else 1
    tq = qblocks * CHUNK
    cq = COL_Q // GROUP_COLS + group
    ck = COL_K // GROUP_COLS + group
    cv = COL_VB // GROUP_COLS + group
    zv = z.reshape(batch, sub // rpc, dil, rpc, IN_COLS)

    def cur(cb):
        return pl.BlockSpec((None, tq // rpc, None, rpc, GROUP_COLS), lambda b, r, i: (b, i, r, 0, cb))

    def prev(cb):
        return pl.BlockSpec((None, CHUNK // rpc, None, rpc, GROUP_COLS),
                            lambda b, r, i: (b, jnp.maximum(i * qblocks - 1, 0), r, 0, cb))

    o, lse = pl.pallas_call(
        functools.partial(_attn_kernel, qblocks=qblocks),
        out_shape=[jax.ShapeDtypeStruct((batch, sub // rpc, dil, rpc, GROUP_COLS), BF16),
                   jax.ShapeDtypeStruct((batch, dil, sub, HEADS_PER_GROUP), F32)],
        grid=(batch, dil, sub // tq),
        in_specs=[cur(cq), cur(ck), prev(ck), cur(cv), prev(cv)],
        out_specs=[pl.BlockSpec((None, tq // rpc, None, rpc, GROUP_COLS), lambda b, r, i: (b, i, r, 0, 0)),
                   pl.BlockSpec((None, None, tq, HEADS_PER_GROUP), lambda b, r, i: (b, r, i, 0))],
        compiler_params=_cparams("parallel", "parallel", "arbitrary"),
        name=f"attn_prompt_g{group}",
    )(zv, zv, zv, zv, zv)
    o = o.reshape(batch * seq, GROUP_COLS)
    lse = lse.transpose(0, 2, 1, 3).reshape(batch * seq, HEADS_PER_GROUP)
    return o, lse


def _attn_sample_kernel(q_ref, k_ref, v_ref, c0_ref, c1_ref, c2_ref, o_ref, lse_ref, *, n_new):
    caches = (c0_ref, c1_ref, c2_ref)
    rowid = lax.broadcasted_iota(jnp.int32, (CHUNK, HEADS_PER_GROUP, 1), 0)
    for g, (win, dil) in enumerate(DILATED_GROUPS):
        cref = caches[g]
        hs = slice(g * HEADS_PER_GROUP, (g + 1) * HEADS_PER_GROUP)
        for t in range(n_new):
            res, first = t % dil, t // dil
            kc = cref[:, res, 0]
            vc = cref[:, res, 1]
            q = q_ref[t, hs, :]
            s = jnp.sum(kc * q[None], axis=-1, keepdims=True) * ATTN_SCALE
            if first > 0:
                s = jnp.where(rowid >= first, s, NEG)
            new_rows = [tn for tn in range(t + 1) if (t - tn) % dil == 0 and (t - tn) // dil <= win // dil]
            s_new = [jnp.sum(q * k_ref[tn, hs, :], axis=-1, keepdims=True) * ATTN_SCALE for tn in new_rows]
            mx = jnp.max(s, axis=0)
            for sn in s_new:
                mx = jnp.maximum(mx, sn)
            p = jnp.exp(s - mx[None])
            l = jnp.sum(p, axis=0)
            acc = jnp.sum(p * vc, axis=0)
            for tn, sn in zip(new_rows, s_new):
                pn = jnp.exp(sn - mx)
                l = l + pn
                acc = acc + pn * v_ref[tn, hs, :]
            o_ref[g, t] = acc / l
            lse_ref[g, t] = jnp.broadcast_to(mx + jnp.log(l), (HEADS_PER_GROUP, HEAD_DIM))


def _attn_sample(q, k, v, caches):
    bd, n_new = q.shape[:2]
    views, specs = [], []
    for (win, dil), c in zip(DILATED_GROUPS, caches):
        assert c.shape[2] == win and win == CHUNK * dil, "cache must hold exactly one full window"
        used = min(dil, n_new)
        views.append(c.reshape(1, bd, CHUNK, dil, 2, HEADS_PER_GROUP, HEAD_DIM))
        specs.append(pl.BlockSpec((None, None, CHUNK, used, 2, HEADS_PER_GROUP, HEAD_DIM),
                                  lambda b: (0, b, 0, 0, 0, 0, 0)))
    new_spec = pl.BlockSpec((None, n_new, N_DIL * HEADS_PER_GROUP, HEAD_DIM), lambda b: (b, 0, 0, 0))
    out_sds = jax.ShapeDtypeStruct((N_DIL, bd, n_new, HEADS_PER_GROUP, HEAD_DIM), F32)
    out_spec = pl.BlockSpec((N_DIL, None, n_new, HEADS_PER_GROUP, HEAD_DIM), lambda b: (0, b, 0, 0, 0))
    return pl.pallas_call(
        functools.partial(_attn_sample_kernel, n_new=n_new),
        out_shape=[out_sds, out_sds],
        grid=(bd,),
        in_specs=[new_spec, new_spec, new_spec] + specs,
        out_specs=[out_spec, out_spec],
        compiler_params=_cparams("parallel"),
        name="attn_sample",
    )(q, k, v, *views)


def _merge_kernel(sg_ref, o0_ref, o1_ref, o2_ref, l0_ref, l1_ref, l2_ref, ga_ref, gb_ref,
                  wa_ref, wb_ref, *rest, permuted):
    if permuted:
        pt_ref, m_ref, ob_ref = rest
    else:
        m_ref, ob_ref = rest
    j = pl.program_id(1)

    @pl.when(j == 0)
    def _():
        l0, l1, l2 = l0_ref[...], l1_ref[...], l2_ref[...]
        mx = jnp.maximum(jnp.maximum(l0, l1), l2)
        e0, e1, e2 = jnp.exp(l0 - mx), jnp.exp(l1 - mx), jnp.exp(l2 - mx)
        inv = 1.0 / (e0 + e1 + e2)
        ws = (e0 * inv, e1 * inv, e2 * inv)
        o_refs = (o0_ref, o1_ref, o2_ref)
        tm = ob_ref.shape[0]
        for blk in range(max(tm // PERM_BLOCK, 1)):
            rows = slice(blk * PERM_BLOCK, min((blk + 1) * PERM_BLOCK, tm))
            og = []
            for g in range(N_DIL):
                if permuted and g > 0:
                    og.append(jnp.dot(pt_ref[g - 1], o_refs[g][rows, :], preferred_element_type=F32))
                else:
                    og.append(o_refs[g][rows, :].astype(F32))
            for h in range(HEADS_PER_GROUP):
                cols = slice(h * HEAD_DIM, (h + 1) * HEAD_DIM)
                ob = sum(ws[g][rows, h:h + 1] * og[g][:, cols] for g in range(N_DIL))
                ob_ref[rows, cols] = ob.astype(BF16)

    ya = jnp.dot(sg_ref[...], wa_ref[...], preferred_element_type=F32)
    yb = jnp.dot(ob_ref[...], wb_ref[...], preferred_element_type=F32)
    m_ref[...] = (ga_ref[...].astype(F32) * ya + gb_ref[...].astype(F32) * yb).astype(BF16)


def _merge(z, sg, os_, lses, wa, wb, tm, tn, perms_t=None):
    m = z.shape[0]
    row = lambda width: pl.BlockSpec((tm, width), lambda i, j: (i, 0))
    in_specs = [row(A_WIDTH)] + [row(GROUP_COLS)] * 3 + [row(HEADS_PER_GROUP)] * 3 + [
        pl.BlockSpec((tm, tn), lambda i, j: (i, COL_GA // tn + j)),
        pl.BlockSpec((tm, tn), lambda i, j: (i, COL_GB // tn + j)),
        pl.BlockSpec((A_WIDTH, tn), lambda i, j: (0, j)),
        pl.BlockSpec((GROUP_COLS, tn), lambda i, j: (0, j)),
    ]
    args = [sg, *os_, *lses, z, z, wa, wb]
    if perms_t is not None:
        assert tm % PERM_BLOCK == 0
        in_specs.append(pl.BlockSpec(perms_t.shape, lambda i, j: (0, 0, 0)))
        args.append(perms_t)
    return pl.pallas_call(
        functools.partial(_merge_kernel, permuted=perms_t is not None),
        out_shape=jax.ShapeDtypeStruct((m, D_MODEL), BF16),
        grid=(m // tm, D_MODEL // tn),
        in_specs=in_specs,
        out_specs=pl.BlockSpec((tm, tn), lambda i, j: (i, j)),
        scratch_shapes=[pltpu.VMEM((tm, GROUP_COLS), BF16)],
        compiler_params=_cparams("parallel", "arbitrary"),
        name="branch_merge",
    )(*args)


def _oproj_kernel(m_ref, x_ref, wo_ref, g_ref, wr_ref, br_ref, h_ref, hn_ref, ri_ref, rw_ref):
    h = x_ref[...] + jnp.dot(m_ref[...], wo_ref[...], preferred_element_type=F32)
    h_ref[...] = h
    ms = jnp.mean(h * h, axis=-1, keepdims=True)
    hn = h * lax.rsqrt(ms + RMS_EPS) * g_ref[...]
    _store_token_tiles(hn_ref, hn)
    logits = jnp.dot(hn.astype(BF16), wr_ref[...], preferred_element_type=F32) + br_ref[...]
    lane = lax.broadcasted_iota(jnp.int32, logits.shape, 1).astype(F32)
    big = float(LANES)

    def first_argmax(vals, vmax):
        return jnp.min(jnp.where(vals == vmax, lane, big), axis=-1, keepdims=True)

    lg = jnp.where(lane < N_EXPERT_GROUPS, logits, NEG)
    gmax = jnp.max(lg, axis=-1, keepdims=True)
    gsel = first_argmax(lg, gmax)
    p_sel = 1.0 / jnp.sum(jnp.exp(lg - gmax), axis=-1, keepdims=True)
    lo = N_EXPERT_GROUPS + EXPERTS_PER_GROUP * gsel
    le = jnp.where(jnp.logical_and(lane >= lo, lane < lo + EXPERTS_PER_GROUP), logits, NEG)
    v1 = jnp.max(le, axis=-1, keepdims=True)
    i1 = first_argmax(le, v1)
    le2 = jnp.where(lane == i1, NEG, le)
    v2 = jnp.max(le2, axis=-1, keepdims=True)
    i2 = first_argmax(le2, v2)
    e2 = jnp.exp(v2 - v1)
    w1 = p_sel / (1.0 + e2)
    w2 = p_sel * e2 / (1.0 + e2)
    ri = jnp.where(lane == 0, i1 - N_EXPERT_GROUPS, jnp.where(lane == 1, i2 - N_EXPERT_GROUPS, 0.0))
    ri_ref[...] = ri.astype(jnp.int32)
    rw_ref[...] = jnp.where(lane == 0, w1, jnp.where(lane == 1, w2, 0.0))


def _oproj(mm, x, wo, norm_g, wr, br, tm):
    m = x.shape[0]
    row = lambda width: pl.BlockSpec((tm, width), lambda i: (i, 0))
    full = lambda a, b: pl.BlockSpec((a, b), lambda i: (0, 0))
    return pl.pallas_call(
        _oproj_kernel,
        out_shape=[jax.ShapeDtypeStruct((m, D_MODEL), F32), jax.ShapeDtypeStruct((m * ROW_TILE, LANES), F32),
                   jax.ShapeDtypeStruct((m, LANES), jnp.int32), jax.ShapeDtypeStruct((m, LANES), F32)],
        grid=(m // tm,),
        in_specs=[row(D_MODEL), row(D_MODEL), full(D_MODEL, D_MODEL), full(1, D_MODEL),
                  full(D_MODEL, LANES), full(1, LANES)],
        out_specs=[row(D_MODEL), pl.BlockSpec((tm * ROW_TILE, LANES), lambda i: (i, 0)), row(LANES), row(LANES)],
        compiler_params=_cparams("parallel"),
        name="oproj_router",
    )(mm, x, wo, norm_g, wr, br)


def _dispatch_kernel(slot_ref, pad_start_ref, pad_len_ref, hn_hbm, xs_hbm, zbuf, sem, zsem, *, n, chunk, tm, n_tiles):
    zbuf[...] = jnp.zeros_like(zbuf)

    def zero_copy(dst_row, rows):
        return pltpu.make_async_copy(zbuf.at[pl.ds(0, rows * ROW_TILE)],
                                     xs_hbm.at[pl.ds(dst_row * ROW_TILE, rows * ROW_TILE)], zsem)

    def zero_fill(act):
        def per_expert(e, carry):
            off, length = pad_start_ref[e], pad_len_ref[e]
            rows = tm // 2
            while rows >= 1:
                @pl.when((length & rows) != 0)
                def _(off=off, rows=rows):
                    act(zero_copy(off, rows))
                off = off + (length & rows)
                rows //= 2
            return carry

        lax.fori_loop(0, N_EXPERTS, per_expert, 0)

        def per_tile(i, carry):
            act(zero_copy(i * tm, tm))
            return carry

        lax.fori_loop(pad_start_ref[N_EXPERTS], n_tiles, per_tile, 0)

    zero_fill(lambda c: c.start())

    def row_copy(src_row, dst_row):
        return pltpu.make_async_copy(hn_hbm.at[pl.ds(src_row * ROW_TILE, ROW_TILE)],
                                     xs_hbm.at[pl.ds(dst_row * ROW_TILE, ROW_TILE)], sem)

    def wait_chunk():
        span = xs_hbm.at[pl.ds(0, chunk * ROW_TILE)]
        pltpu.make_async_copy(span, span, sem).wait()

    def per_chunk(c, carry):
        def issue(u, carry2):
            a = c * chunk + u
            row_copy(lax.div(a, jnp.int32(TOP_K)), slot_ref[a]).start()
            return carry2

        lax.fori_loop(0, chunk, issue, 0, unroll=8)

        @pl.when(c > 0)
        def _():
            wait_chunk()

        return carry

    lax.fori_loop(0, n // chunk, per_chunk, 0)
    wait_chunk()
    zero_fill(lambda c: c.wait())


def _dispatch(hn_tiles, slot, pad_start, pad_len, n_tiles, tm):
    n = slot.shape[0]
    chunk = min(DISPATCH_CHUNK, n)
    assert n % chunk == 0 and chunk <= n_tiles * tm and tm & (tm - 1) == 0
    grid_spec = pltpu.PrefetchScalarGridSpec(
        num_scalar_prefetch=3,
        grid=(1,),
        in_specs=[pl.BlockSpec(memory_space=pl.ANY)],
        out_specs=pl.BlockSpec(memory_space=pl.ANY),
        scratch_shapes=[pltpu.VMEM((tm * ROW_TILE, LANES), F32),
                        pltpu.SemaphoreType.DMA(()), pltpu.SemaphoreType.DMA(())],
    )
    return pl.pallas_call(
        functools.partial(_dispatch_kernel, n=n, chunk=chunk, tm=tm, n_tiles=n_tiles),
        out_shape=jax.ShapeDtypeStruct((n_tiles * tm * ROW_TILE, LANES), F32),
        grid_spec=grid_spec,
        compiler_params=_cparams("arbitrary"),
        name="moe_dispatch",
    )(slot, pad_start, pad_len, hn_tiles)


def _moe_kernel(texp_ref, valid_ref, xs_ref, wg_ref, wu_ref, wd_ref, ys_ref):
    i = pl.program_id(0)
    tm = xs_ref.shape[0] // ROW_TILE
    valid = valid_ref[i]

    @pl.when(valid > 0)
    def _():
        x = jnp.concatenate(
            [_load_token_tile_chunk(xs_ref, (), tm, s).astype(BF16) for s in range(ROW_TILE)], axis=1)
        gate = jnp.dot(x, wg_ref[...], preferred_element_type=F32)
        up = jnp.dot(x, wu_ref[...], preferred_element_type=F32)
        hid = (gate * _sigmoid(gate) * up).astype(BF16)
        _store_token_tiles(ys_ref, jnp.dot(hid, wd_ref[...], preferred_element_type=F32))

    @pl.when(valid == 0)
    def _():
        ys_ref[...] = jnp.zeros_like(ys_ref)


def _moe(xs, wg, wu, wd, tile_expert, tile_valid, tm):
    n_tiles = tile_expert.shape[0]
    grid_spec = pltpu.PrefetchScalarGridSpec(
        num_scalar_prefetch=2,
        grid=(n_tiles,),
        in_specs=[
            pl.BlockSpec((tm * ROW_TILE, LANES), lambda i, te, tv: (i, 0)),
            pl.BlockSpec((None, D_MODEL, D_EXPERT), lambda i, te, tv: (te[i], 0, 0)),
            pl.BlockSpec((None, D_MODEL, D_EXPERT), lambda i, te, tv: (te[i], 0, 0)),
            pl.BlockSpec((None, D_EXPERT, D_MODEL), lambda i, te, tv: (te[i], 0, 0)),
        ],
        out_specs=pl.BlockSpec((tm * ROW_TILE, LANES), lambda i, te, tv: (i, 0)),
    )
    return pl.pallas_call(
        _moe_kernel,
        out_shape=jax.ShapeDtypeStruct((n_tiles * tm * ROW_TILE, LANES), F32),
        grid_spec=grid_spec,
        compiler_params=_cparams("arbitrary"),
        name="moe_experts",
    )(tile_expert, tile_valid, xs, wg, wu, wd)


def _combine_kernel(slot_ref, h_ref, rw_ref, g_ref, ys_hbm, y_ref, ybuf, sem, *, tm):
    i = pl.program_id(0)
    n_steps = pl.num_programs(0)

    def issue(tile, buf):
        def body(r, carry):
            for k in range(TOP_K):
                slot = slot_ref[(tile * tm + r) * TOP_K + k]
                pltpu.make_async_copy(ys_hbm.at[pl.ds(slot * ROW_TILE, ROW_TILE)],
                                      ybuf.at[buf, k, pl.ds(r * ROW_TILE, ROW_TILE)], sem.at[buf]).start()
            return carry

        lax.fori_loop(0, tm, body, 0, unroll=8)

    @pl.when(i == 0)
    def _():
        issue(0, 0)

    @pl.when(i + 1 < n_steps)
    def _():
        issue(i + 1, (i + 1) % 2)

    cur = i % 2
    for k in range(TOP_K):
        pltpu.make_async_copy(ybuf.at[cur, k], ybuf.at[cur, k], sem.at[cur]).wait()
    w0, w1 = rw_ref[:, 0:1], rw_ref[:, 1:2]
    chunks, ssq = [], 0.0
    for s in range(ROW_TILE):
        cols = slice(s * LANES, (s + 1) * LANES)
        y = (h_ref[:, cols] + w0 * _load_token_tile_chunk(ybuf, (cur, 0), tm, s)
             + w1 * _load_token_tile_chunk(ybuf, (cur, 1), tm, s))
        chunks.append(y)
        ssq = ssq + jnp.sum(y * y, axis=-1, keepdims=True)
    scale = lax.rsqrt(ssq * (1.0 / D_MODEL) + RMS_EPS)
    for s in range(ROW_TILE):
        cols = slice(s * LANES, (s + 1) * LANES)
        y_ref[:, cols] = chunks[s] * scale * g_ref[:, cols]


def _combine(h, route_w, norm_g, ys, slot, tm):
    m = h.shape[0]
    grid_spec = pltpu.PrefetchScalarGridSpec(
        num_scalar_prefetch=1,
        grid=(m // tm,),
        in_specs=[
            pl.BlockSpec((tm, D_MODEL), lambda i, s: (i, 0)),
            pl.BlockSpec((tm, LANES), lambda i, s: (i, 0)),
            pl.BlockSpec((1, D_MODEL), lambda i, s: (0, 0)),
            pl.BlockSpec(memory_space=pl.ANY),
        ],
        out_specs=pl.BlockSpec((tm, D_MODEL), lambda i, s: (i, 0)),
        scratch_shapes=[pltpu.VMEM((2, TOP_K, tm * ROW_TILE, LANES), F32), pltpu.SemaphoreType.DMA((2,))],
    )
    return pl.pallas_call(
        functools.partial(_combine_kernel, tm=tm),
        out_shape=jax.ShapeDtypeStruct((m, D_MODEL), F32),
        grid_spec=grid_spec,
        compiler_params=_cparams("arbitrary"),
        name="moe_combine",
    )(slot, h, route_w, norm_g, ys)


def _routing_tables(route_i, m, tm):
    n = m * TOP_K
    n_tiles = pl.cdiv(n, tm) + N_EXPERTS - 1
    e_flat = route_i[:, :TOP_K].reshape(n)
    onehot = (e_flat[:, None] == jnp.arange(N_EXPERTS, dtype=jnp.int32)[None, :]).astype(jnp.int32)
    csum = jnp.cumsum(onehot, axis=0)
    rank = jnp.take_along_axis(csum, e_flat[:, None], axis=1)[:, 0] - 1
    counts = csum[-1]
    tiles_e = (counts + tm - 1) // tm
    tile_end = jnp.cumsum(tiles_e)
    tile_start = tile_end - tiles_e
    slot = (tile_start[e_flat] * tm + rank).astype(jnp.int32)
    tile_ids = jnp.arange(n_tiles, dtype=jnp.int32)
    tile_expert = jnp.sum((tile_end[None, :] <= tile_ids[:, None]).astype(jnp.int32), axis=1)
    tile_expert = jnp.minimum(tile_expert, N_EXPERTS - 1)
    tile_valid = jnp.clip(counts[tile_expert] - (tile_ids - tile_start[tile_expert]) * tm, 0, tm)
    tile_valid = jnp.where(tile_ids < tile_end[-1], tile_valid, 0).astype(jnp.int32)
    pad_start = jnp.concatenate([tile_start * tm + counts, tile_end[-1:]]).astype(jnp.int32)
    pad_len = (tiles_e * tm - counts).astype(jnp.int32)
    return slot, tile_expert.astype(jnp.int32), tile_valid, pad_start, pad_len


def _token_tail(x, z, sg, os_, lses, wts, tm, perms_t=None):
    mm = _merge(z, sg, os_, lses, wts["wa"], wts["wb"], tm=min(tm, 512), tn=512, perms_t=perms_t)
    h, hn, route_i, route_w = _oproj(mm, x, wts["wo"], wts["norm_ffn"], wts["wr"], wts["br"], tm=min(tm, 256))
    moe_tm = min(tm, MOE_TM)
    slot, tile_expert, tile_valid, pad_start, pad_len = _routing_tables(route_i, x.shape[0], moe_tm)
    xs = _dispatch(hn, slot, pad_start, pad_len, tile_expert.shape[0], moe_tm)
    ys = _moe(xs, wts["wg"], wts["wu"], wts["wd"], tile_expert, tile_valid, moe_tm)
    return _combine(h, route_w, wts["norm_final"], ys, slot, tm=min(tm, 256))


def _kv_rows(z, lead, keep_from, group, permuted):
    zz = z.reshape(lead + (IN_COLS,))[:, keep_from:]
    k = zz[..., COL_K + group * GROUP_COLS: COL_K + (group + 1) * GROUP_COLS]
    v = zz[..., COL_VB + group * GROUP_COLS: COL_VB + (group + 1) * GROUP_COLS]
    kv = jnp.stack([k, v], axis=2).astype(F32)
    dil = DILATED_GROUPS[group][1]
    if permuted and dil > 1:
        assert keep_from % PERM_BLOCK == 0 and kv.shape[1] % PERM_BLOCK == 0
        b, rows = kv.shape[:2]
        kv = kv.reshape(b, rows // PERM_BLOCK, dil, PERM_BLOCK // dil, 2, GROUP_COLS)
        kv = kv.transpose(0, 1, 3, 2, 4, 5).reshape(b, rows, 2, GROUP_COLS)
    return kv.reshape(kv.shape[:3] + (HEADS_PER_GROUP, HEAD_DIM))


def kernel(x_prompt, x_sample, cache_kv_w128, cache_kv_w512, cache_kv_w2048, norm_mix, w_in, ln_v_g, ln_v_b, w_s, b_s, w_a_out, w_b_out, w_o, norm_ffn, w_route_group, b_route_group, w_route_expert, b_route_expert, w_gate_e, w_up_e, w_down_e, norm_final):
    assert norm_mix.shape[0] == 1, "single-layer trunk"
    batch, seq, _ = x_prompt.shape
    bd, n_new, _ = x_sample.shape
    caches = (cache_kv_w128, cache_kv_w512, cache_kv_w2048)

    pad = LANES - N_EXPERT_GROUPS - N_EXPERTS
    wr = jnp.concatenate([w_route_group[0], w_route_expert[0], jnp.zeros((D_MODEL, pad), F32)], axis=1)
    br = jnp.concatenate([b_route_group[0], b_route_expert[0], jnp.zeros((pad,), F32)])[None, :]
    wts = dict(
        wa=w_a_out[0].astype(BF16), wb=w_b_out[0].astype(BF16), wo=w_o[0].astype(BF16),
        wg=w_gate_e[0].astype(BF16), wu=w_up_e[0].astype(BF16), wd=w_down_e[0].astype(BF16),
        wr=wr.astype(BF16), br=br, norm_ffn=norm_ffn[0][None, :], norm_final=norm_final[None, :],
    )
    w_in_b = w_in[0].astype(BF16)
    norm_g = norm_mix[0][None, :]
    ln_g, ln_b = ln_v_g[0][None, :], ln_v_b[0][None, :]

    xp = x_prompt.reshape(batch * seq, D_MODEL)
    perms = jnp.stack([_class_major_perm(dil) for _, dil in DILATED_GROUPS[1:]])
    zp = _inproj(xp, norm_g, w_in_b, tm=1024, perms=perms)
    (sgp,) = _sgu(zp, ln_g, ln_b, w_s[0], b_s[0].T, chunks=4, emit_vn=False)
    os_p, lses_p = zip(*[_attn_group(zp, batch, seq, g) for g in range(N_DIL)])
    y_prompt = _token_tail(xp, zp, sgp, os_p, lses_p, wts, tm=512,
                           perms_t=perms.transpose(0, 2, 1)).reshape(batch, seq, D_MODEL)
    kv_prompt = [_kv_rows(zp, (batch, seq), seq - min(win, seq), g, permuted=True)[None]
                 for g, (win, _) in enumerate(DILATED_GROUPS)]

    ms = bd * n_new
    assert ms == CHUNK and n_new <= CHUNK
    xs = x_sample.reshape(ms, D_MODEL)
    zs = _inproj(xs, norm_g, w_in_b, tm=ms)
    eye = jnp.eye(bd, dtype=F32)
    ws_s = jnp.einsum("ab,gts->gatbs", eye, w_s[0][:, :n_new, :n_new]).reshape(A_GROUPS, ms, ms)
    bst_s = jnp.tile(b_s[0][:, :n_new].T, (bd, 1))
    sgs, vns = _sgu(zs, ln_g, ln_b, ws_s, bst_s, chunks=1, emit_vn=True)
    zs3 = zs.reshape(bd, n_new, IN_COLS)
    qkv = [zs3[..., c:c + B_WIDTH].astype(F32).reshape(bd, n_new, N_DIL * HEADS_PER_GROUP, HEAD_DIM)
           for c in (COL_Q, COL_K, COL_VB)]
    o_s, lse_s = _attn_sample(*qkv, caches)
    os_s = [o_s[g].reshape(ms, GROUP_COLS).astype(BF16) for g in range(N_DIL)]
    lses_s = [lse_s[g, ..., 0].reshape(ms, HEADS_PER_GROUP) for g in range(N_DIL)]
    y_sample = _token_tail(xs, zs, sgs, os_s, lses_s, wts, tm=ms).reshape(bd, n_new, D_MODEL)
    kv_sample = [_kv_rows(zs, (bd, n_new), 0, g, permuted=False)[None] for g in range(N_DIL)]
    chunk_v = vns.reshape(1, bd, n_new, A_WIDTH)

    return (y_prompt, y_sample, *kv_prompt, *kv_sample, chunk_v)
```

```python
import functools

import jax
import jax.numpy as jnp
from jax import lax
from jax.experimental import pallas as pl
from jax.experimental.pallas import tpu as pltpu

F32 = jnp.float32
BF16 = jnp.bfloat16

D_MODEL = 2048
CHUNK = 128
A_GROUPS = 16
A_GROUP_DIM = 128
A_WIDTH = A_GROUPS * A_GROUP_DIM
HEAD_DIM = 128
HEADS_PER_GROUP = 4
GROUP_COLS = HEADS_PER_GROUP * HEAD_DIM
DILATED_GROUPS = ((128, 1), (512, 4), (2048, 16))
N_DIL = len(DILATED_GROUPS)
B_WIDTH = N_DIL * GROUP_COLS
IN_COLS = 2 * A_WIDTH + 3 * B_WIDTH + 2 * D_MODEL
COL_U, COL_V = 0, A_WIDTH
COL_Q = 2 * A_WIDTH
COL_K = COL_Q + B_WIDTH
COL_VB = COL_K + B_WIDTH
COL_GA = COL_VB + B_WIDTH
COL_GB = COL_GA + D_MODEL
N_EXPERT_GROUPS = 4
EXPERTS_PER_GROUP = 8
N_EXPERTS = N_EXPERT_GROUPS * EXPERTS_PER_GROUP
TOP_K = 2
D_EXPERT = 256
RMS_EPS = 1e-6
LN_EPS = 1e-5
NEG = -1e30
ATTN_SCALE = HEAD_DIM ** -0.5

LANES = 128
VMEM_LIMIT_BYTES = 56 * 1024 * 1024
IN_TN = 512
MOE_TM = 256
PERM_BLOCK = 256
ROW_TILE = D_MODEL // LANES

def _cparams(*sem):
    return pltpu.CompilerParams(dimension_semantics=sem, vmem_limit_bytes=VMEM_LIMIT_BYTES)


def _gelu_tanh(x):
    return 0.5 * x * (1.0 + jnp.tanh(0.7978845608028654 * (x + 0.044715 * (x * x * x))))


def _sigmoid(x):
    return 1.0 / (1.0 + jnp.exp(-x))


def _store_token_tiles(ref, val):
    rows = val.shape[0]
    for s in range(ROW_TILE):
        ref[pl.ds(s, rows, stride=ROW_TILE), :] = val[:, s * LANES:(s + 1) * LANES]


def _load_token_tile_chunk(ref, lead, rows, s):
    return ref[lead + (pl.ds(s, rows, stride=ROW_TILE), slice(None))]


def _class_major_perm(dil):
    out_row = jnp.arange(PERM_BLOCK, dtype=jnp.int32)
    src = (out_row % (PERM_BLOCK // dil)) * dil + out_row // (PERM_BLOCK // dil)
    return (src[:, None] == jnp.arange(PERM_BLOCK, dtype=jnp.int32)[None, :]).astype(BF16)


def _inproj_kernel(x_ref, g_ref, w_ref, *rest, permute):
    if permute:
        perm_ref, z_ref, xn_ref = rest
    else:
        z_ref, xn_ref = rest
    j = pl.program_id(1)

    @pl.when(j == 0)
    def _():
        x = x_ref[...]
        ms = jnp.mean(x * x, axis=-1, keepdims=True)
        xn_ref[...] = (x * lax.rsqrt(ms + RMS_EPS) * g_ref[...]).astype(BF16)

    z = jnp.dot(xn_ref[...], w_ref[...], preferred_element_type=F32)
    is_gelu = j < COL_Q // IN_TN
    is_gate = j >= COL_GA // IN_TN
    is_plain = jnp.logical_not(jnp.logical_or(is_gelu, is_gate))
    group = (j - COL_Q // IN_TN) % N_DIL

    @pl.when(is_gelu)
    def _():
        z_ref[...] = _gelu_tanh(z).astype(BF16)

    @pl.when(is_gate)
    def _():
        z_ref[...] = _sigmoid(z).astype(BF16)

    if not permute:
        @pl.when(is_plain)
        def _():
            z_ref[...] = z.astype(BF16)
        return

    @pl.when(jnp.logical_and(is_plain, group == 0))
    def _():
        z_ref[...] = z.astype(BF16)

    for g in range(1, N_DIL):
        @pl.when(jnp.logical_and(is_plain, group == g))
        def _():
            zb = z.astype(BF16)
            for blk in range(z.shape[0] // PERM_BLOCK):
                rows = slice(blk * PERM_BLOCK, (blk + 1) * PERM_BLOCK)
                z_ref[rows, :] = jnp.dot(perm_ref[g - 1], zb[rows, :], preferred_element_type=F32).astype(BF16)


def _inproj(x, norm_g, w_in_bf16, tm, perms=None):
    m = x.shape[0]
    in_specs = [
        pl.BlockSpec((tm, D_MODEL), lambda i, j: (i, 0)),
        pl.BlockSpec((1, D_MODEL), lambda i, j: (0, 0)),
        pl.BlockSpec((D_MODEL, IN_TN), lambda i, j: (0, j)),
    ]
    args = [x, norm_g, w_in_bf16]
    if perms is not None:
        in_specs.append(pl.BlockSpec(perms.shape, lambda i, j: (0, 0, 0)))
        args.append(perms)
    return pl.pallas_call(
        functools.partial(_inproj_kernel, permute=perms is not None),
        out_shape=jax.ShapeDtypeStruct((m, IN_COLS), BF16),
        grid=(m // tm, IN_COLS // IN_TN),
        in_specs=in_specs,
        out_specs=pl.BlockSpec((tm, IN_TN), lambda i, j: (i, j)),
        scratch_shapes=[pltpu.VMEM((tm, D_MODEL), BF16)],
        compiler_params=_cparams("parallel", "arbitrary"),
        name="inproj",
    )(*args)


def _sgu_kernel(u_ref, v_ref, lng_ref, lnb_ref, ws_ref, bst_ref, sg_ref, *vn_out, chunks):
    row = lax.broadcasted_iota(jnp.int32, (CHUNK, CHUNK), 0)
    col = lax.broadcasted_iota(jnp.int32, (CHUNK, CHUNK), 1)
    tri = row >= col
    ws = [jnp.where(tri, ws_ref[g], 0.0).astype(BF16) for g in range(A_GROUPS)]
    for c in range(chunks):
        rows = slice(c * CHUNK, (c + 1) * CHUNK)
        v = v_ref[rows, :].astype(F32)
        mu = jnp.mean(v, axis=-1, keepdims=True)
        vc = v - mu
        var = jnp.mean(vc * vc, axis=-1, keepdims=True)
        vn = vc * lax.rsqrt(var + LN_EPS) * lng_ref[...] + lnb_ref[...]
        if vn_out:
            vn_out[0][rows, :] = vn
        vnb = vn.astype(BF16)
        for g in range(A_GROUPS):
            cols = slice(g * A_GROUP_DIM, (g + 1) * A_GROUP_DIM)
            s = jnp.dot(ws[g], vnb[:, cols], preferred_element_type=F32) + bst_ref[:, g:g + 1]
            sg_ref[rows, cols] = (u_ref[rows, cols].astype(F32) * s).astype(BF16)


def _sgu(z, ln_g, ln_b, ws, bst, chunks, emit_vn):
    m = z.shape[0]
    tm = chunks * CHUNK
    out_shape = [jax.ShapeDtypeStruct((m, A_WIDTH), BF16)]
    out_specs = [pl.BlockSpec((tm, A_WIDTH), lambda i: (i, 0))]
    if emit_vn:
        out_shape.append(jax.ShapeDtypeStruct((m, A_WIDTH), F32))
        out_specs.append(pl.BlockSpec((tm, A_WIDTH), lambda i: (i, 0)))
    return pl.pallas_call(
        functools.partial(_sgu_kernel, chunks=chunks),
        out_shape=out_shape,
        grid=(m // tm,),
        in_specs=[
            pl.BlockSpec((tm, A_WIDTH), lambda i: (i, COL_U // A_WIDTH)),
            pl.BlockSpec((tm, A_WIDTH), lambda i: (i, COL_V // A_WIDTH)),
            pl.BlockSpec((1, A_WIDTH), lambda i: (0, 0)),
            pl.BlockSpec((1, A_WIDTH), lambda i: (0, 0)),
            pl.BlockSpec((A_GROUPS, CHUNK, CHUNK), lambda i: (0, 0, 0)),
            pl.BlockSpec((CHUNK, A_GROUPS), lambda i: (0, 0)),
        ],
        out_specs=out_specs,
        compiler_params=_cparams("parallel"),
        name="sgu",
    )(z, z, ln_g, ln_b, ws, bst)


def _attn_kernel(q_ref, kc_ref, kp_ref, vc_ref, vp_ref, o_ref, lse_ref, *, qblocks):
    i = pl.program_id(2)
    qi = lax.broadcasted_iota(jnp.int32, (CHUNK, CHUNK), 0)
    ki = lax.broadcasted_iota(jnp.int32, (CHUNK, CHUNK), 1)
    cur_mask = ki <= qi
    no_prev = jnp.where(i == 0, CHUNK, 0)
    dn = (((1,), (1,)), ((), ()))
    rpc = q_ref.shape[1]
    nbq = CHUNK // rpc

    def load(ref, a, cols):
        return ref[a * nbq:(a + 1) * nbq, :, cols].reshape(CHUNK, HEAD_DIM)

    for a in range(qblocks):
        rows = slice(a * CHUNK, (a + 1) * CHUNK)
        for h in range(HEADS_PER_GROUP):
            cols = slice(h * HEAD_DIM, (h + 1) * HEAD_DIM)
            q = load(q_ref, a, cols)
            kc = load(kc_ref, a, cols)
            vc = load(vc_ref, a, cols)
            if a == 0:
                kp, vp = load(kp_ref, 0, cols), load(vp_ref, 0, cols)
                prev_mask = ki >= qi + no_prev
            else:
                kp, vp = load(kc_ref, a - 1, cols), load(vc_ref, a - 1, cols)
                prev_mask = ki >= qi
            s_c = lax.dot_general(q, kc, dn, preferred_element_type=F32) * ATTN_SCALE
            s_p = lax.dot_general(q, kp, dn, preferred_element_type=F32) * ATTN_SCALE
            s_c = jnp.where(cur_mask, s_c, NEG)
            s_p = jnp.where(prev_mask, s_p, NEG)
            mx = jnp.maximum(jnp.max(s_c, axis=-1, keepdims=True), jnp.max(s_p, axis=-1, keepdims=True))
            p_c = jnp.exp(s_c - mx)
            p_p = jnp.exp(s_p - mx)
            l = jnp.sum(p_c, axis=-1, keepdims=True) + jnp.sum(p_p, axis=-1, keepdims=True)
            acc = jnp.dot(p_c.astype(BF16), vc, preferred_element_type=F32)
            acc = acc + jnp.dot(p_p.astype(BF16), vp, preferred_element_type=F32)
            o_ref[a * nbq:(a + 1) * nbq, :, cols] = (acc / l).astype(BF16).reshape(nbq, rpc, HEAD_DIM)
            lse_ref[rows, h:h + 1] = mx + jnp.log(l)


def _rows_per_class(dil):
    return min(PERM_BLOCK // dil, CHUNK)


def _attn_group(z, batch, seq, group):
    _, dil = DILATED_GROUPS[group]
    sub = seq // dil
    rpc = _rows_per_class(dil)
    qblocks = 2 if sub % (2 * CHUNK) == 0 else 1
    tq = qblocks * CHUNK
    cq = COL_Q // GROUP_COLS + group
    ck = COL_K // GROUP_COLS + group
    cv = COL_VB // GROUP_COLS + group
    zv = z.reshape(batch, sub // rpc, dil, rpc, IN_COLS)

    def cur(cb):
        return pl.BlockSpec((None, tq // rpc, None, rpc, GROUP_COLS), lambda b, r, i: (b, i, r, 0, cb))

    def prev(cb):
        return pl.BlockSpec((None, CHUNK // rpc, None, rpc, GROUP_COLS),
                            lambda b, r, i: (b, jnp.maximum(i * qblocks - 1, 0), r, 0, cb))

    o, lse = pl.pallas_call(
        functools.partial(_attn_kernel, qblocks=qblocks),
        out_shape=[jax.ShapeDtypeStruct((batch, sub // rpc, dil, rpc, GROUP_COLS), BF16),
                   jax.ShapeDtypeStruct((batch, dil, sub, HEADS_PER_GROUP), F32)],
        grid=(batch, dil, sub // tq),
        in_specs=[cur(cq), cur(ck), prev(ck), cur(cv), prev(cv)],
        out_specs=[pl.BlockSpec((None, tq // rpc, None, rpc, GROUP_COLS), lambda b, r, i: (b, i, r, 0, 0)),
                   pl.BlockSpec((None, None, tq, HEADS_PER_GROUP), lambda b, r, i: (b, r, i, 0))],
        compiler_params=_cparams("parallel", "parallel", "arbitrary"),
        name=f"attn_prompt_g{group}",
    )(zv, zv, zv, zv, zv)
    o = o.reshape(batch * seq, GROUP_COLS)
    lse = lse.transpose(0, 2, 1, 3).reshape(batch * seq, HEADS_PER_GROUP)
    return o, lse


def _attn_sample_kernel(q_ref, k_ref, v_ref, c0_ref, c1_ref, c2_ref, o_ref, lse_ref, *, n_new):
    caches = (c0_ref, c1_ref, c2_ref)
    rowid = lax.broadcasted_iota(jnp.int32, (CHUNK, HEADS_PER_GROUP, 1), 0)
    for g, (win, dil) in enumerate(DILATED_GROUPS):
        cref = caches[g]
        hs = slice(g * HEADS_PER_GROUP, (g + 1) * HEADS_PER_GROUP)
        for t in range(n_new):
            res, first = t % dil, t // dil
            kc = cref[:, res, 0]
            vc = cref[:, res, 1]
            q = q_ref[t, hs, :]
            s = jnp.sum(kc * q[None], axis=-1, keepdims=True) * ATTN_SCALE
            if first > 0:
                s = jnp.where(rowid >= first, s, NEG)
            new_rows = [tn for tn in range(t + 1) if (t - tn) % dil == 0 and (t - tn) // dil <= win // dil]
            s_new = [jnp.sum(q * k_ref[tn, hs, :], axis=-1, keepdims=True) * ATTN_SCALE for tn in new_rows]
            mx = jnp.max(s, axis=0)
            for sn in s_new:
                mx = jnp.maximum(mx, sn)
            p = jnp.exp(s - mx[None])
            l = jnp.sum(p, axis=0)
            acc = jnp.sum(p * vc, axis=0)
            for tn, sn in zip(new_rows, s_new):
                pn = jnp.exp(sn - mx)
                l = l + pn
                acc = acc + pn * v_ref[tn, hs, :]
            o_ref[g, t] = acc / l
            lse_ref[g, t] = jnp.broadcast_to(mx + jnp.log(l), (HEADS_PER_GROUP, HEAD_DIM))


def _attn_sample(q, k, v, caches):
    bd, n_new = q.shape[:2]
    views, specs = [], []
    for (win, dil), c in zip(DILATED_GROUPS, caches):
        assert c.shape[2] == win and win == CHUNK * dil, "cache must hold exactly one full window"
        used = min(dil, n_new)
        views.append(c.reshape(1, bd, CHUNK, dil, 2, HEADS_PER_GROUP, HEAD_DIM))
        specs.append(pl.BlockSpec((None, None, CHUNK, used, 2, HEADS_PER_GROUP, HEAD_DIM),
                                  lambda b: (0, b, 0, 0, 0, 0, 0)))
    new_spec = pl.BlockSpec((None, n_new, N_DIL * HEADS_PER_GROUP, HEAD_DIM), lambda b: (b, 0, 0, 0))
    out_sds = jax.ShapeDtypeStruct((N_DIL, bd, n_new, HEADS_PER_GROUP, HEAD_DIM), F32)
    out_spec = pl.BlockSpec((N_DIL, None, n_new, HEADS_PER_GROUP, HEAD_DIM), lambda b: (0, b, 0, 0, 0))
    return pl.pallas_call(
        functools.partial(_attn_sample_kernel, n_new=n_new),
        out_shape=[out_sds, out_sds],
        grid=(bd,),
        in_specs=[new_spec, new_spec, new_spec] + specs,
        out_specs=[out_spec, out_spec],
        compiler_params=_cparams("parallel"),
        name="attn_sample",
    )(q, k, v, *views)


def _merge_kernel(sg_ref, o0_ref, o1_ref, o2_ref, l0_ref, l1_ref, l2_ref, ga_ref, gb_ref,
                  wa_ref, wb_ref, *rest, permuted):
    if permuted:
        pt_ref, m_ref, ob_ref = rest
    else:
        m_ref, ob_ref = rest
    j = pl.program_id(1)

    @pl.when(j == 0)
    def _():
        l0, l1, l2 = l0_ref[...], l1_ref[...], l2_ref[...]
        mx = jnp.maximum(jnp.maximum(l0, l1), l2)
        e0, e1, e2 = jnp.exp(l0 - mx), jnp.exp(l1 - mx), jnp.exp(l2 - mx)
        inv = 1.0 / (e0 + e1 + e2)
        ws = (e0 * inv, e1 * inv, e2 * inv)
        o_refs = (o0_ref, o1_ref, o2_ref)
        tm = ob_ref.shape[0]
        for blk in range(max(tm // PERM_BLOCK, 1)):
            rows = slice(blk * PERM_BLOCK, min((blk + 1) * PERM_BLOCK, tm))
            og = []
            for g in range(N_DIL):
                if permuted and g > 0:
                    og.append(jnp.dot(pt_ref[g - 1], o_refs[g][rows, :], preferred_element_type=F32))
                else:
                    og.append(o_refs[g][rows, :].astype(F32))
            for h in range(HEADS_PER_GROUP):
                cols = slice(h * HEAD_DIM, (h + 1) * HEAD_DIM)
                ob = sum(ws[g][rows, h:h + 1] * og[g][:, cols] for g in range(N_DIL))
                ob_ref[rows, cols] = ob.astype(BF16)

    ya = jnp.dot(sg_ref[...], wa_ref[...], preferred_element_type=F32)
    yb = jnp.dot(ob_ref[...], wb_ref[...], preferred_element_type=F32)
    m_ref[...] = (ga_ref[...].astype(F32) * ya + gb_ref[...].astype(F32) * yb).astype(BF16)


def _merge(z, sg, os_, lses, wa, wb, tm, tn, perms_t=None):
    m = z.shape[0]
    row = lambda width: pl.BlockSpec((tm, width), lambda i, j: (i, 0))
    in_specs = [row(A_WIDTH)] + [row(GROUP_COLS)] * 3 + [row(HEADS_PER_GROUP)] * 3 + [
        pl.BlockSpec((tm, tn), lambda i, j: (i, COL_GA // tn + j)),
        pl.BlockSpec((tm, tn), lambda i, j: (i, COL_GB // tn + j)),
        pl.BlockSpec((A_WIDTH, tn), lambda i, j: (0, j)),
        pl.BlockSpec((GROUP_COLS, tn), lambda i, j: (0, j)),
    ]
    args = [sg, *os_, *lses, z, z, wa, wb]
    if perms_t is not None:
        assert tm % PERM_BLOCK == 0
        in_specs.append(pl.BlockSpec(perms_t.shape, lambda i, j: (0, 0, 0)))
        args.append(perms_t)
    return pl.pallas_call(
        functools.partial(_merge_kernel, permuted=perms_t is not None),
        out_shape=jax.ShapeDtypeStruct((m, D_MODEL), BF16),
        grid=(m // tm, D_MODEL // tn),
        in_specs=in_specs,
        out_specs=pl.BlockSpec((tm, tn), lambda i, j: (i, j)),
        scratch_shapes=[pltpu.VMEM((tm, GROUP_COLS), BF16)],
        compiler_params=_cparams("parallel", "arbitrary"),
        name="branch_merge",
    )(*args)


def _oproj_kernel(m_ref, x_ref, wo_ref, g_ref, wr_ref, br_ref, h_ref, hn_ref, ri_ref, rw_ref):
    h = x_ref[...] + jnp.dot(m_ref[...], wo_ref[...], preferred_element_type=F32)
    h_ref[...] = h
    ms = jnp.mean(h * h, axis=-1, keepdims=True)
    hn = h * lax.rsqrt(ms + RMS_EPS) * g_ref[...]
    _store_token_tiles(hn_ref, hn)
    logits = jnp.dot(hn.astype(BF16), wr_ref[...], preferred_element_type=F32) + br_ref[...]
    lane = lax.broadcasted_iota(jnp.int32, logits.shape, 1).astype(F32)
    big = float(LANES)

    def first_argmax(vals, vmax):
        return jnp.min(jnp.where(vals == vmax, lane, big), axis=-1, keepdims=True)

    lg = jnp.where(lane < N_EXPERT_GROUPS, logits, NEG)
    gmax = jnp.max(lg, axis=-1, keepdims=True)
    gsel = first_argmax(lg, gmax)
    p_sel = 1.0 / jnp.sum(jnp.exp(lg - gmax), axis=-1, keepdims=True)
    lo = N_EXPERT_GROUPS + EXPERTS_PER_GROUP * gsel
    le = jnp.where(jnp.logical_and(lane >= lo, lane < lo + EXPERTS_PER_GROUP), logits, NEG)
    v1 = jnp.max(le, axis=-1, keepdims=True)
    i1 = first_argmax(le, v1)
    le2 = jnp.where(lane == i1, NEG, le)
    v2 = jnp.max(le2, axis=-1, keepdims=True)
    i2 = first_argmax(le2, v2)
    e2 = jnp.exp(v2 - v1)
    w1 = p_sel / (1.0 + e2)
    w2 = p_sel * e2 / (1.0 + e2)
    ri = jnp.where(lane == 0, i1 - N_EXPERT_GROUPS, jnp.where(lane == 1, i2 - N_EXPERT_GROUPS, 0.0))
    ri_ref[...] = ri.astype(jnp.int32)
    rw_ref[...] = jnp.where(lane == 0, w1, jnp.where(lane == 1, w2, 0.0))


def _oproj(mm, x, wo, norm_g, wr, br, tm):
    m = x.shape[0]
    row = lambda width: pl.BlockSpec((tm, width), lambda i: (i, 0))
    full = lambda a, b: pl.BlockSpec((a, b), lambda i: (0, 0))
    return pl.pallas_call(
        _oproj_kernel,
        out_shape=[jax.ShapeDtypeStruct((m, D_MODEL), F32), jax.ShapeDtypeStruct((m * ROW_TILE, LANES), F32),
                   jax.ShapeDtypeStruct((m, LANES), jnp.int32), jax.ShapeDtypeStruct((m, LANES), F32)],
        grid=(m // tm,),
        in_specs=[row(D_MODEL), row(D_MODEL), full(D_MODEL, D_MODEL), full(1, D_MODEL),
                  full(D_MODEL, LANES), full(1, LANES)],
        out_specs=[row(D_MODEL), pl.BlockSpec((tm * ROW_TILE, LANES), lambda i: (i, 0)), row(LANES), row(LANES)],
        compiler_params=_cparams("parallel"),
        name="oproj_router",
    )(mm, x, wo, norm_g, wr, br)


def _dispatch_kernel(slot_ref, pad_start_ref, pad_len_ref, hn_ref, xs_hbm, zbuf, sem, zsem, *, tm, moe_tm, n_tiles):
    i = pl.program_id(0)

    def zero_copy(dst_row, rows):
        return pltpu.make_async_copy(zbuf.at[pl.ds(0, rows * ROW_TILE)],
                                     xs_hbm.at[pl.ds(dst_row * ROW_TILE, rows * ROW_TILE)], zsem)

    def zero_fill(act):
        def per_expert(e, carry):
            off, length = pad_start_ref[e], pad_len_ref[e]
            rows = moe_tm // 2
            while rows >= 1:
                @pl.when((length & rows) != 0)
                def _(off=off, rows=rows):
                    act(zero_copy(off, rows))
                off = off + (length & rows)
                rows //= 2
            return carry

        lax.fori_loop(0, N_EXPERTS, per_expert, 0)

        def per_tile(t, carry):
            act(zero_copy(t * moe_tm, moe_tm))
            return carry

        lax.fori_loop(pad_start_ref[N_EXPERTS], n_tiles, per_tile, 0)

    @pl.when(i == 0)
    def _():
        zbuf[...] = jnp.zeros_like(zbuf)
        zero_fill(lambda c: c.start())

    def issue(r, carry):
        for k in range(TOP_K):
            slot = slot_ref[(i * tm + r) * TOP_K + k]
            pltpu.make_async_copy(hn_ref.at[pl.ds(r * ROW_TILE, ROW_TILE)],
                                  xs_hbm.at[pl.ds(slot * ROW_TILE, ROW_TILE)], sem).start()
        return carry

    lax.fori_loop(0, tm, issue, 0, unroll=8)
    for k in range(TOP_K):
        pltpu.make_async_copy(hn_ref, hn_ref, sem).wait()

    @pl.when(i == pl.num_programs(0) - 1)
    def _():
        zero_fill(lambda c: c.wait())


def _dispatch(hn_tiles, slot, pad_start, pad_len, n_tiles, moe_tm, tm):
    m = hn_tiles.shape[0] // ROW_TILE
    assert m % tm == 0 and moe_tm & (moe_tm - 1) == 0
    grid_spec = pltpu.PrefetchScalarGridSpec(
        num_scalar_prefetch=3,
        grid=(m // tm,),
        in_specs=[pl.BlockSpec((tm * ROW_TILE, LANES), lambda i, s, ps, pn: (i, 0))],
        out_specs=pl.BlockSpec(memory_space=pl.ANY),
        scratch_shapes=[pltpu.VMEM((moe_tm * ROW_TILE, LANES), F32),
                        pltpu.SemaphoreType.DMA(()), pltpu.SemaphoreType.DMA(())],
    )
    return pl.pallas_call(
        functools.partial(_dispatch_kernel, tm=tm, moe_tm=moe_tm, n_tiles=n_tiles),
        out_shape=jax.ShapeDtypeStruct((n_tiles * moe_tm * ROW_TILE, LANES), F32),
        grid_spec=grid_spec,
        compiler_params=_cparams("arbitrary"),
        name="moe_dispatch",
    )(slot, pad_start, pad_len, hn_tiles)


def _moe_kernel(texp_ref, valid_ref, xs_ref, wg_ref, wu_ref, wd_ref, ys_ref):
    i = pl.program_id(0)
    tm = xs_ref.shape[0] // ROW_TILE
    valid = valid_ref[i]

    @pl.when(valid > 0)
    def _():
        x = jnp.concatenate(
            [_load_token_tile_chunk(xs_ref, (), tm, s).astype(BF16) for s in range(ROW_TILE)], axis=1)
        gate = jnp.dot(x, wg_ref[...], preferred_element_type=F32)
        up = jnp.dot(x, wu_ref[...], preferred_element_type=F32)
        hid = (gate * _sigmoid(gate) * up).astype(BF16)
        _store_token_tiles(ys_ref, jnp.dot(hid, wd_ref[...], preferred_element_type=F32))

    @pl.when(valid == 0)
    def _():
        ys_ref[...] = jnp.zeros_like(ys_ref)


def _moe(xs, wg, wu, wd, tile_expert, tile_valid, tm):
    n_tiles = tile_expert.shape[0]
    grid_spec = pltpu.PrefetchScalarGridSpec(
        num_scalar_prefetch=2,
        grid=(n_tiles,),
        in_specs=[
            pl.BlockSpec((tm * ROW_TILE, LANES), lambda i, te, tv: (i, 0)),
            pl.BlockSpec((None, D_MODEL, D_EXPERT), lambda i, te, tv: (te[i], 0, 0)),
            pl.BlockSpec((None, D_MODEL, D_EXPERT), lambda i, te, tv: (te[i], 0, 0)),
            pl.BlockSpec((None, D_EXPERT, D_MODEL), lambda i, te, tv: (te[i], 0, 0)),
        ],
        out_specs=pl.BlockSpec((tm * ROW_TILE, LANES), lambda i, te, tv: (i, 0)),
    )
    return pl.pallas_call(
        _moe_kernel,
        out_shape=jax.ShapeDtypeStruct((n_tiles * tm * ROW_TILE, LANES), F32),
        grid_spec=grid_spec,
        compiler_params=_cparams("arbitrary"),
        name="moe_experts",
    )(tile_expert, tile_valid, xs, wg, wu, wd)


def _combine_kernel(slot_ref, h_ref, rw_ref, g_ref, ys_hbm, y_ref, ybuf, sem, *, tm):
    i = pl.program_id(0)
    n_steps = pl.num_programs(0)

    def issue(tile, buf):
        def body(r, carry):
            for k in range(TOP_K):
                slot = slot_ref[(tile * tm + r) * TOP_K + k]
                pltpu.make_async_copy(ys_hbm.at[pl.ds(slot * ROW_TILE, ROW_TILE)],
                                      ybuf.at[buf, k, pl.ds(r * ROW_TILE, ROW_TILE)], sem.at[buf]).start()
            return carry

        lax.fori_loop(0, tm, body, 0, unroll=8)

    @pl.when(i == 0)
    def _():
        issue(0, 0)

    @pl.when(i + 1 < n_steps)
    def _():
        issue(i + 1, (i + 1) % 2)

    cur = i % 2
    for k in range(TOP_K):
        pltpu.make_async_copy(ybuf.at[cur, k], ybuf.at[cur, k], sem.at[cur]).wait()
    w0, w1 = rw_ref[:, 0:1], rw_ref[:, 1:2]
    chunks, ssq = [], 0.0
    for s in range(ROW_TILE):
        cols = slice(s * LANES, (s + 1) * LANES)
        y = (h_ref[:, cols] + w0 * _load_token_tile_chunk(ybuf, (cur, 0), tm, s)
             + w1 * _load_token_tile_chunk(ybuf, (cur, 1), tm, s))
        chunks.append(y)
        ssq = ssq + jnp.sum(y * y, axis=-1, keepdims=True)
    scale = lax.rsqrt(ssq * (1.0 / D_MODEL) + RMS_EPS)
    for s in range(ROW_TILE):
        cols = slice(s * LANES, (s + 1) * LANES)
        y_ref[:, cols] = chunks[s] * scale * g_ref[:, cols]


def _combine(h, route_w, norm_g, ys, slot, tm):
    m = h.shape[0]
    grid_spec = pltpu.PrefetchScalarGridSpec(
        num_scalar_prefetch=1,
        grid=(m // tm,),
        in_specs=[
            pl.BlockSpec((tm, D_MODEL), lambda i, s: (i, 0)),
            pl.BlockSpec((tm, LANES), lambda i, s: (i, 0)),
            pl.BlockSpec((1, D_MODEL), lambda i, s: (0, 0)),
            pl.BlockSpec(memory_space=pl.ANY),
        ],
        out_specs=pl.BlockSpec((tm, D_MODEL), lambda i, s: (i, 0)),
        scratch_shapes=[pltpu.VMEM((2, TOP_K, tm * ROW_TILE, LANES), F32), pltpu.SemaphoreType.DMA((2,))],
    )
    return pl.pallas_call(
        functools.partial(_combine_kernel, tm=tm),
        out_shape=jax.ShapeDtypeStruct((m, D_MODEL), F32),
        grid_spec=grid_spec,
        compiler_params=_cparams("arbitrary"),
        name="moe_combine",
    )(slot, h, route_w, norm_g, ys)


def _routing_tables(route_i, m, tm):
    n = m * TOP_K
    n_tiles = pl.cdiv(n, tm) + N_EXPERTS - 1
    e_flat = route_i[:, :TOP_K].reshape(n)
    onehot = (e_flat[:, None] == jnp.arange(N_EXPERTS, dtype=jnp.int32)[None, :]).astype(jnp.int32)
    csum = jnp.cumsum(onehot, axis=0)
    rank = jnp.take_along_axis(csum, e_flat[:, None], axis=1)[:, 0] - 1
    counts = csum[-1]
    tiles_e = (counts + tm - 1) // tm
    tile_end = jnp.cumsum(tiles_e)
    tile_start = tile_end - tiles_e
    slot = (tile_start[e_flat] * tm + rank).astype(jnp.int32)
    tile_ids = jnp.arange(n_tiles, dtype=jnp.int32)
    tile_expert = jnp.sum((tile_end[None, :] <= tile_ids[:, None]).astype(jnp.int32), axis=1)
    tile_expert = jnp.minimum(tile_expert, N_EXPERTS - 1)
    tile_valid = jnp.clip(counts[tile_expert] - (tile_ids - tile_start[tile_expert]) * tm, 0, tm)
    tile_valid = jnp.where(tile_ids < tile_end[-1], tile_valid, 0).astype(jnp.int32)
    pad_start = jnp.concatenate([tile_start * tm + counts, tile_end[-1:]]).astype(jnp.int32)
    pad_len = (tiles_e * tm - counts).astype(jnp.int32)
    return slot, tile_expert.astype(jnp.int32), tile_valid, pad_start, pad_len


def _token_tail(x, z, sg, os_, lses, wts, tm, perms_t=None):
    mm = _merge(z, sg, os_, lses, wts["wa"], wts["wb"], tm=min(tm, 512), tn=512, perms_t=perms_t)
    h, hn, route_i, route_w = _oproj(mm, x, wts["wo"], wts["norm_ffn"], wts["wr"], wts["br"], tm=min(tm, 256))
    moe_tm = min(tm, MOE_TM)
    slot, tile_expert, tile_valid, pad_start, pad_len = _routing_tables(route_i, x.shape[0], moe_tm)
    xs = _dispatch(hn, slot, pad_start, pad_len, tile_expert.shape[0], moe_tm, tm=min(tm, 256))
    ys = _moe(xs, wts["wg"], wts["wu"], wts["wd"], tile_expert, tile_valid, moe_tm)
    return _combine(h, route_w, wts["norm_final"], ys, slot, tm=min(tm, 256))


def _kv_rows(z, lead, keep_from, group, permuted):
    zz = z.reshape(lead + (IN_COLS,))[:, keep_from:]
    k = zz[..., COL_K + group * GROUP_COLS: COL_K + (group + 1) * GROUP_COLS]
    v = zz[..., COL_VB + group * GROUP_COLS: COL_VB + (group + 1) * GROUP_COLS]
    kv = jnp.stack([k, v], axis=2).astype(F32)
    dil = DILATED_GROUPS[group][1]
    if permuted and dil > 1:
        assert keep_from % PERM_BLOCK == 0 and kv.shape[1] % PERM_BLOCK == 0
        b, rows = kv.shape[:2]
        kv = kv.reshape(b, rows // PERM_BLOCK, dil, PERM_BLOCK // dil, 2, GROUP_COLS)
        kv = kv.transpose(0, 1, 3, 2, 4, 5).reshape(b, rows, 2, GROUP_COLS)
    return kv.reshape(kv.shape[:3] + (HEADS_PER_GROUP, HEAD_DIM))


def kernel(x_prompt, x_sample, cache_kv_w128, cache_kv_w512, cache_kv_w2048, norm_mix, w_in, ln_v_g, ln_v_b, w_s, b_s, w_a_out, w_b_out, w_o, norm_ffn, w_route_group, b_route_group, w_route_expert, b_route_expert, w_gate_e, w_up_e, w_down_e, norm_final):
    assert norm_mix.shape[0] == 1, "single-layer trunk"
    batch, seq, _ = x_prompt.shape
    bd, n_new, _ = x_sample.shape
    caches = (cache_kv_w128, cache_kv_w512, cache_kv_w2048)

    pad = LANES - N_EXPERT_GROUPS - N_EXPERTS
    wr = jnp.concatenate([w_route_group[0], w_route_expert[0], jnp.zeros((D_MODEL, pad), F32)], axis=1)
    br = jnp.concatenate([b_route_group[0], b_route_expert[0], jnp.zeros((pad,), F32)])[None, :]
    wts = dict(
        wa=w_a_out[0].astype(BF16), wb=w_b_out[0].astype(BF16), wo=w_o[0].astype(BF16),
        wg=w_gate_e[0].astype(BF16), wu=w_up_e[0].astype(BF16), wd=w_down_e[0].astype(BF16),
        wr=wr.astype(BF16), br=br, norm_ffn=norm_ffn[0][None, :], norm_final=norm_final[None, :],
    )
    w_in_b = w_in[0].astype(BF16)
    norm_g = norm_mix[0][None, :]
    ln_g, ln_b = ln_v_g[0][None, :], ln_v_b[0][None, :]

    xp = x_prompt.reshape(batch * seq, D_MODEL)
    perms = jnp.stack([_class_major_perm(dil) for _, dil in DILATED_GROUPS[1:]])
    zp = _inproj(xp, norm_g, w_in_b, tm=1024, perms=perms)
    (sgp,) = _sgu(zp, ln_g, ln_b, w_s[0], b_s[0].T, chunks=4, emit_vn=False)
    os_p, lses_p = zip(*[_attn_group(zp, batch, seq, g) for g in range(N_DIL)])
    y_prompt = _token_tail(xp, zp, sgp, os_p, lses_p, wts, tm=512,
                           perms_t=perms.transpose(0, 2, 1)).reshape(batch, seq, D_MODEL)
    kv_prompt = [_kv_rows(zp, (batch, seq), seq - min(win, seq), g, permuted=True)[None]
                 for g, (win, _) in enumerate(DILATED_GROUPS)]

    ms = bd * n_new
    assert ms == CHUNK and n_new <= CHUNK
    xs = x_sample.reshape(ms, D_MODEL)
    zs = _inproj(xs, norm_g, w_in_b, tm=ms)
    eye = jnp.eye(bd, dtype=F32)
    ws_s = jnp.einsum("ab,gts->gatbs", eye, w_s[0][:, :n_new, :n_new]).reshape(A_GROUPS, ms, ms)
    bst_s = jnp.tile(b_s[0][:, :n_new].T, (bd, 1))
    sgs, vns = _sgu(zs, ln_g, ln_b, ws_s, bst_s, chunks=1, emit_vn=True)
    zs3 = zs.reshape(bd, n_new, IN_COLS)
    qkv = [zs3[..., c:c + B_WIDTH].astype(F32).reshape(bd, n_new, N_DIL * HEADS_PER_GROUP, HEAD_DIM)
           for c in (COL_Q, COL_K, COL_VB)]
    o_s, lse_s = _attn_sample(*qkv, caches)
    os_s = [o_s[g].reshape(ms, GROUP_COLS).astype(BF16) for g in range(N_DIL)]
    lses_s = [lse_s[g, ..., 0].reshape(ms, HEADS_PER_GROUP) for g in range(N_DIL)]
    y_sample = _token_tail(xs, zs, sgs, os_s, lses_s, wts, tm=ms).reshape(bd, n_new, D_MODEL)
    kv_sample = [_kv_rows(zs, (bd, n_new), 0, g, permuted=False)[None] for g in range(N_DIL)]
    chunk_v = vns.reshape(1, bd, n_new, A_WIDTH)

    return (y_prompt, y_sample, *kv_prompt, *kv_sample, chunk_v)
```

```python
import functools

import jax
import jax.numpy as jnp
from jax import lax
from jax.experimental import pallas as pl
from jax.experimental.pallas import tpu as pltpu

F32 = jnp.float32
BF16 = jnp.bfloat16

D_MODEL = 2048
CHUNK = 128
A_GROUPS = 16
A_GROUP_DIM = 128
A_WIDTH = A_GROUPS * A_GROUP_DIM
HEAD_DIM = 128
HEADS_PER_GROUP = 4
GROUP_COLS = HEADS_PER_GROUP * HEAD_DIM
DILATED_GROUPS = ((128, 1), (512, 4), (2048, 16))
N_DIL = len(DILATED_GROUPS)
B_WIDTH = N_DIL * GROUP_COLS
IN_COLS = 2 * A_WIDTH + 3 * B_WIDTH + 2 * D_MODEL
COL_U, COL_V = 0, A_WIDTH
COL_GA = 2 * A_WIDTH
COL_GB = COL_GA + D_MODEL
COL_Q = COL_GB + D_MODEL
COL_K = COL_Q + B_WIDTH
COL_VB = COL_K + B_WIDTH
REF_COL_Q = 2 * A_WIDTH
REF_COL_GA = REF_COL_Q + 3 * B_WIDTH
N_EXPERT_GROUPS = 4
EXPERTS_PER_GROUP = 8
N_EXPERTS = N_EXPERT_GROUPS * EXPERTS_PER_GROUP
TOP_K = 2
D_EXPERT = 256
RMS_EPS = 1e-6
LN_EPS = 1e-5
NEG = -1e30
ATTN_SCALE = HEAD_DIM ** -0.5

LANES = 128
VMEM_LIMIT_BYTES = 56 * 1024 * 1024
IN_TN = 512
MOE_TM = 256
PERM_BLOCK = 256
ROW_TILE = D_MODEL // LANES

def _cparams(*sem):
    return pltpu.CompilerParams(dimension_semantics=sem, vmem_limit_bytes=VMEM_LIMIT_BYTES)


def _gelu_tanh(x):
    return 0.5 * x * (1.0 + jnp.tanh(0.7978845608028654 * (x + 0.044715 * (x * x * x))))


def _sigmoid(x):
    return 1.0 / (1.0 + jnp.exp(-x))


def _store_token_tiles(ref, val):
    rows = val.shape[0]
    for s in range(ROW_TILE):
        ref[pl.ds(s, rows, stride=ROW_TILE), :] = val[:, s * LANES:(s + 1) * LANES]


def _load_token_tile_chunk(ref, lead, rows, s):
    return ref[lead + (pl.ds(s, rows, stride=ROW_TILE), slice(None))]


def _class_major_perm(dil):
    out_row = jnp.arange(PERM_BLOCK, dtype=jnp.int32)
    src = (out_row % (PERM_BLOCK // dil)) * dil + out_row // (PERM_BLOCK // dil)
    return (src[:, None] == jnp.arange(PERM_BLOCK, dtype=jnp.int32)[None, :]).astype(BF16)


def _inproj_kernel(x_ref, g_ref, w_ref, *rest, permute):
    if permute:
        perm_ref, z_ref, xn_ref = rest
    else:
        z_ref, xn_ref = rest
    j = pl.program_id(1)

    @pl.when(j == 0)
    def _():
        x = x_ref[...]
        ms = jnp.mean(x * x, axis=-1, keepdims=True)
        xn_ref[...] = (x * lax.rsqrt(ms + RMS_EPS) * g_ref[...]).astype(BF16)

    is_gelu = j < COL_GA // IN_TN
    is_plain = j >= COL_Q // IN_TN
    is_gate = jnp.logical_not(jnp.logical_or(is_gelu, is_plain))
    group = (j - COL_Q // IN_TN) % N_DIL
    tm = xn_ref.shape[0]
    sub = min(tm, PERM_BLOCK)

    def emit(epilogue):
        for blk in range(tm // sub):
            rows = slice(blk * sub, (blk + 1) * sub)
            z = jnp.dot(xn_ref[rows, :], w_ref[...], preferred_element_type=F32)
            z_ref[rows, :] = epilogue(z).astype(BF16)

    @pl.when(is_gelu)
    def _():
        emit(_gelu_tanh)

    @pl.when(is_gate)
    def _():
        emit(_sigmoid)

    if not permute:
        @pl.when(is_plain)
        def _():
            emit(lambda z: z)
        return

    @pl.when(jnp.logical_and(is_plain, group == 0))
    def _():
        emit(lambda z: z)

    for g in range(1, N_DIL):
        @pl.when(jnp.logical_and(is_plain, group == g))
        def _():
            emit(lambda z: jnp.dot(perm_ref[g - 1], z.astype(BF16), preferred_element_type=F32))


def _inproj(x, norm_g, w_in_bf16, tm, perms=None):
    m = x.shape[0]
    in_specs = [
        pl.BlockSpec((tm, D_MODEL), lambda i, j: (i, 0)),
        pl.BlockSpec((1, D_MODEL), lambda i, j: (0, 0)),
        pl.BlockSpec((D_MODEL, IN_TN), lambda i, j: (0, j)),
    ]
    args = [x, norm_g, w_in_bf16]
    if perms is not None:
        in_specs.append(pl.BlockSpec(perms.shape, lambda i, j: (0, 0, 0)))
        args.append(perms)
    return pl.pallas_call(
        functools.partial(_inproj_kernel, permute=perms is not None),
        out_shape=jax.ShapeDtypeStruct((m, IN_COLS), BF16),
        grid=(m // tm, IN_COLS // IN_TN),
        in_specs=in_specs,
        out_specs=pl.BlockSpec((tm, IN_TN), lambda i, j: (i, j)),
        scratch_shapes=[pltpu.VMEM((tm, D_MODEL), BF16)],
        compiler_params=_cparams("parallel", "arbitrary"),
        name="inproj",
    )(*args)


def _sgu_kernel(u_ref, v_ref, lng_ref, lnb_ref, ws_ref, bst_ref, sg_ref, *vn_out, chunks):
    row = lax.broadcasted_iota(jnp.int32, (CHUNK, CHUNK), 0)
    col = lax.broadcasted_iota(jnp.int32, (CHUNK, CHUNK), 1)
    tri = row >= col
    ws = [jnp.where(tri, ws_ref[g], 0.0).astype(BF16) for g in range(A_GROUPS)]
    for c in range(chunks):
        rows = slice(c * CHUNK, (c + 1) * CHUNK)
        v = v_ref[rows, :].astype(F32)
        mu = jnp.mean(v, axis=-1, keepdims=True)
        vc = v - mu
        var = jnp.mean(vc * vc, axis=-1, keepdims=True)
        vn = vc * lax.rsqrt(var + LN_EPS) * lng_ref[...] + lnb_ref[...]
        if vn_out:
            vn_out[0][rows, :] = vn
        vnb = vn.astype(BF16)
        for g in range(A_GROUPS):
            cols = slice(g * A_GROUP_DIM, (g + 1) * A_GROUP_DIM)
            s = jnp.dot(ws[g], vnb[:, cols], preferred_element_type=F32) + bst_ref[:, g:g + 1]
            sg_ref[rows, cols] = (u_ref[rows, cols].astype(F32) * s).astype(BF16)


def _sgu(z, ln_g, ln_b, ws, bst, chunks, emit_vn):
    m = z.shape[0]
    tm = chunks * CHUNK
    out_shape = [jax.ShapeDtypeStruct((m, A_WIDTH), BF16)]
    out_specs = [pl.BlockSpec((tm, A_WIDTH), lambda i: (i, 0))]
    if emit_vn:
        out_shape.append(jax.ShapeDtypeStruct((m, A_WIDTH), F32))
        out_specs.append(pl.BlockSpec((tm, A_WIDTH), lambda i: (i, 0)))
    return pl.pallas_call(
        functools.partial(_sgu_kernel, chunks=chunks),
        out_shape=out_shape,
        grid=(m // tm,),
        in_specs=[
            pl.BlockSpec((tm, A_WIDTH), lambda i: (i, COL_U // A_WIDTH)),
            pl.BlockSpec((tm, A_WIDTH), lambda i: (i, COL_V // A_WIDTH)),
            pl.BlockSpec((1, A_WIDTH), lambda i: (0, 0)),
            pl.BlockSpec((1, A_WIDTH), lambda i: (0, 0)),
            pl.BlockSpec((A_GROUPS, CHUNK, CHUNK), lambda i: (0, 0, 0)),
            pl.BlockSpec((CHUNK, A_GROUPS), lambda i: (0, 0)),
        ],
        out_specs=out_specs,
        compiler_params=_cparams("parallel"),
        name="sgu",
    )(z, z, ln_g, ln_b, ws, bst)


def _attn_kernel(q_ref, kc_ref, kp_ref, vc_ref, vp_ref, o_ref, lse_ref, *, qblocks):
    i = pl.program_id(2)
    qi = lax.broadcasted_iota(jnp.int32, (CHUNK, CHUNK), 0)
    ki = lax.broadcasted_iota(jnp.int32, (CHUNK, CHUNK), 1)
    cur_mask = ki <= qi
    no_prev = jnp.where(i == 0, CHUNK, 0)
    dn = (((1,), (1,)), ((), ()))
    rpc = q_ref.shape[1]
    nbq = CHUNK // rpc

    def load(ref, a, cols):
        return ref[a * nbq:(a + 1) * nbq, :, cols].reshape(CHUNK, HEAD_DIM)

    for a in range(qblocks):
        rows = slice(a * CHUNK, (a + 1) * CHUNK)
        for h in range(HEADS_PER_GROUP):
            cols = slice(h * HEAD_DIM, (h + 1) * HEAD_DIM)
            q = load(q_ref, a, cols)
            kc = load(kc_ref, a, cols)
            vc = load(vc_ref, a, cols)
            if a == 0:
                kp, vp = load(kp_ref, 0, cols), load(vp_ref, 0, cols)
                prev_mask = ki >= qi + no_prev
            else:
                kp, vp = load(kc_ref, a - 1, cols), load(vc_ref, a - 1, cols)
                prev_mask = ki >= qi
            s_c = lax.dot_general(q, kc, dn, preferred_element_type=F32) * ATTN_SCALE
            s_p = lax.dot_general(q, kp, dn, preferred_element_type=F32) * ATTN_SCALE
            s_c = jnp.where(cur_mask, s_c, NEG)
            s_p = jnp.where(prev_mask, s_p, NEG)
            mx = jnp.maximum(jnp.max(s_c, axis=-1, keepdims=True), jnp.max(s_p, axis=-1, keepdims=True))
            p_c = jnp.exp(s_c - mx)
            p_p = jnp.exp(s_p - mx)
            l = jnp.sum(p_c, axis=-1, keepdims=True) + jnp.sum(p_p, axis=-1, keepdims=True)
            acc = jnp.dot(p_c.astype(BF16), vc, preferred_element_type=F32)
            acc = acc + jnp.dot(p_p.astype(BF16), vp, preferred_element_type=F32)
            o_ref[a * nbq:(a + 1) * nbq, :, cols] = (acc / l).astype(BF16).reshape(nbq, rpc, HEAD_DIM)
            lse_ref[rows, h:h + 1] = mx + jnp.log(l)


def _rows_per_class(dil):
    return min(PERM_BLOCK // dil, CHUNK)


def _attn_group(z, batch, seq, group):
    _, dil = DILATED_GROUPS[group]
    sub = seq // dil
    rpc = _rows_per_class(dil)
    qblocks = 2 if sub % (2 * CHUNK) == 0 else 1
    tq = qblocks * CHUNK
    cq = COL_Q // GROUP_COLS + group
    ck = COL_K // GROUP_COLS + group
    cv = COL_VB // GROUP_COLS + group
    zv = z.reshape(batch, sub // rpc, dil, rpc, IN_COLS)

    def cur(cb):
        return pl.BlockSpec((None, tq // rpc, None, rpc, GROUP_COLS), lambda b, r, i: (b, i, r, 0, cb))

    def prev(cb):
        return pl.BlockSpec((None, CHUNK // rpc, None, rpc, GROUP_COLS),
                            lambda b, r, i: (b, jnp.maximum(i * qblocks - 1, 0), r, 0, cb))

    o, lse = pl.pallas_call(
        functools.partial(_attn_kernel, qblocks=qblocks),
        out_shape=[jax.ShapeDtypeStruct((batch, sub // rpc, dil, rpc, GROUP_COLS), BF16),
                   jax.ShapeDtypeStruct((batch, dil, sub, HEADS_PER_GROUP), F32)],
        grid=(batch, dil, sub // tq),
        in_specs=[cur(cq), cur(ck), prev(ck), cur(cv), prev(cv)],
        out_specs=[pl.BlockSpec((None, tq // rpc, None, rpc, GROUP_COLS), lambda b, r, i: (b, i, r, 0, 0)),
                   pl.BlockSpec((None, None, tq, HEADS_PER_GROUP), lambda b, r, i: (b, r, i, 0))],
        compiler_params=_cparams("parallel", "parallel", "arbitrary"),
        name=f"attn_prompt_g{group}",
    )(zv, zv, zv, zv, zv)
    o = o.reshape(batch * seq, GROUP_COLS)
    lse = lse.transpose(0, 2, 1, 3).reshape(batch * seq, HEADS_PER_GROUP)
    return o, lse


def _attn_sample_kernel(q_ref, k_ref, v_ref, c0_ref, c1_ref, c2_ref, o_ref, lse_ref, *, n_new):
    caches = (c0_ref, c1_ref, c2_ref)
    rowid = lax.broadcasted_iota(jnp.int32, (CHUNK, HEADS_PER_GROUP, 1), 0)
    for g, (win, dil) in enumerate(DILATED_GROUPS):
        cref = caches[g]
        hs = slice(g * HEADS_PER_GROUP, (g + 1) * HEADS_PER_GROUP)
        for t in range(n_new):
            res, first = t % dil, t // dil
            kc = cref[:, res, 0]
            vc = cref[:, res, 1]
            q = q_ref[t, hs, :]
            s = jnp.sum(kc * q[None], axis=-1, keepdims=True) * ATTN_SCALE
            if first > 0:
                s = jnp.where(rowid >= first, s, NEG)
            new_rows = [tn for tn in range(t + 1) if (t - tn) % dil == 0 and (t - tn) // dil <= win // dil]
            s_new = [jnp.sum(q * k_ref[tn, hs, :], axis=-1, keepdims=True) * ATTN_SCALE for tn in new_rows]
            mx = jnp.max(s, axis=0)
            for sn in s_new:
                mx = jnp.maximum(mx, sn)
            p = jnp.exp(s - mx[None])
            l = jnp.sum(p, axis=0)
            acc = jnp.sum(p * vc, axis=0)
            for tn, sn in zip(new_rows, s_new):
                pn = jnp.exp(sn - mx)
                l = l + pn
                acc = acc + pn * v_ref[tn, hs, :]
            o_ref[g, t] = acc / l
            lse_ref[g, t] = jnp.broadcast_to(mx + jnp.log(l), (HEADS_PER_GROUP, HEAD_DIM))


def _attn_sample(q, k, v, caches):
    bd, n_new = q.shape[:2]
    views, specs = [], []
    for (win, dil), c in zip(DILATED_GROUPS, caches):
        assert c.shape[2] == win and win == CHUNK * dil, "cache must hold exactly one full window"
        used = min(dil, n_new)
        views.append(c.reshape(1, bd, CHUNK, dil, 2, HEADS_PER_GROUP, HEAD_DIM))
        specs.append(pl.BlockSpec((None, None, CHUNK, used, 2, HEADS_PER_GROUP, HEAD_DIM),
                                  lambda b: (0, b, 0, 0, 0, 0, 0)))
    new_spec = pl.BlockSpec((None, n_new, N_DIL * HEADS_PER_GROUP, HEAD_DIM), lambda b: (b, 0, 0, 0))
    out_sds = jax.ShapeDtypeStruct((N_DIL, bd, n_new, HEADS_PER_GROUP, HEAD_DIM), F32)
    out_spec = pl.BlockSpec((N_DIL, None, n_new, HEADS_PER_GROUP, HEAD_DIM), lambda b: (0, b, 0, 0, 0))
    return pl.pallas_call(
        functools.partial(_attn_sample_kernel, n_new=n_new),
        out_shape=[out_sds, out_sds],
        grid=(bd,),
        in_specs=[new_spec, new_spec, new_spec] + specs,
        out_specs=[out_spec, out_spec],
        compiler_params=_cparams("parallel"),
        name="attn_sample",
    )(q, k, v, *views)


def _merge_kernel(sg_ref, o0_ref, o1_ref, o2_ref, l0_ref, l1_ref, l2_ref, ga_ref, gb_ref,
                  wa_ref, wb_ref, *rest, permuted):
    if permuted:
        pt_ref, m_ref, ob_ref = rest
    else:
        m_ref, ob_ref = rest
    j = pl.program_id(1)

    @pl.when(j == 0)
    def _():
        l0, l1, l2 = l0_ref[...], l1_ref[...], l2_ref[...]
        mx = jnp.maximum(jnp.maximum(l0, l1), l2)
        e0, e1, e2 = jnp.exp(l0 - mx), jnp.exp(l1 - mx), jnp.exp(l2 - mx)
        inv = 1.0 / (e0 + e1 + e2)
        ws = (e0 * inv, e1 * inv, e2 * inv)
        o_refs = (o0_ref, o1_ref, o2_ref)
        tm = ob_ref.shape[0]
        for blk in range(max(tm // PERM_BLOCK, 1)):
            rows = slice(blk * PERM_BLOCK, min((blk + 1) * PERM_BLOCK, tm))
            og = []
            for g in range(N_DIL):
                if permuted and g > 0:
                    og.append(jnp.dot(pt_ref[g - 1], o_refs[g][rows, :], preferred_element_type=F32))
                else:
                    og.append(o_refs[g][rows, :].astype(F32))
            for h in range(HEADS_PER_GROUP):
                cols = slice(h * HEAD_DIM, (h + 1) * HEAD_DIM)
                ob = sum(ws[g][rows, h:h + 1] * og[g][:, cols] for g in range(N_DIL))
                ob_ref[rows, cols] = ob.astype(BF16)

    ya = jnp.dot(sg_ref[...], wa_ref[...], preferred_element_type=F32)
    yb = jnp.dot(ob_ref[...], wb_ref[...], preferred_element_type=F32)
    m_ref[...] = (ga_ref[...].astype(F32) * ya + gb_ref[...].astype(F32) * yb).astype(BF16)


def _merge(z, sg, os_, lses, wa, wb, tm, tn, perms_t=None):
    m = z.shape[0]
    assert COL_GA % tn == 0 and COL_GB % tn == 0
    row = lambda width: pl.BlockSpec((tm, width), lambda i, j: (i, 0))
    in_specs = [row(A_WIDTH)] + [row(GROUP_COLS)] * 3 + [row(HEADS_PER_GROUP)] * 3 + [
        pl.BlockSpec((tm, tn), lambda i, j: (i, COL_GA // tn + j)),
        pl.BlockSpec((tm, tn), lambda i, j: (i, COL_GB // tn + j)),
        pl.BlockSpec((A_WIDTH, tn), lambda i, j: (0, j)),
        pl.BlockSpec((GROUP_COLS, tn), lambda i, j: (0, j)),
    ]
    args = [sg, *os_, *lses, z, z, wa, wb]
    if perms_t is not None:
        assert tm % PERM_BLOCK == 0
        in_specs.append(pl.BlockSpec(perms_t.shape, lambda i, j: (0, 0, 0)))
        args.append(perms_t)
    return pl.pallas_call(
        functools.partial(_merge_kernel, permuted=perms_t is not None),
        out_shape=jax.ShapeDtypeStruct((m, D_MODEL), BF16),
        grid=(m // tm, D_MODEL // tn),
        in_specs=in_specs,
        out_specs=pl.BlockSpec((tm, tn), lambda i, j: (i, j)),
        scratch_shapes=[pltpu.VMEM((tm, GROUP_COLS), BF16)],
        compiler_params=_cparams("parallel", "arbitrary"),
        name="branch_merge",
    )(*args)


def _oproj_kernel(m_ref, x_ref, wo_ref, g_ref, wr_ref, br_ref, h_ref, hn_ref, ri_ref, rw_ref):
    h = x_ref[...] + jnp.dot(m_ref[...], wo_ref[...], preferred_element_type=F32)
    h_ref[...] = h
    ms = jnp.mean(h * h, axis=-1, keepdims=True)
    hn = h * lax.rsqrt(ms + RMS_EPS) * g_ref[...]
    _store_token_tiles(hn_ref, hn)
    logits = jnp.dot(hn.astype(BF16), wr_ref[...], preferred_element_type=F32) + br_ref[...]
    lane = lax.broadcasted_iota(jnp.int32, logits.shape, 1).astype(F32)
    big = float(LANES)

    def first_argmax(vals, vmax):
        return jnp.min(jnp.where(vals == vmax, lane, big), axis=-1, keepdims=True)

    lg = jnp.where(lane < N_EXPERT_GROUPS, logits, NEG)
    gmax = jnp.max(lg, axis=-1, keepdims=True)
    gsel = first_argmax(lg, gmax)
    p_sel = 1.0 / jnp.sum(jnp.exp(lg - gmax), axis=-1, keepdims=True)
    lo = N_EXPERT_GROUPS + EXPERTS_PER_GROUP * gsel
    le = jnp.where(jnp.logical_and(lane >= lo, lane < lo + EXPERTS_PER_GROUP), logits, NEG)
    v1 = jnp.max(le, axis=-1, keepdims=True)
    i1 = first_argmax(le, v1)
    le2 = jnp.where(lane == i1, NEG, le)
    v2 = jnp.max(le2, axis=-1, keepdims=True)
    i2 = first_argmax(le2, v2)
    e2 = jnp.exp(v2 - v1)
    w1 = p_sel / (1.0 + e2)
    w2 = p_sel * e2 / (1.0 + e2)
    ri = jnp.where(lane == 0, i1 - N_EXPERT_GROUPS, jnp.where(lane == 1, i2 - N_EXPERT_GROUPS, 0.0))
    ri_ref[...] = ri.astype(jnp.int32)
    rw_ref[...] = jnp.where(lane == 0, w1, jnp.where(lane == 1, w2, 0.0))


def _oproj(mm, x, wo, norm_g, wr, br, tm):
    m = x.shape[0]
    row = lambda width: pl.BlockSpec((tm, width), lambda i: (i, 0))
    full = lambda a, b: pl.BlockSpec((a, b), lambda i: (0, 0))
    return pl.pallas_call(
        _oproj_kernel,
        out_shape=[jax.ShapeDtypeStruct((m, D_MODEL), F32), jax.ShapeDtypeStruct((m * ROW_TILE, LANES), F32),
                   jax.ShapeDtypeStruct((m, LANES), jnp.int32), jax.ShapeDtypeStruct((m, LANES), F32)],
        grid=(m // tm,),
        in_specs=[row(D_MODEL), row(D_MODEL), full(D_MODEL, D_MODEL), full(1, D_MODEL),
                  full(D_MODEL, LANES), full(1, LANES)],
        out_specs=[row(D_MODEL), pl.BlockSpec((tm * ROW_TILE, LANES), lambda i: (i, 0)), row(LANES), row(LANES)],
        compiler_params=_cparams("parallel"),
        name="oproj_router",
    )(mm, x, wo, norm_g, wr, br)


def _dispatch_kernel(slot_ref, pad_start_ref, pad_len_ref, hn_ref, xs_hbm, zbuf, sem, zsem, *, tm, moe_tm, n_tiles):
    i = pl.program_id(0)

    def zero_copy(dst_row, rows):
        return pltpu.make_async_copy(zbuf.at[pl.ds(0, rows * ROW_TILE)],
                                     xs_hbm.at[pl.ds(dst_row * ROW_TILE, rows * ROW_TILE)], zsem)

    def zero_fill(act):
        def per_expert(e, carry):
            off, length = pad_start_ref[e], pad_len_ref[e]
            rows = moe_tm // 2
            while rows >= 1:
                @pl.when((length & rows) != 0)
                def _(off=off, rows=rows):
                    act(zero_copy(off, rows))
                off = off + (length & rows)
                rows //= 2
            return carry

        lax.fori_loop(0, N_EXPERTS, per_expert, 0)

        def per_tile(t, carry):
            act(zero_copy(t * moe_tm, moe_tm))
            return carry

        lax.fori_loop(pad_start_ref[N_EXPERTS], n_tiles, per_tile, 0)

    @pl.when(i == 0)
    def _():
        zbuf[...] = jnp.zeros_like(zbuf)
        zero_fill(lambda c: c.start())

    def issue(r, carry):
        for k in range(TOP_K):
            slot = slot_ref[(i * tm + r) * TOP_K + k]
            pltpu.make_async_copy(hn_ref.at[pl.ds(r * ROW_TILE, ROW_TILE)],
                                  xs_hbm.at[pl.ds(slot * ROW_TILE, ROW_TILE)], sem).start()
        return carry

    lax.fori_loop(0, tm, issue, 0, unroll=8)
    for k in range(TOP_K):
        pltpu.make_async_copy(hn_ref, hn_ref, sem).wait()

    @pl.when(i == pl.num_programs(0) - 1)
    def _():
        zero_fill(lambda c: c.wait())


def _dispatch(hn_tiles, slot, pad_start, pad_len, n_tiles, moe_tm, tm):
    m = hn_tiles.shape[0] // ROW_TILE
    assert m % tm == 0 and moe_tm & (moe_tm - 1) == 0
    grid_spec = pltpu.PrefetchScalarGridSpec(
        num_scalar_prefetch=3,
        grid=(m // tm,),
        in_specs=[pl.BlockSpec((tm * ROW_TILE, LANES), lambda i, s, ps, pn: (i, 0))],
        out_specs=pl.BlockSpec(memory_space=pl.ANY),
        scratch_shapes=[pltpu.VMEM((moe_tm * ROW_TILE, LANES), F32),
                        pltpu.SemaphoreType.DMA(()), pltpu.SemaphoreType.DMA(())],
    )
    return pl.pallas_call(
        functools.partial(_dispatch_kernel, tm=tm, moe_tm=moe_tm, n_tiles=n_tiles),
        out_shape=jax.ShapeDtypeStruct((n_tiles * moe_tm * ROW_TILE, LANES), F32),
        grid_spec=grid_spec,
        compiler_params=_cparams("arbitrary"),
        name="moe_dispatch",
    )(slot, pad_start, pad_len, hn_tiles)


def _moe_kernel(texp_ref, valid_ref, xs_ref, wg_ref, wu_ref, wd_ref, ys_ref):
    i = pl.program_id(0)
    tm = xs_ref.shape[0] // ROW_TILE
    valid = valid_ref[i]

    @pl.when(valid > 0)
    def _():
        x = jnp.concatenate(
            [_load_token_tile_chunk(xs_ref, (), tm, s).astype(BF16) for s in range(ROW_TILE)], axis=1)
        gate = jnp.dot(x, wg_ref[...], preferred_element_type=F32)
        up = jnp.dot(x, wu_ref[...], preferred_element_type=F32)
        hid = (gate * _sigmoid(gate) * up).astype(BF16)
        _store_token_tiles(ys_ref, jnp.dot(hid, wd_ref[...], preferred_element_type=F32))

    @pl.when(valid == 0)
    def _():
        ys_ref[...] = jnp.zeros_like(ys_ref)


def _moe(xs, wg, wu, wd, tile_expert, tile_valid, tm):
    n_tiles = tile_expert.shape[0]
    grid_spec = pltpu.PrefetchScalarGridSpec(
        num_scalar_prefetch=2,
        grid=(n_tiles,),
        in_specs=[
            pl.BlockSpec((tm * ROW_TILE, LANES), lambda i, te, tv: (i, 0)),
            pl.BlockSpec((None, D_MODEL, D_EXPERT), lambda i, te, tv: (te[i], 0, 0)),
            pl.BlockSpec((None, D_MODEL, D_EXPERT), lambda i, te, tv: (te[i], 0, 0)),
            pl.BlockSpec((None, D_EXPERT, D_MODEL), lambda i, te, tv: (te[i], 0, 0)),
        ],
        out_specs=pl.BlockSpec((tm * ROW_TILE, LANES), lambda i, te, tv: (i, 0)),
    )
    return pl.pallas_call(
        _moe_kernel,
        out_shape=jax.ShapeDtypeStruct((n_tiles * tm * ROW_TILE, LANES), F32),
        grid_spec=grid_spec,
        compiler_params=_cparams("arbitrary"),
        name="moe_experts",
    )(tile_expert, tile_valid, xs, wg, wu, wd)


def _combine_kernel(slot_ref, h_ref, rw_ref, g_ref, ys_hbm, y_ref, ybuf, sem, *, tm):
    i = pl.program_id(0)
    n_steps = pl.num_programs(0)

    def issue(tile, buf):
        def body(r, carry):
            for k in range(TOP_K):
                slot = slot_ref[(tile * tm + r) * TOP_K + k]
                pltpu.make_async_copy(ys_hbm.at[pl.ds(slot * ROW_TILE, ROW_TILE)],
                                      ybuf.at[buf, k, pl.ds(r * ROW_TILE, ROW_TILE)], sem.at[buf]).start()
            return carry

        lax.fori_loop(0, tm, body, 0, unroll=8)

    @pl.when(i == 0)
    def _():
        issue(0, 0)

    @pl.when(i + 1 < n_steps)
    def _():
        issue(i + 1, (i + 1) % 2)

    cur = i % 2
    for k in range(TOP_K):
        pltpu.make_async_copy(ybuf.at[cur, k], ybuf.at[cur, k], sem.at[cur]).wait()
    w0, w1 = rw_ref[:, 0:1], rw_ref[:, 1:2]
    chunks, ssq = [], 0.0
    for s in range(ROW_TILE):
        cols = slice(s * LANES, (s + 1) * LANES)
        y = (h_ref[:, cols] + w0 * _load_token_tile_chunk(ybuf, (cur, 0), tm, s)
             + w1 * _load_token_tile_chunk(ybuf, (cur, 1), tm, s))
        chunks.append(y)
        ssq = ssq + jnp.sum(y * y, axis=-1, keepdims=True)
    scale = lax.rsqrt(ssq * (1.0 / D_MODEL) + RMS_EPS)
    for s in range(ROW_TILE):
        cols = slice(s * LANES, (s + 1) * LANES)
        y_ref[:, cols] = chunks[s] * scale * g_ref[:, cols]


def _combine(h, route_w, norm_g, ys, slot, tm):
    m = h.shape[0]
    grid_spec = pltpu.PrefetchScalarGridSpec(
        num_scalar_prefetch=1,
        grid=(m // tm,),
        in_specs=[
            pl.BlockSpec((tm, D_MODEL), lambda i, s: (i, 0)),
            pl.BlockSpec((tm, LANES), lambda i, s: (i, 0)),
            pl.BlockSpec((1, D_MODEL), lambda i, s: (0, 0)),
            pl.BlockSpec(memory_space=pl.ANY),
        ],
        out_specs=pl.BlockSpec((tm, D_MODEL), lambda i, s: (i, 0)),
        scratch_shapes=[pltpu.VMEM((2, TOP_K, tm * ROW_TILE, LANES), F32), pltpu.SemaphoreType.DMA((2,))],
    )
    return pl.pallas_call(
        functools.partial(_combine_kernel, tm=tm),
        out_shape=jax.ShapeDtypeStruct((m, D_MODEL), F32),
        grid_spec=grid_spec,
        compiler_params=_cparams("arbitrary"),
        name="moe_combine",
    )(slot, h, route_w, norm_g, ys)


def _routing_tables(route_i, m, tm):
    n = m * TOP_K
    n_tiles = pl.cdiv(n, tm) + N_EXPERTS - 1
    e_flat = route_i[:, :TOP_K].reshape(n)
    onehot = (e_flat[:, None] == jnp.arange(N_EXPERTS, dtype=jnp.int32)[None, :]).astype(jnp.int32)
    csum = jnp.cumsum(onehot, axis=0)
    rank = jnp.take_along_axis(csum, e_flat[:, None], axis=1)[:, 0] - 1
    counts = csum[-1]
    tiles_e = (counts + tm - 1) // tm
    tile_end = jnp.cumsum(tiles_e)
    tile_start = tile_end - tiles_e
    slot = (tile_start[e_flat] * tm + rank).astype(jnp.int32)
    tile_ids = jnp.arange(n_tiles, dtype=jnp.int32)
    tile_expert = jnp.sum((tile_end[None, :] <= tile_ids[:, None]).astype(jnp.int32), axis=1)
    tile_expert = jnp.minimum(tile_expert, N_EXPERTS - 1)
    tile_valid = jnp.clip(counts[tile_expert] - (tile_ids - tile_start[tile_expert]) * tm, 0, tm)
    tile_valid = jnp.where(tile_ids < tile_end[-1], tile_valid, 0).astype(jnp.int32)
    pad_start = jnp.concatenate([tile_start * tm + counts, tile_end[-1:]]).astype(jnp.int32)
    pad_len = (tiles_e * tm - counts).astype(jnp.int32)
    return slot, tile_expert.astype(jnp.int32), tile_valid, pad_start, pad_len


def _token_tail(x, z, sg, os_, lses, wts, tm, perms_t=None):
    mm = _merge(z, sg, os_, lses, wts["wa"], wts["wb"], tm=min(tm, 256), tn=D_MODEL, perms_t=perms_t)
    h, hn, route_i, route_w = _oproj(mm, x, wts["wo"], wts["norm_ffn"], wts["wr"], wts["br"], tm=min(tm, 256))
    moe_tm = min(tm, MOE_TM)
    slot, tile_expert, tile_valid, pad_start, pad_len = _routing_tables(route_i, x.shape[0], moe_tm)
    xs = _dispatch(hn, slot, pad_start, pad_len, tile_expert.shape[0], moe_tm, tm=min(tm, 256))
    ys = _moe(xs, wts["wg"], wts["wu"], wts["wd"], tile_expert, tile_valid, moe_tm)
    return _combine(h, route_w, wts["norm_final"], ys, slot, tm=min(tm, 256))


def _kv_rows(z, lead, keep_from, group, permuted):
    zz = z.reshape(lead + (IN_COLS,))[:, keep_from:]
    k = zz[..., COL_K + group * GROUP_COLS: COL_K + (group + 1) * GROUP_COLS]
    v = zz[..., COL_VB + group * GROUP_COLS: COL_VB + (group + 1) * GROUP_COLS]
    kv = jnp.stack([k, v], axis=2).astype(F32)
    dil = DILATED_GROUPS[group][1]
    if permuted and dil > 1:
        assert keep_from % PERM_BLOCK == 0 and kv.shape[1] % PERM_BLOCK == 0
        b, rows = kv.shape[:2]
        kv = kv.reshape(b, rows // PERM_BLOCK, dil, PERM_BLOCK // dil, 2, GROUP_COLS)
        kv = kv.transpose(0, 1, 3, 2, 4, 5).reshape(b, rows, 2, GROUP_COLS)
    return kv.reshape(kv.shape[:3] + (HEADS_PER_GROUP, HEAD_DIM))


def kernel(x_prompt, x_sample, cache_kv_w128, cache_kv_w512, cache_kv_w2048, norm_mix, w_in, ln_v_g, ln_v_b, w_s, b_s, w_a_out, w_b_out, w_o, norm_ffn, w_route_group, b_route_group, w_route_expert, b_route_expert, w_gate_e, w_up_e, w_down_e, norm_final):
    assert norm_mix.shape[0] == 1, "single-layer trunk"
    batch, seq, _ = x_prompt.shape
    bd, n_new, _ = x_sample.shape
    caches = (cache_kv_w128, cache_kv_w512, cache_kv_w2048)

    pad = LANES - N_EXPERT_GROUPS - N_EXPERTS
    wr = jnp.concatenate([w_route_group[0], w_route_expert[0], jnp.zeros((D_MODEL, pad), F32)], axis=1)
    br = jnp.concatenate([b_route_group[0], b_route_expert[0], jnp.zeros((pad,), F32)])[None, :]
    wts = dict(
        wa=w_a_out[0].astype(BF16), wb=w_b_out[0].astype(BF16), wo=w_o[0].astype(BF16),
        wg=w_gate_e[0].astype(BF16), wu=w_up_e[0].astype(BF16), wd=w_down_e[0].astype(BF16),
        wr=wr.astype(BF16), br=br, norm_ffn=norm_ffn[0][None, :], norm_final=norm_final[None, :],
    )
    w_in_b = jnp.concatenate([w_in[0][:, :REF_COL_Q], w_in[0][:, REF_COL_GA:], w_in[0][:, REF_COL_Q:REF_COL_GA]],
                             axis=1).astype(BF16)
    norm_g = norm_mix[0][None, :]
    ln_g, ln_b = ln_v_g[0][None, :], ln_v_b[0][None, :]

    xp = x_prompt.reshape(batch * seq, D_MODEL)
    perms = jnp.stack([_class_major_perm(dil) for _, dil in DILATED_GROUPS[1:]])
    zp = _inproj(xp, norm_g, w_in_b, tm=1024, perms=perms)
    (sgp,) = _sgu(zp, ln_g, ln_b, w_s[0], b_s[0].T, chunks=4, emit_vn=False)
    os_p, lses_p = zip(*[_attn_group(zp, batch, seq, g) for g in range(N_DIL)])
    y_prompt = _token_tail(xp, zp, sgp, os_p, lses_p, wts, tm=512,
                           perms_t=perms.transpose(0, 2, 1)).reshape(batch, seq, D_MODEL)
    kv_prompt = [_kv_rows(zp, (batch, seq), seq - min(win, seq), g, permuted=True)[None]
                 for g, (win, _) in enumerate(DILATED_GROUPS)]

    ms = bd * n_new
    assert ms == CHUNK and n_new <= CHUNK
    xs = x_sample.reshape(ms, D_MODEL)
    zs = _inproj(xs, norm_g, w_in_b, tm=ms)
    eye = jnp.eye(bd, dtype=F32)
    ws_s = jnp.einsum("ab,gts->gatbs", eye, w_s[0][:, :n_new, :n_new]).reshape(A_GROUPS, ms, ms)
    bst_s = jnp.tile(b_s[0][:, :n_new].T, (bd, 1))
    sgs, vns = _sgu(zs, ln_g, ln_b, ws_s, bst_s, chunks=1, emit_vn=True)
    zs3 = zs.reshape(bd, n_new, IN_COLS)
    qkv = [zs3[..., c:c + B_WIDTH].astype(F32).reshape(bd, n_new, N_DIL * HEADS_PER_GROUP, HEAD_DIM)
           for c in (COL_Q, COL_K, COL_VB)]
    o_s, lse_s = _attn_sample(*qkv, caches)
    os_s = [o_s[g].reshape(ms, GROUP_COLS).astype(BF16) for g in range(N_DIL)]
    lses_s = [lse_s[g, ..., 0].reshape(ms, HEADS_PER_GROUP) for g in range(N_DIL)]
    y_sample = _token_tail(xs, zs, sgs, os_s, lses_s, wts, tm=ms).reshape(bd, n_new, D_MODEL)
    kv_sample = [_kv_rows(zs, (bd, n_new), 0, g, permuted=False)[None] for g in range(N_DIL)]
    chunk_v = vns.reshape(1, bd, n_new, A_WIDTH)

    return (y_prompt, y_sample, *kv_prompt, *kv_sample, chunk_v)
```

```python
import functools

import jax
import jax.numpy as jnp
from jax import lax
from jax.experimental import pallas as pl
from jax.experimental.pallas import tpu as pltpu

F32 = jnp.float32
BF16 = jnp.bfloat16

D_MODEL = 2048
CHUNK = 128
A_GROUPS = 16
A_GROUP_DIM = 128
A_WIDTH = A_GROUPS * A_GROUP_DIM
HEAD_DIM = 128
HEADS_PER_GROUP = 4
GROUP_COLS = HEADS_PER_GROUP * HEAD_DIM
DILATED_GROUPS = ((128, 1), (512, 4), (2048, 16))
N_DIL = len(DILATED_GROUPS)
B_WIDTH = N_DIL * GROUP_COLS
IN_COLS = 2 * A_WIDTH + 3 * B_WIDTH + 2 * D_MODEL
COL_U, COL_V = 0, A_WIDTH
COL_GA = 2 * A_WIDTH
COL_GB = COL_GA + D_MODEL
COL_Q = COL_GB + D_MODEL
COL_K = COL_Q + B_WIDTH
COL_VB = COL_K + B_WIDTH
N_EXPERT_GROUPS = 4
EXPERTS_PER_GROUP = 8
N_EXPERTS = N_EXPERT_GROUPS * EXPERTS_PER_GROUP
TOP_K = 2
D_EXPERT = 256
RMS_EPS = 1e-6
LN_EPS = 1e-5
NEG = -1e30
ATTN_SCALE = HEAD_DIM ** -0.5

LANES = 128
VMEM_LIMIT_BYTES = 56 * 1024 * 1024
IN_TN = 512
MOE_TM = 256
PERM_BLOCK = 256
ROW_TILE = D_MODEL // LANES

def _cparams(*sem):
    return pltpu.CompilerParams(dimension_semantics=sem, vmem_limit_bytes=VMEM_LIMIT_BYTES)


def _gelu_tanh(x):
    return 0.5 * x * (1.0 + jnp.tanh(0.7978845608028654 * (x + 0.044715 * (x * x * x))))


def _sigmoid(x):
    return 1.0 / (1.0 + jnp.exp(-x))


def _store_token_tiles(ref, val):
    rows = val.shape[0]
    for s in range(ROW_TILE):
        ref[pl.ds(s, rows, stride=ROW_TILE), :] = val[:, s * LANES:(s + 1) * LANES]


def _load_token_tile_chunk(ref, lead, rows, s):
    return ref[lead + (pl.ds(s, rows, stride=ROW_TILE), slice(None))]


def _class_major_perm(dil):
    out_row = jnp.arange(PERM_BLOCK, dtype=jnp.int32)
    src = (out_row % (PERM_BLOCK // dil)) * dil + out_row // (PERM_BLOCK // dil)
    return (src[:, None] == jnp.arange(PERM_BLOCK, dtype=jnp.int32)[None, :]).astype(BF16)


def _inproj_kernel(x_ref, g_ref, w_ref, *rest, permute):
    if permute:
        perm_ref, z_ref, xn_ref = rest
    else:
        z_ref, xn_ref = rest
    j = pl.program_id(1)

    @pl.when(j == 0)
    def _():
        x = x_ref[...]
        ms = jnp.mean(x * x, axis=-1, keepdims=True)
        xn_ref[...] = (x * lax.rsqrt(ms + RMS_EPS) * g_ref[...]).astype(BF16)

    is_gelu = j < COL_GA // IN_TN
    is_plain = j >= COL_Q // IN_TN
    is_gate = jnp.logical_not(jnp.logical_or(is_gelu, is_plain))
    group = (j - COL_Q // IN_TN) % N_DIL
    tm = xn_ref.shape[0]
    sub = min(tm, PERM_BLOCK)

    def emit(epilogue):
        for blk in range(tm // sub):
            rows = slice(blk * sub, (blk + 1) * sub)
            z = jnp.dot(xn_ref[rows, :], w_ref[...], preferred_element_type=F32)
            z_ref[rows, :] = epilogue(z).astype(BF16)

    @pl.when(is_gelu)
    def _():
        emit(_gelu_tanh)

    @pl.when(is_gate)
    def _():
        emit(_sigmoid)

    if not permute:
        @pl.when(is_plain)
        def _():
            emit(lambda z: z)
        return

    @pl.when(jnp.logical_and(is_plain, group == 0))
    def _():
        emit(lambda z: z)

    for g in range(1, N_DIL):
        @pl.when(jnp.logical_and(is_plain, group == g))
        def _():
            emit(lambda z: jnp.dot(perm_ref[g - 1], z.astype(BF16), preferred_element_type=F32))


def _ref_col_tile(j):
    n_gelu, n_gate, n_qkv = COL_GA // IN_TN, 2 * D_MODEL // IN_TN, 3 * B_WIDTH // IN_TN
    return jnp.where(j < n_gelu, j, jnp.where(j < n_gelu + n_gate, j + n_qkv, j - n_gate))


def _inproj(x, norm_g, w_in_tiles, tm, perms=None):
    m = x.shape[0]
    in_specs = [
        pl.BlockSpec((tm, D_MODEL), lambda i, j: (i, 0)),
        pl.BlockSpec((1, D_MODEL), lambda i, j: (0, 0)),
        pl.BlockSpec((None, D_MODEL, IN_TN), lambda i, j: (_ref_col_tile(j), 0, 0)),
    ]
    args = [x, norm_g, w_in_tiles]
    if perms is not None:
        in_specs.append(pl.BlockSpec(perms.shape, lambda i, j: (0, 0, 0)))
        args.append(perms)
    return pl.pallas_call(
        functools.partial(_inproj_kernel, permute=perms is not None),
        out_shape=jax.ShapeDtypeStruct((m, IN_COLS), BF16),
        grid=(m // tm, IN_COLS // IN_TN),
        in_specs=in_specs,
        out_specs=pl.BlockSpec((tm, IN_TN), lambda i, j: (i, j)),
        scratch_shapes=[pltpu.VMEM((tm, D_MODEL), BF16)],
        compiler_params=_cparams("parallel", "arbitrary"),
        name="inproj",
    )(*args)


def _sgu_kernel(u_ref, v_ref, lng_ref, lnb_ref, ws_ref, bst_ref, sg_ref, *vn_out, chunks):
    row = lax.broadcasted_iota(jnp.int32, (CHUNK, CHUNK), 0)
    col = lax.broadcasted_iota(jnp.int32, (CHUNK, CHUNK), 1)
    tri = row >= col
    ws = [jnp.where(tri, ws_ref[g], 0.0).astype(BF16) for g in range(A_GROUPS)]
    for c in range(chunks):
        rows = slice(c * CHUNK, (c + 1) * CHUNK)
        v = v_ref[rows, :].astype(F32)
        mu = jnp.mean(v, axis=-1, keepdims=True)
        vc = v - mu
        var = jnp.mean(vc * vc, axis=-1, keepdims=True)
        vn = vc * lax.rsqrt(var + LN_EPS) * lng_ref[...] + lnb_ref[...]
        if vn_out:
            vn_out[0][rows, :] = vn
        vnb = vn.astype(BF16)
        for g in range(A_GROUPS):
            cols = slice(g * A_GROUP_DIM, (g + 1) * A_GROUP_DIM)
            s = jnp.dot(ws[g], vnb[:, cols], preferred_element_type=F32) + bst_ref[:, g:g + 1]
            sg_ref[rows, cols] = (u_ref[rows, cols].astype(F32) * s).astype(BF16)


def _sgu(z, ln_g, ln_b, ws, bst, chunks, emit_vn):
    m = z.shape[0]
    tm = chunks * CHUNK
    out_shape = [jax.ShapeDtypeStruct((m, A_WIDTH), BF16)]
    out_specs = [pl.BlockSpec((tm, A_WIDTH), lambda i: (i, 0))]
    if emit_vn:
        out_shape.append(jax.ShapeDtypeStruct((m, A_WIDTH), F32))
        out_specs.append(pl.BlockSpec((tm, A_WIDTH), lambda i: (i, 0)))
    return pl.pallas_call(
        functools.partial(_sgu_kernel, chunks=chunks),
        out_shape=out_shape,
        grid=(m // tm,),
        in_specs=[
            pl.BlockSpec((tm, A_WIDTH), lambda i: (i, COL_U // A_WIDTH)),
            pl.BlockSpec((tm, A_WIDTH), lambda i: (i, COL_V // A_WIDTH)),
            pl.BlockSpec((1, A_WIDTH), lambda i: (0, 0)),
            pl.BlockSpec((1, A_WIDTH), lambda i: (0, 0)),
            pl.BlockSpec((A_GROUPS, CHUNK, CHUNK), lambda i: (0, 0, 0)),
            pl.BlockSpec((CHUNK, A_GROUPS), lambda i: (0, 0)),
        ],
        out_specs=out_specs,
        compiler_params=_cparams("parallel"),
        name="sgu",
    )(z, z, ln_g, ln_b, ws, bst)


def _attn_kernel(q_ref, kc_ref, kp_ref, vc_ref, vp_ref, o_ref, lse_ref, *, qblocks):
    i = pl.program_id(2)
    qi = lax.broadcasted_iota(jnp.int32, (CHUNK, CHUNK), 0)
    ki = lax.broadcasted_iota(jnp.int32, (CHUNK, CHUNK), 1)
    cur_mask = ki <= qi
    no_prev = jnp.where(i == 0, CHUNK, 0)
    dn = (((1,), (1,)), ((), ()))
    rpc = q_ref.shape[1]
    nbq = CHUNK // rpc

    def load(ref, a, cols):
        return ref[a * nbq:(a + 1) * nbq, :, cols].reshape(CHUNK, HEAD_DIM)

    for a in range(qblocks):
        rows = slice(a * CHUNK, (a + 1) * CHUNK)
        for h in range(HEADS_PER_GROUP):
            cols = slice(h * HEAD_DIM, (h + 1) * HEAD_DIM)
            q = load(q_ref, a, cols)
            kc = load(kc_ref, a, cols)
            vc = load(vc_ref, a, cols)
            if a == 0:
                kp, vp = load(kp_ref, 0, cols), load(vp_ref, 0, cols)
                prev_mask = ki >= qi + no_prev
            else:
                kp, vp = load(kc_ref, a - 1, cols), load(vc_ref, a - 1, cols)
                prev_mask = ki >= qi
            s_c = lax.dot_general(q, kc, dn, preferred_element_type=F32) * ATTN_SCALE
            s_p = lax.dot_general(q, kp, dn, preferred_element_type=F32) * ATTN_SCALE
            s_c = jnp.where(cur_mask, s_c, NEG)
            s_p = jnp.where(prev_mask, s_p, NEG)
            mx = jnp.maximum(jnp.max(s_c, axis=-1, keepdims=True), jnp.max(s_p, axis=-1, keepdims=True))
            p_c = jnp.exp(s_c - mx)
            p_p = jnp.exp(s_p - mx)
            l = jnp.sum(p_c, axis=-1, keepdims=True) + jnp.sum(p_p, axis=-1, keepdims=True)
            acc = jnp.dot(p_c.astype(BF16), vc, preferred_element_type=F32)
            acc = acc + jnp.dot(p_p.astype(BF16), vp, preferred_element_type=F32)
            o_ref[a * nbq:(a + 1) * nbq, :, cols] = (acc / l).astype(BF16).reshape(nbq, rpc, HEAD_DIM)
            lse_ref[rows, h:h + 1] = mx + jnp.log(l)


def _rows_per_class(dil):
    return min(PERM_BLOCK // dil, CHUNK)


def _attn_group(z, batch, seq, group):
    _, dil = DILATED_GROUPS[group]
    sub = seq // dil
    rpc = _rows_per_class(dil)
    qblocks = 2 if sub % (2 * CHUNK) == 0 else 1
    tq = qblocks * CHUNK
    cq = COL_Q // GROUP_COLS + group
    ck = COL_K // GROUP_COLS + group
    cv = COL_VB // GROUP_COLS + group
    zv = z.reshape(batch, sub // rpc, dil, rpc, IN_COLS)

    def cur(cb):
        return pl.BlockSpec((None, tq // rpc, None, rpc, GROUP_COLS), lambda b, r, i: (b, i, r, 0, cb))

    def prev(cb):
        return pl.BlockSpec((None, CHUNK // rpc, None, rpc, GROUP_COLS),
                            lambda b, r, i: (b, jnp.maximum(i * qblocks - 1, 0), r, 0, cb))

    o, lse = pl.pallas_call(
        functools.partial(_attn_kernel, qblocks=qblocks),
        out_shape=[jax.ShapeDtypeStruct((batch, sub // rpc, dil, rpc, GROUP_COLS), BF16),
                   jax.ShapeDtypeStruct((batch, dil, sub, HEADS_PER_GROUP), F32)],
        grid=(batch, dil, sub // tq),
        in_specs=[cur(cq), cur(ck), prev(ck), cur(cv), prev(cv)],
        out_specs=[pl.BlockSpec((None, tq // rpc, None, rpc, GROUP_COLS), lambda b, r, i: (b, i, r, 0, 0)),
                   pl.BlockSpec((None, None, tq, HEADS_PER_GROUP), lambda b, r, i: (b, r, i, 0))],
        compiler_params=_cparams("parallel", "parallel", "arbitrary"),
        name=f"attn_prompt_g{group}",
    )(zv, zv, zv, zv, zv)
    o = o.reshape(batch * seq, GROUP_COLS)
    lse = lse.transpose(0, 2, 1, 3).reshape(batch * seq, HEADS_PER_GROUP)
    return o, lse


def _attn_sample_kernel(q_ref, k_ref, v_ref, c0_ref, c1_ref, c2_ref, o_ref, lse_ref, *, n_new):
    caches = (c0_ref, c1_ref, c2_ref)
    rowid = lax.broadcasted_iota(jnp.int32, (CHUNK, HEADS_PER_GROUP, 1), 0)
    for g, (win, dil) in enumerate(DILATED_GROUPS):
        cref = caches[g]
        hs = slice(g * HEADS_PER_GROUP, (g + 1) * HEADS_PER_GROUP)
        for t in range(n_new):
            res, first = t % dil, t // dil
            kc = cref[:, res, 0]
            vc = cref[:, res, 1]
            q = q_ref[t, hs, :]
            s = jnp.sum(kc * q[None], axis=-1, keepdims=True) * ATTN_SCALE
            if first > 0:
                s = jnp.where(rowid >= first, s, NEG)
            new_rows = [tn for tn in range(t + 1) if (t - tn) % dil == 0 and (t - tn) // dil <= win // dil]
            s_new = [jnp.sum(q * k_ref[tn, hs, :], axis=-1, keepdims=True) * ATTN_SCALE for tn in new_rows]
            mx = jnp.max(s, axis=0)
            for sn in s_new:
                mx = jnp.maximum(mx, sn)
            p = jnp.exp(s - mx[None])
            l = jnp.sum(p, axis=0)
            acc = jnp.sum(p * vc, axis=0)
            for tn, sn in zip(new_rows, s_new):
                pn = jnp.exp(sn - mx)
                l = l + pn
                acc = acc + pn * v_ref[tn, hs, :]
            o_ref[g, t] = acc / l
            lse_ref[g, t] = jnp.broadcast_to(mx + jnp.log(l), (HEADS_PER_GROUP, HEAD_DIM))


def _attn_sample(q, k, v, caches):
    bd, n_new = q.shape[:2]
    views, specs = [], []
    for (win, dil), c in zip(DILATED_GROUPS, caches):
        assert c.shape[2] == win and win == CHUNK * dil, "cache must hold exactly one full window"
        used = min(dil, n_new)
        views.append(c.reshape(1, bd, CHUNK, dil, 2, HEADS_PER_GROUP, HEAD_DIM))
        specs.append(pl.BlockSpec((None, None, CHUNK, used, 2, HEADS_PER_GROUP, HEAD_DIM),
                                  lambda b: (0, b, 0, 0, 0, 0, 0)))
    new_spec = pl.BlockSpec((None, n_new, N_DIL * HEADS_PER_GROUP, HEAD_DIM), lambda b: (b, 0, 0, 0))
    out_sds = jax.ShapeDtypeStruct((N_DIL, bd, n_new, HEADS_PER_GROUP, HEAD_DIM), F32)
    out_spec = pl.BlockSpec((N_DIL, None, n_new, HEADS_PER_GROUP, HEAD_DIM), lambda b: (0, b, 0, 0, 0))
    return pl.pallas_call(
        functools.partial(_attn_sample_kernel, n_new=n_new),
        out_shape=[out_sds, out_sds],
        grid=(bd,),
        in_specs=[new_spec, new_spec, new_spec] + specs,
        out_specs=[out_spec, out_spec],
        compiler_params=_cparams("parallel"),
        name="attn_sample",
    )(q, k, v, *views)


def _merge_kernel(sg_ref, o0_ref, o1_ref, o2_ref, l0_ref, l1_ref, l2_ref, ga_ref, gb_ref,
                  wa_ref, wb_ref, *rest, permuted):
    if permuted:
        pt_ref, m_ref, ob_ref = rest
    else:
        m_ref, ob_ref = rest
    j = pl.program_id(1)

    @pl.when(j == 0)
    def _():
        l0, l1, l2 = l0_ref[...], l1_ref[...], l2_ref[...]
        mx = jnp.maximum(jnp.maximum(l0, l1), l2)
        e0, e1, e2 = jnp.exp(l0 - mx), jnp.exp(l1 - mx), jnp.exp(l2 - mx)
        inv = 1.0 / (e0 + e1 + e2)
        ws = (e0 * inv, e1 * inv, e2 * inv)
        o_refs = (o0_ref, o1_ref, o2_ref)
        tm = ob_ref.shape[0]
        for blk in range(max(tm // PERM_BLOCK, 1)):
            rows = slice(blk * PERM_BLOCK, min((blk + 1) * PERM_BLOCK, tm))
            og = []
            for g in range(N_DIL):
                if permuted and g > 0:
                    og.append(jnp.dot(pt_ref[g - 1], o_refs[g][rows, :], preferred_element_type=F32))
                else:
                    og.append(o_refs[g][rows, :].astype(F32))
            for h in range(HEADS_PER_GROUP):
                cols = slice(h * HEAD_DIM, (h + 1) * HEAD_DIM)
                ob = sum(ws[g][rows, h:h + 1] * og[g][:, cols] for g in range(N_DIL))
                ob_ref[rows, cols] = ob.astype(BF16)

    ya = jnp.dot(sg_ref[...], wa_ref[...], preferred_element_type=F32)
    yb = jnp.dot(ob_ref[...], wb_ref[...], preferred_element_type=F32)
    m_ref[...] = (ga_ref[...].astype(F32) * ya + gb_ref[...].astype(F32) * yb).astype(BF16)


def _merge(z, sg, os_, lses, wa, wb, tm, tn, perms_t=None):
    m = z.shape[0]
    assert COL_GA % tn == 0 and COL_GB % tn == 0
    row = lambda width: pl.BlockSpec((tm, width), lambda i, j: (i, 0))
    in_specs = [row(A_WIDTH)] + [row(GROUP_COLS)] * 3 + [row(HEADS_PER_GROUP)] * 3 + [
        pl.BlockSpec((tm, tn), lambda i, j: (i, COL_GA // tn + j)),
        pl.BlockSpec((tm, tn), lambda i, j: (i, COL_GB // tn + j)),
        pl.BlockSpec((A_WIDTH, tn), lambda i, j: (0, j)),
        pl.BlockSpec((GROUP_COLS, tn), lambda i, j: (0, j)),
    ]
    args = [sg, *os_, *lses, z, z, wa, wb]
    if perms_t is not None:
        assert tm % PERM_BLOCK == 0
        in_specs.append(pl.BlockSpec(perms_t.shape, lambda i, j: (0, 0, 0)))
        args.append(perms_t)
    return pl.pallas_call(
        functools.partial(_merge_kernel, permuted=perms_t is not None),
        out_shape=jax.ShapeDtypeStruct((m, D_MODEL), BF16),
        grid=(m // tm, D_MODEL // tn),
        in_specs=in_specs,
        out_specs=pl.BlockSpec((tm, tn), lambda i, j: (i, j)),
        scratch_shapes=[pltpu.VMEM((tm, GROUP_COLS), BF16)],
        compiler_params=_cparams("parallel", "arbitrary"),
        name="branch_merge",
    )(*args)


def _oproj_kernel(m_ref, x_ref, wo_ref, g_ref, wr_ref, br_ref, h_ref, hn_ref, ri_ref, rw_ref):
    h = x_ref[...] + jnp.dot(m_ref[...], wo_ref[...], preferred_element_type=F32)
    h_ref[...] = h
    ms = jnp.mean(h * h, axis=-1, keepdims=True)
    hn = h * lax.rsqrt(ms + RMS_EPS) * g_ref[...]
    _store_token_tiles(hn_ref, hn)
    logits = jnp.dot(hn.astype(BF16), wr_ref[...], preferred_element_type=F32) + br_ref[...]
    lane = lax.broadcasted_iota(jnp.int32, logits.shape, 1).astype(F32)
    big = float(LANES)

    def first_argmax(vals, vmax):
        return jnp.min(jnp.where(vals == vmax, lane, big), axis=-1, keepdims=True)

    lg = jnp.where(lane < N_EXPERT_GROUPS, logits, NEG)
    gmax = jnp.max(lg, axis=-1, keepdims=True)
    gsel = first_argmax(lg, gmax)
    p_sel = 1.0 / jnp.sum(jnp.exp(lg - gmax), axis=-1, keepdims=True)
    lo = N_EXPERT_GROUPS + EXPERTS_PER_GROUP * gsel
    le = jnp.where(jnp.logical_and(lane >= lo, lane < lo + EXPERTS_PER_GROUP), logits, NEG)
    v1 = jnp.max(le, axis=-1, keepdims=True)
    i1 = first_argmax(le, v1)
    le2 = jnp.where(lane == i1, NEG, le)
    v2 = jnp.max(le2, axis=-1, keepdims=True)
    i2 = first_argmax(le2, v2)
    e2 = jnp.exp(v2 - v1)
    w1 = p_sel / (1.0 + e2)
    w2 = p_sel * e2 / (1.0 + e2)
    ri = jnp.where(lane == 0, i1 - N_EXPERT_GROUPS, jnp.where(lane == 1, i2 - N_EXPERT_GROUPS, 0.0))
    ri_ref[...] = ri.astype(jnp.int32)
    rw_ref[...] = jnp.where(lane == 0, w1, jnp.where(lane == 1, w2, 0.0))


def _oproj(mm, x, wo, norm_g, wr, br, tm):
    m = x.shape[0]
    row = lambda width: pl.BlockSpec((tm, width), lambda i: (i, 0))
    full = lambda a, b: pl.BlockSpec((a, b), lambda i: (0, 0))
    return pl.pallas_call(
        _oproj_kernel,
        out_shape=[jax.ShapeDtypeStruct((m, D_MODEL), F32), jax.ShapeDtypeStruct((m * ROW_TILE, LANES), F32),
                   jax.ShapeDtypeStruct((m, LANES), jnp.int32), jax.ShapeDtypeStruct((m, LANES), F32)],
        grid=(m // tm,),
        in_specs=[row(D_MODEL), row(D_MODEL), full(D_MODEL, D_MODEL), full(1, D_MODEL),
                  full(D_MODEL, LANES), full(1, LANES)],
        out_specs=[row(D_MODEL), pl.BlockSpec((tm * ROW_TILE, LANES), lambda i: (i, 0)), row(LANES), row(LANES)],
        compiler_params=_cparams("parallel"),
        name="oproj_router",
    )(mm, x, wo, norm_g, wr, br)


def _dispatch_kernel(slot_ref, pad_start_ref, pad_len_ref, hn_ref, xs_hbm, zbuf, sem, zsem, *, tm, moe_tm, n_tiles):
    i = pl.program_id(0)

    def zero_copy(dst_row, rows):
        return pltpu.make_async_copy(zbuf.at[pl.ds(0, rows * ROW_TILE)],
                                     xs_hbm.at[pl.ds(dst_row * ROW_TILE, rows * ROW_TILE)], zsem)

    def zero_fill(act):
        def per_expert(e, carry):
            off, length = pad_start_ref[e], pad_len_ref[e]
            rows = moe_tm // 2
            while rows >= 1:
                @pl.when((length & rows) != 0)
                def _(off=off, rows=rows):
                    act(zero_copy(off, rows))
                off = off + (length & rows)
                rows //= 2
            return carry

        lax.fori_loop(0, N_EXPERTS, per_expert, 0)

        def per_tile(t, carry):
            act(zero_copy(t * moe_tm, moe_tm))
            return carry

        lax.fori_loop(pad_start_ref[N_EXPERTS], n_tiles, per_tile, 0)

    @pl.when(i == 0)
    def _():
        zbuf[...] = jnp.zeros_like(zbuf)
        zero_fill(lambda c: c.start())

    def issue(r, carry):
        for k in range(TOP_K):
            slot = slot_ref[(i * tm + r) * TOP_K + k]
            pltpu.make_async_copy(hn_ref.at[pl.ds(r * ROW_TILE, ROW_TILE)],
                                  xs_hbm.at[pl.ds(slot * ROW_TILE, ROW_TILE)], sem).start()
        return carry

    lax.fori_loop(0, tm, issue, 0, unroll=8)
    for k in range(TOP_K):
        pltpu.make_async_copy(hn_ref, hn_ref, sem).wait()

    @pl.when(i == pl.num_programs(0) - 1)
    def _():
        zero_fill(lambda c: c.wait())


def _dispatch(hn_tiles, slot, pad_start, pad_len, n_tiles, moe_tm, tm):
    m = hn_tiles.shape[0] // ROW_TILE
    assert m % tm == 0 and moe_tm & (moe_tm - 1) == 0
    grid_spec = pltpu.PrefetchScalarGridSpec(
        num_scalar_prefetch=3,
        grid=(m // tm,),
        in_specs=[pl.BlockSpec((tm * ROW_TILE, LANES), lambda i, s, ps, pn: (i, 0))],
        out_specs=pl.BlockSpec(memory_space=pl.ANY),
        scratch_shapes=[pltpu.VMEM((moe_tm * ROW_TILE, LANES), F32),
                        pltpu.SemaphoreType.DMA(()), pltpu.SemaphoreType.DMA(())],
    )
    return pl.pallas_call(
        functools.partial(_dispatch_kernel, tm=tm, moe_tm=moe_tm, n_tiles=n_tiles),
        out_shape=jax.ShapeDtypeStruct((n_tiles * moe_tm * ROW_TILE, LANES), F32),
        grid_spec=grid_spec,
        compiler_params=_cparams("arbitrary"),
        name="moe_dispatch",
    )(slot, pad_start, pad_len, hn_tiles)


def _moe_kernel(texp_ref, valid_ref, xs_ref, wg_ref, wu_ref, wd_ref, ys_ref):
    i = pl.program_id(0)
    tm = xs_ref.shape[0] // ROW_TILE
    valid = valid_ref[i]

    @pl.when(valid > 0)
    def _():
        x = jnp.concatenate(
            [_load_token_tile_chunk(xs_ref, (), tm, s).astype(BF16) for s in range(ROW_TILE)], axis=1)
        gate = jnp.dot(x, wg_ref[...], preferred_element_type=F32)
        up = jnp.dot(x, wu_ref[...], preferred_element_type=F32)
        hid = (gate * _sigmoid(gate) * up).astype(BF16)
        _store_token_tiles(ys_ref, jnp.dot(hid, wd_ref[...], preferred_element_type=F32))

    @pl.when(valid == 0)
    def _():
        ys_ref[...] = jnp.zeros_like(ys_ref)


def _moe(xs, wg, wu, wd, tile_expert, tile_valid, tm):
    n_tiles = tile_expert.shape[0]
    grid_spec = pltpu.PrefetchScalarGridSpec(
        num_scalar_prefetch=2,
        grid=(n_tiles,),
        in_specs=[
            pl.BlockSpec((tm * ROW_TILE, LANES), lambda i, te, tv: (i, 0)),
            pl.BlockSpec((None, D_MODEL, D_EXPERT), lambda i, te, tv: (te[i], 0, 0)),
            pl.BlockSpec((None, D_MODEL, D_EXPERT), lambda i, te, tv: (te[i], 0, 0)),
            pl.BlockSpec((None, D_EXPERT, D_MODEL), lambda i, te, tv: (te[i], 0, 0)),
        ],
        out_specs=pl.BlockSpec((tm * ROW_TILE, LANES), lambda i, te, tv: (i, 0)),
    )
    return pl.pallas_call(
        _moe_kernel,
        out_shape=jax.ShapeDtypeStruct((n_tiles * tm * ROW_TILE, LANES), F32),
        grid_spec=grid_spec,
        compiler_params=_cparams("arbitrary"),
        name="moe_experts",
    )(tile_expert, tile_valid, xs, wg, wu, wd)


def _combine_kernel(slot_ref, h_ref, rw_ref, g_ref, ys_hbm, y_ref, ybuf, sem, *, tm):
    i = pl.program_id(0)
    n_steps = pl.num_programs(0)

    def issue(tile, buf):
        def body(r, carry):
            for k in range(TOP_K):
                slot = slot_ref[(tile * tm + r) * TOP_K + k]
                pltpu.make_async_copy(ys_hbm.at[pl.ds(slot * ROW_TILE, ROW_TILE)],
                                      ybuf.at[buf, k, pl.ds(r * ROW_TILE, ROW_TILE)], sem.at[buf]).start()
            return carry

        lax.fori_loop(0, tm, body, 0, unroll=8)

    @pl.when(i == 0)
    def _():
        issue(0, 0)

    @pl.when(i + 1 < n_steps)
    def _():
        issue(i + 1, (i + 1) % 2)

    cur = i % 2
    for k in range(TOP_K):
        pltpu.make_async_copy(ybuf.at[cur, k], ybuf.at[cur, k], sem.at[cur]).wait()
    w0, w1 = rw_ref[:, 0:1], rw_ref[:, 1:2]
    chunks, ssq = [], 0.0
    for s in range(ROW_TILE):
        cols = slice(s * LANES, (s + 1) * LANES)
        y = (h_ref[:, cols] + w0 * _load_token_tile_chunk(ybuf, (cur, 0), tm, s)
             + w1 * _load_token_tile_chunk(ybuf, (cur, 1), tm, s))
        chunks.append(y)
        ssq = ssq + jnp.sum(y * y, axis=-1, keepdims=True)
    scale = lax.rsqrt(ssq * (1.0 / D_MODEL) + RMS_EPS)
    for s in range(ROW_TILE):
        cols = slice(s * LANES, (s + 1) * LANES)
        y_ref[:, cols] = chunks[s] * scale * g_ref[:, cols]


def _combine(h, route_w, norm_g, ys, slot, tm):
    m = h.shape[0]
    grid_spec = pltpu.PrefetchScalarGridSpec(
        num_scalar_prefetch=1,
        grid=(m // tm,),
        in_specs=[
            pl.BlockSpec((tm, D_MODEL), lambda i, s: (i, 0)),
            pl.BlockSpec((tm, LANES), lambda i, s: (i, 0)),
            pl.BlockSpec((1, D_MODEL), lambda i, s: (0, 0)),
            pl.BlockSpec(memory_space=pl.ANY),
        ],
        out_specs=pl.BlockSpec((tm, D_MODEL), lambda i, s: (i, 0)),
        scratch_shapes=[pltpu.VMEM((2, TOP_K, tm * ROW_TILE, LANES), F32), pltpu.SemaphoreType.DMA((2,))],
    )
    return pl.pallas_call(
        functools.partial(_combine_kernel, tm=tm),
        out_shape=jax.ShapeDtypeStruct((m, D_MODEL), F32),
        grid_spec=grid_spec,
        compiler_params=_cparams("arbitrary"),
        name="moe_combine",
    )(slot, h, route_w, norm_g, ys)


def _routing_tables(route_i, m, tm):
    n = m * TOP_K
    n_tiles = pl.cdiv(n, tm) + N_EXPERTS - 1
    e_flat = route_i[:, :TOP_K].reshape(n)
    onehot = (e_flat[:, None] == jnp.arange(N_EXPERTS, dtype=jnp.int32)[None, :]).astype(jnp.int32)
    csum = jnp.cumsum(onehot, axis=0)
    rank = jnp.take_along_axis(csum, e_flat[:, None], axis=1)[:, 0] - 1
    counts = csum[-1]
    tiles_e = (counts + tm - 1) // tm
    tile_end = jnp.cumsum(tiles_e)
    tile_start = tile_end - tiles_e
    slot = (tile_start[e_flat] * tm + rank).astype(jnp.int32)
    tile_ids = jnp.arange(n_tiles, dtype=jnp.int32)
    tile_expert = jnp.sum((tile_end[None, :] <= tile_ids[:, None]).astype(jnp.int32), axis=1)
    tile_expert = jnp.minimum(tile_expert, N_EXPERTS - 1)
    tile_valid = jnp.clip(counts[tile_expert] - (tile_ids - tile_start[tile_expert]) * tm, 0, tm)
    tile_valid = jnp.where(tile_ids < tile_end[-1], tile_valid, 0).astype(jnp.int32)
    pad_start = jnp.concatenate([tile_start * tm + counts, tile_end[-1:]]).astype(jnp.int32)
    pad_len = (tiles_e * tm - counts).astype(jnp.int32)
    return slot, tile_expert.astype(jnp.int32), tile_valid, pad_start, pad_len


def _token_tail(x, z, sg, os_, lses, wts, tm, perms_t=None):
    mm = _merge(z, sg, os_, lses, wts["wa"], wts["wb"], tm=min(tm, 256), tn=D_MODEL, perms_t=perms_t)
    h, hn, route_i, route_w = _oproj(mm, x, wts["wo"], wts["norm_ffn"], wts["wr"], wts["br"], tm=min(tm, 256))
    moe_tm = min(tm, MOE_TM)
    slot, tile_expert, tile_valid, pad_start, pad_len = _routing_tables(route_i, x.shape[0], moe_tm)
    xs = _dispatch(hn, slot, pad_start, pad_len, tile_expert.shape[0], moe_tm, tm=min(tm, 256))
    ys = _moe(xs, wts["wg"], wts["wu"], wts["wd"], tile_expert, tile_valid, moe_tm)
    return _combine(h, route_w, wts["norm_final"], ys, slot, tm=min(tm, 256))


def _kv_rows(z, lead, keep_from, group, permuted):
    zz = z.reshape(lead + (IN_COLS,))[:, keep_from:]
    k = zz[..., COL_K + group * GROUP_COLS: COL_K + (group + 1) * GROUP_COLS]
    v = zz[..., COL_VB + group * GROUP_COLS: COL_VB + (group + 1) * GROUP_COLS]
    kv = jnp.stack([k, v], axis=2).astype(F32)
    dil = DILATED_GROUPS[group][1]
    if permuted and dil > 1:
        assert keep_from % PERM_BLOCK == 0 and kv.shape[1] % PERM_BLOCK == 0
        b, rows = kv.shape[:2]
        kv = kv.reshape(b, rows // PERM_BLOCK, dil, PERM_BLOCK // dil, 2, GROUP_COLS)
        kv = kv.transpose(0, 1, 3, 2, 4, 5).reshape(b, rows, 2, GROUP_COLS)
    return kv.reshape(kv.shape[:3] + (HEADS_PER_GROUP, HEAD_DIM))


def kernel(x_prompt, x_sample, cache_kv_w128, cache_kv_w512, cache_kv_w2048, norm_mix, w_in, ln_v_g, ln_v_b, w_s, b_s, w_a_out, w_b_out, w_o, norm_ffn, w_route_group, b_route_group, w_route_expert, b_route_expert, w_gate_e, w_up_e, w_down_e, norm_final):
    assert norm_mix.shape[0] == 1, "single-layer trunk"
    batch, seq, _ = x_prompt.shape
    bd, n_new, _ = x_sample.shape
    caches = (cache_kv_w128, cache_kv_w512, cache_kv_w2048)

    pad = LANES - N_EXPERT_GROUPS - N_EXPERTS
    wr = jnp.concatenate([w_route_group[0], w_route_expert[0], jnp.zeros((D_MODEL, pad), F32)], axis=1)
    br = jnp.concatenate([b_route_group[0], b_route_expert[0], jnp.zeros((pad,), F32)])[None, :]
    wts = dict(
        wa=w_a_out[0].astype(BF16), wb=w_b_out[0].astype(BF16), wo=w_o[0].astype(BF16),
        wg=w_gate_e[0].astype(BF16), wu=w_up_e[0].astype(BF16), wd=w_down_e[0].astype(BF16),
        wr=wr.astype(BF16), br=br, norm_ffn=norm_ffn[0][None, :], norm_final=norm_final[None, :],
    )
    w_in_b = w_in[0].astype(BF16).reshape(D_MODEL, IN_COLS // IN_TN, IN_TN).transpose(1, 0, 2)
    norm_g = norm_mix[0][None, :]
    ln_g, ln_b = ln_v_g[0][None, :], ln_v_b[0][None, :]

    xp = x_prompt.reshape(batch * seq, D_MODEL)
    perms = jnp.stack([_class_major_perm(dil) for _, dil in DILATED_GROUPS[1:]])
    zp = _inproj(xp, norm_g, w_in_b, tm=1024, perms=perms)
    (sgp,) = _sgu(zp, ln_g, ln_b, w_s[0], b_s[0].T, chunks=4, emit_vn=False)
    os_p, lses_p = zip(*[_attn_group(zp, batch, seq, g) for g in range(N_DIL)])
    y_prompt = _token_tail(xp, zp, sgp, os_p, lses_p, wts, tm=512,
                           perms_t=perms.transpose(0, 2, 1)).reshape(batch, seq, D_MODEL)
    kv_prompt = [_kv_rows(zp, (batch, seq), seq - min(win, seq), g, permuted=True)[None]
                 for g, (win, _) in enumerate(DILATED_GROUPS)]

    ms = bd * n_new
    assert ms == CHUNK and n_new <= CHUNK
    xs = x_sample.reshape(ms, D_MODEL)
    zs = _inproj(xs, norm_g, w_in_b, tm=ms)
    eye = jnp.eye(bd, dtype=F32)
    ws_s = jnp.einsum("ab,gts->gatbs", eye, w_s[0][:, :n_new, :n_new]).reshape(A_GROUPS, ms, ms)
    bst_s = jnp.tile(b_s[0][:, :n_new].T, (bd, 1))
    sgs, vns = _sgu(zs, ln_g, ln_b, ws_s, bst_s, chunks=1, emit_vn=True)
    zs3 = zs.reshape(bd, n_new, IN_COLS)
    qkv = [zs3[..., c:c + B_WIDTH].astype(F32).reshape(bd, n_new, N_DIL * HEADS_PER_GROUP, HEAD_DIM)
           for c in (COL_Q, COL_K, COL_VB)]
    o_s, lse_s = _attn_sample(*qkv, caches)
    os_s = [o_s[g].reshape(ms, GROUP_COLS).astype(BF16) for g in range(N_DIL)]
    lses_s = [lse_s[g, ..., 0].reshape(ms, HEADS_PER_GROUP) for g in range(N_DIL)]
    y_sample = _token_tail(xs, zs, sgs, os_s, lses_s, wts, tm=ms).reshape(bd, n_new, D_MODEL)
    kv_sample = [_kv_rows(zs, (bd, n_new), 0, g, permuted=False)[None] for g in range(N_DIL)]
    chunk_v = vns.reshape(1, bd, n_new, A_WIDTH)

    return (y_prompt, y_sample, *kv_prompt, *kv_sample, chunk_v)
```

```python
import functools

import jax
import jax.numpy as jnp
from jax import lax
from jax.experimental import pallas as pl
from jax.experimental.pallas import tpu as pltpu

F32 = jnp.float32
BF16 = jnp.bfloat16

D_MODEL = 2048
CHUNK = 128
A_GROUPS = 16
A_GROUP_DIM = 128
A_WIDTH = A_GROUPS * A_GROUP_DIM
HEAD_DIM = 128
HEADS_PER_GROUP = 4
GROUP_COLS = HEADS_PER_GROUP * HEAD_DIM
DILATED_GROUPS = ((128, 1), (512, 4), (2048, 16))
N_DIL = len(DILATED_GROUPS)
B_WIDTH = N_DIL * GROUP_COLS
IN_COLS = 2 * A_WIDTH + 3 * B_WIDTH + 2 * D_MODEL
COL_U, COL_V = 0, A_WIDTH
COL_GA = 2 * A_WIDTH
COL_GB = COL_GA + D_MODEL
QKV_COLS = 3 * GROUP_COLS
N_EXPERT_GROUPS = 4
EXPERTS_PER_GROUP = 8
N_EXPERTS = N_EXPERT_GROUPS * EXPERTS_PER_GROUP
TOP_K = 2
D_EXPERT = 256
RMS_EPS = 1e-6
LN_EPS = 1e-5
NEG = -1e30
ATTN_SCALE = HEAD_DIM ** -0.5

LANES = 128
VMEM_LIMIT_BYTES = 56 * 1024 * 1024
IN_TN = 512
MOE_TM = 256
PERM_BLOCK = 256
ROW_TILE = D_MODEL // LANES

def _cparams(*sem):
    return pltpu.CompilerParams(dimension_semantics=sem, vmem_limit_bytes=VMEM_LIMIT_BYTES)


def _gelu_tanh(x):
    return 0.5 * x * (1.0 + jnp.tanh(0.7978845608028654 * (x + 0.044715 * (x * x * x))))


def _sigmoid(x):
    return 1.0 / (1.0 + jnp.exp(-x))


def _store_token_tiles(ref, val):
    rows = val.shape[0]
    for s in range(ROW_TILE):
        ref[pl.ds(s, rows, stride=ROW_TILE), :] = val[:, s * LANES:(s + 1) * LANES]


def _load_token_tile_chunk(ref, lead, rows, s):
    return ref[lead + (pl.ds(s, rows, stride=ROW_TILE), slice(None))]


def _class_major_perm(dil):
    out_row = jnp.arange(PERM_BLOCK, dtype=jnp.int32)
    src = (out_row % (PERM_BLOCK // dil)) * dil + out_row // (PERM_BLOCK // dil)
    return (src[:, None] == jnp.arange(PERM_BLOCK, dtype=jnp.int32)[None, :]).astype(BF16)


def _inproj_kernel(x_ref, g_ref, *rest, strips, plan, permute):
    w_refs, rest = rest[:strips], rest[strips:]
    if permute:
        perm_ref, z_ref, xn_ref = rest
    else:
        z_ref, xn_ref = rest
    j = pl.program_id(1)

    @pl.when(j == 0)
    def _():
        x = x_ref[...]
        ms = jnp.mean(x * x, axis=-1, keepdims=True)
        xn_ref[...] = (x * lax.rsqrt(ms + RMS_EPS) * g_ref[...]).astype(BF16)

    tm = xn_ref.shape[0]
    sub = min(tm, PERM_BLOCK)

    def emit(epilogue):
        for blk in range(tm // sub):
            rows = slice(blk * sub, (blk + 1) * sub)
            for s, w_ref in enumerate(w_refs):
                z = jnp.dot(xn_ref[rows, :], w_ref[...], preferred_element_type=F32)
                z_ref[rows, s * IN_TN:(s + 1) * IN_TN] = epilogue(z).astype(BF16)

    def epilogue_of(kind):
        if kind == "gelu":
            return _gelu_tanh
        if kind == "gate":
            return _sigmoid
        if kind == "plain":
            return lambda z: z
        k = int(kind[len("perm"):])
        return lambda z: jnp.dot(perm_ref[k], z.astype(BF16), preferred_element_type=F32)

    for kind in sorted(set(plan)):
        tiles = [t for t, p in enumerate(plan) if p == kind]
        hit = functools.reduce(jnp.logical_or, [j == t for t in tiles])

        @pl.when(hit)
        def _():
            emit(epilogue_of(kind))


def _inproj(x, norm_g, w_in_bf16, tm, strips, plan, ref_tile, perms=None):
    m = x.shape[0]
    tn = strips * IN_TN
    in_specs = [pl.BlockSpec((tm, D_MODEL), lambda i, j: (i, 0)), pl.BlockSpec((1, D_MODEL), lambda i, j: (0, 0))]
    in_specs += [pl.BlockSpec((D_MODEL, IN_TN), functools.partial(lambda i, j, s: (0, ref_tile(j, s)), s=s))
                 for s in range(strips)]
    args = [x, norm_g] + [w_in_bf16] * strips
    if perms is not None:
        in_specs.append(pl.BlockSpec(perms.shape, lambda i, j: (0, 0, 0)))
        args.append(perms)
    return pl.pallas_call(
        functools.partial(_inproj_kernel, strips=strips, plan=plan, permute=perms is not None),
        out_shape=jax.ShapeDtypeStruct((m, len(plan) * tn), BF16),
        grid=(m // tm, len(plan)),
        in_specs=in_specs,
        out_specs=pl.BlockSpec((tm, tn), lambda i, j: (i, j)),
        scratch_shapes=[pltpu.VMEM((tm, D_MODEL), BF16)],
        compiler_params=_cparams("parallel", "arbitrary"),
        name="inproj",
    )(*args)


def _inproj_all(x, norm_g, w_in_bf16, tm, perms=None):
    n_gelu = 2 * A_WIDTH // IN_TN
    n_qkv = 3 * B_WIDTH // IN_TN
    strips_a = 2
    plan_a = ("gelu",) * (n_gelu // strips_a) + ("gate",) * (2 * D_MODEL // IN_TN // strips_a)

    def ref_a(j, s):
        t = j * strips_a + s
        return jnp.where(t < n_gelu, t, t + n_qkv)

    z_a = _inproj(x, norm_g, w_in_bf16, tm, strips_a, plan_a, ref_a)
    plan_b = ("plain",) + tuple(f"perm{k}" for k in range(N_DIL - 1)) if perms is not None else ("plain",) * N_DIL
    z_b = _inproj(x, norm_g, w_in_bf16, tm, 3, plan_b, lambda j, s: n_gelu + s * N_DIL + j, perms)
    return z_a, z_b


def _sgu_kernel(u_ref, v_ref, lng_ref, lnb_ref, ws_ref, bst_ref, sg_ref, *vn_out, chunks):
    row = lax.broadcasted_iota(jnp.int32, (CHUNK, CHUNK), 0)
    col = lax.broadcasted_iota(jnp.int32, (CHUNK, CHUNK), 1)
    tri = row >= col
    ws = [jnp.where(tri, ws_ref[g], 0.0).astype(BF16) for g in range(A_GROUPS)]
    for c in range(chunks):
        rows = slice(c * CHUNK, (c + 1) * CHUNK)
        v = v_ref[rows, :].astype(F32)
        mu = jnp.mean(v, axis=-1, keepdims=True)
        vc = v - mu
        var = jnp.mean(vc * vc, axis=-1, keepdims=True)
        vn = vc * lax.rsqrt(var + LN_EPS) * lng_ref[...] + lnb_ref[...]
        if vn_out:
            vn_out[0][rows, :] = vn
        vnb = vn.astype(BF16)
        for g in range(A_GROUPS):
            cols = slice(g * A_GROUP_DIM, (g + 1) * A_GROUP_DIM)
            s = jnp.dot(ws[g], vnb[:, cols], preferred_element_type=F32) + bst_ref[:, g:g + 1]
            sg_ref[rows, cols] = (u_ref[rows, cols].astype(F32) * s).astype(BF16)


def _sgu(z, ln_g, ln_b, ws, bst, chunks, emit_vn):
    m = z.shape[0]
    tm = chunks * CHUNK
    out_shape = [jax.ShapeDtypeStruct((m, A_WIDTH), BF16)]
    out_specs = [pl.BlockSpec((tm, A_WIDTH), lambda i: (i, 0))]
    if emit_vn:
        out_shape.append(jax.ShapeDtypeStruct((m, A_WIDTH), F32))
        out_specs.append(pl.BlockSpec((tm, A_WIDTH), lambda i: (i, 0)))
    return pl.pallas_call(
        functools.partial(_sgu_kernel, chunks=chunks),
        out_shape=out_shape,
        grid=(m // tm,),
        in_specs=[
            pl.BlockSpec((tm, A_WIDTH), lambda i: (i, COL_U // A_WIDTH)),
            pl.BlockSpec((tm, A_WIDTH), lambda i: (i, COL_V // A_WIDTH)),
            pl.BlockSpec((1, A_WIDTH), lambda i: (0, 0)),
            pl.BlockSpec((1, A_WIDTH), lambda i: (0, 0)),
            pl.BlockSpec((A_GROUPS, CHUNK, CHUNK), lambda i: (0, 0, 0)),
            pl.BlockSpec((CHUNK, A_GROUPS), lambda i: (0, 0)),
        ],
        out_specs=out_specs,
        compiler_params=_cparams("parallel"),
        name="sgu",
    )(z, z, ln_g, ln_b, ws, bst)


def _attn_kernel(q_ref, kc_ref, kp_ref, vc_ref, vp_ref, o_ref, lse_ref, *, qblocks):
    i = pl.program_id(2)
    qi = lax.broadcasted_iota(jnp.int32, (CHUNK, CHUNK), 0)
    ki = lax.broadcasted_iota(jnp.int32, (CHUNK, CHUNK), 1)
    cur_mask = ki <= qi
    no_prev = jnp.where(i == 0, CHUNK, 0)
    dn = (((1,), (1,)), ((), ()))
    rpc = q_ref.shape[1]
    nbq = CHUNK // rpc

    def load(ref, a, cols):
        return ref[a * nbq:(a + 1) * nbq, :, cols].reshape(CHUNK, HEAD_DIM)

    for a in range(qblocks):
        rows = slice(a * CHUNK, (a + 1) * CHUNK)
        for h in range(HEADS_PER_GROUP):
            cols = slice(h * HEAD_DIM, (h + 1) * HEAD_DIM)
            q = load(q_ref, a, cols)
            kc = load(kc_ref, a, cols)
            vc = load(vc_ref, a, cols)
            if a == 0:
                kp, vp = load(kp_ref, 0, cols), load(vp_ref, 0, cols)
                prev_mask = ki >= qi + no_prev
            else:
                kp, vp = load(kc_ref, a - 1, cols), load(vc_ref, a - 1, cols)
                prev_mask = ki >= qi
            s_c = lax.dot_general(q, kc, dn, preferred_element_type=F32) * ATTN_SCALE
            s_p = lax.dot_general(q, kp, dn, preferred_element_type=F32) * ATTN_SCALE
            s_c = jnp.where(cur_mask, s_c, NEG)
            s_p = jnp.where(prev_mask, s_p, NEG)
            mx = jnp.maximum(jnp.max(s_c, axis=-1, keepdims=True), jnp.max(s_p, axis=-1, keepdims=True))
            p_c = jnp.exp(s_c - mx)
            p_p = jnp.exp(s_p - mx)
            l = jnp.sum(p_c, axis=-1, keepdims=True) + jnp.sum(p_p, axis=-1, keepdims=True)
            acc = jnp.dot(p_c.astype(BF16), vc, preferred_element_type=F32)
            acc = acc + jnp.dot(p_p.astype(BF16), vp, preferred_element_type=F32)
            o_ref[a * nbq:(a + 1) * nbq, :, cols] = (acc / l).astype(BF16).reshape(nbq, rpc, HEAD_DIM)
            lse_ref[rows, h:h + 1] = mx + jnp.log(l)


def _rows_per_class(dil):
    return min(PERM_BLOCK // dil, CHUNK)


def _attn_group(z, batch, seq, group):
    _, dil = DILATED_GROUPS[group]
    sub = seq // dil
    rpc = _rows_per_class(dil)
    qblocks = 2 if sub % (2 * CHUNK) == 0 else 1
    tq = qblocks * CHUNK
    cq, ck, cv = (group * QKV_COLS // GROUP_COLS + part for part in range(3))
    zv = z.reshape(batch, sub // rpc, dil, rpc, z.shape[-1])

    def cur(cb):
        return pl.BlockSpec((None, tq // rpc, None, rpc, GROUP_COLS), lambda b, r, i: (b, i, r, 0, cb))

    def prev(cb):
        return pl.BlockSpec((None, CHUNK // rpc, None, rpc, GROUP_COLS),
                            lambda b, r, i: (b, jnp.maximum(i * qblocks - 1, 0), r, 0, cb))

    o, lse = pl.pallas_call(
        functools.partial(_attn_kernel, qblocks=qblocks),
        out_shape=[jax.ShapeDtypeStruct((batch, sub // rpc, dil, rpc, GROUP_COLS), BF16),
                   jax.ShapeDtypeStruct((batch, dil, sub, HEADS_PER_GROUP), F32)],
        grid=(batch, dil, sub // tq),
        in_specs=[cur(cq), cur(ck), prev(ck), cur(cv), prev(cv)],
        out_specs=[pl.BlockSpec((None, tq // rpc, None, rpc, GROUP_COLS), lambda b, r, i: (b, i, r, 0, 0)),
                   pl.BlockSpec((None, None, tq, HEADS_PER_GROUP), lambda b, r, i: (b, r, i, 0))],
        compiler_params=_cparams("parallel", "parallel", "arbitrary"),
        name=f"attn_prompt_g{group}",
    )(zv, zv, zv, zv, zv)
    o = o.reshape(batch * seq, GROUP_COLS)
    lse = lse.transpose(0, 2, 1, 3).reshape(batch * seq, HEADS_PER_GROUP)
    return o, lse


def _attn_sample_kernel(q_ref, k_ref, v_ref, c0_ref, c1_ref, c2_ref, o_ref, lse_ref, *, n_new):
    caches = (c0_ref, c1_ref, c2_ref)
    rowid = lax.broadcasted_iota(jnp.int32, (CHUNK, HEADS_PER_GROUP, 1), 0)
    for g, (win, dil) in enumerate(DILATED_GROUPS):
        cref = caches[g]
        hs = slice(g * HEADS_PER_GROUP, (g + 1) * HEADS_PER_GROUP)
        for t in range(n_new):
            res, first = t % dil, t // dil
            kc = cref[:, res, 0]
            vc = cref[:, res, 1]
            q = q_ref[t, hs, :]
            s = jnp.sum(kc * q[None], axis=-1, keepdims=True) * ATTN_SCALE
            if first > 0:
                s = jnp.where(rowid >= first, s, NEG)
            new_rows = [tn for tn in range(t + 1) if (t - tn) % dil == 0 and (t - tn) // dil <= win // dil]
            s_new = [jnp.sum(q * k_ref[tn, hs, :], axis=-1, keepdims=True) * ATTN_SCALE for tn in new_rows]
            mx = jnp.max(s, axis=0)
            for sn in s_new:
                mx = jnp.maximum(mx, sn)
            p = jnp.exp(s - mx[None])
            l = jnp.sum(p, axis=0)
            acc = jnp.sum(p * vc, axis=0)
            for tn, sn in zip(new_rows, s_new):
                pn = jnp.exp(sn - mx)
                l = l + pn
                acc = acc + pn * v_ref[tn, hs, :]
            o_ref[g, t] = acc / l
            lse_ref[g, t] = jnp.broadcast_to(mx + jnp.log(l), (HEADS_PER_GROUP, HEAD_DIM))


def _attn_sample(q, k, v, caches):
    bd, n_new = q.shape[:2]
    views, specs = [], []
    for (win, dil), c in zip(DILATED_GROUPS, caches):
        assert c.shape[2] == win and win == CHUNK * dil, "cache must hold exactly one full window"
        used = min(dil, n_new)
        views.append(c.reshape(1, bd, CHUNK, dil, 2, HEADS_PER_GROUP, HEAD_DIM))
        specs.append(pl.BlockSpec((None, None, CHUNK, used, 2, HEADS_PER_GROUP, HEAD_DIM),
                                  lambda b: (0, b, 0, 0, 0, 0, 0)))
    new_spec = pl.BlockSpec((None, n_new, N_DIL * HEADS_PER_GROUP, HEAD_DIM), lambda b: (b, 0, 0, 0))
    out_sds = jax.ShapeDtypeStruct((N_DIL, bd, n_new, HEADS_PER_GROUP, HEAD_DIM), F32)
    out_spec = pl.BlockSpec((N_DIL, None, n_new, HEADS_PER_GROUP, HEAD_DIM), lambda b: (0, b, 0, 0, 0))
    return pl.pallas_call(
        functools.partial(_attn_sample_kernel, n_new=n_new),
        out_shape=[out_sds, out_sds],
        grid=(bd,),
        in_specs=[new_spec, new_spec, new_spec] + specs,
        out_specs=[out_spec, out_spec],
        compiler_params=_cparams("parallel"),
        name="attn_sample",
    )(q, k, v, *views)


def _merge_kernel(sg_ref, o0_ref, o1_ref, o2_ref, l0_ref, l1_ref, l2_ref, ga_ref, gb_ref,
                  wa_ref, wb_ref, *rest, permuted):
    if permuted:
        pt_ref, m_ref, ob_ref = rest
    else:
        m_ref, ob_ref = rest
    j = pl.program_id(1)

    @pl.when(j == 0)
    def _():
        l0, l1, l2 = l0_ref[...], l1_ref[...], l2_ref[...]
        mx = jnp.maximum(jnp.maximum(l0, l1), l2)
        e0, e1, e2 = jnp.exp(l0 - mx), jnp.exp(l1 - mx), jnp.exp(l2 - mx)
        inv = 1.0 / (e0 + e1 + e2)
        ws = (e0 * inv, e1 * inv, e2 * inv)
        o_refs = (o0_ref, o1_ref, o2_ref)
        tm = ob_ref.shape[0]
        for blk in range(max(tm // PERM_BLOCK, 1)):
            rows = slice(blk * PERM_BLOCK, min((blk + 1) * PERM_BLOCK, tm))
            og = []
            for g in range(N_DIL):
                if permuted and g > 0:
                    og.append(jnp.dot(pt_ref[g - 1], o_refs[g][rows, :], preferred_element_type=F32))
                else:
                    og.append(o_refs[g][rows, :].astype(F32))
            for h in range(HEADS_PER_GROUP):
                cols = slice(h * HEAD_DIM, (h + 1) * HEAD_DIM)
                ob = sum(ws[g][rows, h:h + 1] * og[g][:, cols] for g in range(N_DIL))
                ob_ref[rows, cols] = ob.astype(BF16)

    ya = jnp.dot(sg_ref[...], wa_ref[...], preferred_element_type=F32)
    yb = jnp.dot(ob_ref[...], wb_ref[...], preferred_element_type=F32)
    m_ref[...] = (ga_ref[...].astype(F32) * ya + gb_ref[...].astype(F32) * yb).astype(BF16)


def _merge(z, sg, os_, lses, wa, wb, tm, tn, perms_t=None):
    m = z.shape[0]
    assert COL_GA % tn == 0 and COL_GB % tn == 0
    row = lambda width: pl.BlockSpec((tm, width), lambda i, j: (i, 0))
    in_specs = [row(A_WIDTH)] + [row(GROUP_COLS)] * 3 + [row(HEADS_PER_GROUP)] * 3 + [
        pl.BlockSpec((tm, tn), lambda i, j: (i, COL_GA // tn + j)),
        pl.BlockSpec((tm, tn), lambda i, j: (i, COL_GB // tn + j)),
        pl.BlockSpec((A_WIDTH, tn), lambda i, j: (0, j)),
        pl.BlockSpec((GROUP_COLS, tn), lambda i, j: (0, j)),
    ]
    args = [sg, *os_, *lses, z, z, wa, wb]
    if perms_t is not None:
        assert tm % PERM_BLOCK == 0
        in_specs.append(pl.BlockSpec(perms_t.shape, lambda i, j: (0, 0, 0)))
        args.append(perms_t)
    return pl.pallas_call(
        functools.partial(_merge_kernel, permuted=perms_t is not None),
        out_shape=jax.ShapeDtypeStruct((m, D_MODEL), BF16),
        grid=(m // tm, D_MODEL // tn),
        in_specs=in_specs,
        out_specs=pl.BlockSpec((tm, tn), lambda i, j: (i, j)),
        scratch_shapes=[pltpu.VMEM((tm, GROUP_COLS), BF16)],
        compiler_params=_cparams("parallel", "arbitrary"),
        name="branch_merge",
    )(*args)


def _oproj_kernel(m_ref, x_ref, wo_ref, g_ref, wr_ref, br_ref, h_ref, hn_ref, ri_ref, rw_ref):
    h = x_ref[...] + jnp.dot(m_ref[...], wo_ref[...], preferred_element_type=F32)
    h_ref[...] = h
    ms = jnp.mean(h * h, axis=-1, keepdims=True)
    hn = h * lax.rsqrt(ms + RMS_EPS) * g_ref[...]
    _store_token_tiles(hn_ref, hn)
    logits = jnp.dot(hn.astype(BF16), wr_ref[...], preferred_element_type=F32) + br_ref[...]
    lane = lax.broadcasted_iota(jnp.int32, logits.shape, 1).astype(F32)
    big = float(LANES)

    def first_argmax(vals, vmax):
        return jnp.min(jnp.where(vals == vmax, lane, big), axis=-1, keepdims=True)

    lg = jnp.where(lane < N_EXPERT_GROUPS, logits, NEG)
    gmax = jnp.max(lg, axis=-1, keepdims=True)
    gsel = first_argmax(lg, gmax)
    p_sel = 1.0 / jnp.sum(jnp.exp(lg - gmax), axis=-1, keepdims=True)
    lo = N_EXPERT_GROUPS + EXPERTS_PER_GROUP * gsel
    le = jnp.where(jnp.logical_and(lane >= lo, lane < lo + EXPERTS_PER_GROUP), logits, NEG)
    v1 = jnp.max(le, axis=-1, keepdims=True)
    i1 = first_argmax(le, v1)
    le2 = jnp.where(lane == i1, NEG, le)
    v2 = jnp.max(le2, axis=-1, keepdims=True)
    i2 = first_argmax(le2, v2)
    e2 = jnp.exp(v2 - v1)
    w1 = p_sel / (1.0 + e2)
    w2 = p_sel * e2 / (1.0 + e2)
    ri = jnp.where(lane == 0, i1 - N_EXPERT_GROUPS, jnp.where(lane == 1, i2 - N_EXPERT_GROUPS, 0.0))
    ri_ref[...] = ri.astype(jnp.int32)
    rw_ref[...] = jnp.where(lane == 0, w1, jnp.where(lane == 1, w2, 0.0))


def _oproj(mm, x, wo, norm_g, wr, br, tm):
    m = x.shape[0]
    row = lambda width: pl.BlockSpec((tm, width), lambda i: (i, 0))
    full = lambda a, b: pl.BlockSpec((a, b), lambda i: (0, 0))
    return pl.pallas_call(
        _oproj_kernel,
        out_shape=[jax.ShapeDtypeStruct((m, D_MODEL), F32), jax.ShapeDtypeStruct((m * ROW_TILE, LANES), F32),
                   jax.ShapeDtypeStruct((m, LANES), jnp.int32), jax.ShapeDtypeStruct((m, LANES), F32)],
        grid=(m // tm,),
        in_specs=[row(D_MODEL), row(D_MODEL), full(D_MODEL, D_MODEL), full(1, D_MODEL),
                  full(D_MODEL, LANES), full(1, LANES)],
        out_specs=[row(D_MODEL), pl.BlockSpec((tm * ROW_TILE, LANES), lambda i: (i, 0)), row(LANES), row(LANES)],
        compiler_params=_cparams("parallel"),
        name="oproj_router",
    )(mm, x, wo, norm_g, wr, br)


def _dispatch_kernel(slot_ref, pad_start_ref, pad_len_ref, hn_ref, xs_hbm, zbuf, sem, zsem, *, tm, moe_tm, n_tiles):
    i = pl.program_id(0)

    def zero_copy(dst_row, rows):
        return pltpu.make_async_copy(zbuf.at[pl.ds(0, rows * ROW_TILE)],
                                     xs_hbm.at[pl.ds(dst_row * ROW_TILE, rows * ROW_TILE)], zsem)

    def zero_fill(act):
        def per_expert(e, carry):
            off, length = pad_start_ref[e], pad_len_ref[e]
            rows = moe_tm // 2
            while rows >= 1:
                @pl.when((length & rows) != 0)
                def _(off=off, rows=rows):
                    act(zero_copy(off, rows))
                off = off + (length & rows)
                rows //= 2
            return carry

        lax.fori_loop(0, N_EXPERTS, per_expert, 0)

        def per_tile(t, carry):
            act(zero_copy(t * moe_tm, moe_tm))
            return carry

        lax.fori_loop(pad_start_ref[N_EXPERTS], n_tiles, per_tile, 0)

    @pl.when(i == 0)
    def _():
        zbuf[...] = jnp.zeros_like(zbuf)
        zero_fill(lambda c: c.start())

    def issue(r, carry):
        for k in range(TOP_K):
            slot = slot_ref[(i * tm + r) * TOP_K + k]
            pltpu.make_async_copy(hn_ref.at[pl.ds(r * ROW_TILE, ROW_TILE)],
                                  xs_hbm.at[pl.ds(slot * ROW_TILE, ROW_TILE)], sem).start()
        return carry

    lax.fori_loop(0, tm, issue, 0, unroll=8)
    for k in range(TOP_K):
        pltpu.make_async_copy(hn_ref, hn_ref, sem).wait()

    @pl.when(i == pl.num_programs(0) - 1)
    def _():
        zero_fill(lambda c: c.wait())


def _dispatch(hn_tiles, slot, pad_start, pad_len, n_tiles, moe_tm, tm):
    m = hn_tiles.shape[0] // ROW_TILE
    assert m % tm == 0 and moe_tm & (moe_tm - 1) == 0
    grid_spec = pltpu.PrefetchScalarGridSpec(
        num_scalar_prefetch=3,
        grid=(m // tm,),
        in_specs=[pl.BlockSpec((tm * ROW_TILE, LANES), lambda i, s, ps, pn: (i, 0))],
        out_specs=pl.BlockSpec(memory_space=pl.ANY),
        scratch_shapes=[pltpu.VMEM((moe_tm * ROW_TILE, LANES), F32),
                        pltpu.SemaphoreType.DMA(()), pltpu.SemaphoreType.DMA(())],
    )
    return pl.pallas_call(
        functools.partial(_dispatch_kernel, tm=tm, moe_tm=moe_tm, n_tiles=n_tiles),
        out_shape=jax.ShapeDtypeStruct((n_tiles * moe_tm * ROW_TILE, LANES), F32),
        grid_spec=grid_spec,
        compiler_params=_cparams("arbitrary"),
        name="moe_dispatch",
    )(slot, pad_start, pad_len, hn_tiles)


def _moe_kernel(texp_ref, valid_ref, xs_ref, wg_ref, wu_ref, wd_ref, ys_ref):
    i = pl.program_id(0)
    tm = xs_ref.shape[0] // ROW_TILE
    valid = valid_ref[i]

    @pl.when(valid > 0)
    def _():
        x = jnp.concatenate(
            [_load_token_tile_chunk(xs_ref, (), tm, s).astype(BF16) for s in range(ROW_TILE)], axis=1)
        gate = jnp.dot(x, wg_ref[...], preferred_element_type=F32)
        up = jnp.dot(x, wu_ref[...], preferred_element_type=F32)
        hid = (gate * _sigmoid(gate) * up).astype(BF16)
        _store_token_tiles(ys_ref, jnp.dot(hid, wd_ref[...], preferred_element_type=F32))

    @pl.when(valid == 0)
    def _():
        ys_ref[...] = jnp.zeros_like(ys_ref)


def _moe(xs, wg, wu, wd, tile_expert, tile_valid, tm):
    n_tiles = tile_expert.shape[0]
    grid_spec = pltpu.PrefetchScalarGridSpec(
        num_scalar_prefetch=2,
        grid=(n_tiles,),
        in_specs=[
            pl.BlockSpec((tm * ROW_TILE, LANES), lambda i, te, tv: (i, 0)),
            pl.BlockSpec((None, D_MODEL, D_EXPERT), lambda i, te, tv: (te[i], 0, 0)),
            pl.BlockSpec((None, D_MODEL, D_EXPERT), lambda i, te, tv: (te[i], 0, 0)),
            pl.BlockSpec((None, D_EXPERT, D_MODEL), lambda i, te, tv: (te[i], 0, 0)),
        ],
        out_specs=pl.BlockSpec((tm * ROW_TILE, LANES), lambda i, te, tv: (i, 0)),
    )
    return pl.pallas_call(
        _moe_kernel,
        out_shape=jax.ShapeDtypeStruct((n_tiles * tm * ROW_TILE, LANES), F32),
        grid_spec=grid_spec,
        compiler_params=_cparams("arbitrary"),
        name="moe_experts",
    )(tile_expert, tile_valid, xs, wg, wu, wd)


def _combine_kernel(slot_ref, h_ref, rw_ref, g_ref, ys_hbm, y_ref, ybuf, sem, *, tm):
    i = pl.program_id(0)
    n_steps = pl.num_programs(0)

    def issue(tile, buf):
        def body(r, carry):
            for k in range(TOP_K):
                slot = slot_ref[(tile * tm + r) * TOP_K + k]
                pltpu.make_async_copy(ys_hbm.at[pl.ds(slot * ROW_TILE, ROW_TILE)],
                                      ybuf.at[buf, k, pl.ds(r * ROW_TILE, ROW_TILE)], sem.at[buf]).start()
            return carry

        lax.fori_loop(0, tm, body, 0, unroll=8)

    @pl.when(i == 0)
    def _():
        issue(0, 0)

    @pl.when(i + 1 < n_steps)
    def _():
        issue(i + 1, (i + 1) % 2)

    cur = i % 2
    for k in range(TOP_K):
        pltpu.make_async_copy(ybuf.at[cur, k], ybuf.at[cur, k], sem.at[cur]).wait()
    w0, w1 = rw_ref[:, 0:1], rw_ref[:, 1:2]
    chunks, ssq = [], 0.0
    for s in range(ROW_TILE):
        cols = slice(s * LANES, (s + 1) * LANES)
        y = (h_ref[:, cols] + w0 * _load_token_tile_chunk(ybuf, (cur, 0), tm, s)
             + w1 * _load_token_tile_chunk(ybuf, (cur, 1), tm, s))
        chunks.append(y)
        ssq = ssq + jnp.sum(y * y, axis=-1, keepdims=True)
    scale = lax.rsqrt(ssq * (1.0 / D_MODEL) + RMS_EPS)
    for s in range(ROW_TILE):
        cols = slice(s * LANES, (s + 1) * LANES)
        y_ref[:, cols] = chunks[s] * scale * g_ref[:, cols]


def _combine(h, route_w, norm_g, ys, slot, tm):
    m = h.shape[0]
    grid_spec = pltpu.PrefetchScalarGridSpec(
        num_scalar_prefetch=1,
        grid=(m // tm,),
        in_specs=[
            pl.BlockSpec((tm, D_MODEL), lambda i, s: (i, 0)),
            pl.BlockSpec((tm, LANES), lambda i, s: (i, 0)),
            pl.BlockSpec((1, D_MODEL), lambda i, s: (0, 0)),
            pl.BlockSpec(memory_space=pl.ANY),
        ],
        out_specs=pl.BlockSpec((tm, D_MODEL), lambda i, s: (i, 0)),
        scratch_shapes=[pltpu.VMEM((2, TOP_K, tm * ROW_TILE, LANES), F32), pltpu.SemaphoreType.DMA((2,))],
    )
    return pl.pallas_call(
        functools.partial(_combine_kernel, tm=tm),
        out_shape=jax.ShapeDtypeStruct((m, D_MODEL), F32),
        grid_spec=grid_spec,
        compiler_params=_cparams("arbitrary"),
        name="moe_combine",
    )(slot, h, route_w, norm_g, ys)


def _routing_tables(route_i, m, tm):
    n = m * TOP_K
    n_tiles = pl.cdiv(n, tm) + N_EXPERTS - 1
    e_flat = route_i[:, :TOP_K].reshape(n)
    onehot = (e_flat[:, None] == jnp.arange(N_EXPERTS, dtype=jnp.int32)[None, :]).astype(jnp.int32)
    csum = jnp.cumsum(onehot, axis=0)
    rank = jnp.take_along_axis(csum, e_flat[:, None], axis=1)[:, 0] - 1
    counts = csum[-1]
    tiles_e = (counts + tm - 1) // tm
    tile_end = jnp.cumsum(tiles_e)
    tile_start = tile_end - tiles_e
    slot = (tile_start[e_flat] * tm + rank).astype(jnp.int32)
    tile_ids = jnp.arange(n_tiles, dtype=jnp.int32)
    tile_expert = jnp.sum((tile_end[None, :] <= tile_ids[:, None]).astype(jnp.int32), axis=1)
    tile_expert = jnp.minimum(tile_expert, N_EXPERTS - 1)
    tile_valid = jnp.clip(counts[tile_expert] - (tile_ids - tile_start[tile_expert]) * tm, 0, tm)
    tile_valid = jnp.where(tile_ids < tile_end[-1], tile_valid, 0).astype(jnp.int32)
    pad_start = jnp.concatenate([tile_start * tm + counts, tile_end[-1:]]).astype(jnp.int32)
    pad_len = (tiles_e * tm - counts).astype(jnp.int32)
    return slot, tile_expert.astype(jnp.int32), tile_valid, pad_start, pad_len


def _token_tail(x, z, sg, os_, lses, wts, tm, perms_t=None):
    mm = _merge(z, sg, os_, lses, wts["wa"], wts["wb"], tm=min(tm, 256), tn=D_MODEL, perms_t=perms_t)
    h, hn, route_i, route_w = _oproj(mm, x, wts["wo"], wts["norm_ffn"], wts["wr"], wts["br"], tm=min(tm, 256))
    moe_tm = min(tm, MOE_TM)
    slot, tile_expert, tile_valid, pad_start, pad_len = _routing_tables(route_i, x.shape[0], moe_tm)
    xs = _dispatch(hn, slot, pad_start, pad_len, tile_expert.shape[0], moe_tm, tm=min(tm, 512))
    ys = _moe(xs, wts["wg"], wts["wu"], wts["wd"], tile_expert, tile_valid, moe_tm)
    return _combine(h, route_w, wts["norm_final"], ys, slot, tm=min(tm, 512))


def _kv_rows(z, lead, keep_from, group, permuted):
    zz = z.reshape(lead + (z.shape[-1],))[:, keep_from:]
    base = group * QKV_COLS
    k = zz[..., base + GROUP_COLS: base + 2 * GROUP_COLS]
    v = zz[..., base + 2 * GROUP_COLS: base + 3 * GROUP_COLS]
    kv = jnp.stack([k, v], axis=2).astype(F32)
    dil = DILATED_GROUPS[group][1]
    if permuted and dil > 1:
        assert keep_from % PERM_BLOCK == 0 and kv.shape[1] % PERM_BLOCK == 0
        b, rows = kv.shape[:2]
        kv = kv.reshape(b, rows // PERM_BLOCK, dil, PERM_BLOCK // dil, 2, GROUP_COLS)
        kv = kv.transpose(0, 1, 3, 2, 4, 5).reshape(b, rows, 2, GROUP_COLS)
    return kv.reshape(kv.shape[:3] + (HEADS_PER_GROUP, HEAD_DIM))


def kernel(x_prompt, x_sample, cache_kv_w128, cache_kv_w512, cache_kv_w2048, norm_mix, w_in, ln_v_g, ln_v_b, w_s, b_s, w_a_out, w_b_out, w_o, norm_ffn, w_route_group, b_route_group, w_route_expert, b_route_expert, w_gate_e, w_up_e, w_down_e, norm_final):
    assert norm_mix.shape[0] == 1, "single-layer trunk"
    batch, seq, _ = x_prompt.shape
    bd, n_new, _ = x_sample.shape
    caches = (cache_kv_w128, cache_kv_w512, cache_kv_w2048)

    pad = LANES - N_EXPERT_GROUPS - N_EXPERTS
    wr = jnp.concatenate([w_route_group[0], w_route_expert[0], jnp.zeros((D_MODEL, pad), F32)], axis=1)
    br = jnp.concatenate([b_route_group[0], b_route_expert[0], jnp.zeros((pad,), F32)])[None, :]
    wts = dict(
        wa=w_a_out[0].astype(BF16), wb=w_b_out[0].astype(BF16), wo=w_o[0].astype(BF16),
        wg=w_gate_e[0].astype(BF16), wu=w_up_e[0].astype(BF16), wd=w_down_e[0].astype(BF16),
        wr=wr.astype(BF16), br=br, norm_ffn=norm_ffn[0][None, :], norm_final=norm_final[None, :],
    )
    w_in_b = w_in[0].astype(BF16)
    norm_g = norm_mix[0][None, :]
    ln_g, ln_b = ln_v_g[0][None, :], ln_v_b[0][None, :]

    xp = x_prompt.reshape(batch * seq, D_MODEL)
    perms = jnp.stack([_class_major_perm(dil) for _, dil in DILATED_GROUPS[1:]])
    zp_a, zp_b = _inproj_all(xp, norm_g, w_in_b, tm=1024, perms=perms)
    (sgp,) = _sgu(zp_a, ln_g, ln_b, w_s[0], b_s[0].T, chunks=8, emit_vn=False)
    os_p, lses_p = zip(*[_attn_group(zp_b, batch, seq, g) for g in range(N_DIL)])
    y_prompt = _token_tail(xp, zp_a, sgp, os_p, lses_p, wts, tm=512,
                           perms_t=perms.transpose(0, 2, 1)).reshape(batch, seq, D_MODEL)
    kv_prompt = [_kv_rows(zp_b, (batch, seq), seq - min(win, seq), g, permuted=True)[None]
                 for g, (win, _) in enumerate(DILATED_GROUPS)]

    ms = bd * n_new
    assert ms == CHUNK and n_new <= CHUNK
    xs = x_sample.reshape(ms, D_MODEL)
    zs_a, zs_b = _inproj_all(xs, norm_g, w_in_b, tm=ms)
    eye = jnp.eye(bd, dtype=F32)
    ws_s = jnp.einsum("ab,gts->gatbs", eye, w_s[0][:, :n_new, :n_new]).reshape(A_GROUPS, ms, ms)
    bst_s = jnp.tile(b_s[0][:, :n_new].T, (bd, 1))
    sgs, vns = _sgu(zs_a, ln_g, ln_b, ws_s, bst_s, chunks=1, emit_vn=True)
    zs5 = zs_b.astype(F32).reshape(bd, n_new, N_DIL, 3, HEADS_PER_GROUP, HEAD_DIM)
    qkv = [zs5[:, :, :, part].reshape(bd, n_new, N_DIL * HEADS_PER_GROUP, HEAD_DIM) for part in range(3)]
    o_s, lse_s = _attn_sample(*qkv, caches)
    os_s = [o_s[g].reshape(ms, GROUP_COLS).astype(BF16) for g in range(N_DIL)]
    lses_s = [lse_s[g, ..., 0].reshape(ms, HEADS_PER_GROUP) for g in range(N_DIL)]
    y_sample = _token_tail(xs, zs_a, sgs, os_s, lses_s, wts, tm=ms).reshape(bd, n_new, D_MODEL)
    kv_sample = [_kv_rows(zs_b, (bd, n_new), 0, g, permuted=False)[None] for g in range(N_DIL)]
    chunk_v = vns.reshape(1, bd, n_new, A_WIDTH)

    return (y_prompt, y_sample, *kv_prompt, *kv_sample, chunk_v)
```

```python
import functools

import jax
import jax.numpy as jnp
from jax import lax
from jax.experimental import pallas as pl
from jax.experimental.pallas import tpu as pltpu

F32 = jnp.float32
BF16 = jnp.bfloat16

D_MODEL = 2048
CHUNK = 128
A_GROUPS = 16
A_GROUP_DIM = 128
A_WIDTH = A_GROUPS * A_GROUP_DIM
HEAD_DIM = 128
HEADS_PER_GROUP = 4
GROUP_COLS = HEADS_PER_GROUP * HEAD_DIM
DILATED_GROUPS = ((128, 1), (512, 4), (2048, 16))
N_DIL = len(DILATED_GROUPS)
B_WIDTH = N_DIL * GROUP_COLS
IN_COLS = 2 * A_WIDTH + 3 * B_WIDTH + 2 * D_MODEL
COL_U, COL_V = 0, A_WIDTH
COL_GA = 2 * A_WIDTH
COL_GB = COL_GA + D_MODEL
QKV_COLS = 3 * GROUP_COLS
N_EXPERT_GROUPS = 4
EXPERTS_PER_GROUP = 8
N_EXPERTS = N_EXPERT_GROUPS * EXPERTS_PER_GROUP
TOP_K = 2
D_EXPERT = 256
RMS_EPS = 1e-6
LN_EPS = 1e-5
NEG = -1e30
ATTN_SCALE = HEAD_DIM ** -0.5

LANES = 128
VMEM_LIMIT_BYTES = 56 * 1024 * 1024
IN_TN = 512
MOE_TM = 256
PERM_BLOCK = 256
ROW_TILE = D_MODEL // 2 // LANES

def _cparams(*sem):
    return pltpu.CompilerParams(dimension_semantics=sem, vmem_limit_bytes=VMEM_LIMIT_BYTES)


def _gelu_tanh(x):
    return 0.5 * x * (1.0 + jnp.tanh(0.7978845608028654 * (x + 0.044715 * (x * x * x))))


def _sigmoid(x):
    return 1.0 / (1.0 + jnp.exp(-x))


def _bf16_bits(x):
    return pltpu.bitcast(x.astype(BF16).astype(F32), jnp.uint32)


def _store_token_tiles(ref, val):
    rows = val.shape[0]
    for s in range(ROW_TILE):
        lo = _bf16_bits(val[:, s * LANES:(s + 1) * LANES])
        hi = _bf16_bits(val[:, D_MODEL // 2 + s * LANES:D_MODEL // 2 + (s + 1) * LANES])
        ref[pl.ds(s, rows, stride=ROW_TILE), :] = lax.shift_right_logical(lo, jnp.uint32(16)) | hi


def _load_token_tile_chunk(ref, lead, rows, s):
    w = ref[lead + (pl.ds(s, rows, stride=ROW_TILE), slice(None))]
    lo = pltpu.bitcast(lax.shift_left(w, jnp.uint32(16)), F32)
    hi = pltpu.bitcast(w & jnp.uint32(0xFFFF0000), F32)
    return lo, hi


def _class_major_perm(dil):
    out_row = jnp.arange(PERM_BLOCK, dtype=jnp.int32)
    src = (out_row % (PERM_BLOCK // dil)) * dil + out_row // (PERM_BLOCK // dil)
    return (src[:, None] == jnp.arange(PERM_BLOCK, dtype=jnp.int32)[None, :]).astype(BF16)


def _inproj_kernel(x_ref, g_ref, *rest, strips, plan, permute):
    w_refs, rest = rest[:strips], rest[strips:]
    if permute:
        perm_ref, z_ref, xn_ref = rest
    else:
        z_ref, xn_ref = rest
    j = pl.program_id(1)

    @pl.when(j == 0)
    def _():
        x = x_ref[...]
        ms = jnp.mean(x * x, axis=-1, keepdims=True)
        xn_ref[...] = (x * lax.rsqrt(ms + RMS_EPS) * g_ref[...]).astype(BF16)

    tm = xn_ref.shape[0]
    sub = min(tm, PERM_BLOCK)

    def emit(epilogue):
        for blk in range(tm // sub):
            rows = slice(blk * sub, (blk + 1) * sub)
            for s, w_ref in enumerate(w_refs):
                z = jnp.dot(xn_ref[rows, :], w_ref[...], preferred_element_type=F32)
                z_ref[rows, s * IN_TN:(s + 1) * IN_TN] = epilogue(z).astype(BF16)

    def epilogue_of(kind):
        if kind == "gelu":
            return _gelu_tanh
        if kind == "gate":
            return _sigmoid
        if kind == "plain":
            return lambda z: z
        k = int(kind[len("perm"):])
        return lambda z: jnp.dot(perm_ref[k], z.astype(BF16), preferred_element_type=F32)

    for kind in sorted(set(plan)):
        tiles = [t for t, p in enumerate(plan) if p == kind]
        hit = functools.reduce(jnp.logical_or, [j == t for t in tiles])

        @pl.when(hit)
        def _():
            emit(epilogue_of(kind))


def _inproj(x, norm_g, w_in_bf16, tm, strips, plan, ref_tile, perms=None):
    m = x.shape[0]
    tn = strips * IN_TN
    in_specs = [pl.BlockSpec((tm, D_MODEL), lambda i, j: (i, 0)), pl.BlockSpec((1, D_MODEL), lambda i, j: (0, 0))]
    in_specs += [pl.BlockSpec((D_MODEL, IN_TN), functools.partial(lambda i, j, s: (0, ref_tile(j, s)), s=s))
                 for s in range(strips)]
    args = [x, norm_g] + [w_in_bf16] * strips
    if perms is not None:
        in_specs.append(pl.BlockSpec(perms.shape, lambda i, j: (0, 0, 0)))
        args.append(perms)
    return pl.pallas_call(
        functools.partial(_inproj_kernel, strips=strips, plan=plan, permute=perms is not None),
        out_shape=jax.ShapeDtypeStruct((m, len(plan) * tn), BF16),
        grid=(m // tm, len(plan)),
        in_specs=in_specs,
        out_specs=pl.BlockSpec((tm, tn), lambda i, j: (i, j)),
        scratch_shapes=[pltpu.VMEM((tm, D_MODEL), BF16)],
        compiler_params=_cparams("parallel", "arbitrary"),
        name="inproj",
    )(*args)


def _inproj_all(x, norm_g, w_in_bf16, tm, perms=None):
    n_gelu = 2 * A_WIDTH // IN_TN
    n_qkv = 3 * B_WIDTH // IN_TN
    strips_a = 2
    plan_a = ("gelu",) * (n_gelu // strips_a) + ("gate",) * (2 * D_MODEL // IN_TN // strips_a)

    def ref_a(j, s):
        t = j * strips_a + s
        return jnp.where(t < n_gelu, t, t + n_qkv)

    z_a = _inproj(x, norm_g, w_in_bf16, tm, strips_a, plan_a, ref_a)
    plan_b = ("plain",) + tuple(f"perm{k}" for k in range(N_DIL - 1)) if perms is not None else ("plain",) * N_DIL
    z_b = _inproj(x, norm_g, w_in_bf16, tm, 3, plan_b, lambda j, s: n_gelu + s * N_DIL + j, perms)
    return z_a, z_b


def _sgu_kernel(u_ref, v_ref, lng_ref, lnb_ref, ws_ref, bst_ref, sg_ref, *vn_out, chunks):
    row = lax.broadcasted_iota(jnp.int32, (CHUNK, CHUNK), 0)
    col = lax.broadcasted_iota(jnp.int32, (CHUNK, CHUNK), 1)
    tri = row >= col
    ws = [jnp.where(tri, ws_ref[g], 0.0).astype(BF16) for g in range(A_GROUPS)]
    for c in range(chunks):
        rows = slice(c * CHUNK, (c + 1) * CHUNK)
        v = v_ref[rows, :].astype(F32)
        mu = jnp.mean(v, axis=-1, keepdims=True)
        vc = v - mu
        var = jnp.mean(vc * vc, axis=-1, keepdims=True)
        vn = vc * lax.rsqrt(var + LN_EPS) * lng_ref[...] + lnb_ref[...]
        if vn_out:
            vn_out[0][rows, :] = vn
        vnb = vn.astype(BF16)
        for g in range(A_GROUPS):
            cols = slice(g * A_GROUP_DIM, (g + 1) * A_GROUP_DIM)
            s = jnp.dot(ws[g], vnb[:, cols], preferred_element_type=F32) + bst_ref[:, g:g + 1]
            sg_ref[rows, cols] = (u_ref[rows, cols].astype(F32) * s).astype(BF16)


def _sgu(z, ln_g, ln_b, ws, bst, chunks, emit_vn):
    m = z.shape[0]
    tm = chunks * CHUNK
    out_shape = [jax.ShapeDtypeStruct((m, A_WIDTH), BF16)]
    out_specs = [pl.BlockSpec((tm, A_WIDTH), lambda i: (i, 0))]
    if emit_vn:
        out_shape.append(jax.ShapeDtypeStruct((m, A_WIDTH), F32))
        out_specs.append(pl.BlockSpec((tm, A_WIDTH), lambda i: (i, 0)))
    return pl.pallas_call(
        functools.partial(_sgu_kernel, chunks=chunks),
        out_shape=out_shape,
        grid=(m // tm,),
        in_specs=[
            pl.BlockSpec((tm, A_WIDTH), lambda i: (i, COL_U // A_WIDTH)),
            pl.BlockSpec((tm, A_WIDTH), lambda i: (i, COL_V // A_WIDTH)),
            pl.BlockSpec((1, A_WIDTH), lambda i: (0, 0)),
            pl.BlockSpec((1, A_WIDTH), lambda i: (0, 0)),
            pl.BlockSpec((A_GROUPS, CHUNK, CHUNK), lambda i: (0, 0, 0)),
            pl.BlockSpec((CHUNK, A_GROUPS), lambda i: (0, 0)),
        ],
        out_specs=out_specs,
        compiler_params=_cparams("parallel"),
        name="sgu",
    )(z, z, ln_g, ln_b, ws, bst)


def _attn_kernel(q_ref, kc_ref, kp_ref, vc_ref, vp_ref, o_ref, lse_ref, *, qblocks):
    i = pl.program_id(2)
    qi = lax.broadcasted_iota(jnp.int32, (CHUNK, CHUNK), 0)
    ki = lax.broadcasted_iota(jnp.int32, (CHUNK, CHUNK), 1)
    cur_mask = ki <= qi
    no_prev = jnp.where(i == 0, CHUNK, 0)
    dn = (((1,), (1,)), ((), ()))
    rpc = q_ref.shape[1]
    nbq = CHUNK // rpc

    def load(ref, a, cols):
        return ref[a * nbq:(a + 1) * nbq, :, cols].reshape(CHUNK, HEAD_DIM)

    for a in range(qblocks):
        rows = slice(a * CHUNK, (a + 1) * CHUNK)
        for h in range(HEADS_PER_GROUP):
            cols = slice(h * HEAD_DIM, (h + 1) * HEAD_DIM)
            q = load(q_ref, a, cols)
            kc = load(kc_ref, a, cols)
            vc = load(vc_ref, a, cols)
            if a == 0:
                kp, vp = load(kp_ref, 0, cols), load(vp_ref, 0, cols)
                prev_mask = ki >= qi + no_prev
            else:
                kp, vp = load(kc_ref, a - 1, cols), load(vc_ref, a - 1, cols)
                prev_mask = ki >= qi
            s_c = lax.dot_general(q, kc, dn, preferred_element_type=F32) * ATTN_SCALE
            s_p = lax.dot_general(q, kp, dn, preferred_element_type=F32) * ATTN_SCALE
            s_c = jnp.where(cur_mask, s_c, NEG)
            s_p = jnp.where(prev_mask, s_p, NEG)
            mx = jnp.maximum(jnp.max(s_c, axis=-1, keepdims=True), jnp.max(s_p, axis=-1, keepdims=True))
            p_c = jnp.exp(s_c - mx)
            p_p = jnp.exp(s_p - mx)
            l = jnp.sum(p_c, axis=-1, keepdims=True) + jnp.sum(p_p, axis=-1, keepdims=True)
            acc = jnp.dot(p_c.astype(BF16), vc, preferred_element_type=F32)
            acc = acc + jnp.dot(p_p.astype(BF16), vp, preferred_element_type=F32)
            o_ref[a * nbq:(a + 1) * nbq, :, cols] = (acc / l).astype(BF16).reshape(nbq, rpc, HEAD_DIM)
            lse_ref[rows, h:h + 1] = mx + jnp.log(l)


def _rows_per_class(dil):
    return min(PERM_BLOCK // dil, CHUNK)


def _attn_group(z, batch, seq, group):
    _, dil = DILATED_GROUPS[group]
    sub = seq // dil
    rpc = _rows_per_class(dil)
    qblocks = 2 if sub % (2 * CHUNK) == 0 else 1
    tq = qblocks * CHUNK
    cq, ck, cv = (group * QKV_COLS // GROUP_COLS + part for part in range(3))
    zv = z.reshape(batch, sub // rpc, dil, rpc, z.shape[-1])

    def cur(cb):
        return pl.BlockSpec((None, tq // rpc, None, rpc, GROUP_COLS), lambda b, r, i: (b, i, r, 0, cb))

    def prev(cb):
        return pl.BlockSpec((None, CHUNK // rpc, None, rpc, GROUP_COLS),
                            lambda b, r, i: (b, jnp.maximum(i * qblocks - 1, 0), r, 0, cb))

    o, lse = pl.pallas_call(
        functools.partial(_attn_kernel, qblocks=qblocks),
        out_shape=[jax.ShapeDtypeStruct((batch, sub // rpc, dil, rpc, GROUP_COLS), BF16),
                   jax.ShapeDtypeStruct((batch, dil, sub, HEADS_PER_GROUP), F32)],
        grid=(batch, dil, sub // tq),
        in_specs=[cur(cq), cur(ck), prev(ck), cur(cv), prev(cv)],
        out_specs=[pl.BlockSpec((None, tq // rpc, None, rpc, GROUP_COLS), lambda b, r, i: (b, i, r, 0, 0)),
                   pl.BlockSpec((None, None, tq, HEADS_PER_GROUP), lambda b, r, i: (b, r, i, 0))],
        compiler_params=_cparams("parallel", "parallel", "arbitrary"),
        name=f"attn_prompt_g{group}",
    )(zv, zv, zv, zv, zv)
    lse = lse.transpose(0, 2, 1, 3).reshape(batch * seq, HEADS_PER_GROUP)
    return o, lse


def _attn_sample_kernel(q_ref, k_ref, v_ref, c0_ref, c1_ref, c2_ref, o_ref, lse_ref, *, n_new):
    caches = (c0_ref, c1_ref, c2_ref)
    rowid = lax.broadcasted_iota(jnp.int32, (CHUNK, HEADS_PER_GROUP, 1), 0)
    for g, (win, dil) in enumerate(DILATED_GROUPS):
        cref = caches[g]
        hs = slice(g * HEADS_PER_GROUP, (g + 1) * HEADS_PER_GROUP)
        for t in range(n_new):
            res, first = t % dil, t // dil
            kc = cref[:, res, 0]
            vc = cref[:, res, 1]
            q = q_ref[t, hs, :]
            s = jnp.sum(kc * q[None], axis=-1, keepdims=True) * ATTN_SCALE
            if first > 0:
                s = jnp.where(rowid >= first, s, NEG)
            new_rows = [tn for tn in range(t + 1) if (t - tn) % dil == 0 and (t - tn) // dil <= win // dil]
            s_new = [jnp.sum(q * k_ref[tn, hs, :], axis=-1, keepdims=True) * ATTN_SCALE for tn in new_rows]
            mx = jnp.max(s, axis=0)
            for sn in s_new:
                mx = jnp.maximum(mx, sn)
            p = jnp.exp(s - mx[None])
            l = jnp.sum(p, axis=0)
            acc = jnp.sum(p * vc, axis=0)
            for tn, sn in zip(new_rows, s_new):
                pn = jnp.exp(sn - mx)
                l = l + pn
                acc = acc + pn * v_ref[tn, hs, :]
            o_ref[g, t] = acc / l
            lse_ref[g, t] = jnp.broadcast_to(mx + jnp.log(l), (HEADS_PER_GROUP, HEAD_DIM))


def _attn_sample(q, k, v, caches):
    bd, n_new = q.shape[:2]
    views, specs = [], []
    for (win, dil), c in zip(DILATED_GROUPS, caches):
        assert c.shape[2] == win and win == CHUNK * dil, "cache must hold exactly one full window"
        used = min(dil, n_new)
        views.append(c.reshape(1, bd, CHUNK, dil, 2, HEADS_PER_GROUP, HEAD_DIM))
        specs.append(pl.BlockSpec((None, None, CHUNK, used, 2, HEADS_PER_GROUP, HEAD_DIM),
                                  lambda b: (0, b, 0, 0, 0, 0, 0)))
    new_spec = pl.BlockSpec((None, n_new, N_DIL * HEADS_PER_GROUP, HEAD_DIM), lambda b: (b, 0, 0, 0))
    out_sds = jax.ShapeDtypeStruct((N_DIL, bd, n_new, HEADS_PER_GROUP, HEAD_DIM), F32)
    out_spec = pl.BlockSpec((N_DIL, None, n_new, HEADS_PER_GROUP, HEAD_DIM), lambda b: (0, b, 0, 0, 0))
    return pl.pallas_call(
        functools.partial(_attn_sample_kernel, n_new=n_new),
        out_shape=[out_sds, out_sds],
        grid=(bd,),
        in_specs=[new_spec, new_spec, new_spec] + specs,
        out_specs=[out_spec, out_spec],
        compiler_params=_cparams("parallel"),
        name="attn_sample",
    )(q, k, v, *views)


def _merge_kernel(sg_ref, o0_ref, o1_ref, o2_ref, l0_ref, l1_ref, l2_ref, ga_ref, gb_ref,
                  wa_ref, wb_ref, *rest, permuted):
    if permuted:
        pt_ref, m_ref, ob_ref = rest
    else:
        m_ref, ob_ref = rest
    j = pl.program_id(1)

    @pl.when(j == 0)
    def _():
        l0, l1, l2 = l0_ref[...], l1_ref[...], l2_ref[...]
        mx = jnp.maximum(jnp.maximum(l0, l1), l2)
        e0, e1, e2 = jnp.exp(l0 - mx), jnp.exp(l1 - mx), jnp.exp(l2 - mx)
        inv = 1.0 / (e0 + e1 + e2)
        ws = (e0 * inv, e1 * inv, e2 * inv)
        o_refs = (o0_ref, o1_ref, o2_ref)
        tm = ob_ref.shape[0]

        def o_rows(ref, rows):
            if len(ref.shape) == 2:
                return ref[rows, :]
            per = ref.shape[1] * ref.shape[2]
            return ref[rows.start // per:rows.stop // per].reshape(rows.stop - rows.start, GROUP_COLS)

        for blk in range(max(tm // PERM_BLOCK, 1)):
            rows = slice(blk * PERM_BLOCK, min((blk + 1) * PERM_BLOCK, tm))
            og = []
            for g in range(N_DIL):
                if permuted and g > 0:
                    og.append(jnp.dot(pt_ref[g - 1], o_rows(o_refs[g], rows), preferred_element_type=F32))
                else:
                    og.append(o_rows(o_refs[g], rows).astype(F32))
            for h in range(HEADS_PER_GROUP):
                cols = slice(h * HEAD_DIM, (h + 1) * HEAD_DIM)
                ob = sum(ws[g][rows, h:h + 1] * og[g][:, cols] for g in range(N_DIL))
                ob_ref[rows, cols] = ob.astype(BF16)

    ya = jnp.dot(sg_ref[...], wa_ref[...], preferred_element_type=F32)
    yb = jnp.dot(ob_ref[...], wb_ref[...], preferred_element_type=F32)
    m_ref[...] = (ga_ref[...].astype(F32) * ya + gb_ref[...].astype(F32) * yb).astype(BF16)


def _merge(z, sg, os_, lses, wa, wb, tm, tn, perms_t=None):
    m = z.shape[0]
    assert COL_GA % tn == 0 and COL_GB % tn == 0
    row = lambda width: pl.BlockSpec((tm, width), lambda i, j: (i, 0))

    def o_spec(o):
        if o.ndim == 2:
            return row(GROUP_COLS)
        _, blocks, dil, rpc, _ = o.shape
        per_tile, tiles = tm // (dil * rpc), blocks * dil * rpc // tm
        return pl.BlockSpec((None, per_tile, dil, rpc, GROUP_COLS), lambda i, j: (i // tiles, i % tiles, 0, 0, 0))

    in_specs = [row(A_WIDTH)] + [o_spec(o) for o in os_] + [row(HEADS_PER_GROUP)] * 3 + [
        pl.BlockSpec((tm, tn), lambda i, j: (i, COL_GA // tn + j)),
        pl.BlockSpec((tm, tn), lambda i, j: (i, COL_GB // tn + j)),
        pl.BlockSpec((A_WIDTH, tn), lambda i, j: (0, j)),
        pl.BlockSpec((GROUP_COLS, tn), lambda i, j: (0, j)),
    ]
    args = [sg, *os_, *lses, z, z, wa, wb]
    if perms_t is not None:
        assert tm % PERM_BLOCK == 0
        in_specs.append(pl.BlockSpec(perms_t.shape, lambda i, j: (0, 0, 0)))
        args.append(perms_t)
    return pl.pallas_call(
        functools.partial(_merge_kernel, permuted=perms_t is not None),
        out_shape=jax.ShapeDtypeStruct((m, D_MODEL), BF16),
        grid=(m // tm, D_MODEL // tn),
        in_specs=in_specs,
        out_specs=pl.BlockSpec((tm, tn), lambda i, j: (i, j)),
        scratch_shapes=[pltpu.VMEM((tm, GROUP_COLS), BF16)],
        compiler_params=_cparams("parallel", "arbitrary"),
        name="branch_merge",
    )(*args)


def _oproj_kernel(m_ref, x_ref, wo_ref, g_ref, wr_ref, br_ref, h_ref, hn_ref, ri_ref, rw_ref):
    h = x_ref[...] + jnp.dot(m_ref[...], wo_ref[...], preferred_element_type=F32)
    h_ref[...] = h
    ms = jnp.mean(h * h, axis=-1, keepdims=True)
    hn = h * lax.rsqrt(ms + RMS_EPS) * g_ref[...]
    _store_token_tiles(hn_ref, hn)
    logits = jnp.dot(hn.astype(BF16), wr_ref[...], preferred_element_type=F32) + br_ref[...]
    lane = lax.broadcasted_iota(jnp.int32, logits.shape, 1).astype(F32)
    big = float(LANES)

    def first_argmax(vals, vmax):
        return jnp.min(jnp.where(vals == vmax, lane, big), axis=-1, keepdims=True)

    lg = jnp.where(lane < N_EXPERT_GROUPS, logits, NEG)
    gmax = jnp.max(lg, axis=-1, keepdims=True)
    gsel = first_argmax(lg, gmax)
    p_sel = 1.0 / jnp.sum(jnp.exp(lg - gmax), axis=-1, keepdims=True)
    lo = N_EXPERT_GROUPS + EXPERTS_PER_GROUP * gsel
    le = jnp.where(jnp.logical_and(lane >= lo, lane < lo + EXPERTS_PER_GROUP), logits, NEG)
    v1 = jnp.max(le, axis=-1, keepdims=True)
    i1 = first_argmax(le, v1)
    le2 = jnp.where(lane == i1, NEG, le)
    v2 = jnp.max(le2, axis=-1, keepdims=True)
    i2 = first_argmax(le2, v2)
    e2 = jnp.exp(v2 - v1)
    w1 = p_sel / (1.0 + e2)
    w2 = p_sel * e2 / (1.0 + e2)
    ri = jnp.where(lane == 0, i1 - N_EXPERT_GROUPS, jnp.where(lane == 1, i2 - N_EXPERT_GROUPS, 0.0))
    ri_ref[...] = ri.astype(jnp.int32)
    rw_ref[...] = jnp.where(lane == 0, w1, jnp.where(lane == 1, w2, 0.0))


def _oproj(mm, x, wo, norm_g, wr, br, tm):
    m = x.shape[0]
    row = lambda width: pl.BlockSpec((tm, width), lambda i: (i, 0))
    full = lambda a, b: pl.BlockSpec((a, b), lambda i: (0, 0))
    return pl.pallas_call(
        _oproj_kernel,
        out_shape=[jax.ShapeDtypeStruct((m, D_MODEL), F32), jax.ShapeDtypeStruct((m * ROW_TILE, LANES), jnp.uint32),
                   jax.ShapeDtypeStruct((m, LANES), jnp.int32), jax.ShapeDtypeStruct((m, LANES), F32)],
        grid=(m // tm,),
        in_specs=[row(D_MODEL), row(D_MODEL), full(D_MODEL, D_MODEL), full(1, D_MODEL),
                  full(D_MODEL, LANES), full(1, LANES)],
        out_specs=[row(D_MODEL), pl.BlockSpec((tm * ROW_TILE, LANES), lambda i: (i, 0)), row(LANES), row(LANES)],
        compiler_params=_cparams("parallel"),
        name="oproj_router",
    )(mm, x, wo, norm_g, wr, br)


def _dispatch_kernel(slot_ref, pad_start_ref, pad_len_ref, hn_ref, xs_hbm, zbuf, sem, zsem, *, tm, moe_tm, n_tiles):
    i = pl.program_id(0)

    def zero_copy(dst_row, rows):
        return pltpu.make_async_copy(zbuf.at[pl.ds(0, rows * ROW_TILE)],
                                     xs_hbm.at[pl.ds(dst_row * ROW_TILE, rows * ROW_TILE)], zsem)

    def zero_fill(act):
        def per_expert(e, carry):
            off, length = pad_start_ref[e], pad_len_ref[e]
            rows = moe_tm // 2
            while rows >= 1:
                @pl.when((length & rows) != 0)
                def _(off=off, rows=rows):
                    act(zero_copy(off, rows))
                off = off + (length & rows)
                rows //= 2
            return carry

        lax.fori_loop(0, N_EXPERTS, per_expert, 0)

        def per_tile(t, carry):
            act(zero_copy(t * moe_tm, moe_tm))
            return carry

        lax.fori_loop(pad_start_ref[N_EXPERTS], n_tiles, per_tile, 0)

    @pl.when(i == 0)
    def _():
        zbuf[...] = jnp.zeros_like(zbuf)
        zero_fill(lambda c: c.start())

    def issue(r, carry):
        for k in range(TOP_K):
            slot = slot_ref[(i * tm + r) * TOP_K + k]
            pltpu.make_async_copy(hn_ref.at[pl.ds(r * ROW_TILE, ROW_TILE)],
                                  xs_hbm.at[pl.ds(slot * ROW_TILE, ROW_TILE)], sem).start()
        return carry

    lax.fori_loop(0, tm, issue, 0, unroll=8)
    for k in range(TOP_K):
        pltpu.make_async_copy(hn_ref, hn_ref, sem).wait()

    @pl.when(i == pl.num_programs(0) - 1)
    def _():
        zero_fill(lambda c: c.wait())


def _dispatch(hn_tiles, slot, pad_start, pad_len, n_tiles, moe_tm, tm):
    m = hn_tiles.shape[0] // ROW_TILE
    assert m % tm == 0 and moe_tm & (moe_tm - 1) == 0
    grid_spec = pltpu.PrefetchScalarGridSpec(
        num_scalar_prefetch=3,
        grid=(m // tm,),
        in_specs=[pl.BlockSpec((tm * ROW_TILE, LANES), lambda i, s, ps, pn: (i, 0))],
        out_specs=pl.BlockSpec(memory_space=pl.ANY),
        scratch_shapes=[pltpu.VMEM((moe_tm * ROW_TILE, LANES), jnp.uint32),
                        pltpu.SemaphoreType.DMA(()), pltpu.SemaphoreType.DMA(())],
    )
    return pl.pallas_call(
        functools.partial(_dispatch_kernel, tm=tm, moe_tm=moe_tm, n_tiles=n_tiles),
        out_shape=jax.ShapeDtypeStruct((n_tiles * moe_tm * ROW_TILE, LANES), jnp.uint32),
        grid_spec=grid_spec,
        compiler_params=_cparams("arbitrary"),
        name="moe_dispatch",
    )(slot, pad_start, pad_len, hn_tiles)


def _moe_kernel(texp_ref, valid_ref, xs_ref, wg_ref, wu_ref, wd_ref, ys_ref):
    i = pl.program_id(0)
    tm = xs_ref.shape[0] // ROW_TILE
    valid = valid_ref[i]

    @pl.when(valid > 0)
    def _():
        halves = [_load_token_tile_chunk(xs_ref, (), tm, s) for s in range(ROW_TILE)]
        x = jnp.concatenate([lo.astype(BF16) for lo, _ in halves] + [hi.astype(BF16) for _, hi in halves], axis=1)
        gate = jnp.dot(x, wg_ref[...], preferred_element_type=F32)
        up = jnp.dot(x, wu_ref[...], preferred_element_type=F32)
        hid = (gate * _sigmoid(gate) * up).astype(BF16)
        _store_token_tiles(ys_ref, jnp.dot(hid, wd_ref[...], preferred_element_type=F32))

    @pl.when(valid == 0)
    def _():
        ys_ref[...] = jnp.zeros_like(ys_ref)


def _moe(xs, wg, wu, wd, tile_expert, tile_valid, tm):
    n_tiles = tile_expert.shape[0]
    grid_spec = pltpu.PrefetchScalarGridSpec(
        num_scalar_prefetch=2,
        grid=(n_tiles,),
        in_specs=[
            pl.BlockSpec((tm * ROW_TILE, LANES), lambda i, te, tv: (i, 0)),
            pl.BlockSpec((None, D_MODEL, D_EXPERT), lambda i, te, tv: (te[i], 0, 0)),
            pl.BlockSpec((None, D_MODEL, D_EXPERT), lambda i, te, tv: (te[i], 0, 0)),
            pl.BlockSpec((None, D_EXPERT, D_MODEL), lambda i, te, tv: (te[i], 0, 0)),
        ],
        out_specs=pl.BlockSpec((tm * ROW_TILE, LANES), lambda i, te, tv: (i, 0)),
    )
    return pl.pallas_call(
        _moe_kernel,
        out_shape=jax.ShapeDtypeStruct((n_tiles * tm * ROW_TILE, LANES), jnp.uint32),
        grid_spec=grid_spec,
        compiler_params=_cparams("arbitrary"),
        name="moe_experts",
    )(tile_expert, tile_valid, xs, wg, wu, wd)


def _combine_kernel(slot_ref, h_ref, rw_ref, g_ref, ys_hbm, y_ref, ybuf, sem, *, tm):
    i = pl.program_id(0)
    n_steps = pl.num_programs(0)

    def issue(tile, buf):
        def body(r, carry):
            for k in range(TOP_K):
                slot = slot_ref[(tile * tm + r) * TOP_K + k]
                pltpu.make_async_copy(ys_hbm.at[pl.ds(slot * ROW_TILE, ROW_TILE)],
                                      ybuf.at[buf, k, pl.ds(r * ROW_TILE, ROW_TILE)], sem.at[buf]).start()
            return carry

        lax.fori_loop(0, tm, body, 0, unroll=8)

    @pl.when(i == 0)
    def _():
        issue(0, 0)

    @pl.when(i + 1 < n_steps)
    def _():
        issue(i + 1, (i + 1) % 2)

    cur = i % 2
    for k in range(TOP_K):
        pltpu.make_async_copy(ybuf.at[cur, k], ybuf.at[cur, k], sem.at[cur]).wait()
    w0, w1 = rw_ref[:, 0:1], rw_ref[:, 1:2]
    chunks, ssq = {}, 0.0
    for s in range(ROW_TILE):
        e0 = _load_token_tile_chunk(ybuf, (cur, 0), tm, s)
        e1 = _load_token_tile_chunk(ybuf, (cur, 1), tm, s)
        for half in range(2):
            c = half * (D_MODEL // 2 // LANES) + s
            y = h_ref[:, c * LANES:(c + 1) * LANES] + w0 * e0[half] + w1 * e1[half]
            chunks[c] = y
            ssq = ssq + jnp.sum(y * y, axis=-1, keepdims=True)
    scale = lax.rsqrt(ssq * (1.0 / D_MODEL) + RMS_EPS)
    for c, y in chunks.items():
        cols = slice(c * LANES, (c + 1) * LANES)
        y_ref[:, cols] = y * scale * g_ref[:, cols]


def _combine(h, route_w, norm_g, ys, slot, tm):
    m = h.shape[0]
    grid_spec = pltpu.PrefetchScalarGridSpec(
        num_scalar_prefetch=1,
        grid=(m // tm,),
        in_specs=[
            pl.BlockSpec((tm, D_MODEL), lambda i, s: (i, 0)),
            pl.BlockSpec((tm, LANES), lambda i, s: (i, 0)),
            pl.BlockSpec((1, D_MODEL), lambda i, s: (0, 0)),
            pl.BlockSpec(memory_space=pl.ANY),
        ],
        out_specs=pl.BlockSpec((tm, D_MODEL), lambda i, s: (i, 0)),
        scratch_shapes=[pltpu.VMEM((2, TOP_K, tm * ROW_TILE, LANES), jnp.uint32), pltpu.SemaphoreType.DMA((2,))],
    )
    return pl.pallas_call(
        functools.partial(_combine_kernel, tm=tm),
        out_shape=jax.ShapeDtypeStruct((m, D_MODEL), F32),
        grid_spec=grid_spec,
        compiler_params=_cparams("arbitrary"),
        name="moe_combine",
    )(slot, h, route_w, norm_g, ys)


def _routing_tables(route_i, m, tm):
    n = m * TOP_K
    n_tiles = pl.cdiv(n, tm) + N_EXPERTS - 1
    e_flat = route_i[:, :TOP_K].reshape(n)
    onehot = (e_flat[:, None] == jnp.arange(N_EXPERTS, dtype=jnp.int32)[None, :]).astype(jnp.int32)
    csum = jnp.cumsum(onehot, axis=0)
    rank = jnp.take_along_axis(csum, e_flat[:, None], axis=1)[:, 0] - 1
    counts = csum[-1]
    tiles_e = (counts + tm - 1) // tm
    tile_end = jnp.cumsum(tiles_e)
    tile_start = tile_end - tiles_e
    slot = (tile_start[e_flat] * tm + rank).astype(jnp.int32)
    tile_ids = jnp.arange(n_tiles, dtype=jnp.int32)
    tile_expert = jnp.sum((tile_end[None, :] <= tile_ids[:, None]).astype(jnp.int32), axis=1)
    tile_expert = jnp.minimum(tile_expert, N_EXPERTS - 1)
    tile_valid = jnp.clip(counts[tile_expert] - (tile_ids - tile_start[tile_expert]) * tm, 0, tm)
    tile_valid = jnp.where(tile_ids < tile_end[-1], tile_valid, 0).astype(jnp.int32)
    pad_start = jnp.concatenate([tile_start * tm + counts, tile_end[-1:]]).astype(jnp.int32)
    pad_len = (tiles_e * tm - counts).astype(jnp.int32)
    return slot, tile_expert.astype(jnp.int32), tile_valid, pad_start, pad_len


def _token_tail(x, z, sg, os_, lses, wts, tm, perms_t=None):
    mm = _merge(z, sg, os_, lses, wts["wa"], wts["wb"], tm=min(tm, 256), tn=D_MODEL, perms_t=perms_t)
    h, hn, route_i, route_w = _oproj(mm, x, wts["wo"], wts["norm_ffn"], wts["wr"], wts["br"], tm=min(tm, 256))
    moe_tm = min(tm, MOE_TM)
    slot, tile_expert, tile_valid, pad_start, pad_len = _routing_tables(route_i, x.shape[0], moe_tm)
    xs = _dispatch(hn, slot, pad_start, pad_len, tile_expert.shape[0], moe_tm, tm=min(tm, 512))
    ys = _moe(xs, wts["wg"], wts["wu"], wts["wd"], tile_expert, tile_valid, moe_tm)
    return _combine(h, route_w, wts["norm_final"], ys, slot, tm=min(tm, 512))


def _kv_rows(z, lead, keep_from, group, permuted):
    zz = z.reshape(lead + (z.shape[-1],))[:, keep_from:]
    base = group * QKV_COLS
    k = zz[..., base + GROUP_COLS: base + 2 * GROUP_COLS]
    v = zz[..., base + 2 * GROUP_COLS: base + 3 * GROUP_COLS]
    kv = jnp.stack([k, v], axis=2).astype(F32)
    dil = DILATED_GROUPS[group][1]
    if permuted and dil > 1:
        assert keep_from % PERM_BLOCK == 0 and kv.shape[1] % PERM_BLOCK == 0
        b, rows = kv.shape[:2]
        kv = kv.reshape(b, rows // PERM_BLOCK, dil, PERM_BLOCK // dil, 2, GROUP_COLS)
        kv = kv.transpose(0, 1, 3, 2, 4, 5).reshape(b, rows, 2, GROUP_COLS)
    return kv.reshape(kv.shape[:3] + (HEADS_PER_GROUP, HEAD_DIM))


def kernel(x_prompt, x_sample, cache_kv_w128, cache_kv_w512, cache_kv_w2048, norm_mix, w_in, ln_v_g, ln_v_b, w_s, b_s, w_a_out, w_b_out, w_o, norm_ffn, w_route_group, b_route_group, w_route_expert, b_route_expert, w_gate_e, w_up_e, w_down_e, norm_final):
    assert norm_mix.shape[0] == 1, "single-layer trunk"
    batch, seq, _ = x_prompt.shape
    bd, n_new, _ = x_sample.shape
    caches = (cache_kv_w128, cache_kv_w512, cache_kv_w2048)

    pad = LANES - N_EXPERT_GROUPS - N_EXPERTS
    wr = jnp.concatenate([w_route_group[0], w_route_expert[0], jnp.zeros((D_MODEL, pad), F32)], axis=1)
    br = jnp.concatenate([b_route_group[0], b_route_expert[0], jnp.zeros((pad,), F32)])[None, :]
    wts = dict(
        wa=w_a_out[0].astype(BF16), wb=w_b_out[0].astype(BF16), wo=w_o[0].astype(BF16),
        wg=w_gate_e[0].astype(BF16), wu=w_up_e[0].astype(BF16), wd=w_down_e[0].astype(BF16),
        wr=wr.astype(BF16), br=br, norm_ffn=norm_ffn[0][None, :], norm_final=norm_final[None, :],
    )
    w_in_b = w_in[0].astype(BF16)
    norm_g = norm_mix[0][None, :]
    ln_g, ln_b = ln_v_g[0][None, :], ln_v_b[0][None, :]

    xp = x_prompt.reshape(batch * seq, D_MODEL)
    perms = jnp.stack([_class_major_perm(dil) for _, dil in DILATED_GROUPS[1:]])
    zp_a, zp_b = _inproj_all(xp, norm_g, w_in_b, tm=1024, perms=perms)
    (sgp,) = _sgu(zp_a, ln_g, ln_b, w_s[0], b_s[0].T, chunks=8, emit_vn=False)
    os_p, lses_p = zip(*[_attn_group(zp_b, batch, seq, g) for g in range(N_DIL)])
    y_prompt = _token_tail(xp, zp_a, sgp, os_p, lses_p, wts, tm=512,
                           perms_t=perms.transpose(0, 2, 1)).reshape(batch, seq, D_MODEL)
    kv_prompt = [_kv_rows(zp_b, (batch, seq), seq - min(win, seq), g, permuted=True)[None]
                 for g, (win, _) in enumerate(DILATED_GROUPS)]

    ms = bd * n_new
    assert ms == CHUNK and n_new <= CHUNK
    xs = x_sample.reshape(ms, D_MODEL)
    zs_a, zs_b = _inproj_all(xs, norm_g, w_in_b, tm=ms)
    eye = jnp.eye(bd, dtype=F32)
    ws_s = jnp.einsum("ab,gts->gatbs", eye, w_s[0][:, :n_new, :n_new]).reshape(A_GROUPS, ms, ms)
    bst_s = jnp.tile(b_s[0][:, :n_new].T, (bd, 1))
    sgs, vns = _sgu(zs_a, ln_g, ln_b, ws_s, bst_s, chunks=1, emit_vn=True)
    zs5 = zs_b.astype(F32).reshape(bd, n_new, N_DIL, 3, HEADS_PER_GROUP, HEAD_DIM)
    qkv = [zs5[:, :, :, part].reshape(bd, n_new, N_DIL * HEADS_PER_GROUP, HEAD_DIM) for part in range(3)]
    o_s, lse_s = _attn_sample(*qkv, caches)
    os_s = [o_s[g].reshape(ms, GROUP_COLS).astype(BF16) for g in range(N_DIL)]
    lses_s = [lse_s[g, ..., 0].reshape(ms, HEADS_PER_GROUP) for g in range(N_DIL)]
    y_sample = _token_tail(xs, zs_a, sgs, os_s, lses_s, wts, tm=ms).reshape(bd, n_new, D_MODEL)
    kv_sample = [_kv_rows(zs_b, (bd, n_new), 0, g, permuted=False)[None] for g in range(N_DIL)]
    chunk_v = vns.reshape(1, bd, n_new, A_WIDTH)

    return (y_prompt, y_sample, *kv_prompt, *kv_sample, chunk_v)
```

```python
import functools

import jax
import jax.numpy as jnp
from jax import lax
from jax.experimental import pallas as pl
from jax.experimental.pallas import tpu as pltpu

F32 = jnp.float32
BF16 = jnp.bfloat16

D_MODEL = 2048
CHUNK = 128
A_GROUPS = 16
A_GROUP_DIM = 128
A_WIDTH = A_GROUPS * A_GROUP_DIM
HEAD_DIM = 128
HEADS_PER_GROUP = 4
GROUP_COLS = HEADS_PER_GROUP * HEAD_DIM
DILATED_GROUPS = ((128, 1), (512, 4), (2048, 16))
N_DIL = len(DILATED_GROUPS)
B_WIDTH = N_DIL * GROUP_COLS
IN_COLS = 2 * A_WIDTH + 3 * B_WIDTH + 2 * D_MODEL
COL_U, COL_V = 0, A_WIDTH
COL_GA = 2 * A_WIDTH
COL_GB = COL_GA + D_MODEL
QKV_COLS = 3 * GROUP_COLS
N_EXPERT_GROUPS = 4
EXPERTS_PER_GROUP = 8
N_EXPERTS = N_EXPERT_GROUPS * EXPERTS_PER_GROUP
TOP_K = 2
D_EXPERT = 256
RMS_EPS = 1e-6
LN_EPS = 1e-5
NEG = -1e30
ATTN_SCALE = HEAD_DIM ** -0.5

LANES = 128
VMEM_LIMIT_BYTES = 56 * 1024 * 1024
IN_TN = 512
MOE_TM = 256
PERM_BLOCK = 256
ATTN_UNITS_PER_BATCH = 8
ROW_TILE = D_MODEL // 2 // LANES

def _cparams(*sem):
    return pltpu.CompilerParams(dimension_semantics=sem, vmem_limit_bytes=VMEM_LIMIT_BYTES)


def _gelu_tanh(x):
    return 0.5 * x * (1.0 + jnp.tanh(0.7978845608028654 * (x + 0.044715 * (x * x * x))))


def _sigmoid(x):
    return 1.0 / (1.0 + jnp.exp(-x))


def _bf16_bits(x):
    return pltpu.bitcast(x.astype(BF16).astype(F32), jnp.uint32)


def _store_token_tiles(ref, val):
    rows = val.shape[0]
    for s in range(ROW_TILE):
        lo = _bf16_bits(val[:, s * LANES:(s + 1) * LANES])
        hi = _bf16_bits(val[:, D_MODEL // 2 + s * LANES:D_MODEL // 2 + (s + 1) * LANES])
        ref[pl.ds(s, rows, stride=ROW_TILE), :] = lax.shift_right_logical(lo, jnp.uint32(16)) | hi


def _load_token_tile_chunk(ref, lead, rows, s):
    w = ref[lead + (pl.ds(s, rows, stride=ROW_TILE), slice(None))]
    lo = pltpu.bitcast(lax.shift_left(w, jnp.uint32(16)), F32)
    hi = pltpu.bitcast(w & jnp.uint32(0xFFFF0000), F32)
    return lo, hi


def _class_major_perm(dil):
    out_row = jnp.arange(PERM_BLOCK, dtype=jnp.int32)
    src = (out_row % (PERM_BLOCK // dil)) * dil + out_row // (PERM_BLOCK // dil)
    return (src[:, None] == jnp.arange(PERM_BLOCK, dtype=jnp.int32)[None, :]).astype(BF16)


def _inproj_kernel(x_ref, g_ref, *rest, strips, plan, permute):
    w_refs, rest = rest[:strips], rest[strips:]
    if permute:
        perm_ref, z_ref, xn_ref = rest
    else:
        z_ref, xn_ref = rest
    j = pl.program_id(1)

    @pl.when(j == 0)
    def _():
        x = x_ref[...]
        ms = jnp.mean(x * x, axis=-1, keepdims=True)
        xn_ref[...] = (x * lax.rsqrt(ms + RMS_EPS) * g_ref[...]).astype(BF16)

    tm = xn_ref.shape[0]
    sub = min(tm, PERM_BLOCK)

    def emit(epilogue):
        for blk in range(tm // sub):
            rows = slice(blk * sub, (blk + 1) * sub)
            for s, w_ref in enumerate(w_refs):
                z = jnp.dot(xn_ref[rows, :], w_ref[...], preferred_element_type=F32)
                z_ref[rows, s * IN_TN:(s + 1) * IN_TN] = epilogue(z).astype(BF16)

    def epilogue_of(kind):
        if kind == "gelu":
            return _gelu_tanh
        if kind == "gate":
            return _sigmoid
        if kind == "plain":
            return lambda z: z
        k = int(kind[len("perm"):])
        return lambda z: jnp.dot(perm_ref[k], z.astype(BF16), preferred_element_type=F32)

    for kind in sorted(set(plan)):
        tiles = [t for t, p in enumerate(plan) if p == kind]
        hit = functools.reduce(jnp.logical_or, [j == t for t in tiles])

        @pl.when(hit)
        def _():
            emit(epilogue_of(kind))


def _inproj(x, norm_g, w_in_bf16, tm, strips, plan, ref_tile, perms=None):
    m = x.shape[0]
    tn = strips * IN_TN
    in_specs = [pl.BlockSpec((tm, D_MODEL), lambda i, j: (i, 0)), pl.BlockSpec((1, D_MODEL), lambda i, j: (0, 0))]
    in_specs += [pl.BlockSpec((D_MODEL, IN_TN), functools.partial(lambda i, j, s: (0, ref_tile(j, s)), s=s))
                 for s in range(strips)]
    args = [x, norm_g] + [w_in_bf16] * strips
    if perms is not None:
        in_specs.append(pl.BlockSpec(perms.shape, lambda i, j: (0, 0, 0)))
        args.append(perms)
    return pl.pallas_call(
        functools.partial(_inproj_kernel, strips=strips, plan=plan, permute=perms is not None),
        out_shape=jax.ShapeDtypeStruct((m, len(plan) * tn), BF16),
        grid=(m // tm, len(plan)),
        in_specs=in_specs,
        out_specs=pl.BlockSpec((tm, tn), lambda i, j: (i, j)),
        scratch_shapes=[pltpu.VMEM((tm, D_MODEL), BF16)],
        compiler_params=_cparams("parallel", "arbitrary"),
        name="inproj",
    )(*args)


def _inproj_all(x, norm_g, w_in_bf16, tm, perms=None):
    n_gelu = 2 * A_WIDTH // IN_TN
    n_qkv = 3 * B_WIDTH // IN_TN
    strips_a = 2
    plan_a = ("gelu",) * (n_gelu // strips_a) + ("gate",) * (2 * D_MODEL // IN_TN // strips_a)

    def ref_a(j, s):
        t = j * strips_a + s
        return jnp.where(t < n_gelu, t, t + n_qkv)

    z_a = _inproj(x, norm_g, w_in_bf16, tm, strips_a, plan_a, ref_a)
    plan_b = ("plain",) + tuple(f"perm{k}" for k in range(N_DIL - 1)) if perms is not None else ("plain",) * N_DIL
    z_b = _inproj(x, norm_g, w_in_bf16, tm, 3, plan_b, lambda j, s: n_gelu + s * N_DIL + j, perms)
    return z_a, z_b


def _sgu_kernel(u_ref, v_ref, lng_ref, lnb_ref, ws_ref, bst_ref, sg_ref, *vn_out, chunks):
    row = lax.broadcasted_iota(jnp.int32, (CHUNK, CHUNK), 0)
    col = lax.broadcasted_iota(jnp.int32, (CHUNK, CHUNK), 1)
    tri = row >= col
    ws = [jnp.where(tri, ws_ref[g], 0.0).astype(BF16) for g in range(A_GROUPS)]
    for c in range(chunks):
        rows = slice(c * CHUNK, (c + 1) * CHUNK)
        v = v_ref[rows, :].astype(F32)
        mu = jnp.mean(v, axis=-1, keepdims=True)
        vc = v - mu
        var = jnp.mean(vc * vc, axis=-1, keepdims=True)
        vn = vc * lax.rsqrt(var + LN_EPS) * lng_ref[...] + lnb_ref[...]
        if vn_out:
            vn_out[0][rows, :] = vn
        vnb = vn.astype(BF16)
        for g in range(A_GROUPS):
            cols = slice(g * A_GROUP_DIM, (g + 1) * A_GROUP_DIM)
            s = jnp.dot(ws[g], vnb[:, cols], preferred_element_type=F32) + bst_ref[:, g:g + 1]
            sg_ref[rows, cols] = (u_ref[rows, cols].astype(F32) * s).astype(BF16)


def _sgu(z, ln_g, ln_b, ws, bst, chunks, emit_vn):
    m = z.shape[0]
    tm = chunks * CHUNK
    out_shape = [jax.ShapeDtypeStruct((m, A_WIDTH), BF16)]
    out_specs = [pl.BlockSpec((tm, A_WIDTH), lambda i: (i, 0))]
    if emit_vn:
        out_shape.append(jax.ShapeDtypeStruct((m, A_WIDTH), F32))
        out_specs.append(pl.BlockSpec((tm, A_WIDTH), lambda i: (i, 0)))
    return pl.pallas_call(
        functools.partial(_sgu_kernel, chunks=chunks),
        out_shape=out_shape,
        grid=(m // tm,),
        in_specs=[
            pl.BlockSpec((tm, A_WIDTH), lambda i: (i, COL_U // A_WIDTH)),
            pl.BlockSpec((tm, A_WIDTH), lambda i: (i, COL_V // A_WIDTH)),
            pl.BlockSpec((1, A_WIDTH), lambda i: (0, 0)),
            pl.BlockSpec((1, A_WIDTH), lambda i: (0, 0)),
            pl.BlockSpec((A_GROUPS, CHUNK, CHUNK), lambda i: (0, 0, 0)),
            pl.BlockSpec((CHUNK, A_GROUPS), lambda i: (0, 0)),
        ],
        out_specs=out_specs,
        compiler_params=_cparams("parallel"),
        name="sgu",
    )(z, z, ln_g, ln_b, ws, bst)


def _attn_kernel(q_ref, kc_ref, kp_ref, vc_ref, vp_ref, o_ref, lse_ref, *, qblocks):
    i = pl.program_id(2)
    qi = lax.broadcasted_iota(jnp.int32, (CHUNK, CHUNK), 0)
    ki = lax.broadcasted_iota(jnp.int32, (CHUNK, CHUNK), 1)
    cur_mask = ki <= qi
    no_prev = jnp.where(i == 0, CHUNK, 0)
    dn = (((1,), (1,)), ((), ()))
    rpc = q_ref.shape[1]
    nbq = CHUNK // rpc

    def load(ref, a, cols):
        return ref[a * nbq:(a + 1) * nbq, :, cols].reshape(CHUNK, HEAD_DIM)

    units = [(a, h) for a in range(qblocks) for h in range(HEADS_PER_GROUP)]
    for start in range(0, len(units), ATTN_UNITS_PER_BATCH):
        batch_units = units[start:start + ATTN_UNITS_PER_BATCH]
        scores = []
        for a, h in batch_units:
            cols = slice(h * HEAD_DIM, (h + 1) * HEAD_DIM)
            q = load(q_ref, a, cols)
            if a == 0:
                kp, prev_mask = load(kp_ref, 0, cols), ki >= qi + no_prev
            else:
                kp, prev_mask = load(kc_ref, a - 1, cols), ki >= qi
            s_c = lax.dot_general(q, load(kc_ref, a, cols), dn, preferred_element_type=F32) * ATTN_SCALE
            s_p = lax.dot_general(q, kp, dn, preferred_element_type=F32) * ATTN_SCALE
            scores.append((jnp.where(cur_mask, s_c, NEG), jnp.where(prev_mask, s_p, NEG)))
        probs = []
        for s_c, s_p in scores:
            mx = jnp.maximum(jnp.max(s_c, axis=-1, keepdims=True), jnp.max(s_p, axis=-1, keepdims=True))
            p_c = jnp.exp(s_c - mx)
            p_p = jnp.exp(s_p - mx)
            l = jnp.sum(p_c, axis=-1, keepdims=True) + jnp.sum(p_p, axis=-1, keepdims=True)
            probs.append((p_c.astype(BF16), p_p.astype(BF16), mx, l))
        for (a, h), (p_c, p_p, mx, l) in zip(batch_units, probs):
            cols = slice(h * HEAD_DIM, (h + 1) * HEAD_DIM)
            vp = load(vp_ref, 0, cols) if a == 0 else load(vc_ref, a - 1, cols)
            acc = jnp.dot(p_c, load(vc_ref, a, cols), preferred_element_type=F32)
            acc = acc + jnp.dot(p_p, vp, preferred_element_type=F32)
            o_ref[a * nbq:(a + 1) * nbq, :, cols] = (acc / l).astype(BF16).reshape(nbq, rpc, HEAD_DIM)
            lse_ref[a * CHUNK:(a + 1) * CHUNK, h:h + 1] = mx + jnp.log(l)


def _rows_per_class(dil):
    return min(PERM_BLOCK // dil, CHUNK)


def _attn_group(z, batch, seq, group):
    _, dil = DILATED_GROUPS[group]
    sub = seq // dil
    rpc = _rows_per_class(dil)
    qblocks = max(b for b in (4, 2, 1) if sub % (b * CHUNK) == 0)
    tq = qblocks * CHUNK
    cq, ck, cv = (group * QKV_COLS // GROUP_COLS + part for part in range(3))
    zv = z.reshape(batch, sub // rpc, dil, rpc, z.shape[-1])

    def cur(cb):
        return pl.BlockSpec((None, tq // rpc, None, rpc, GROUP_COLS), lambda b, r, i: (b, i, r, 0, cb))

    def prev(cb):
        return pl.BlockSpec((None, CHUNK // rpc, None, rpc, GROUP_COLS),
                            lambda b, r, i: (b, jnp.maximum(i * qblocks - 1, 0), r, 0, cb))

    o, lse = pl.pallas_call(
        functools.partial(_attn_kernel, qblocks=qblocks),
        out_shape=[jax.ShapeDtypeStruct((batch, sub // rpc, dil, rpc, GROUP_COLS), BF16),
                   jax.ShapeDtypeStruct((batch, dil, sub, HEADS_PER_GROUP), F32)],
        grid=(batch, dil, sub // tq),
        in_specs=[cur(cq), cur(ck), prev(ck), cur(cv), prev(cv)],
        out_specs=[pl.BlockSpec((None, tq // rpc, None, rpc, GROUP_COLS), lambda b, r, i: (b, i, r, 0, 0)),
                   pl.BlockSpec((None, None, tq, HEADS_PER_GROUP), lambda b, r, i: (b, r, i, 0))],
        compiler_params=_cparams("parallel", "parallel", "arbitrary"),
        name=f"attn_prompt_g{group}",
    )(zv, zv, zv, zv, zv)
    lse = lse.transpose(0, 2, 1, 3).reshape(batch * seq, HEADS_PER_GROUP)
    return o, lse


def _attn_sample_kernel(q_ref, k_ref, v_ref, c0_ref, c1_ref, c2_ref, o_ref, lse_ref, *, n_new):
    caches = (c0_ref, c1_ref, c2_ref)
    rowid = lax.broadcasted_iota(jnp.int32, (CHUNK, HEADS_PER_GROUP, 1), 0)
    for g, (win, dil) in enumerate(DILATED_GROUPS):
        cref = caches[g]
        hs = slice(g * HEADS_PER_GROUP, (g + 1) * HEADS_PER_GROUP)
        for t in range(n_new):
            res, first = t % dil, t // dil
            kc = cref[:, res, 0]
            vc = cref[:, res, 1]
            q = q_ref[t, hs, :]
            s = jnp.sum(kc * q[None], axis=-1, keepdims=True) * ATTN_SCALE
            if first > 0:
                s = jnp.where(rowid >= first, s, NEG)
            new_rows = [tn for tn in range(t + 1) if (t - tn) % dil == 0 and (t - tn) // dil <= win // dil]
            s_new = [jnp.sum(q * k_ref[tn, hs, :], axis=-1, keepdims=True) * ATTN_SCALE for tn in new_rows]
            mx = jnp.max(s, axis=0)
            for sn in s_new:
                mx = jnp.maximum(mx, sn)
            p = jnp.exp(s - mx[None])
            l = jnp.sum(p, axis=0)
            acc = jnp.sum(p * vc, axis=0)
            for tn, sn in zip(new_rows, s_new):
                pn = jnp.exp(sn - mx)
                l = l + pn
                acc = acc + pn * v_ref[tn, hs, :]
            o_ref[g, t] = acc / l
            lse_ref[g, t] = jnp.broadcast_to(mx + jnp.log(l), (HEADS_PER_GROUP, HEAD_DIM))


def _attn_sample(q, k, v, caches):
    bd, n_new = q.shape[:2]
    views, specs = [], []
    for (win, dil), c in zip(DILATED_GROUPS, caches):
        assert c.shape[2] == win and win == CHUNK * dil, "cache must hold exactly one full window"
        used = min(dil, n_new)
        views.append(c.reshape(1, bd, CHUNK, dil, 2, HEADS_PER_GROUP, HEAD_DIM))
        specs.append(pl.BlockSpec((None, None, CHUNK, used, 2, HEADS_PER_GROUP, HEAD_DIM),
                                  lambda b: (0, b, 0, 0, 0, 0, 0)))
    new_spec = pl.BlockSpec((None, n_new, N_DIL * HEADS_PER_GROUP, HEAD_DIM), lambda b: (b, 0, 0, 0))
    out_sds = jax.ShapeDtypeStruct((N_DIL, bd, n_new, HEADS_PER_GROUP, HEAD_DIM), F32)
    out_spec = pl.BlockSpec((N_DIL, None, n_new, HEADS_PER_GROUP, HEAD_DIM), lambda b: (0, b, 0, 0, 0))
    return pl.pallas_call(
        functools.partial(_attn_sample_kernel, n_new=n_new),
        out_shape=[out_sds, out_sds],
        grid=(bd,),
        in_specs=[new_spec, new_spec, new_spec] + specs,
        out_specs=[out_spec, out_spec],
        compiler_params=_cparams("parallel"),
        name="attn_sample",
    )(q, k, v, *views)


def _merge_kernel(sg_ref, o0_ref, o1_ref, o2_ref, l0_ref, l1_ref, l2_ref, ga_ref, gb_ref,
                  wa_ref, wb_ref, *rest, permuted):
    if permuted:
        pt_ref, m_ref, ob_ref = rest
    else:
        m_ref, ob_ref = rest
    j = pl.program_id(1)

    @pl.when(j == 0)
    def _():
        l0, l1, l2 = l0_ref[...], l1_ref[...], l2_ref[...]
        mx = jnp.maximum(jnp.maximum(l0, l1), l2)
        e0, e1, e2 = jnp.exp(l0 - mx), jnp.exp(l1 - mx), jnp.exp(l2 - mx)
        inv = 1.0 / (e0 + e1 + e2)
        ws = (e0 * inv, e1 * inv, e2 * inv)
        o_refs = (o0_ref, o1_ref, o2_ref)
        tm = ob_ref.shape[0]

        def o_rows(ref, rows):
            if len(ref.shape) == 2:
                return ref[rows, :]
            per = ref.shape[1] * ref.shape[2]
            return ref[rows.start // per:rows.stop // per].reshape(rows.stop - rows.start, GROUP_COLS)

        for blk in range(max(tm // PERM_BLOCK, 1)):
            rows = slice(blk * PERM_BLOCK, min((blk + 1) * PERM_BLOCK, tm))
            og = []
            for g in range(N_DIL):
                if permuted and g > 0:
                    og.append(jnp.dot(pt_ref[g - 1], o_rows(o_refs[g], rows), preferred_element_type=F32))
                else:
                    og.append(o_rows(o_refs[g], rows).astype(F32))
            for h in range(HEADS_PER_GROUP):
                cols = slice(h * HEAD_DIM, (h + 1) * HEAD_DIM)
                ob = sum(ws[g][rows, h:h + 1] * og[g][:, cols] for g in range(N_DIL))
                ob_ref[rows, cols] = ob.astype(BF16)

    ya = jnp.dot(sg_ref[...], wa_ref[...], preferred_element_type=F32)
    yb = jnp.dot(ob_ref[...], wb_ref[...], preferred_element_type=F32)
    m_ref[...] = (ga_ref[...].astype(F32) * ya + gb_ref[...].astype(F32) * yb).astype(BF16)


def _merge(z, sg, os_, lses, wa, wb, tm, tn, perms_t=None):
    m = z.shape[0]
    assert COL_GA % tn == 0 and COL_GB % tn == 0
    row = lambda width: pl.BlockSpec((tm, width), lambda i, j: (i, 0))

    def o_spec(o):
        if o.ndim == 2:
            return row(GROUP_COLS)
        _, blocks, dil, rpc, _ = o.shape
        per_tile, tiles = tm // (dil * rpc), blocks * dil * rpc // tm
        return pl.BlockSpec((None, per_tile, dil, rpc, GROUP_COLS), lambda i, j: (i // tiles, i % tiles, 0, 0, 0))

    in_specs = [row(A_WIDTH)] + [o_spec(o) for o in os_] + [row(HEADS_PER_GROUP)] * 3 + [
        pl.BlockSpec((tm, tn), lambda i, j: (i, COL_GA // tn + j)),
        pl.BlockSpec((tm, tn), lambda i, j: (i, COL_GB // tn + j)),
        pl.BlockSpec((A_WIDTH, tn), lambda i, j: (0, j)),
        pl.BlockSpec((GROUP_COLS, tn), lambda i, j: (0, j)),
    ]
    args = [sg, *os_, *lses, z, z, wa, wb]
    if perms_t is not None:
        assert tm % PERM_BLOCK == 0
        in_specs.append(pl.BlockSpec(perms_t.shape, lambda i, j: (0, 0, 0)))
        args.append(perms_t)
    return pl.pallas_call(
        functools.partial(_merge_kernel, permuted=perms_t is not None),
        out_shape=jax.ShapeDtypeStruct((m, D_MODEL), BF16),
        grid=(m // tm, D_MODEL // tn),
        in_specs=in_specs,
        out_specs=pl.BlockSpec((tm, tn), lambda i, j: (i, j)),
        scratch_shapes=[pltpu.VMEM((tm, GROUP_COLS), BF16)],
        compiler_params=_cparams("parallel", "arbitrary"),
        name="branch_merge",
    )(*args)


def _oproj_kernel(m_ref, x_ref, wo_ref, g_ref, wr_ref, br_ref, h_ref, hn_ref, ri_ref, rw_ref):
    h = x_ref[...] + jnp.dot(m_ref[...], wo_ref[...], preferred_element_type=F32)
    h_ref[...] = h
    ms = jnp.mean(h * h, axis=-1, keepdims=True)
    hn = h * lax.rsqrt(ms + RMS_EPS) * g_ref[...]
    _store_token_tiles(hn_ref, hn)
    logits = jnp.dot(hn.astype(BF16), wr_ref[...], preferred_element_type=F32) + br_ref[...]
    lane = lax.broadcasted_iota(jnp.int32, logits.shape, 1).astype(F32)
    big = float(LANES)

    def first_argmax(vals, vmax):
        return jnp.min(jnp.where(vals == vmax, lane, big), axis=-1, keepdims=True)

    lg = jnp.where(lane < N_EXPERT_GROUPS, logits, NEG)
    gmax = jnp.max(lg, axis=-1, keepdims=True)
    gsel = first_argmax(lg, gmax)
    p_sel = 1.0 / jnp.sum(jnp.exp(lg - gmax), axis=-1, keepdims=True)
    lo = N_EXPERT_GROUPS + EXPERTS_PER_GROUP * gsel
    le = jnp.where(jnp.logical_and(lane >= lo, lane < lo + EXPERTS_PER_GROUP), logits, NEG)
    v1 = jnp.max(le, axis=-1, keepdims=True)
    i1 = first_argmax(le, v1)
    le2 = jnp.where(lane == i1, NEG, le)
    v2 = jnp.max(le2, axis=-1, keepdims=True)
    i2 = first_argmax(le2, v2)
    e2 = jnp.exp(v2 - v1)
    w1 = p_sel / (1.0 + e2)
    w2 = p_sel * e2 / (1.0 + e2)
    ri = jnp.where(lane == 0, i1 - N_EXPERT_GROUPS, jnp.where(lane == 1, i2 - N_EXPERT_GROUPS, 0.0))
    ri_ref[...] = ri.astype(jnp.int32)
    rw_ref[...] = jnp.where(lane == 0, w1, jnp.where(lane == 1, w2, 0.0))


def _oproj(mm, x, wo, norm_g, wr, br, tm):
    m = x.shape[0]
    row = lambda width: pl.BlockSpec((tm, width), lambda i: (i, 0))
    full = lambda a, b: pl.BlockSpec((a, b), lambda i: (0, 0))
    return pl.pallas_call(
        _oproj_kernel,
        out_shape=[jax.ShapeDtypeStruct((m, D_MODEL), F32), jax.ShapeDtypeStruct((m * ROW_TILE, LANES), jnp.uint32),
                   jax.ShapeDtypeStruct((m, LANES), jnp.int32), jax.ShapeDtypeStruct((m, LANES), F32)],
        grid=(m // tm,),
        in_specs=[row(D_MODEL), row(D_MODEL), full(D_MODEL, D_MODEL), full(1, D_MODEL),
                  full(D_MODEL, LANES), full(1, LANES)],
        out_specs=[row(D_MODEL), pl.BlockSpec((tm * ROW_TILE, LANES), lambda i: (i, 0)), row(LANES), row(LANES)],
        compiler_params=_cparams("parallel"),
        name="oproj_router",
    )(mm, x, wo, norm_g, wr, br)


def _dispatch_kernel(slot_ref, pad_start_ref, pad_len_ref, hn_ref, xs_hbm, zbuf, sem, zsem, *, tm, moe_tm, n_tiles):
    i = pl.program_id(0)

    def zero_copy(dst_row, rows):
        return pltpu.make_async_copy(zbuf.at[pl.ds(0, rows * ROW_TILE)],
                                     xs_hbm.at[pl.ds(dst_row * ROW_TILE, rows * ROW_TILE)], zsem)

    def zero_fill(act):
        def per_expert(e, carry):
            off, length = pad_start_ref[e], pad_len_ref[e]
            rows = moe_tm // 2
            while rows >= 1:
                @pl.when((length & rows) != 0)
                def _(off=off, rows=rows):
                    act(zero_copy(off, rows))
                off = off + (length & rows)
                rows //= 2
            return carry

        lax.fori_loop(0, N_EXPERTS, per_expert, 0)

        def per_tile(t, carry):
            act(zero_copy(t * moe_tm, moe_tm))
            return carry

        lax.fori_loop(pad_start_ref[N_EXPERTS], n_tiles, per_tile, 0)

    @pl.when(i == 0)
    def _():
        zbuf[...] = jnp.zeros_like(zbuf)
        zero_fill(lambda c: c.start())

    def issue(r, carry):
        for k in range(TOP_K):
            slot = slot_ref[(i * tm + r) * TOP_K + k]
            pltpu.make_async_copy(hn_ref.at[pl.ds(r * ROW_TILE, ROW_TILE)],
                                  xs_hbm.at[pl.ds(slot * ROW_TILE, ROW_TILE)], sem).start(priority=k % 2)
        return carry

    lax.fori_loop(0, tm, issue, 0, unroll=8)
    for k in range(TOP_K):
        pltpu.make_async_copy(hn_ref, hn_ref, sem).wait()

    @pl.when(i == pl.num_programs(0) - 1)
    def _():
        zero_fill(lambda c: c.wait())


def _dispatch(hn_tiles, slot, pad_start, pad_len, n_tiles, moe_tm, tm):
    m = hn_tiles.shape[0] // ROW_TILE
    assert m % tm == 0 and moe_tm & (moe_tm - 1) == 0
    grid_spec = pltpu.PrefetchScalarGridSpec(
        num_scalar_prefetch=3,
        grid=(m // tm,),
        in_specs=[pl.BlockSpec((tm * ROW_TILE, LANES), lambda i, s, ps, pn: (i, 0))],
        out_specs=pl.BlockSpec(memory_space=pl.ANY),
        scratch_shapes=[pltpu.VMEM((moe_tm * ROW_TILE, LANES), jnp.uint32),
                        pltpu.SemaphoreType.DMA(()), pltpu.SemaphoreType.DMA(())],
    )
    return pl.pallas_call(
        functools.partial(_dispatch_kernel, tm=tm, moe_tm=moe_tm, n_tiles=n_tiles),
        out_shape=jax.ShapeDtypeStruct((n_tiles * moe_tm * ROW_TILE, LANES), jnp.uint32),
        grid_spec=grid_spec,
        compiler_params=_cparams("arbitrary"),
        name="moe_dispatch",
    )(slot, pad_start, pad_len, hn_tiles)


def _moe_kernel(texp_ref, valid_ref, xs_ref, wg_ref, wu_ref, wd_ref, ys_ref):
    i = pl.program_id(0)
    tm = xs_ref.shape[0] // ROW_TILE
    valid = valid_ref[i]

    @pl.when(valid > 0)
    def _():
        halves = [_load_token_tile_chunk(xs_ref, (), tm, s) for s in range(ROW_TILE)]
        x = jnp.concatenate([lo.astype(BF16) for lo, _ in halves] + [hi.astype(BF16) for _, hi in halves], axis=1)
        gate = jnp.dot(x, wg_ref[...], preferred_element_type=F32)
        up = jnp.dot(x, wu_ref[...], preferred_element_type=F32)
        hid = (gate * _sigmoid(gate) * up).astype(BF16)
        _store_token_tiles(ys_ref, jnp.dot(hid, wd_ref[...], preferred_element_type=F32))

    @pl.when(valid == 0)
    def _():
        ys_ref[...] = jnp.zeros_like(ys_ref)


def _moe(xs, wg, wu, wd, tile_expert, tile_valid, tm):
    n_tiles = tile_expert.shape[0]
    grid_spec = pltpu.PrefetchScalarGridSpec(
        num_scalar_prefetch=2,
        grid=(n_tiles,),
        in_specs=[
            pl.BlockSpec((tm * ROW_TILE, LANES), lambda i, te, tv: (i, 0)),
            pl.BlockSpec((None, D_MODEL, D_EXPERT), lambda i, te, tv: (te[i], 0, 0)),
            pl.BlockSpec((None, D_MODEL, D_EXPERT), lambda i, te, tv: (te[i], 0, 0)),
            pl.BlockSpec((None, D_EXPERT, D_MODEL), lambda i, te, tv: (te[i], 0, 0)),
        ],
        out_specs=pl.BlockSpec((tm * ROW_TILE, LANES), lambda i, te, tv: (i, 0)),
    )
    return pl.pallas_call(
        _moe_kernel,
        out_shape=jax.ShapeDtypeStruct((n_tiles * tm * ROW_TILE, LANES), jnp.uint32),
        grid_spec=grid_spec,
        compiler_params=_cparams("arbitrary"),
        name="moe_experts",
    )(tile_expert, tile_valid, xs, wg, wu, wd)


def _combine_kernel(slot_ref, h_ref, rw_ref, g_ref, ys_hbm, y_ref, ybuf, sem, *, tm):
    i = pl.program_id(0)
    n_steps = pl.num_programs(0)

    def issue(tile, buf):
        def body(r, carry):
            for k in range(TOP_K):
                slot = slot_ref[(tile * tm + r) * TOP_K + k]
                pltpu.make_async_copy(ys_hbm.at[pl.ds(slot * ROW_TILE, ROW_TILE)],
                                      ybuf.at[buf, k, pl.ds(r * ROW_TILE, ROW_TILE)], sem.at[buf]).start(priority=k % 2)
            return carry

        lax.fori_loop(0, tm, body, 0, unroll=8)

    @pl.when(i == 0)
    def _():
        issue(0, 0)

    @pl.when(i + 1 < n_steps)
    def _():
        issue(i + 1, (i + 1) % 2)

    cur = i % 2
    for k in range(TOP_K):
        pltpu.make_async_copy(ybuf.at[cur, k], ybuf.at[cur, k], sem.at[cur]).wait()
    w0, w1 = rw_ref[:, 0:1], rw_ref[:, 1:2]
    chunks, ssq = {}, 0.0
    for s in range(ROW_TILE):
        e0 = _load_token_tile_chunk(ybuf, (cur, 0), tm, s)
        e1 = _load_token_tile_chunk(ybuf, (cur, 1), tm, s)
        for half in range(2):
            c = half * (D_MODEL // 2 // LANES) + s
            y = h_ref[:, c * LANES:(c + 1) * LANES] + w0 * e0[half] + w1 * e1[half]
            chunks[c] = y
            ssq = ssq + jnp.sum(y * y, axis=-1, keepdims=True)
    scale = lax.rsqrt(ssq * (1.0 / D_MODEL) + RMS_EPS)
    for c, y in chunks.items():
        cols = slice(c * LANES, (c + 1) * LANES)
        y_ref[:, cols] = y * scale * g_ref[:, cols]


def _combine(h, route_w, norm_g, ys, slot, tm):
    m = h.shape[0]
    grid_spec = pltpu.PrefetchScalarGridSpec(
        num_scalar_prefetch=1,
        grid=(m // tm,),
        in_specs=[
            pl.BlockSpec((tm, D_MODEL), lambda i, s: (i, 0)),
            pl.BlockSpec((tm, LANES), lambda i, s: (i, 0)),
            pl.BlockSpec((1, D_MODEL), lambda i, s: (0, 0)),
            pl.BlockSpec(memory_space=pl.ANY),
        ],
        out_specs=pl.BlockSpec((tm, D_MODEL), lambda i, s: (i, 0)),
        scratch_shapes=[pltpu.VMEM((2, TOP_K, tm * ROW_TILE, LANES), jnp.uint32), pltpu.SemaphoreType.DMA((2,))],
    )
    return pl.pallas_call(
        functools.partial(_combine_kernel, tm=tm),
        out_shape=jax.ShapeDtypeStruct((m, D_MODEL), F32),
        grid_spec=grid_spec,
        compiler_params=_cparams("arbitrary"),
        name="moe_combine",
    )(slot, h, route_w, norm_g, ys)


def _routing_tables(route_i, m, tm):
    n = m * TOP_K
    n_tiles = pl.cdiv(n, tm) + N_EXPERTS - 1
    e_flat = route_i[:, :TOP_K].reshape(n)
    onehot = (e_flat[:, None] == jnp.arange(N_EXPERTS, dtype=jnp.int32)[None, :]).astype(jnp.int32)
    csum = jnp.cumsum(onehot, axis=0)
    rank = jnp.take_along_axis(csum, e_flat[:, None], axis=1)[:, 0] - 1
    counts = csum[-1]
    tiles_e = (counts + tm - 1) // tm
    tile_end = jnp.cumsum(tiles_e)
    tile_start = tile_end - tiles_e
    slot = (tile_start[e_flat] * tm + rank).astype(jnp.int32)
    tile_ids = jnp.arange(n_tiles, dtype=jnp.int32)
    tile_expert = jnp.sum((tile_end[None, :] <= tile_ids[:, None]).astype(jnp.int32), axis=1)
    tile_expert = jnp.minimum(tile_expert, N_EXPERTS - 1)
    tile_valid = jnp.clip(counts[tile_expert] - (tile_ids - tile_start[tile_expert]) * tm, 0, tm)
    tile_valid = jnp.where(tile_ids < tile_end[-1], tile_valid, 0).astype(jnp.int32)
    pad_start = jnp.concatenate([tile_start * tm + counts, tile_end[-1:]]).astype(jnp.int32)
    pad_len = (tiles_e * tm - counts).astype(jnp.int32)
    return slot, tile_expert.astype(jnp.int32), tile_valid, pad_start, pad_len


def _token_tail(x, z, sg, os_, lses, wts, tm, perms_t=None):
    mm = _merge(z, sg, os_, lses, wts["wa"], wts["wb"], tm=min(tm, 256), tn=D_MODEL, perms_t=perms_t)
    h, hn, route_i, route_w = _oproj(mm, x, wts["wo"], wts["norm_ffn"], wts["wr"], wts["br"], tm=min(tm, 256))
    moe_tm = min(tm, MOE_TM)
    slot, tile_expert, tile_valid, pad_start, pad_len = _routing_tables(route_i, x.shape[0], moe_tm)
    xs = _dispatch(hn, slot, pad_start, pad_len, tile_expert.shape[0], moe_tm, tm=min(tm, 512))
    ys = _moe(xs, wts["wg"], wts["wu"], wts["wd"], tile_expert, tile_valid, moe_tm)
    return _combine(h, route_w, wts["norm_final"], ys, slot, tm=min(tm, 512))


def _kv_rows(z, lead, keep_from, group, permuted):
    zz = z.reshape(lead + (z.shape[-1],))[:, keep_from:]
    base = group * QKV_COLS
    k = zz[..., base + GROUP_COLS: base + 2 * GROUP_COLS]
    v = zz[..., base + 2 * GROUP_COLS: base + 3 * GROUP_COLS]
    kv = jnp.stack([k, v], axis=2).astype(F32)
    dil = DILATED_GROUPS[group][1]
    if permuted and dil > 1:
        assert keep_from % PERM_BLOCK == 0 and kv.shape[1] % PERM_BLOCK == 0
        b, rows = kv.shape[:2]
        kv = kv.reshape(b, rows // PERM_BLOCK, dil, PERM_BLOCK // dil, 2, GROUP_COLS)
        kv = kv.transpose(0, 1, 3, 2, 4, 5).reshape(b, rows, 2, GROUP_COLS)
    return kv.reshape(kv.shape[:3] + (HEADS_PER_GROUP, HEAD_DIM))


def kernel(x_prompt, x_sample, cache_kv_w128, cache_kv_w512, cache_kv_w2048, norm_mix, w_in, ln_v_g, ln_v_b, w_s, b_s, w_a_out, w_b_out, w_o, norm_ffn, w_route_group, b_route_group, w_route_expert, b_route_expert, w_gate_e, w_up_e, w_down_e, norm_final):
    assert norm_mix.shape[0] == 1, "single-layer trunk"
    batch, seq, _ = x_prompt.shape
    bd, n_new, _ = x_sample.shape
    caches = (cache_kv_w128, cache_kv_w512, cache_kv_w2048)

    pad = LANES - N_EXPERT_GROUPS - N_EXPERTS
    wr = jnp.concatenate([w_route_group[0], w_route_expert[0], jnp.zeros((D_MODEL, pad), F32)], axis=1)
    br = jnp.concatenate([b_route_group[0], b_route_expert[0], jnp.zeros((pad,), F32)])[None, :]
    wts = dict(
        wa=w_a_out[0].astype(BF16), wb=w_b_out[0].astype(BF16), wo=w_o[0].astype(BF16),
        wg=w_gate_e[0].astype(BF16), wu=w_up_e[0].astype(BF16), wd=w_down_e[0].astype(BF16),
        wr=wr.astype(BF16), br=br, norm_ffn=norm_ffn[0][None, :], norm_final=norm_final[None, :],
    )
    w_in_b = w_in[0].astype(BF16)
    norm_g = norm_mix[0][None, :]
    ln_g, ln_b = ln_v_g[0][None, :], ln_v_b[0][None, :]

    xp = x_prompt.reshape(batch * seq, D_MODEL)
    perms = jnp.stack([_class_major_perm(dil) for _, dil in DILATED_GROUPS[1:]])
    zp_a, zp_b = _inproj_all(xp, norm_g, w_in_b, tm=1024, perms=perms)
    (sgp,) = _sgu(zp_a, ln_g, ln_b, w_s[0], b_s[0].T, chunks=8, emit_vn=False)
    os_p, lses_p = zip(*[_attn_group(zp_b, batch, seq, g) for g in range(N_DIL)])
    y_prompt = _token_tail(xp, zp_a, sgp, os_p, lses_p, wts, tm=512,
                           perms_t=perms.transpose(0, 2, 1)).reshape(batch, seq, D_MODEL)
    kv_prompt = [_kv_rows(zp_b, (batch, seq), seq - min(win, seq), g, permuted=True)[None]
                 for g, (win, _) in enumerate(DILATED_GROUPS)]

    ms = bd * n_new
    assert ms == CHUNK and n_new <= CHUNK
    xs = x_sample.reshape(ms, D_MODEL)
    zs_a, zs_b = _inproj_all(xs, norm_g, w_in_b, tm=ms)
    eye = jnp.eye(bd, dtype=F32)
    ws_s = jnp.einsum("ab,gts->gatbs", eye, w_s[0][:, :n_new, :n_new]).reshape(A_GROUPS, ms, ms)
    bst_s = jnp.tile(b_s[0][:, :n_new].T, (bd, 1))
    sgs, vns = _sgu(zs_a, ln_g, ln_b, ws_s, bst_s, chunks=1, emit_vn=True)
    zs5 = zs_b.astype(F32).reshape(bd, n_new, N_DIL, 3, HEADS_PER_GROUP, HEAD_DIM)
    qkv = [zs5[:, :, :, part].reshape(bd, n_new, N_DIL * HEADS_PER_GROUP, HEAD_DIM) for part in range(3)]
    o_s, lse_s = _attn_sample(*qkv, caches)
    os_s = [o_s[g].reshape(ms, GROUP_COLS).astype(BF16) for g in range(N_DIL)]
    lses_s = [lse_s[g, ..., 0].reshape(ms, HEADS_PER_GROUP) for g in range(N_DIL)]
    y_sample = _token_tail(xs, zs_a, sgs, os_s, lses_s, wts, tm=ms).reshape(bd, n_new, D_MODEL)
    kv_sample = [_kv_rows(zs_b, (bd, n_new), 0, g, permuted=False)[None] for g in range(N_DIL)]
    chunk_v = vns.reshape(1, bd, n_new, A_WIDTH)

    return (y_prompt, y_sample, *kv_prompt, *kv_sample, chunk_v)
```

```python
import functools

import jax
import jax.numpy as jnp
from jax import lax
from jax.experimental import pallas as pl
from jax.experimental.pallas import tpu as pltpu

F32 = jnp.float32
BF16 = jnp.bfloat16

D_MODEL = 2048
CHUNK = 128
A_GROUPS = 16
A_GROUP_DIM = 128
A_WIDTH = A_GROUPS * A_GROUP_DIM
HEAD_DIM = 128
HEADS_PER_GROUP = 4
GROUP_COLS = HEADS_PER_GROUP * HEAD_DIM
DILATED_GROUPS = ((128, 1), (512, 4), (2048, 16))
N_DIL = len(DILATED_GROUPS)
B_WIDTH = N_DIL * GROUP_COLS
IN_COLS = 2 * A_WIDTH + 3 * B_WIDTH + 2 * D_MODEL
COL_U, COL_V = 0, A_WIDTH
COL_GA = 2 * A_WIDTH
COL_GB = COL_GA + D_MODEL
QKV_COLS = 3 * GROUP_COLS
N_EXPERT_GROUPS = 4
EXPERTS_PER_GROUP = 8
N_EXPERTS = N_EXPERT_GROUPS * EXPERTS_PER_GROUP
TOP_K = 2
D_EXPERT = 256
RMS_EPS = 1e-6
LN_EPS = 1e-5
NEG = -1e30
ATTN_SCALE = HEAD_DIM ** -0.5

LANES = 128
VMEM_LIMIT_BYTES = 56 * 1024 * 1024
IN_TN = 512
MOE_TM = 256
PERM_BLOCK = 256
ATTN_UNITS_PER_BATCH = 8
ROW_TILE = D_MODEL // LANES

def _cparams(*sem):
    return pltpu.CompilerParams(dimension_semantics=sem, vmem_limit_bytes=VMEM_LIMIT_BYTES)


def _gelu_tanh(x):
    return 0.5 * x * (1.0 + jnp.tanh(0.7978845608028654 * (x + 0.044715 * (x * x * x))))


def _sigmoid(x):
    return 1.0 / (1.0 + jnp.exp(-x))


def _store_token_tiles(ref, val):
    rows = val.shape[0]
    chunks = jnp.stack([val[:, s * LANES:(s + 1) * LANES].astype(BF16) for s in range(ROW_TILE)], axis=0)
    ref[...] = pltpu.einshape("stl->tsl", chunks).reshape(rows * ROW_TILE, LANES)


def _load_token_tile_chunks(ref, lead, rows):
    tiles = ref[lead + (slice(None), slice(None))].reshape(rows, ROW_TILE, LANES)
    by_chunk = pltpu.einshape("tsl->stl", tiles)
    return [by_chunk[s] for s in range(ROW_TILE)]


def _class_major_perm(dil):
    out_row = jnp.arange(PERM_BLOCK, dtype=jnp.int32)
    src = (out_row % (PERM_BLOCK // dil)) * dil + out_row // (PERM_BLOCK // dil)
    return (src[:, None] == jnp.arange(PERM_BLOCK, dtype=jnp.int32)[None, :]).astype(BF16)


def _inproj_kernel(x_ref, g_ref, *rest, strips, plan, permute):
    w_refs, rest = rest[:strips], rest[strips:]
    if permute:
        perm_ref, z_ref, xn_ref = rest
    else:
        z_ref, xn_ref = rest
    j = pl.program_id(1)

    @pl.when(j == 0)
    def _():
        x = x_ref[...]
        ms = jnp.mean(x * x, axis=-1, keepdims=True)
        xn_ref[...] = (x * lax.rsqrt(ms + RMS_EPS) * g_ref[...]).astype(BF16)

    tm = xn_ref.shape[0]
    sub = min(tm, PERM_BLOCK)

    def emit(epilogue):
        for blk in range(tm // sub):
            rows = slice(blk * sub, (blk + 1) * sub)
            for s, w_ref in enumerate(w_refs):
                z = jnp.dot(xn_ref[rows, :], w_ref[...], preferred_element_type=F32)
                z_ref[rows, s * IN_TN:(s + 1) * IN_TN] = epilogue(z).astype(BF16)

    def epilogue_of(kind):
        if kind == "gelu":
            return _gelu_tanh
        if kind == "gate":
            return _sigmoid
        if kind == "plain":
            return lambda z: z
        k = int(kind[len("perm"):])
        return lambda z: jnp.dot(perm_ref[k], z.astype(BF16), preferred_element_type=F32)

    for kind in sorted(set(plan)):
        tiles = [t for t, p in enumerate(plan) if p == kind]
        hit = functools.reduce(jnp.logical_or, [j == t for t in tiles])

        @pl.when(hit)
        def _():
            emit(epilogue_of(kind))


def _inproj(x, norm_g, w_in_bf16, tm, strips, plan, ref_tile, perms=None):
    m = x.shape[0]
    tn = strips * IN_TN
    in_specs = [pl.BlockSpec((tm, D_MODEL), lambda i, j: (i, 0)), pl.BlockSpec((1, D_MODEL), lambda i, j: (0, 0))]
    in_specs += [pl.BlockSpec((D_MODEL, IN_TN), functools.partial(lambda i, j, s: (0, ref_tile(j, s)), s=s))
                 for s in range(strips)]
    args = [x, norm_g] + [w_in_bf16] * strips
    if perms is not None:
        in_specs.append(pl.BlockSpec(perms.shape, lambda i, j: (0, 0, 0)))
        args.append(perms)
    return pl.pallas_call(
        functools.partial(_inproj_kernel, strips=strips, plan=plan, permute=perms is not None),
        out_shape=jax.ShapeDtypeStruct((m, len(plan) * tn), BF16),
        grid=(m // tm, len(plan)),
        in_specs=in_specs,
        out_specs=pl.BlockSpec((tm, tn), lambda i, j: (i, j)),
        scratch_shapes=[pltpu.VMEM((tm, D_MODEL), BF16)],
        compiler_params=_cparams("parallel", "arbitrary"),
        name="inproj",
    )(*args)


def _inproj_all(x, norm_g, w_in_bf16, tm, perms=None):
    n_gelu = 2 * A_WIDTH // IN_TN
    n_qkv = 3 * B_WIDTH // IN_TN
    strips_a = 2
    plan_a = ("gelu",) * (n_gelu // strips_a) + ("gate",) * (2 * D_MODEL // IN_TN // strips_a)

    def ref_a(j, s):
        t = j * strips_a + s
        return jnp.where(t < n_gelu, t, t + n_qkv)

    z_a = _inproj(x, norm_g, w_in_bf16, tm, strips_a, plan_a, ref_a)
    plan_b = ("plain",) + tuple(f"perm{k}" for k in range(N_DIL - 1)) if perms is not None else ("plain",) * N_DIL
    z_b = _inproj(x, norm_g, w_in_bf16, tm, 3, plan_b, lambda j, s: n_gelu + s * N_DIL + j, perms)
    return z_a, z_b


def _sgu_kernel(u_ref, v_ref, lng_ref, lnb_ref, ws_ref, bst_ref, sg_ref, *vn_out, chunks):
    row = lax.broadcasted_iota(jnp.int32, (CHUNK, CHUNK), 0)
    col = lax.broadcasted_iota(jnp.int32, (CHUNK, CHUNK), 1)
    tri = row >= col
    ws = [jnp.where(tri, ws_ref[g], 0.0).astype(BF16) for g in range(A_GROUPS)]
    for c in range(chunks):
        rows = slice(c * CHUNK, (c + 1) * CHUNK)
        v = v_ref[rows, :].astype(F32)
        mu = jnp.mean(v, axis=-1, keepdims=True)
        vc = v - mu
        var = jnp.mean(vc * vc, axis=-1, keepdims=True)
        vn = vc * lax.rsqrt(var + LN_EPS) * lng_ref[...] + lnb_ref[...]
        if vn_out:
            vn_out[0][rows, :] = vn
        vnb = vn.astype(BF16)
        for g in range(A_GROUPS):
            cols = slice(g * A_GROUP_DIM, (g + 1) * A_GROUP_DIM)
            s = jnp.dot(ws[g], vnb[:, cols], preferred_element_type=F32) + bst_ref[:, g:g + 1]
            sg_ref[rows, cols] = (u_ref[rows, cols].astype(F32) * s).astype(BF16)


def _sgu(z, ln_g, ln_b, ws, bst, chunks, emit_vn):
    m = z.shape[0]
    tm = chunks * CHUNK
    out_shape = [jax.ShapeDtypeStruct((m, A_WIDTH), BF16)]
    out_specs = [pl.BlockSpec((tm, A_WIDTH), lambda i: (i, 0))]
    if emit_vn:
        out_shape.append(jax.ShapeDtypeStruct((m, A_WIDTH), F32))
        out_specs.append(pl.BlockSpec((tm, A_WIDTH), lambda i: (i, 0)))
    return pl.pallas_call(
        functools.partial(_sgu_kernel, chunks=chunks),
        out_shape=out_shape,
        grid=(m // tm,),
        in_specs=[
            pl.BlockSpec((tm, A_WIDTH), lambda i: (i, COL_U // A_WIDTH)),
            pl.BlockSpec((tm, A_WIDTH), lambda i: (i, COL_V // A_WIDTH)),
            pl.BlockSpec((1, A_WIDTH), lambda i: (0, 0)),
            pl.BlockSpec((1, A_WIDTH), lambda i: (0, 0)),
            pl.BlockSpec((A_GROUPS, CHUNK, CHUNK), lambda i: (0, 0, 0)),
            pl.BlockSpec((CHUNK, A_GROUPS), lambda i: (0, 0)),
        ],
        out_specs=out_specs,
        compiler_params=_cparams("parallel"),
        name="sgu",
    )(z, z, ln_g, ln_b, ws, bst)


def _attn_kernel(q_ref, kc_ref, kp_ref, vc_ref, vp_ref, o_ref, lse_ref, *, qblocks):
    i = pl.program_id(2)
    qi = lax.broadcasted_iota(jnp.int32, (CHUNK, CHUNK), 0)
    ki = lax.broadcasted_iota(jnp.int32, (CHUNK, CHUNK), 1)
    cur_mask = ki <= qi
    no_prev = jnp.where(i == 0, CHUNK, 0)
    dn = (((1,), (1,)), ((), ()))
    rpc = q_ref.shape[1]
    nbq = CHUNK // rpc

    def load(ref, a, cols):
        return ref[a * nbq:(a + 1) * nbq, :, cols].reshape(CHUNK, HEAD_DIM)

    units = [(a, h) for a in range(qblocks) for h in range(HEADS_PER_GROUP)]
    for start in range(0, len(units), ATTN_UNITS_PER_BATCH):
        batch_units = units[start:start + ATTN_UNITS_PER_BATCH]
        scores = []
        for a, h in batch_units:
            cols = slice(h * HEAD_DIM, (h + 1) * HEAD_DIM)
            q = load(q_ref, a, cols)
            if a == 0:
                kp, prev_mask = load(kp_ref, 0, cols), ki >= qi + no_prev
            else:
                kp, prev_mask = load(kc_ref, a - 1, cols), ki >= qi
            s_c = lax.dot_general(q, load(kc_ref, a, cols), dn, preferred_element_type=F32) * ATTN_SCALE
            s_p = lax.dot_general(q, kp, dn, preferred_element_type=F32) * ATTN_SCALE
            scores.append((jnp.where(cur_mask, s_c, NEG), jnp.where(prev_mask, s_p, NEG)))
        probs = []
        for s_c, s_p in scores:
            mx = jnp.maximum(jnp.max(s_c, axis=-1, keepdims=True), jnp.max(s_p, axis=-1, keepdims=True))
            p_c = jnp.exp(s_c - mx)
            p_p = jnp.exp(s_p - mx)
            l = jnp.sum(p_c, axis=-1, keepdims=True) + jnp.sum(p_p, axis=-1, keepdims=True)
            probs.append((p_c.astype(BF16), p_p.astype(BF16), mx, l))
        for (a, h), (p_c, p_p, mx, l) in zip(batch_units, probs):
            cols = slice(h * HEAD_DIM, (h + 1) * HEAD_DIM)
            vp = load(vp_ref, 0, cols) if a == 0 else load(vc_ref, a - 1, cols)
            acc = jnp.dot(p_c, load(vc_ref, a, cols), preferred_element_type=F32)
            acc = acc + jnp.dot(p_p, vp, preferred_element_type=F32)
            o_ref[a * nbq:(a + 1) * nbq, :, cols] = (acc / l).astype(BF16).reshape(nbq, rpc, HEAD_DIM)
            lse_ref[a * CHUNK:(a + 1) * CHUNK, h:h + 1] = mx + jnp.log(l)


def _rows_per_class(dil):
    return min(PERM_BLOCK // dil, CHUNK)


def _attn_group(z, batch, seq, group):
    _, dil = DILATED_GROUPS[group]
    sub = seq // dil
    rpc = _rows_per_class(dil)
    qblocks = max(b for b in (4, 2, 1) if sub % (b * CHUNK) == 0)
    tq = qblocks * CHUNK
    cq, ck, cv = (group * QKV_COLS // GROUP_COLS + part for part in range(3))
    zv = z.reshape(batch, sub // rpc, dil, rpc, z.shape[-1])

    def cur(cb):
        return pl.BlockSpec((None, tq // rpc, None, rpc, GROUP_COLS), lambda b, r, i: (b, i, r, 0, cb))

    def prev(cb):
        return pl.BlockSpec((None, CHUNK // rpc, None, rpc, GROUP_COLS),
                            lambda b, r, i: (b, jnp.maximum(i * qblocks - 1, 0), r, 0, cb))

    o, lse = pl.pallas_call(
        functools.partial(_attn_kernel, qblocks=qblocks),
        out_shape=[jax.ShapeDtypeStruct((batch, sub // rpc, dil, rpc, GROUP_COLS), BF16),
                   jax.ShapeDtypeStruct((batch, dil, sub, HEADS_PER_GROUP), F32)],
        grid=(batch, dil, sub // tq),
        in_specs=[cur(cq), cur(ck), prev(ck), cur(cv), prev(cv)],
        out_specs=[pl.BlockSpec((None, tq // rpc, None, rpc, GROUP_COLS), lambda b, r, i: (b, i, r, 0, 0)),
                   pl.BlockSpec((None, None, tq, HEADS_PER_GROUP), lambda b, r, i: (b, r, i, 0))],
        compiler_params=_cparams("parallel", "parallel", "arbitrary"),
        name=f"attn_prompt_g{group}",
    )(zv, zv, zv, zv, zv)
    lse = lse.transpose(0, 2, 1, 3).reshape(batch * seq, HEADS_PER_GROUP)
    return o, lse


def _attn_sample_kernel(q_ref, k_ref, v_ref, c0_ref, c1_ref, c2_ref, o_ref, lse_ref, *, n_new):
    caches = (c0_ref, c1_ref, c2_ref)
    rowid = lax.broadcasted_iota(jnp.int32, (CHUNK, HEADS_PER_GROUP, 1), 0)
    for g, (win, dil) in enumerate(DILATED_GROUPS):
        cref = caches[g]
        hs = slice(g * HEADS_PER_GROUP, (g + 1) * HEADS_PER_GROUP)
        for t in range(n_new):
            res, first = t % dil, t // dil
            kc = cref[:, res, 0]
            vc = cref[:, res, 1]
            q = q_ref[t, hs, :]
            s = jnp.sum(kc * q[None], axis=-1, keepdims=True) * ATTN_SCALE
            if first > 0:
                s = jnp.where(rowid >= first, s, NEG)
            new_rows = [tn for tn in range(t + 1) if (t - tn) % dil == 0 and (t - tn) // dil <= win // dil]
            s_new = [jnp.sum(q * k_ref[tn, hs, :], axis=-1, keepdims=True) * ATTN_SCALE for tn in new_rows]
            mx = jnp.max(s, axis=0)
            for sn in s_new:
                mx = jnp.maximum(mx, sn)
            p = jnp.exp(s - mx[None])
            l = jnp.sum(p, axis=0)
            acc = jnp.sum(p * vc, axis=0)
            for tn, sn in zip(new_rows, s_new):
                pn = jnp.exp(sn - mx)
                l = l + pn
                acc = acc + pn * v_ref[tn, hs, :]
            o_ref[g, t] = acc / l
            lse_ref[g, t] = jnp.broadcast_to(mx + jnp.log(l), (HEADS_PER_GROUP, HEAD_DIM))


def _attn_sample(q, k, v, caches):
    bd, n_new = q.shape[:2]
    views, specs = [], []
    for (win, dil), c in zip(DILATED_GROUPS, caches):
        assert c.shape[2] == win and win == CHUNK * dil, "cache must hold exactly one full window"
        used = min(dil, n_new)
        views.append(c.reshape(1, bd, CHUNK, dil, 2, HEADS_PER_GROUP, HEAD_DIM))
        specs.append(pl.BlockSpec((None, None, CHUNK, used, 2, HEADS_PER_GROUP, HEAD_DIM),
                                  lambda b: (0, b, 0, 0, 0, 0, 0)))
    new_spec = pl.BlockSpec((None, n_new, N_DIL * HEADS_PER_GROUP, HEAD_DIM), lambda b: (b, 0, 0, 0))
    out_sds = jax.ShapeDtypeStruct((N_DIL, bd, n_new, HEADS_PER_GROUP, HEAD_DIM), F32)
    out_spec = pl.BlockSpec((N_DIL, None, n_new, HEADS_PER_GROUP, HEAD_DIM), lambda b: (0, b, 0, 0, 0))
    return pl.pallas_call(
        functools.partial(_attn_sample_kernel, n_new=n_new),
        out_shape=[out_sds, out_sds],
        grid=(bd,),
        in_specs=[new_spec, new_spec, new_spec] + specs,
        out_specs=[out_spec, out_spec],
        compiler_params=_cparams("parallel"),
        name="attn_sample",
    )(q, k, v, *views)


def _merge_kernel(sg_ref, o0_ref, o1_ref, o2_ref, l0_ref, l1_ref, l2_ref, ga_ref, gb_ref,
                  wa_ref, wb_ref, *rest, permuted):
    if permuted:
        pt_ref, m_ref, ob_ref = rest
    else:
        m_ref, ob_ref = rest
    j = pl.program_id(1)

    @pl.when(j == 0)
    def _():
        l0, l1, l2 = l0_ref[...], l1_ref[...], l2_ref[...]
        mx = jnp.maximum(jnp.maximum(l0, l1), l2)
        e0, e1, e2 = jnp.exp(l0 - mx), jnp.exp(l1 - mx), jnp.exp(l2 - mx)
        inv = 1.0 / (e0 + e1 + e2)
        ws = (e0 * inv, e1 * inv, e2 * inv)
        o_refs = (o0_ref, o1_ref, o2_ref)
        tm = ob_ref.shape[0]

        def o_rows(ref, rows):
            if len(ref.shape) == 2:
                return ref[rows, :]
            per = ref.shape[1] * ref.shape[2]
            return ref[rows.start // per:rows.stop // per].reshape(rows.stop - rows.start, GROUP_COLS)

        for blk in range(max(tm // PERM_BLOCK, 1)):
            rows = slice(blk * PERM_BLOCK, min((blk + 1) * PERM_BLOCK, tm))
            og = []
            for g in range(N_DIL):
                if permuted and g > 0:
                    og.append(jnp.dot(pt_ref[g - 1], o_rows(o_refs[g], rows), preferred_element_type=F32))
                else:
                    og.append(o_rows(o_refs[g], rows).astype(F32))
            for h in range(HEADS_PER_GROUP):
                cols = slice(h * HEAD_DIM, (h + 1) * HEAD_DIM)
                ob = sum(ws[g][rows, h:h + 1] * og[g][:, cols] for g in range(N_DIL))
                ob_ref[rows, cols] = ob.astype(BF16)

    ya = jnp.dot(sg_ref[...], wa_ref[...], preferred_element_type=F32)
    yb = jnp.dot(ob_ref[...], wb_ref[...], preferred_element_type=F32)
    m_ref[...] = (ga_ref[...].astype(F32) * ya + gb_ref[...].astype(F32) * yb).astype(BF16)


def _merge(z, sg, os_, lses, wa, wb, tm, tn, perms_t=None):
    m = z.shape[0]
    assert COL_GA % tn == 0 and COL_GB % tn == 0
    row = lambda width: pl.BlockSpec((tm, width), lambda i, j: (i, 0))

    def o_spec(o):
        if o.ndim == 2:
            return row(GROUP_COLS)
        _, blocks, dil, rpc, _ = o.shape
        per_tile, tiles = tm // (dil * rpc), blocks * dil * rpc // tm
        return pl.BlockSpec((None, per_tile, dil, rpc, GROUP_COLS), lambda i, j: (i // tiles, i % tiles, 0, 0, 0))

    in_specs = [row(A_WIDTH)] + [o_spec(o) for o in os_] + [row(HEADS_PER_GROUP)] * 3 + [
        pl.BlockSpec((tm, tn), lambda i, j: (i, COL_GA // tn + j)),
        pl.BlockSpec((tm, tn), lambda i, j: (i, COL_GB // tn + j)),
        pl.BlockSpec((A_WIDTH, tn), lambda i, j: (0, j)),
        pl.BlockSpec((GROUP_COLS, tn), lambda i, j: (0, j)),
    ]
    args = [sg, *os_, *lses, z, z, wa, wb]
    if perms_t is not None:
        assert tm % PERM_BLOCK == 0
        in_specs.append(pl.BlockSpec(perms_t.shape, lambda i, j: (0, 0, 0)))
        args.append(perms_t)
    return pl.pallas_call(
        functools.partial(_merge_kernel, permuted=perms_t is not None),
        out_shape=jax.ShapeDtypeStruct((m, D_MODEL), BF16),
        grid=(m // tm, D_MODEL // tn),
        in_specs=in_specs,
        out_specs=pl.BlockSpec((tm, tn), lambda i, j: (i, j)),
        scratch_shapes=[pltpu.VMEM((tm, GROUP_COLS), BF16)],
        compiler_params=_cparams("parallel", "arbitrary"),
        name="branch_merge",
    )(*args)


def _oproj_kernel(m_ref, x_ref, wo_ref, g_ref, wr_ref, br_ref, h_ref, hn_ref, ri_ref, rw_ref):
    h = x_ref[...] + jnp.dot(m_ref[...], wo_ref[...], preferred_element_type=F32)
    h_ref[...] = h
    ms = jnp.mean(h * h, axis=-1, keepdims=True)
    hn = h * lax.rsqrt(ms + RMS_EPS) * g_ref[...]
    _store_token_tiles(hn_ref, hn)
    logits = jnp.dot(hn.astype(BF16), wr_ref[...], preferred_element_type=F32) + br_ref[...]
    lane = lax.broadcasted_iota(jnp.int32, logits.shape, 1).astype(F32)
    big = float(LANES)

    def first_argmax(vals, vmax):
        return jnp.min(jnp.where(vals == vmax, lane, big), axis=-1, keepdims=True)

    lg = jnp.where(lane < N_EXPERT_GROUPS, logits, NEG)
    gmax = jnp.max(lg, axis=-1, keepdims=True)
    gsel = first_argmax(lg, gmax)
    p_sel = 1.0 / jnp.sum(jnp.exp(lg - gmax), axis=-1, keepdims=True)
    lo = N_EXPERT_GROUPS + EXPERTS_PER_GROUP * gsel
    le = jnp.where(jnp.logical_and(lane >= lo, lane < lo + EXPERTS_PER_GROUP), logits, NEG)
    v1 = jnp.max(le, axis=-1, keepdims=True)
    i1 = first_argmax(le, v1)
    le2 = jnp.where(lane == i1, NEG, le)
    v2 = jnp.max(le2, axis=-1, keepdims=True)
    i2 = first_argmax(le2, v2)
    e2 = jnp.exp(v2 - v1)
    w1 = p_sel / (1.0 + e2)
    w2 = p_sel * e2 / (1.0 + e2)
    ri = jnp.where(lane == 0, i1 - N_EXPERT_GROUPS, jnp.where(lane == 1, i2 - N_EXPERT_GROUPS, 0.0))
    ri_ref[...] = ri.astype(jnp.int32)
    rw_ref[...] = jnp.where(lane == 0, w1, jnp.where(lane == 1, w2, 0.0))


def _oproj(mm, x, wo, norm_g, wr, br, tm):
    m = x.shape[0]
    row = lambda width: pl.BlockSpec((tm, width), lambda i: (i, 0))
    full = lambda a, b: pl.BlockSpec((a, b), lambda i: (0, 0))
    return pl.pallas_call(
        _oproj_kernel,
        out_shape=[jax.ShapeDtypeStruct((m, D_MODEL), F32), jax.ShapeDtypeStruct((m * ROW_TILE, LANES), BF16),
                   jax.ShapeDtypeStruct((m, LANES), jnp.int32), jax.ShapeDtypeStruct((m, LANES), F32)],
        grid=(m // tm,),
        in_specs=[row(D_MODEL), row(D_MODEL), full(D_MODEL, D_MODEL), full(1, D_MODEL),
                  full(D_MODEL, LANES), full(1, LANES)],
        out_specs=[row(D_MODEL), pl.BlockSpec((tm * ROW_TILE, LANES), lambda i: (i, 0)), row(LANES), row(LANES)],
        compiler_params=_cparams("parallel"),
        name="oproj_router",
    )(mm, x, wo, norm_g, wr, br)


def _dispatch_kernel(slot_ref, pad_start_ref, pad_len_ref, hn_ref, xs_hbm, zbuf, sem, zsem, *, tm, moe_tm, n_tiles):
    i = pl.program_id(0)

    def zero_copy(dst_row, rows):
        return pltpu.make_async_copy(zbuf.at[pl.ds(0, rows * ROW_TILE)],
                                     xs_hbm.at[pl.ds(dst_row * ROW_TILE, rows * ROW_TILE)], zsem)

    def zero_fill(act):
        def per_expert(e, carry):
            off, length = pad_start_ref[e], pad_len_ref[e]
            rows = moe_tm // 2
            while rows >= 1:
                @pl.when((length & rows) != 0)
                def _(off=off, rows=rows):
                    act(zero_copy(off, rows))
                off = off + (length & rows)
                rows //= 2
            return carry

        lax.fori_loop(0, N_EXPERTS, per_expert, 0)

        def per_tile(t, carry):
            act(zero_copy(t * moe_tm, moe_tm))
            return carry

        lax.fori_loop(pad_start_ref[N_EXPERTS], n_tiles, per_tile, 0)

    @pl.when(i == 0)
    def _():
        zbuf[...] = jnp.zeros_like(zbuf)
        zero_fill(lambda c: c.start())

    def issue(r, carry):
        for k in range(TOP_K):
            slot = slot_ref[(i * tm + r) * TOP_K + k]
            pltpu.make_async_copy(hn_ref.at[pl.ds(r * ROW_TILE, ROW_TILE)],
                                  xs_hbm.at[pl.ds(slot * ROW_TILE, ROW_TILE)], sem).start(priority=k % 2)
        return carry

    lax.fori_loop(0, tm, issue, 0, unroll=8)
    for k in range(TOP_K):
        pltpu.make_async_copy(hn_ref, hn_ref, sem).wait()

    @pl.when(i == pl.num_programs(0) - 1)
    def _():
        zero_fill(lambda c: c.wait())


def _dispatch(hn_tiles, slot, pad_start, pad_len, n_tiles, moe_tm, tm):
    m = hn_tiles.shape[0] // ROW_TILE
    assert m % tm == 0 and moe_tm & (moe_tm - 1) == 0
    grid_spec = pltpu.PrefetchScalarGridSpec(
        num_scalar_prefetch=3,
        grid=(m // tm,),
        in_specs=[pl.BlockSpec((tm * ROW_TILE, LANES), lambda i, s, ps, pn: (i, 0))],
        out_specs=pl.BlockSpec(memory_space=pl.ANY),
        scratch_shapes=[pltpu.VMEM((moe_tm * ROW_TILE, LANES), BF16),
                        pltpu.SemaphoreType.DMA(()), pltpu.SemaphoreType.DMA(())],
    )
    return pl.pallas_call(
        functools.partial(_dispatch_kernel, tm=tm, moe_tm=moe_tm, n_tiles=n_tiles),
        out_shape=jax.ShapeDtypeStruct((n_tiles * moe_tm * ROW_TILE, LANES), BF16),
        grid_spec=grid_spec,
        compiler_params=_cparams("arbitrary"),
        name="moe_dispatch",
    )(slot, pad_start, pad_len, hn_tiles)


def _moe_kernel(texp_ref, valid_ref, xs_ref, wg_ref, wu_ref, wd_ref, ys_ref, wg_bf, wu_bf, wd_bf):
    i = pl.program_id(0)
    tm = xs_ref.shape[0] // ROW_TILE
    valid = valid_ref[i]
    new_expert = jnp.logical_or(i == 0, texp_ref[i] != texp_ref[jnp.maximum(i - 1, 0)])

    @pl.when(jnp.logical_and(valid > 0, new_expert))
    def _():
        wg_bf[...] = wg_ref[...].astype(BF16)
        wu_bf[...] = wu_ref[...].astype(BF16)
        wd_bf[...] = wd_ref[...].astype(BF16)

    @pl.when(valid > 0)
    def _():
        x = jnp.concatenate(_load_token_tile_chunks(xs_ref, (), tm), axis=1)
        gate = jnp.dot(x, wg_bf[...], preferred_element_type=F32)
        up = jnp.dot(x, wu_bf[...], preferred_element_type=F32)
        hid = (gate * _sigmoid(gate) * up).astype(BF16)
        _store_token_tiles(ys_ref, jnp.dot(hid, wd_bf[...], preferred_element_type=F32))

    @pl.when(valid == 0)
    def _():
        ys_ref[...] = jnp.zeros_like(ys_ref)


def _moe(xs, wg, wu, wd, tile_expert, tile_valid, tm):
    n_tiles = tile_expert.shape[0]
    grid_spec = pltpu.PrefetchScalarGridSpec(
        num_scalar_prefetch=2,
        grid=(n_tiles,),
        in_specs=[
            pl.BlockSpec((tm * ROW_TILE, LANES), lambda i, te, tv: (i, 0)),
            pl.BlockSpec((None, D_MODEL, D_EXPERT), lambda i, te, tv: (te[i], 0, 0)),
            pl.BlockSpec((None, D_MODEL, D_EXPERT), lambda i, te, tv: (te[i], 0, 0)),
            pl.BlockSpec((None, D_EXPERT, D_MODEL), lambda i, te, tv: (te[i], 0, 0)),
        ],
        out_specs=pl.BlockSpec((tm * ROW_TILE, LANES), lambda i, te, tv: (i, 0)),
        scratch_shapes=[pltpu.VMEM((D_MODEL, D_EXPERT), BF16), pltpu.VMEM((D_MODEL, D_EXPERT), BF16),
                        pltpu.VMEM((D_EXPERT, D_MODEL), BF16)],
    )
    return pl.pallas_call(
        _moe_kernel,
        out_shape=jax.ShapeDtypeStruct((n_tiles * tm * ROW_TILE, LANES), BF16),
        grid_spec=grid_spec,
        compiler_params=_cparams("arbitrary"),
        name="moe_experts",
    )(tile_expert, tile_valid, xs, wg, wu, wd)


def _combine_kernel(slot_ref, h_ref, rw_ref, g_ref, ys_hbm, y_ref, ybuf, sem, *, tm):
    i = pl.program_id(0)
    n_steps = pl.num_programs(0)

    def issue(tile, buf):
        def body(r, carry):
            for k in range(TOP_K):
                slot = slot_ref[(tile * tm + r) * TOP_K + k]
                pltpu.make_async_copy(ys_hbm.at[pl.ds(slot * ROW_TILE, ROW_TILE)],
                                      ybuf.at[buf, k, pl.ds(r * ROW_TILE, ROW_TILE)], sem.at[buf]).start(priority=k % 2)
            return carry

        lax.fori_loop(0, tm, body, 0, unroll=8)

    @pl.when(i == 0)
    def _():
        issue(0, 0)

    @pl.when(i + 1 < n_steps)
    def _():
        issue(i + 1, (i + 1) % 2)

    cur = i % 2
    for k in range(TOP_K):
        pltpu.make_async_copy(ybuf.at[cur, k], ybuf.at[cur, k], sem.at[cur]).wait()
    w0, w1 = rw_ref[:, 0:1], rw_ref[:, 1:2]
    e0 = _load_token_tile_chunks(ybuf, (cur, 0), tm)
    e1 = _load_token_tile_chunks(ybuf, (cur, 1), tm)
    chunks, ssq = [], 0.0
    for s in range(ROW_TILE):
        y = h_ref[:, s * LANES:(s + 1) * LANES] + w0 * e0[s].astype(F32) + w1 * e1[s].astype(F32)
        chunks.append(y)
        ssq = ssq + jnp.sum(y * y, axis=-1, keepdims=True)
    scale = lax.rsqrt(ssq * (1.0 / D_MODEL) + RMS_EPS)
    for s, y in enumerate(chunks):
        cols = slice(s * LANES, (s + 1) * LANES)
        y_ref[:, cols] = y * scale * g_ref[:, cols]


def _combine(h, route_w, norm_g, ys, slot, tm):
    m = h.shape[0]
    grid_spec = pltpu.PrefetchScalarGridSpec(
        num_scalar_prefetch=1,
        grid=(m // tm,),
        in_specs=[
            pl.BlockSpec((tm, D_MODEL), lambda i, s: (i, 0)),
            pl.BlockSpec((tm, LANES), lambda i, s: (i, 0)),
            pl.BlockSpec((1, D_MODEL), lambda i, s: (0, 0)),
            pl.BlockSpec(memory_space=pl.ANY),
        ],
        out_specs=pl.BlockSpec((tm, D_MODEL), lambda i, s: (i, 0)),
        scratch_shapes=[pltpu.VMEM((2, TOP_K, tm * ROW_TILE, LANES), BF16), pltpu.SemaphoreType.DMA((2,))],
    )
    return pl.pallas_call(
        functools.partial(_combine_kernel, tm=tm),
        out_shape=jax.ShapeDtypeStruct((m, D_MODEL), F32),
        grid_spec=grid_spec,
        compiler_params=_cparams("arbitrary"),
        name="moe_combine",
    )(slot, h, route_w, norm_g, ys)


def _routing_tables(route_i, m, tm):
    n = m * TOP_K
    n_tiles = pl.cdiv(n, tm) + N_EXPERTS - 1
    e_flat = route_i[:, :TOP_K].reshape(n)
    onehot = (e_flat[:, None] == jnp.arange(N_EXPERTS, dtype=jnp.int32)[None, :]).astype(jnp.int32)
    csum = jnp.cumsum(onehot, axis=0)
    rank = jnp.take_along_axis(csum, e_flat[:, None], axis=1)[:, 0] - 1
    counts = csum[-1]
    tiles_e = (counts + tm - 1) // tm
    tile_end = jnp.cumsum(tiles_e)
    tile_start = tile_end - tiles_e
    slot = (tile_start[e_flat] * tm + rank).astype(jnp.int32)
    tile_ids = jnp.arange(n_tiles, dtype=jnp.int32)
    tile_expert = jnp.sum((tile_end[None, :] <= tile_ids[:, None]).astype(jnp.int32), axis=1)
    tile_expert = jnp.minimum(tile_expert, N_EXPERTS - 1)
    tile_valid = jnp.clip(counts[tile_expert] - (tile_ids - tile_start[tile_expert]) * tm, 0, tm)
    tile_valid = jnp.where(tile_ids < tile_end[-1], tile_valid, 0).astype(jnp.int32)
    pad_start = jnp.concatenate([tile_start * tm + counts, tile_end[-1:]]).astype(jnp.int32)
    pad_len = (tiles_e * tm - counts).astype(jnp.int32)
    return slot, tile_expert.astype(jnp.int32), tile_valid, pad_start, pad_len


def _token_tail(x, z, sg, os_, lses, wts, tm, perms_t=None):
    mm = _merge(z, sg, os_, lses, wts["wa"], wts["wb"], tm=min(tm, 256), tn=D_MODEL, perms_t=perms_t)
    h, hn, route_i, route_w = _oproj(mm, x, wts["wo"], wts["norm_ffn"], wts["wr"], wts["br"], tm=min(tm, 256))
    moe_tm = min(tm, MOE_TM)
    slot, tile_expert, tile_valid, pad_start, pad_len = _routing_tables(route_i, x.shape[0], moe_tm)
    xs = _dispatch(hn, slot, pad_start, pad_len, tile_expert.shape[0], moe_tm, tm=min(tm, 512))
    ys = _moe(xs, wts["wg"], wts["wu"], wts["wd"], tile_expert, tile_valid, moe_tm)
    return _combine(h, route_w, wts["norm_final"], ys, slot, tm=min(tm, 512))


def _kv_window_kernel(k0_ref, v0_ref, k1_ref, v1_ref, k2_ref, v2_ref, pt_ref, out0_ref, out1_ref, out2_ref,
                      *, first_step):
    i = pl.program_id(1)

    def emit(out_ref, refs, group, rows_in, rows_out):
        for part, ref in enumerate(refs):
            if group > 0:
                val = jnp.dot(pt_ref[group - 1], ref[...], preferred_element_type=F32)
            else:
                val = ref[...].astype(F32)
            for h in range(HEADS_PER_GROUP):
                out_ref[rows_out, part, h, :] = val[rows_in, h * HEAD_DIM:(h + 1) * HEAD_DIM]

    whole = slice(0, PERM_BLOCK)
    emit(out2_ref, (k2_ref, v2_ref), 2, whole, whole)

    @pl.when(i >= first_step[1])
    def _():
        emit(out1_ref, (k1_ref, v1_ref), 1, whole, whole)

    @pl.when(i >= first_step[0])
    def _():
        keep = out0_ref.shape[0]
        emit(out0_ref, (k0_ref, v0_ref), 0, slice(PERM_BLOCK - keep, PERM_BLOCK), slice(0, keep))


def _kv_windows(z_b, batch, seq, perms_t):
    wins = [min(win, seq) for win, _ in DILATED_GROUPS]
    assert wins[2] % PERM_BLOCK == 0 and wins[1] % PERM_BLOCK == 0 and wins[0] <= PERM_BLOCK and seq % PERM_BLOCK == 0
    steps = wins[2] // PERM_BLOCK
    blocks_per_seq = seq // PERM_BLOCK
    first_step = [steps - max(w // PERM_BLOCK, 1) for w in wins]

    def col_spec(group, part):
        cb = (group * QKV_COLS + (1 + part) * GROUP_COLS) // GROUP_COLS

        def index(b, i):
            return (b * blocks_per_seq + blocks_per_seq - steps + jnp.maximum(i, first_step[group]), cb)

        return pl.BlockSpec((PERM_BLOCK, GROUP_COLS), index)

    def out_spec(group):
        rows = min(wins[group], PERM_BLOCK)
        return pl.BlockSpec((None, None, rows, 2, HEADS_PER_GROUP, HEAD_DIM),
                            lambda b, i: (0, b, jnp.maximum(i - first_step[group], 0), 0, 0, 0))

    in_specs = [col_spec(g, part) for g in range(N_DIL) for part in range(2)]
    in_specs.append(pl.BlockSpec(perms_t.shape, lambda b, i: (0, 0, 0)))
    return pl.pallas_call(
        functools.partial(_kv_window_kernel, first_step=tuple(first_step)),
        out_shape=[jax.ShapeDtypeStruct((1, batch, w, 2, HEADS_PER_GROUP, HEAD_DIM), F32) for w in wins],
        grid=(batch, steps),
        in_specs=in_specs,
        out_specs=[out_spec(g) for g in range(N_DIL)],
        compiler_params=_cparams("arbitrary", "arbitrary"),
        name="kv_windows",
    )(*([z_b] * 6), perms_t)


def _kv_rows(z, lead, group):
    zz = z.reshape(lead + (z.shape[-1],))
    base = group * QKV_COLS
    k = zz[..., base + GROUP_COLS: base + 2 * GROUP_COLS]
    v = zz[..., base + 2 * GROUP_COLS: base + 3 * GROUP_COLS]
    kv = jnp.stack([k, v], axis=2).astype(F32)
    return kv.reshape(kv.shape[:3] + (HEADS_PER_GROUP, HEAD_DIM))


def kernel(x_prompt, x_sample, cache_kv_w128, cache_kv_w512, cache_kv_w2048, norm_mix, w_in, ln_v_g, ln_v_b, w_s, b_s, w_a_out, w_b_out, w_o, norm_ffn, w_route_group, b_route_group, w_route_expert, b_route_expert, w_gate_e, w_up_e, w_down_e, norm_final):
    assert norm_mix.shape[0] == 1, "single-layer trunk"
    batch, seq, _ = x_prompt.shape
    bd, n_new, _ = x_sample.shape
    caches = (cache_kv_w128, cache_kv_w512, cache_kv_w2048)

    pad = LANES - N_EXPERT_GROUPS - N_EXPERTS
    wr = jnp.concatenate([w_route_group[0], w_route_expert[0], jnp.zeros((D_MODEL, pad), F32)], axis=1)
    br = jnp.concatenate([b_route_group[0], b_route_expert[0], jnp.zeros((pad,), F32)])[None, :]
    wts = dict(
        wa=w_a_out[0].astype(BF16), wb=w_b_out[0].astype(BF16), wo=w_o[0].astype(BF16),
        wg=w_gate_e[0], wu=w_up_e[0], wd=w_down_e[0],
        wr=wr.astype(BF16), br=br, norm_ffn=norm_ffn[0][None, :], norm_final=norm_final[None, :],
    )
    w_in_b = w_in[0].astype(BF16)
    norm_g = norm_mix[0][None, :]
    ln_g, ln_b = ln_v_g[0][None, :], ln_v_b[0][None, :]

    xp = x_prompt.reshape(batch * seq, D_MODEL)
    perms = jnp.stack([_class_major_perm(dil) for _, dil in DILATED_GROUPS[1:]])
    zp_a, zp_b = _inproj_all(xp, norm_g, w_in_b, tm=1024, perms=perms)
    (sgp,) = _sgu(zp_a, ln_g, ln_b, w_s[0], b_s[0].T, chunks=8, emit_vn=False)
    os_p, lses_p = zip(*[_attn_group(zp_b, batch, seq, g) for g in range(N_DIL)])
    perms_t = perms.transpose(0, 2, 1)
    y_prompt = _token_tail(xp, zp_a, sgp, os_p, lses_p, wts, tm=512, perms_t=perms_t).reshape(batch, seq, D_MODEL)
    kv_prompt = _kv_windows(zp_b, batch, seq, perms_t)

    ms = bd * n_new
    assert ms == CHUNK and n_new <= CHUNK
    xs = x_sample.reshape(ms, D_MODEL)
    zs_a, zs_b = _inproj_all(xs, norm_g, w_in_b, tm=ms)
    eye = jnp.eye(bd, dtype=F32)
    ws_s = jnp.einsum("ab,gts->gatbs", eye, w_s[0][:, :n_new, :n_new]).reshape(A_GROUPS, ms, ms)
    bst_s = jnp.tile(b_s[0][:, :n_new].T, (bd, 1))
    sgs, vns = _sgu(zs_a, ln_g, ln_b, ws_s, bst_s, chunks=1, emit_vn=True)
    zs5 = zs_b.astype(F32).reshape(bd, n_new, N_DIL, 3, HEADS_PER_GROUP, HEAD_DIM)
    qkv = [zs5[:, :, :, part].reshape(bd, n_new, N_DIL * HEADS_PER_GROUP, HEAD_DIM) for part in range(3)]
    o_s, lse_s = _attn_sample(*qkv, caches)
    os_s = [o_s[g].reshape(ms, GROUP_COLS).astype(BF16) for g in range(N_DIL)]
    lses_s = [lse_s[g, ..., 0].reshape(ms, HEADS_PER_GROUP) for g in range(N_DIL)]
    y_sample = _token_tail(xs, zs_a, sgs, os_s, lses_s, wts, tm=ms).reshape(bd, n_new, D_MODEL)
    kv_sample = [_kv_rows(zs_b, (bd, n_new), g)[None] for g in range(N_DIL)]
    chunk_v = vns.reshape(1, bd, n_new, A_WIDTH)

    return (y_prompt, y_sample, *kv_prompt, *kv_sample, chunk_v)
```

```python
import functools

import jax
import jax.numpy as jnp
from jax import lax
from jax.experimental import pallas as pl
from jax.experimental.pallas import tpu as pltpu

F32 = jnp.float32
BF16 = jnp.bfloat16

D_MODEL = 2048
CHUNK = 128
A_GROUPS = 16
A_GROUP_DIM = 128
A_WIDTH = A_GROUPS * A_GROUP_DIM
HEAD_DIM = 128
HEADS_PER_GROUP = 4
GROUP_COLS = HEADS_PER_GROUP * HEAD_DIM
DILATED_GROUPS = ((128, 1), (512, 4), (2048, 16))
N_DIL = len(DILATED_GROUPS)
B_WIDTH = N_DIL * GROUP_COLS
IN_COLS = 2 * A_WIDTH + 3 * B_WIDTH + 2 * D_MODEL
COL_U, COL_V = 0, A_WIDTH
COL_GA = 2 * A_WIDTH
COL_GB = COL_GA + D_MODEL
QKV_COLS = 3 * GROUP_COLS
N_EXPERT_GROUPS = 4
EXPERTS_PER_GROUP = 8
N_EXPERTS = N_EXPERT_GROUPS * EXPERTS_PER_GROUP
TOP_K = 2
D_EXPERT = 256
RMS_EPS = 1e-6
LN_EPS = 1e-5
NEG = -1e30
ATTN_SCALE = HEAD_DIM ** -0.5

LANES = 128
VMEM_LIMIT_BYTES = 56 * 1024 * 1024
IN_TN = 512
MOE_TM = 256
PERM_BLOCK = 256
ATTN_UNITS_PER_BATCH = 8
ROW_TILE = D_MODEL // LANES

def _cparams(*sem):
    return pltpu.CompilerParams(dimension_semantics=sem, vmem_limit_bytes=VMEM_LIMIT_BYTES)


def _gelu_tanh(x):
    return 0.5 * x * (1.0 + jnp.tanh(0.7978845608028654 * (x + 0.044715 * (x * x * x))))


def _sigmoid(x):
    return 1.0 / (1.0 + jnp.exp(-x))


def _store_token_tiles(ref, val):
    rows = val.shape[0]
    chunks = jnp.stack([val[:, s * LANES:(s + 1) * LANES].astype(BF16) for s in range(ROW_TILE)], axis=0)
    ref[...] = pltpu.einshape("stl->tsl", chunks).reshape(rows * ROW_TILE, LANES)


def _load_token_tile_chunks(ref, lead, rows):
    tiles = ref[lead + (slice(None), slice(None))].reshape(rows, ROW_TILE, LANES)
    by_chunk = pltpu.einshape("tsl->stl", tiles)
    return [by_chunk[s] for s in range(ROW_TILE)]


def _class_major_perm(dil):
    out_row = jnp.arange(PERM_BLOCK, dtype=jnp.int32)
    src = (out_row % (PERM_BLOCK // dil)) * dil + out_row // (PERM_BLOCK // dil)
    return (src[:, None] == jnp.arange(PERM_BLOCK, dtype=jnp.int32)[None, :]).astype(BF16)


def _inproj_kernel(x_ref, g_ref, *rest, strips, plan, permute):
    w_refs, rest = rest[:strips], rest[strips:]
    if permute:
        perm_ref, z_ref, xn_ref = rest
    else:
        z_ref, xn_ref = rest
    j = pl.program_id(1)

    @pl.when(j == 0)
    def _():
        x = x_ref[...]
        ms = jnp.mean(x * x, axis=-1, keepdims=True)
        xn_ref[...] = (x * lax.rsqrt(ms + RMS_EPS) * g_ref[...]).astype(BF16)

    tm = xn_ref.shape[0]
    sub = min(tm, PERM_BLOCK)

    def emit(epilogue):
        for blk in range(tm // sub):
            rows = slice(blk * sub, (blk + 1) * sub)
            for s, w_ref in enumerate(w_refs):
                z = jnp.dot(xn_ref[rows, :], w_ref[...], preferred_element_type=F32)
                z_ref[rows, s * IN_TN:(s + 1) * IN_TN] = epilogue(z).astype(BF16)

    def epilogue_of(kind):
        if kind == "gelu":
            return _gelu_tanh
        if kind == "gate":
            return _sigmoid
        if kind == "plain":
            return lambda z: z
        k = int(kind[len("perm"):])
        return lambda z: jnp.dot(perm_ref[k], z.astype(BF16), preferred_element_type=F32)

    for kind in sorted(set(plan)):
        tiles = [t for t, p in enumerate(plan) if p == kind]
        hit = functools.reduce(jnp.logical_or, [j == t for t in tiles])

        @pl.when(hit)
        def _():
            emit(epilogue_of(kind))


def _inproj(x, norm_g, w_in_bf16, tm, strips, plan, ref_tile, perms=None):
    m = x.shape[0]
    tn = strips * IN_TN
    in_specs = [pl.BlockSpec((tm, D_MODEL), lambda i, j: (i, 0)), pl.BlockSpec((1, D_MODEL), lambda i, j: (0, 0))]
    in_specs += [pl.BlockSpec((D_MODEL, IN_TN), functools.partial(lambda i, j, s: (0, ref_tile(j, s)), s=s))
                 for s in range(strips)]
    args = [x, norm_g] + [w_in_bf16] * strips
    if perms is not None:
        in_specs.append(pl.BlockSpec(perms.shape, lambda i, j: (0, 0, 0)))
        args.append(perms)
    return pl.pallas_call(
        functools.partial(_inproj_kernel, strips=strips, plan=plan, permute=perms is not None),
        out_shape=jax.ShapeDtypeStruct((m, len(plan) * tn), BF16),
        grid=(m // tm, len(plan)),
        in_specs=in_specs,
        out_specs=pl.BlockSpec((tm, tn), lambda i, j: (i, j)),
        scratch_shapes=[pltpu.VMEM((tm, D_MODEL), BF16)],
        compiler_params=_cparams("parallel", "arbitrary"),
        name="inproj",
    )(*args)


def _inproj_all(x, norm_g, w_in_bf16, tm, perms=None):
    n_gelu = 2 * A_WIDTH // IN_TN
    n_qkv = 3 * B_WIDTH // IN_TN
    strips_a = 2
    plan_a = ("gelu",) * (n_gelu // strips_a) + ("gate",) * (2 * D_MODEL // IN_TN // strips_a)

    def ref_a(j, s):
        t = j * strips_a + s
        return jnp.where(t < n_gelu, t, t + n_qkv)

    z_a = _inproj(x, norm_g, w_in_bf16, tm, strips_a, plan_a, ref_a)
    plan_b = ("plain",) + tuple(f"perm{k}" for k in range(N_DIL - 1)) if perms is not None else ("plain",) * N_DIL
    z_b = _inproj(x, norm_g, w_in_bf16, tm, 3, plan_b, lambda j, s: n_gelu + s * N_DIL + j, perms)
    return z_a, z_b


def _sgu_kernel(u_ref, v_ref, lng_ref, lnb_ref, ws_ref, bst_ref, sg_ref, *vn_out, chunks):
    row = lax.broadcasted_iota(jnp.int32, (CHUNK, CHUNK), 0)
    col = lax.broadcasted_iota(jnp.int32, (CHUNK, CHUNK), 1)
    tri = row >= col
    ws = [jnp.where(tri, ws_ref[g], 0.0).astype(BF16) for g in range(A_GROUPS)]
    for c in range(chunks):
        rows = slice(c * CHUNK, (c + 1) * CHUNK)
        v = v_ref[rows, :].astype(F32)
        mu = jnp.mean(v, axis=-1, keepdims=True)
        vc = v - mu
        var = jnp.mean(vc * vc, axis=-1, keepdims=True)
        vn = vc * lax.rsqrt(var + LN_EPS) * lng_ref[...] + lnb_ref[...]
        if vn_out:
            vn_out[0][rows, :] = vn
        vnb = vn.astype(BF16)
        for g in range(A_GROUPS):
            cols = slice(g * A_GROUP_DIM, (g + 1) * A_GROUP_DIM)
            s = jnp.dot(ws[g], vnb[:, cols], preferred_element_type=F32) + bst_ref[:, g:g + 1]
            sg_ref[rows, cols] = (u_ref[rows, cols].astype(F32) * s).astype(BF16)


def _sgu(z, ln_g, ln_b, ws, bst, chunks, emit_vn):
    m = z.shape[0]
    tm = chunks * CHUNK
    out_shape = [jax.ShapeDtypeStruct((m, A_WIDTH), BF16)]
    out_specs = [pl.BlockSpec((tm, A_WIDTH), lambda i: (i, 0))]
    if emit_vn:
        out_shape.append(jax.ShapeDtypeStruct((m, A_WIDTH), F32))
        out_specs.append(pl.BlockSpec((tm, A_WIDTH), lambda i: (i, 0)))
    return pl.pallas_call(
        functools.partial(_sgu_kernel, chunks=chunks),
        out_shape=out_shape,
        grid=(m // tm,),
        in_specs=[
            pl.BlockSpec((tm, A_WIDTH), lambda i: (i, COL_U // A_WIDTH)),
            pl.BlockSpec((tm, A_WIDTH), lambda i: (i, COL_V // A_WIDTH)),
            pl.BlockSpec((1, A_WIDTH), lambda i: (0, 0)),
            pl.BlockSpec((1, A_WIDTH), lambda i: (0, 0)),
            pl.BlockSpec((A_GROUPS, CHUNK, CHUNK), lambda i: (0, 0, 0)),
            pl.BlockSpec((CHUNK, A_GROUPS), lambda i: (0, 0)),
        ],
        out_specs=out_specs,
        compiler_params=_cparams("parallel"),
        name="sgu",
    )(z, z, ln_g, ln_b, ws, bst)


def _attn_kernel(q_ref, kc_ref, kp_ref, vc_ref, vp_ref, o_ref, lse_ref, *, qblocks):
    i = pl.program_id(2)
    qi = lax.broadcasted_iota(jnp.int32, (CHUNK, CHUNK), 0)
    ki = lax.broadcasted_iota(jnp.int32, (CHUNK, CHUNK), 1)
    cur_mask = ki <= qi
    no_prev = jnp.where(i == 0, CHUNK, 0)
    dn = (((1,), (1,)), ((), ()))
    rpc = q_ref.shape[1]
    nbq = CHUNK // rpc

    def load(ref, a, cols):
        return ref[a * nbq:(a + 1) * nbq, :, cols].reshape(CHUNK, HEAD_DIM)

    units = [(a, h) for a in range(qblocks) for h in range(HEADS_PER_GROUP)]
    for start in range(0, len(units), ATTN_UNITS_PER_BATCH):
        batch_units = units[start:start + ATTN_UNITS_PER_BATCH]
        scores = []
        for a, h in batch_units:
            cols = slice(h * HEAD_DIM, (h + 1) * HEAD_DIM)
            q = load(q_ref, a, cols)
            if a == 0:
                kp, prev_mask = load(kp_ref, 0, cols), ki >= qi + no_prev
            else:
                kp, prev_mask = load(kc_ref, a - 1, cols), ki >= qi
            s_c = lax.dot_general(q, load(kc_ref, a, cols), dn, preferred_element_type=F32) * ATTN_SCALE
            s_p = lax.dot_general(q, kp, dn, preferred_element_type=F32) * ATTN_SCALE
            scores.append((jnp.where(cur_mask, s_c, NEG), jnp.where(prev_mask, s_p, NEG)))
        probs = []
        for s_c, s_p in scores:
            mx = jnp.maximum(jnp.max(s_c, axis=-1, keepdims=True), jnp.max(s_p, axis=-1, keepdims=True))
            p_c = jnp.exp(s_c - mx)
            p_p = jnp.exp(s_p - mx)
            l = jnp.sum(p_c, axis=-1, keepdims=True) + jnp.sum(p_p, axis=-1, keepdims=True)
            probs.append((p_c.astype(BF16), p_p.astype(BF16), mx, l))
        for (a, h), (p_c, p_p, mx, l) in zip(batch_units, probs):
            cols = slice(h * HEAD_DIM, (h + 1) * HEAD_DIM)
            vp = load(vp_ref, 0, cols) if a == 0 else load(vc_ref, a - 1, cols)
            acc = jnp.dot(p_c, load(vc_ref, a, cols), preferred_element_type=F32)
            acc = acc + jnp.dot(p_p, vp, preferred_element_type=F32)
            o_ref[a * nbq:(a + 1) * nbq, :, cols] = (acc / l).astype(BF16).reshape(nbq, rpc, HEAD_DIM)
            lse_ref[a * CHUNK:(a + 1) * CHUNK, h:h + 1] = mx + jnp.log(l)


def _rows_per_class(dil):
    return min(PERM_BLOCK // dil, CHUNK)


def _attn_group(z, batch, seq, group):
    _, dil = DILATED_GROUPS[group]
    sub = seq // dil
    rpc = _rows_per_class(dil)
    qblocks = max(b for b in (4, 2, 1) if sub % (b * CHUNK) == 0)
    tq = qblocks * CHUNK
    cq, ck, cv = (group * QKV_COLS // GROUP_COLS + part for part in range(3))
    zv = z.reshape(batch, sub // rpc, dil, rpc, z.shape[-1])

    def cur(cb):
        return pl.BlockSpec((None, tq // rpc, None, rpc, GROUP_COLS), lambda b, r, i: (b, i, r, 0, cb))

    def prev(cb):
        return pl.BlockSpec((None, CHUNK // rpc, None, rpc, GROUP_COLS),
                            lambda b, r, i: (b, jnp.maximum(i * qblocks - 1, 0), r, 0, cb))

    o, lse = pl.pallas_call(
        functools.partial(_attn_kernel, qblocks=qblocks),
        out_shape=[jax.ShapeDtypeStruct((batch, sub // rpc, dil, rpc, GROUP_COLS), BF16),
                   jax.ShapeDtypeStruct((batch, dil, sub, HEADS_PER_GROUP), F32)],
        grid=(batch, dil, sub // tq),
        in_specs=[cur(cq), cur(ck), prev(ck), cur(cv), prev(cv)],
        out_specs=[pl.BlockSpec((None, tq // rpc, None, rpc, GROUP_COLS), lambda b, r, i: (b, i, r, 0, 0)),
                   pl.BlockSpec((None, None, tq, HEADS_PER_GROUP), lambda b, r, i: (b, r, i, 0))],
        compiler_params=_cparams("parallel", "parallel", "arbitrary"),
        name=f"attn_prompt_g{group}",
    )(zv, zv, zv, zv, zv)
    lse = lse.transpose(0, 2, 1, 3).reshape(batch * seq, HEADS_PER_GROUP)
    return o, lse


def _attn_sample_kernel(q_ref, k_ref, v_ref, c0_ref, c1_ref, c2_ref, o_ref, lse_ref, *, n_new):
    caches = (c0_ref, c1_ref, c2_ref)
    rowid = lax.broadcasted_iota(jnp.int32, (CHUNK, HEADS_PER_GROUP, 1), 0)
    for g, (win, dil) in enumerate(DILATED_GROUPS):
        cref = caches[g]
        hs = slice(g * HEADS_PER_GROUP, (g + 1) * HEADS_PER_GROUP)
        for t in range(n_new):
            res, first = t % dil, t // dil
            kc = cref[:, res, 0]
            vc = cref[:, res, 1]
            q = q_ref[t, hs, :]
            s = jnp.sum(kc * q[None], axis=-1, keepdims=True) * ATTN_SCALE
            if first > 0:
                s = jnp.where(rowid >= first, s, NEG)
            new_rows = [tn for tn in range(t + 1) if (t - tn) % dil == 0 and (t - tn) // dil <= win // dil]
            s_new = [jnp.sum(q * k_ref[tn, hs, :], axis=-1, keepdims=True) * ATTN_SCALE for tn in new_rows]
            mx = jnp.max(s, axis=0)
            for sn in s_new:
                mx = jnp.maximum(mx, sn)
            p = jnp.exp(s - mx[None])
            l = jnp.sum(p, axis=0)
            acc = jnp.sum(p * vc, axis=0)
            for tn, sn in zip(new_rows, s_new):
                pn = jnp.exp(sn - mx)
                l = l + pn
                acc = acc + pn * v_ref[tn, hs, :]
            o_ref[g, t] = acc / l
            lse_ref[g, t] = jnp.broadcast_to(mx + jnp.log(l), (HEADS_PER_GROUP, HEAD_DIM))


def _attn_sample(q, k, v, caches):
    bd, n_new = q.shape[:2]
    views, specs = [], []
    for (win, dil), c in zip(DILATED_GROUPS, caches):
        assert c.shape[2] == win and win == CHUNK * dil, "cache must hold exactly one full window"
        used = min(dil, n_new)
        views.append(c.reshape(1, bd, CHUNK, dil, 2, HEADS_PER_GROUP, HEAD_DIM))
        specs.append(pl.BlockSpec((None, None, CHUNK, used, 2, HEADS_PER_GROUP, HEAD_DIM),
                                  lambda b: (0, b, 0, 0, 0, 0, 0)))
    new_spec = pl.BlockSpec((None, n_new, N_DIL * HEADS_PER_GROUP, HEAD_DIM), lambda b: (b, 0, 0, 0))
    out_sds = jax.ShapeDtypeStruct((N_DIL, bd, n_new, HEADS_PER_GROUP, HEAD_DIM), F32)
    out_spec = pl.BlockSpec((N_DIL, None, n_new, HEADS_PER_GROUP, HEAD_DIM), lambda b: (0, b, 0, 0, 0))
    return pl.pallas_call(
        functools.partial(_attn_sample_kernel, n_new=n_new),
        out_shape=[out_sds, out_sds],
        grid=(bd,),
        in_specs=[new_spec, new_spec, new_spec] + specs,
        out_specs=[out_spec, out_spec],
        compiler_params=_cparams("parallel"),
        name="attn_sample",
    )(q, k, v, *views)


def _merge_kernel(sg_ref, o0_ref, o1_ref, o2_ref, l0_ref, l1_ref, l2_ref, ga_ref, gb_ref,
                  wa_ref, wb_ref, *rest, permuted):
    if permuted:
        pt_ref, m_ref, ob_ref = rest
    else:
        m_ref, ob_ref = rest
    j = pl.program_id(1)

    @pl.when(j == 0)
    def _():
        l0, l1, l2 = l0_ref[...], l1_ref[...], l2_ref[...]
        mx = jnp.maximum(jnp.maximum(l0, l1), l2)
        e0, e1, e2 = jnp.exp(l0 - mx), jnp.exp(l1 - mx), jnp.exp(l2 - mx)
        inv = 1.0 / (e0 + e1 + e2)
        ws = (e0 * inv, e1 * inv, e2 * inv)
        o_refs = (o0_ref, o1_ref, o2_ref)
        tm = ob_ref.shape[0]

        def o_rows(ref, rows):
            if len(ref.shape) == 2:
                return ref[rows, :]
            per = ref.shape[1] * ref.shape[2]
            return ref[rows.start // per:rows.stop // per].reshape(rows.stop - rows.start, GROUP_COLS)

        for blk in range(max(tm // PERM_BLOCK, 1)):
            rows = slice(blk * PERM_BLOCK, min((blk + 1) * PERM_BLOCK, tm))
            og = []
            for g in range(N_DIL):
                if permuted and g > 0:
                    og.append(jnp.dot(pt_ref[g - 1], o_rows(o_refs[g], rows), preferred_element_type=F32))
                else:
                    og.append(o_rows(o_refs[g], rows).astype(F32))
            for h in range(HEADS_PER_GROUP):
                cols = slice(h * HEAD_DIM, (h + 1) * HEAD_DIM)
                ob = sum(ws[g][rows, h:h + 1] * og[g][:, cols] for g in range(N_DIL))
                ob_ref[rows, cols] = ob.astype(BF16)

    ya = jnp.dot(sg_ref[...], wa_ref[...], preferred_element_type=F32)
    yb = jnp.dot(ob_ref[...], wb_ref[...], preferred_element_type=F32)
    m_ref[...] = (ga_ref[...].astype(F32) * ya + gb_ref[...].astype(F32) * yb).astype(BF16)


def _merge(z, sg, os_, lses, wa, wb, tm, tn, perms_t=None):
    m = z.shape[0]
    assert COL_GA % tn == 0 and COL_GB % tn == 0
    row = lambda width: pl.BlockSpec((tm, width), lambda i, j: (i, 0))

    def o_spec(o):
        if o.ndim == 2:
            return row(GROUP_COLS)
        _, blocks, dil, rpc, _ = o.shape
        per_tile, tiles = tm // (dil * rpc), blocks * dil * rpc // tm
        return pl.BlockSpec((None, per_tile, dil, rpc, GROUP_COLS), lambda i, j: (i // tiles, i % tiles, 0, 0, 0))

    in_specs = [row(A_WIDTH)] + [o_spec(o) for o in os_] + [row(HEADS_PER_GROUP)] * 3 + [
        pl.BlockSpec((tm, tn), lambda i, j: (i, COL_GA // tn + j)),
        pl.BlockSpec((tm, tn), lambda i, j: (i, COL_GB // tn + j)),
        pl.BlockSpec((A_WIDTH, tn), lambda i, j: (0, j)),
        pl.BlockSpec((GROUP_COLS, tn), lambda i, j: (0, j)),
    ]
    args = [sg, *os_, *lses, z, z, wa, wb]
    if perms_t is not None:
        assert tm % PERM_BLOCK == 0
        in_specs.append(pl.BlockSpec(perms_t.shape, lambda i, j: (0, 0, 0)))
        args.append(perms_t)
    return pl.pallas_call(
        functools.partial(_merge_kernel, permuted=perms_t is not None),
        out_shape=jax.ShapeDtypeStruct((m, D_MODEL), BF16),
        grid=(m // tm, D_MODEL // tn),
        in_specs=in_specs,
        out_specs=pl.BlockSpec((tm, tn), lambda i, j: (i, j)),
        scratch_shapes=[pltpu.VMEM((tm, GROUP_COLS), BF16)],
        compiler_params=_cparams("parallel", "arbitrary"),
        name="branch_merge",
    )(*args)


def _oproj_kernel(m_ref, x_ref, wo_ref, g_ref, wr_ref, br_ref, h_ref, hn_ref, ri_ref, rw_ref):
    h = x_ref[...] + jnp.dot(m_ref[...], wo_ref[...], preferred_element_type=F32)
    h_ref[...] = h
    ms = jnp.mean(h * h, axis=-1, keepdims=True)
    hn = h * lax.rsqrt(ms + RMS_EPS) * g_ref[...]
    _store_token_tiles(hn_ref, hn)
    logits = jnp.dot(hn.astype(BF16), wr_ref[...], preferred_element_type=F32) + br_ref[...]
    lane = lax.broadcasted_iota(jnp.int32, logits.shape, 1).astype(F32)
    big = float(LANES)

    def first_argmax(vals, vmax):
        return jnp.min(jnp.where(vals == vmax, lane, big), axis=-1, keepdims=True)

    lg = jnp.where(lane < N_EXPERT_GROUPS, logits, NEG)
    gmax = jnp.max(lg, axis=-1, keepdims=True)
    gsel = first_argmax(lg, gmax)
    p_sel = 1.0 / jnp.sum(jnp.exp(lg - gmax), axis=-1, keepdims=True)
    lo = N_EXPERT_GROUPS + EXPERTS_PER_GROUP * gsel
    le = jnp.where(jnp.logical_and(lane >= lo, lane < lo + EXPERTS_PER_GROUP), logits, NEG)
    v1 = jnp.max(le, axis=-1, keepdims=True)
    i1 = first_argmax(le, v1)
    le2 = jnp.where(lane == i1, NEG, le)
    v2 = jnp.max(le2, axis=-1, keepdims=True)
    i2 = first_argmax(le2, v2)
    e2 = jnp.exp(v2 - v1)
    w1 = p_sel / (1.0 + e2)
    w2 = p_sel * e2 / (1.0 + e2)
    ri = jnp.where(lane == 0, i1 - N_EXPERT_GROUPS, jnp.where(lane == 1, i2 - N_EXPERT_GROUPS, 0.0))
    ri_ref[...] = ri.astype(jnp.int32)
    rw_ref[...] = jnp.where(lane == 0, w1, jnp.where(lane == 1, w2, 0.0))


def _oproj(mm, x, wo, norm_g, wr, br, tm):
    m = x.shape[0]
    row = lambda width: pl.BlockSpec((tm, width), lambda i: (i, 0))
    full = lambda a, b: pl.BlockSpec((a, b), lambda i: (0, 0))
    return pl.pallas_call(
        _oproj_kernel,
        out_shape=[jax.ShapeDtypeStruct((m, D_MODEL), F32), jax.ShapeDtypeStruct((m * ROW_TILE, LANES), BF16),
                   jax.ShapeDtypeStruct((m, LANES), jnp.int32), jax.ShapeDtypeStruct((m, LANES), F32)],
        grid=(m // tm,),
        in_specs=[row(D_MODEL), row(D_MODEL), full(D_MODEL, D_MODEL), full(1, D_MODEL),
                  full(D_MODEL, LANES), full(1, LANES)],
        out_specs=[row(D_MODEL), pl.BlockSpec((tm * ROW_TILE, LANES), lambda i: (i, 0)), row(LANES), row(LANES)],
        compiler_params=_cparams("parallel"),
        name="oproj_router",
    )(mm, x, wo, norm_g, wr, br)


def _dispatch_kernel(slot_ref, pad_start_ref, pad_len_ref, hn_ref, hs_ref, xs_hbm, zbuf, sem, zsem,
                     *, tm, moe_tm, n_tiles):
    i = pl.program_id(0)
    n_main = pl.num_programs(0) - 1

    def zero_copy(dst_row, rows):
        return pltpu.make_async_copy(zbuf.at[pl.ds(0, rows * ROW_TILE)],
                                     xs_hbm.at[pl.ds(dst_row * ROW_TILE, rows * ROW_TILE)], zsem)

    def zero_fill(act):
        def per_expert(e, carry):
            off, length = pad_start_ref[e], pad_len_ref[e]
            rows = moe_tm // 2
            while rows >= 1:
                @pl.when((length & rows) != 0)
                def _(off=off, rows=rows):
                    act(zero_copy(off, rows))
                off = off + (length & rows)
                rows //= 2
            return carry

        lax.fori_loop(0, N_EXPERTS, per_expert, 0)

        def per_tile(t, carry):
            act(zero_copy(t * moe_tm, moe_tm))
            return carry

        lax.fori_loop(pad_start_ref[N_EXPERTS], n_tiles, per_tile, 0)

    @pl.when(i == 0)
    def _():
        zbuf[...] = jnp.zeros_like(zbuf)
        zero_fill(lambda c: c.start())

    def scatter_rows(ref, first_token):
        rows = ref.shape[0] // ROW_TILE

        def issue(r, carry):
            for k in range(TOP_K):
                slot = slot_ref[(first_token + r) * TOP_K + k]
                pltpu.make_async_copy(ref.at[pl.ds(r * ROW_TILE, ROW_TILE)],
                                      xs_hbm.at[pl.ds(slot * ROW_TILE, ROW_TILE)], sem).start(priority=k % 2)
            return carry

        lax.fori_loop(0, rows, issue, 0, unroll=8)
        for k in range(TOP_K):
            pltpu.make_async_copy(ref, ref, sem).wait()

    @pl.when(i < n_main)
    def _():
        scatter_rows(hn_ref, i * tm)

    @pl.when(i == n_main)
    def _():
        scatter_rows(hs_ref, n_main * tm)
        zero_fill(lambda c: c.wait())


def _dispatch(hn_tiles, hs_tiles, slot, pad_start, pad_len, n_tiles, moe_tm, tm):
    m = hn_tiles.shape[0] // ROW_TILE
    assert m % tm == 0 and moe_tm & (moe_tm - 1) == 0
    assert slot.shape[0] == (m + hs_tiles.shape[0] // ROW_TILE) * TOP_K
    n_main = m // tm
    grid_spec = pltpu.PrefetchScalarGridSpec(
        num_scalar_prefetch=3,
        grid=(n_main + 1,),
        in_specs=[pl.BlockSpec((tm * ROW_TILE, LANES), lambda i, s, ps, pn: (jnp.minimum(i, n_main - 1), 0)),
                  pl.BlockSpec(hs_tiles.shape, lambda i, s, ps, pn: (0, 0))],
        out_specs=pl.BlockSpec(memory_space=pl.ANY),
        scratch_shapes=[pltpu.VMEM((moe_tm * ROW_TILE, LANES), BF16),
                        pltpu.SemaphoreType.DMA(()), pltpu.SemaphoreType.DMA(())],
    )
    return pl.pallas_call(
        functools.partial(_dispatch_kernel, tm=tm, moe_tm=moe_tm, n_tiles=n_tiles),
        out_shape=jax.ShapeDtypeStruct((n_tiles * moe_tm * ROW_TILE, LANES), BF16),
        grid_spec=grid_spec,
        compiler_params=_cparams("arbitrary"),
        name="moe_dispatch",
    )(slot, pad_start, pad_len, hn_tiles, hs_tiles)


def _moe_kernel(texp_ref, valid_ref, xs_ref, wg_ref, wu_ref, wd_ref, ys_ref, wg_bf, wu_bf, wd_bf):
    i = pl.program_id(0)
    tm = xs_ref.shape[0] // ROW_TILE
    valid = valid_ref[i]
    new_expert = jnp.logical_or(i == 0, texp_ref[i] != texp_ref[jnp.maximum(i - 1, 0)])

    @pl.when(jnp.logical_and(valid > 0, new_expert))
    def _():
        wg_bf[...] = wg_ref[...].astype(BF16)
        wu_bf[...] = wu_ref[...].astype(BF16)
        wd_bf[...] = wd_ref[...].astype(BF16)

    @pl.when(valid > 0)
    def _():
        x = jnp.concatenate(_load_token_tile_chunks(xs_ref, (), tm), axis=1)
        gate = jnp.dot(x, wg_bf[...], preferred_element_type=F32)
        up = jnp.dot(x, wu_bf[...], preferred_element_type=F32)
        hid = (gate * _sigmoid(gate) * up).astype(BF16)
        _store_token_tiles(ys_ref, jnp.dot(hid, wd_bf[...], preferred_element_type=F32))

    @pl.when(valid == 0)
    def _():
        ys_ref[...] = jnp.zeros_like(ys_ref)


def _moe(xs, wg, wu, wd, tile_expert, tile_valid, tm):
    n_tiles = tile_expert.shape[0]
    grid_spec = pltpu.PrefetchScalarGridSpec(
        num_scalar_prefetch=2,
        grid=(n_tiles,),
        in_specs=[
            pl.BlockSpec((tm * ROW_TILE, LANES), lambda i, te, tv: (i, 0)),
            pl.BlockSpec((None, D_MODEL, D_EXPERT), lambda i, te, tv: (te[i], 0, 0)),
            pl.BlockSpec((None, D_MODEL, D_EXPERT), lambda i, te, tv: (te[i], 0, 0)),
            pl.BlockSpec((None, D_EXPERT, D_MODEL), lambda i, te, tv: (te[i], 0, 0)),
        ],
        out_specs=pl.BlockSpec((tm * ROW_TILE, LANES), lambda i, te, tv: (i, 0)),
        scratch_shapes=[pltpu.VMEM((D_MODEL, D_EXPERT), BF16), pltpu.VMEM((D_MODEL, D_EXPERT), BF16),
                        pltpu.VMEM((D_EXPERT, D_MODEL), BF16)],
    )
    return pl.pallas_call(
        _moe_kernel,
        out_shape=jax.ShapeDtypeStruct((n_tiles * tm * ROW_TILE, LANES), BF16),
        grid_spec=grid_spec,
        compiler_params=_cparams("arbitrary"),
        name="moe_experts",
    )(tile_expert, tile_valid, xs, wg, wu, wd)


def _combine_kernel(slot_ref, h_ref, rw_ref, g_ref, ys_hbm, y_ref, ybuf, sem, *, tm):
    i = pl.program_id(0)
    n_steps = pl.num_programs(0)

    def issue(tile, buf):
        def body(r, carry):
            for k in range(TOP_K):
                slot = slot_ref[(tile * tm + r) * TOP_K + k]
                pltpu.make_async_copy(ys_hbm.at[pl.ds(slot * ROW_TILE, ROW_TILE)],
                                      ybuf.at[buf, k, pl.ds(r * ROW_TILE, ROW_TILE)], sem.at[buf]).start(priority=k % 2)
            return carry

        lax.fori_loop(0, tm, body, 0, unroll=8)

    @pl.when(i == 0)
    def _():
        issue(0, 0)

    @pl.when(i + 1 < n_steps)
    def _():
        issue(i + 1, (i + 1) % 2)

    cur = i % 2
    for k in range(TOP_K):
        pltpu.make_async_copy(ybuf.at[cur, k], ybuf.at[cur, k], sem.at[cur]).wait()
    w0, w1 = rw_ref[:, 0:1], rw_ref[:, 1:2]
    e0 = _load_token_tile_chunks(ybuf, (cur, 0), tm)
    e1 = _load_token_tile_chunks(ybuf, (cur, 1), tm)
    chunks, ssq = [], 0.0
    for s in range(ROW_TILE):
        y = h_ref[:, s * LANES:(s + 1) * LANES] + w0 * e0[s].astype(F32) + w1 * e1[s].astype(F32)
        chunks.append(y)
        ssq = ssq + jnp.sum(y * y, axis=-1, keepdims=True)
    scale = lax.rsqrt(ssq * (1.0 / D_MODEL) + RMS_EPS)
    for s, y in enumerate(chunks):
        cols = slice(s * LANES, (s + 1) * LANES)
        y_ref[:, cols] = y * scale * g_ref[:, cols]


def _combine(h, route_w, norm_g, ys, slot, tm):
    m = h.shape[0]
    grid_spec = pltpu.PrefetchScalarGridSpec(
        num_scalar_prefetch=1,
        grid=(m // tm,),
        in_specs=[
            pl.BlockSpec((tm, D_MODEL), lambda i, s: (i, 0)),
            pl.BlockSpec((tm, LANES), lambda i, s: (i, 0)),
            pl.BlockSpec((1, D_MODEL), lambda i, s: (0, 0)),
            pl.BlockSpec(memory_space=pl.ANY),
        ],
        out_specs=pl.BlockSpec((tm, D_MODEL), lambda i, s: (i, 0)),
        scratch_shapes=[pltpu.VMEM((2, TOP_K, tm * ROW_TILE, LANES), BF16), pltpu.SemaphoreType.DMA((2,))],
    )
    return pl.pallas_call(
        functools.partial(_combine_kernel, tm=tm),
        out_shape=jax.ShapeDtypeStruct((m, D_MODEL), F32),
        grid_spec=grid_spec,
        compiler_params=_cparams("arbitrary"),
        name="moe_combine",
    )(slot, h, route_w, norm_g, ys)


def _routing_tables(route_i, m, tm):
    n = m * TOP_K
    n_tiles = pl.cdiv(n, tm) + N_EXPERTS - 1
    e_flat = route_i[:, :TOP_K].reshape(n)
    onehot = (e_flat[:, None] == jnp.arange(N_EXPERTS, dtype=jnp.int32)[None, :]).astype(jnp.int32)
    csum = jnp.cumsum(onehot, axis=0)
    rank = jnp.take_along_axis(csum, e_flat[:, None], axis=1)[:, 0] - 1
    counts = csum[-1]
    tiles_e = (counts + tm - 1) // tm
    tile_end = jnp.cumsum(tiles_e)
    tile_start = tile_end - tiles_e
    slot = (tile_start[e_flat] * tm + rank).astype(jnp.int32)
    tile_ids = jnp.arange(n_tiles, dtype=jnp.int32)
    tile_expert = jnp.sum((tile_end[None, :] <= tile_ids[:, None]).astype(jnp.int32), axis=1)
    tile_expert = jnp.minimum(tile_expert, N_EXPERTS - 1)
    tile_valid = jnp.clip(counts[tile_expert] - (tile_ids - tile_start[tile_expert]) * tm, 0, tm)
    tile_valid = jnp.where(tile_ids < tile_end[-1], tile_valid, 0).astype(jnp.int32)
    pad_start = jnp.concatenate([tile_start * tm + counts, tile_end[-1:]]).astype(jnp.int32)
    pad_len = (tiles_e * tm - counts).astype(jnp.int32)
    return slot, tile_expert.astype(jnp.int32), tile_valid, pad_start, pad_len


def _mixer_out(x, z, sg, os_, lses, wts, tm, perms_t=None):
    mm = _merge(z, sg, os_, lses, wts["wa"], wts["wb"], tm=tm, tn=D_MODEL, perms_t=perms_t)
    return _oproj(mm, x, wts["wo"], wts["norm_ffn"], wts["wr"], wts["br"], tm=tm)


def _moe_ffn(prompt, sample, wts):
    (h_p, hn_p, ri_p, rw_p), (h_s, hn_s, ri_s, rw_s) = prompt, sample
    m_p, m_s = h_p.shape[0], h_s.shape[0]
    route_i = jnp.concatenate([ri_p, ri_s], axis=0)
    slot, tile_expert, tile_valid, pad_start, pad_len = _routing_tables(route_i, m_p + m_s, MOE_TM)
    xs = _dispatch(hn_p, hn_s, slot, pad_start, pad_len, tile_expert.shape[0], MOE_TM, tm=512)
    ys = _moe(xs, wts["wg"], wts["wu"], wts["wd"], tile_expert, tile_valid, MOE_TM)
    y_p = _combine(h_p, rw_p, wts["norm_final"], ys, slot[:m_p * TOP_K], tm=512)
    y_s = _combine(h_s, rw_s, wts["norm_final"], ys, slot[m_p * TOP_K:], tm=m_s)
    return y_p, y_s


def _kv_window_kernel(k0_ref, v0_ref, k1_ref, v1_ref, k2_ref, v2_ref, pt_ref, out0_ref, out1_ref, out2_ref,
                      *, first_step):
    i = pl.program_id(1)

    def emit(out_ref, refs, group, rows_in, rows_out):
        for part, ref in enumerate(refs):
            if group > 0:
                val = jnp.dot(pt_ref[group - 1], ref[...], preferred_element_type=F32)
            else:
                val = ref[...].astype(F32)
            for h in range(HEADS_PER_GROUP):
                out_ref[rows_out, part, h, :] = val[rows_in, h * HEAD_DIM:(h + 1) * HEAD_DIM]

    whole = slice(0, PERM_BLOCK)
    emit(out2_ref, (k2_ref, v2_ref), 2, whole, whole)

    @pl.when(i >= first_step[1])
    def _():
        emit(out1_ref, (k1_ref, v1_ref), 1, whole, whole)

    @pl.when(i >= first_step[0])
    def _():
        keep = out0_ref.shape[0]
        emit(out0_ref, (k0_ref, v0_ref), 0, slice(PERM_BLOCK - keep, PERM_BLOCK), slice(0, keep))


def _kv_windows(z_b, batch, seq, perms_t):
    wins = [min(win, seq) for win, _ in DILATED_GROUPS]
    assert wins[2] % PERM_BLOCK == 0 and wins[1] % PERM_BLOCK == 0 and wins[0] <= PERM_BLOCK and seq % PERM_BLOCK == 0
    steps = wins[2] // PERM_BLOCK
    blocks_per_seq = seq // PERM_BLOCK
    first_step = [steps - max(w // PERM_BLOCK, 1) for w in wins]

    def col_spec(group, part):
        cb = (group * QKV_COLS + (1 + part) * GROUP_COLS) // GROUP_COLS

        def index(b, i):
            return (b * blocks_per_seq + blocks_per_seq - steps + jnp.maximum(i, first_step[group]), cb)

        return pl.BlockSpec((PERM_BLOCK, GROUP_COLS), index)

    def out_spec(group):
        rows = min(wins[group], PERM_BLOCK)
        return pl.BlockSpec((None, None, rows, 2, HEADS_PER_GROUP, HEAD_DIM),
                            lambda b, i: (0, b, jnp.maximum(i - first_step[group], 0), 0, 0, 0))

    in_specs = [col_spec(g, part) for g in range(N_DIL) for part in range(2)]
    in_specs.append(pl.BlockSpec(perms_t.shape, lambda b, i: (0, 0, 0)))
    return pl.pallas_call(
        functools.partial(_kv_window_kernel, first_step=tuple(first_step)),
        out_shape=[jax.ShapeDtypeStruct((1, batch, w, 2, HEADS_PER_GROUP, HEAD_DIM), F32) for w in wins],
        grid=(batch, steps),
        in_specs=in_specs,
        out_specs=[out_spec(g) for g in range(N_DIL)],
        compiler_params=_cparams("arbitrary", "arbitrary"),
        name="kv_windows",
    )(*([z_b] * 6), perms_t)


def _kv_rows(z, lead, group):
    zz = z.reshape(lead + (z.shape[-1],))
    base = group * QKV_COLS
    k = zz[..., base + GROUP_COLS: base + 2 * GROUP_COLS]
    v = zz[..., base + 2 * GROUP_COLS: base + 3 * GROUP_COLS]
    kv = jnp.stack([k, v], axis=2).astype(F32)
    return kv.reshape(kv.shape[:3] + (HEADS_PER_GROUP, HEAD_DIM))


def kernel(x_prompt, x_sample, cache_kv_w128, cache_kv_w512, cache_kv_w2048, norm_mix, w_in, ln_v_g, ln_v_b, w_s, b_s, w_a_out, w_b_out, w_o, norm_ffn, w_route_group, b_route_group, w_route_expert, b_route_expert, w_gate_e, w_up_e, w_down_e, norm_final):
    assert norm_mix.shape[0] == 1, "single-layer trunk"
    batch, seq, _ = x_prompt.shape
    bd, n_new, _ = x_sample.shape
    caches = (cache_kv_w128, cache_kv_w512, cache_kv_w2048)

    pad = LANES - N_EXPERT_GROUPS - N_EXPERTS
    wr = jnp.concatenate([w_route_group[0], w_route_expert[0], jnp.zeros((D_MODEL, pad), F32)], axis=1)
    br = jnp.concatenate([b_route_group[0], b_route_expert[0], jnp.zeros((pad,), F32)])[None, :]
    wts = dict(
        wa=w_a_out[0].astype(BF16), wb=w_b_out[0].astype(BF16), wo=w_o[0].astype(BF16),
        wg=w_gate_e[0], wu=w_up_e[0], wd=w_down_e[0],
        wr=wr.astype(BF16), br=br, norm_ffn=norm_ffn[0][None, :], norm_final=norm_final[None, :],
    )
    w_in_b = w_in[0].astype(BF16)
    norm_g = norm_mix[0][None, :]
    ln_g, ln_b = ln_v_g[0][None, :], ln_v_b[0][None, :]

    xp = x_prompt.reshape(batch * seq, D_MODEL)
    perms = jnp.stack([_class_major_perm(dil) for _, dil in DILATED_GROUPS[1:]])
    zp_a, zp_b = _inproj_all(xp, norm_g, w_in_b, tm=1024, perms=perms)
    (sgp,) = _sgu(zp_a, ln_g, ln_b, w_s[0], b_s[0].T, chunks=8, emit_vn=False)
    os_p, lses_p = zip(*[_attn_group(zp_b, batch, seq, g) for g in range(N_DIL)])
    perms_t = perms.transpose(0, 2, 1)
    mixed_p = _mixer_out(xp, zp_a, sgp, os_p, lses_p, wts, tm=256, perms_t=perms_t)
    kv_prompt = _kv_windows(zp_b, batch, seq, perms_t)

    ms = bd * n_new
    assert ms == CHUNK and n_new <= CHUNK
    xs = x_sample.reshape(ms, D_MODEL)
    zs_a, zs_b = _inproj_all(xs, norm_g, w_in_b, tm=ms)
    eye = jnp.eye(bd, dtype=F32)
    ws_s = jnp.einsum("ab,gts->gatbs", eye, w_s[0][:, :n_new, :n_new]).reshape(A_GROUPS, ms, ms)
    bst_s = jnp.tile(b_s[0][:, :n_new].T, (bd, 1))
    sgs, vns = _sgu(zs_a, ln_g, ln_b, ws_s, bst_s, chunks=1, emit_vn=True)
    zs5 = zs_b.astype(F32).reshape(bd, n_new, N_DIL, 3, HEADS_PER_GROUP, HEAD_DIM)
    qkv = [zs5[:, :, :, part].reshape(bd, n_new, N_DIL * HEADS_PER_GROUP, HEAD_DIM) for part in range(3)]
    o_s, lse_s = _attn_sample(*qkv, caches)
    os_s = [o_s[g].reshape(ms, GROUP_COLS).astype(BF16) for g in range(N_DIL)]
    lses_s = [lse_s[g, ..., 0].reshape(ms, HEADS_PER_GROUP) for g in range(N_DIL)]
    mixed_s = _mixer_out(xs, zs_a, sgs, os_s, lses_s, wts, tm=ms)
    y_prompt, y_sample = _moe_ffn(mixed_p, mixed_s, wts)
    y_prompt = y_prompt.reshape(batch, seq, D_MODEL)
    y_sample = y_sample.reshape(bd, n_new, D_MODEL)
    kv_sample = [_kv_rows(zs_b, (bd, n_new), g)[None] for g in range(N_DIL)]
    chunk_v = vns.reshape(1, bd, n_new, A_WIDTH)

    return (y_prompt, y_sample, *kv_prompt, *kv_sample, chunk_v)
```

```python
import functools

import jax
import jax.numpy as jnp
from jax import lax
from jax.experimental import pallas as pl
from jax.experimental.pallas import tpu as pltpu

F32 = jnp.float32
BF16 = jnp.bfloat16

D_MODEL = 2048
CHUNK = 128
A_GROUPS = 16
A_GROUP_DIM = 128
A_WIDTH = A_GROUPS * A_GROUP_DIM
HEAD_DIM = 128
HEADS_PER_GROUP = 4
GROUP_COLS = HEADS_PER_GROUP * HEAD_DIM
DILATED_GROUPS = ((128, 1), (512, 4), (2048, 16))
N_DIL = len(DILATED_GROUPS)
B_WIDTH = N_DIL * GROUP_COLS
COL_U, COL_V = 0, A_WIDTH
COL_GA = 2 * A_WIDTH
COL_GB = COL_GA + D_MODEL
QKV_COLS = 3 * GROUP_COLS
N_EXPERT_GROUPS = 4
EXPERTS_PER_GROUP = 8
N_EXPERTS = N_EXPERT_GROUPS * EXPERTS_PER_GROUP
TOP_K = 2
D_EXPERT = 256
RMS_EPS = 1e-6
LN_EPS = 1e-5
NEG = -1e30
ATTN_SCALE = HEAD_DIM ** -0.5

LANES = 128
VMEM_LIMIT_BYTES = 56 * 1024 * 1024
IN_TN = 512
MOE_TM = 256
PERM_BLOCK = 256
ATTN_UNITS_PER_BATCH = 8
ROW_TILE = D_MODEL // LANES

def _cparams(*sem):
    return pltpu.CompilerParams(dimension_semantics=sem, vmem_limit_bytes=VMEM_LIMIT_BYTES)


def _gelu_tanh(x):
    return 0.5 * x * (1.0 + jnp.tanh(0.7978845608028654 * (x + 0.044715 * (x * x * x))))


def _sigmoid(x):
    return 1.0 / (1.0 + jnp.exp(-x))


def _store_token_tiles(ref, val):
    rows = val.shape[0]
    chunks = jnp.stack([val[:, s * LANES:(s + 1) * LANES].astype(BF16) for s in range(ROW_TILE)], axis=0)
    ref[...] = pltpu.einshape("stl->tsl", chunks).reshape(rows * ROW_TILE, LANES)


def _load_token_tile_chunks(ref, lead, rows):
    tiles = ref[lead + (slice(None), slice(None))].reshape(rows, ROW_TILE, LANES)
    by_chunk = pltpu.einshape("tsl->stl", tiles)
    return [by_chunk[s] for s in range(ROW_TILE)]


def _class_major_perm(dil):
    out_row = jnp.arange(PERM_BLOCK, dtype=jnp.int32)
    src = (out_row % (PERM_BLOCK // dil)) * dil + out_row // (PERM_BLOCK // dil)
    return (src[:, None] == jnp.arange(PERM_BLOCK, dtype=jnp.int32)[None, :]).astype(BF16)


def _inproj_kernel(x_ref, g_ref, *rest, strips, plan, permute):
    w_refs, rest = rest[:strips], rest[strips:]
    if permute:
        perm_ref, z_ref, xn_ref = rest
    else:
        z_ref, xn_ref = rest
    j = pl.program_id(1)

    @pl.when(j == 0)
    def _():
        x = x_ref[...]
        ms = jnp.mean(x * x, axis=-1, keepdims=True)
        xn_ref[...] = (x * lax.rsqrt(ms + RMS_EPS) * g_ref[...]).astype(BF16)

    tm = xn_ref.shape[0]
    sub = min(tm, PERM_BLOCK)

    def emit(epilogue):
        for blk in range(tm // sub):
            rows = slice(blk * sub, (blk + 1) * sub)
            for s, w_ref in enumerate(w_refs):
                z = jnp.dot(xn_ref[rows, :], w_ref[...], preferred_element_type=F32)
                z_ref[rows, s * IN_TN:(s + 1) * IN_TN] = epilogue(z).astype(BF16)

    def epilogue_of(kind):
        if kind == "gelu":
            return _gelu_tanh
        if kind == "gate":
            return _sigmoid
        if kind == "plain":
            return lambda z: z
        k = int(kind[len("perm"):])
        return lambda z: jnp.dot(perm_ref[k], z.astype(BF16), preferred_element_type=F32)

    for kind in sorted(set(plan)):
        tiles = [t for t, p in enumerate(plan) if p == kind]
        hit = functools.reduce(jnp.logical_or, [j == t for t in tiles])

        @pl.when(hit)
        def _():
            emit(epilogue_of(kind))


def _inproj(x, norm_g, w_in_bf16, tm, strips, plan, ref_tile, perms=None):
    m = x.shape[0]
    tn = strips * IN_TN
    in_specs = [pl.BlockSpec((tm, D_MODEL), lambda i, j: (i, 0)), pl.BlockSpec((1, D_MODEL), lambda i, j: (0, 0))]
    in_specs += [pl.BlockSpec((D_MODEL, IN_TN), functools.partial(lambda i, j, s: (0, ref_tile(j, s)), s=s))
                 for s in range(strips)]
    args = [x, norm_g] + [w_in_bf16] * strips
    if perms is not None:
        in_specs.append(pl.BlockSpec(perms.shape, lambda i, j: (0, 0, 0)))
        args.append(perms)
    return pl.pallas_call(
        functools.partial(_inproj_kernel, strips=strips, plan=plan, permute=perms is not None),
        out_shape=jax.ShapeDtypeStruct((m, len(plan) * tn), BF16),
        grid=(m // tm, len(plan)),
        in_specs=in_specs,
        out_specs=pl.BlockSpec((tm, tn), lambda i, j: (i, j)),
        scratch_shapes=[pltpu.VMEM((tm, D_MODEL), BF16)],
        compiler_params=_cparams("parallel", "arbitrary"),
        name="inproj",
    )(*args)


def _inproj_all(x, norm_g, w_in_bf16, tm, perms=None):
    n_gelu = 2 * A_WIDTH // IN_TN
    n_qkv = 3 * B_WIDTH // IN_TN
    strips_a = 2
    plan_a = ("gelu",) * (n_gelu // strips_a) + ("gate",) * (2 * D_MODEL // IN_TN // strips_a)

    def ref_a(j, s):
        t = j * strips_a + s
        return jnp.where(t < n_gelu, t, t + n_qkv)

    z_a = _inproj(x, norm_g, w_in_bf16, tm, strips_a, plan_a, ref_a)
    plan_b = ("plain",) + tuple(f"perm{k}" for k in range(N_DIL - 1)) if perms is not None else ("plain",) * N_DIL
    z_b = _inproj(x, norm_g, w_in_bf16, tm, 3, plan_b, lambda j, s: n_gelu + s * N_DIL + j, perms)
    return z_a, z_b


def _sgu_kernel(u_ref, v_ref, lng_ref, lnb_ref, ws_ref, bst_ref, sg_ref, *vn_out, chunks):
    row = lax.broadcasted_iota(jnp.int32, (CHUNK, CHUNK), 0)
    col = lax.broadcasted_iota(jnp.int32, (CHUNK, CHUNK), 1)
    tri = row >= col
    ws = [jnp.where(tri, ws_ref[g], 0.0).astype(BF16) for g in range(A_GROUPS)]
    for c in range(chunks):
        rows = slice(c * CHUNK, (c + 1) * CHUNK)
        v = v_ref[rows, :].astype(F32)
        mu = jnp.mean(v, axis=-1, keepdims=True)
        vc = v - mu
        var = jnp.mean(vc * vc, axis=-1, keepdims=True)
        vn = vc * lax.rsqrt(var + LN_EPS) * lng_ref[...] + lnb_ref[...]
        if vn_out:
            vn_out[0][rows, :] = vn
        vnb = vn.astype(BF16)
        for g in range(A_GROUPS):
            cols = slice(g * A_GROUP_DIM, (g + 1) * A_GROUP_DIM)
            s = jnp.dot(ws[g], vnb[:, cols], preferred_element_type=F32) + bst_ref[:, g:g + 1]
            sg_ref[rows, cols] = (u_ref[rows, cols].astype(F32) * s).astype(BF16)


def _sgu(z, ln_g, ln_b, ws, bst, chunks, emit_vn):
    m = z.shape[0]
    tm = chunks * CHUNK
    out_shape = [jax.ShapeDtypeStruct((m, A_WIDTH), BF16)]
    out_specs = [pl.BlockSpec((tm, A_WIDTH), lambda i: (i, 0))]
    if emit_vn:
        out_shape.append(jax.ShapeDtypeStruct((m, A_WIDTH), F32))
        out_specs.append(pl.BlockSpec((tm, A_WIDTH), lambda i: (i, 0)))
    return pl.pallas_call(
        functools.partial(_sgu_kernel, chunks=chunks),
        out_shape=out_shape,
        grid=(m // tm,),
        in_specs=[
            pl.BlockSpec((tm, A_WIDTH), lambda i: (i, COL_U // A_WIDTH)),
            pl.BlockSpec((tm, A_WIDTH), lambda i: (i, COL_V // A_WIDTH)),
            pl.BlockSpec((1, A_WIDTH), lambda i: (0, 0)),
            pl.BlockSpec((1, A_WIDTH), lambda i: (0, 0)),
            pl.BlockSpec((A_GROUPS, CHUNK, CHUNK), lambda i: (0, 0, 0)),
            pl.BlockSpec((CHUNK, A_GROUPS), lambda i: (0, 0)),
        ],
        out_specs=out_specs,
        compiler_params=_cparams("parallel"),
        name="sgu",
    )(z, z, ln_g, ln_b, ws, bst)


def _attn_kernel(q_ref, kc_ref, kp_ref, vc_ref, vp_ref, o_ref, lse_ref, *, qblocks):
    i = pl.program_id(2)
    qi = lax.broadcasted_iota(jnp.int32, (CHUNK, CHUNK), 0)
    ki = lax.broadcasted_iota(jnp.int32, (CHUNK, CHUNK), 1)
    cur_mask = ki <= qi
    no_prev = jnp.where(i == 0, CHUNK, 0)
    dn = (((1,), (1,)), ((), ()))
    rpc = q_ref.shape[1]
    nbq = CHUNK // rpc

    def load(ref, a, cols):
        return ref[a * nbq:(a + 1) * nbq, :, cols].reshape(CHUNK, HEAD_DIM)

    units = [(a, h) for a in range(qblocks) for h in range(HEADS_PER_GROUP)]
    for start in range(0, len(units), ATTN_UNITS_PER_BATCH):
        batch_units = units[start:start + ATTN_UNITS_PER_BATCH]
        scores = []
        for a, h in batch_units:
            cols = slice(h * HEAD_DIM, (h + 1) * HEAD_DIM)
            q = load(q_ref, a, cols)
            if a == 0:
                kp, prev_mask = load(kp_ref, 0, cols), ki >= qi + no_prev
            else:
                kp, prev_mask = load(kc_ref, a - 1, cols), ki >= qi
            s_c = lax.dot_general(q, load(kc_ref, a, cols), dn, preferred_element_type=F32) * ATTN_SCALE
            s_p = lax.dot_general(q, kp, dn, preferred_element_type=F32) * ATTN_SCALE
            scores.append((jnp.where(cur_mask, s_c, NEG), jnp.where(prev_mask, s_p, NEG)))
        probs = []
        for s_c, s_p in scores:
            mx = jnp.maximum(jnp.max(s_c, axis=-1, keepdims=True), jnp.max(s_p, axis=-1, keepdims=True))
            p_c = jnp.exp(s_c - mx)
            p_p = jnp.exp(s_p - mx)
            l = jnp.sum(p_c, axis=-1, keepdims=True) + jnp.sum(p_p, axis=-1, keepdims=True)
            probs.append((p_c.astype(BF16), p_p.astype(BF16), mx, l))
        for (a, h), (p_c, p_p, mx, l) in zip(batch_units, probs):
            cols = slice(h * HEAD_DIM, (h + 1) * HEAD_DIM)
            vp = load(vp_ref, 0, cols) if a == 0 else load(vc_ref, a - 1, cols)
            acc = jnp.dot(p_c, load(vc_ref, a, cols), preferred_element_type=F32)
            acc = acc + jnp.dot(p_p, vp, preferred_element_type=F32)
            o_ref[a * nbq:(a + 1) * nbq, :, cols] = (acc / l).astype(BF16).reshape(nbq, rpc, HEAD_DIM)
            lse_ref[a * CHUNK:(a + 1) * CHUNK, h:h + 1] = mx + jnp.log(l)


def _rows_per_class(dil):
    return min(PERM_BLOCK // dil, CHUNK)


def _attn_group(z, batch, seq, group):
    _, dil = DILATED_GROUPS[group]
    sub = seq // dil
    rpc = _rows_per_class(dil)
    qblocks = max(b for b in (4, 2, 1) if sub % (b * CHUNK) == 0)
    tq = qblocks * CHUNK
    cq, ck, cv = (group * QKV_COLS // GROUP_COLS + part for part in range(3))
    zv = z.reshape(batch, sub // rpc, dil, rpc, z.shape[-1])

    def cur(cb):
        return pl.BlockSpec((None, tq // rpc, None, rpc, GROUP_COLS), lambda b, r, i: (b, i, r, 0, cb))

    def prev(cb):
        return pl.BlockSpec((None, CHUNK // rpc, None, rpc, GROUP_COLS),
                            lambda b, r, i: (b, jnp.maximum(i * qblocks - 1, 0), r, 0, cb))

    o, lse = pl.pallas_call(
        functools.partial(_attn_kernel, qblocks=qblocks),
        out_shape=[jax.ShapeDtypeStruct((batch, sub // rpc, dil, rpc, GROUP_COLS), BF16),
                   jax.ShapeDtypeStruct((batch, dil, sub, HEADS_PER_GROUP), F32)],
        grid=(batch, dil, sub // tq),
        in_specs=[cur(cq), cur(ck), prev(ck), cur(cv), prev(cv)],
        out_specs=[pl.BlockSpec((None, tq // rpc, None, rpc, GROUP_COLS), lambda b, r, i: (b, i, r, 0, 0)),
                   pl.BlockSpec((None, None, tq, HEADS_PER_GROUP), lambda b, r, i: (b, r, i, 0))],
        compiler_params=_cparams("parallel", "parallel", "arbitrary"),
        name=f"attn_prompt_g{group}",
    )(zv, zv, zv, zv, zv)
    lse = lse.transpose(0, 2, 1, 3).reshape(batch * seq, HEADS_PER_GROUP)
    return o, lse


def _attn_sample_kernel(q_ref, k_ref, v_ref, c0_ref, c1_ref, c2_ref, o_ref, lse_ref, *, n_new):
    caches = (c0_ref, c1_ref, c2_ref)
    rowid = lax.broadcasted_iota(jnp.int32, (CHUNK, HEADS_PER_GROUP, 1), 0)
    for g, (win, dil) in enumerate(DILATED_GROUPS):
        cref = caches[g]
        hs = slice(g * HEADS_PER_GROUP, (g + 1) * HEADS_PER_GROUP)
        for t in range(n_new):
            res, first = t % dil, t // dil
            kc = cref[:, res, 0]
            vc = cref[:, res, 1]
            q = q_ref[t, hs, :]
            s = jnp.sum(kc * q[None], axis=-1, keepdims=True) * ATTN_SCALE
            if first > 0:
                s = jnp.where(rowid >= first, s, NEG)
            new_rows = [tn for tn in range(t + 1) if (t - tn) % dil == 0 and (t - tn) // dil <= win // dil]
            s_new = [jnp.sum(q * k_ref[tn, hs, :], axis=-1, keepdims=True) * ATTN_SCALE for tn in new_rows]
            mx = jnp.max(s, axis=0)
            for sn in s_new:
                mx = jnp.maximum(mx, sn)
            p = jnp.exp(s - mx[None])
            l = jnp.sum(p, axis=0)
            acc = jnp.sum(p * vc, axis=0)
            for tn, sn in zip(new_rows, s_new):
                pn = jnp.exp(sn - mx)
                l = l + pn
                acc = acc + pn * v_ref[tn, hs, :]
            o_ref[g, t] = acc / l
            lse_ref[g, t] = jnp.broadcast_to(mx + jnp.log(l), (HEADS_PER_GROUP, HEAD_DIM))


def _attn_sample(q, k, v, caches):
    bd, n_new = q.shape[:2]
    views, specs = [], []
    for (win, dil), c in zip(DILATED_GROUPS, caches):
        assert c.shape[2] == win and win == CHUNK * dil, "cache must hold exactly one full window"
        used = min(dil, n_new)
        views.append(c.reshape(1, bd, CHUNK, dil, 2, HEADS_PER_GROUP, HEAD_DIM))
        specs.append(pl.BlockSpec((None, None, CHUNK, used, 2, HEADS_PER_GROUP, HEAD_DIM),
                                  lambda b: (0, b, 0, 0, 0, 0, 0)))
    new_spec = pl.BlockSpec((None, n_new, N_DIL * HEADS_PER_GROUP, HEAD_DIM), lambda b: (b, 0, 0, 0))
    out_sds = jax.ShapeDtypeStruct((N_DIL, bd, n_new, HEADS_PER_GROUP, HEAD_DIM), F32)
    out_spec = pl.BlockSpec((N_DIL, None, n_new, HEADS_PER_GROUP, HEAD_DIM), lambda b: (0, b, 0, 0, 0))
    return pl.pallas_call(
        functools.partial(_attn_sample_kernel, n_new=n_new),
        out_shape=[out_sds, out_sds],
        grid=(bd,),
        in_specs=[new_spec, new_spec, new_spec] + specs,
        out_specs=[out_spec, out_spec],
        compiler_params=_cparams("parallel"),
        name="attn_sample",
    )(q, k, v, *views)


def _mixer_out_kernel(sg_ref, o0_ref, o1_ref, o2_ref, l0_ref, l1_ref, l2_ref, ga_ref, gb_ref,
                      wa_ref, wb_ref, x_ref, wo_ref, g_ref, wr_ref, br_ref, *rest, permuted):
    if permuted:
        pt_ref, h_ref, hn_ref, ri_ref, rw_ref, ob_ref = rest
    else:
        h_ref, hn_ref, ri_ref, rw_ref, ob_ref = rest

    l0, l1, l2 = l0_ref[...], l1_ref[...], l2_ref[...]
    mx = jnp.maximum(jnp.maximum(l0, l1), l2)
    e0, e1, e2 = jnp.exp(l0 - mx), jnp.exp(l1 - mx), jnp.exp(l2 - mx)
    inv = 1.0 / (e0 + e1 + e2)
    ws = (e0 * inv, e1 * inv, e2 * inv)
    o_refs = (o0_ref, o1_ref, o2_ref)
    tm = ob_ref.shape[0]

    def o_rows(ref, rows):
        if len(ref.shape) == 2:
            return ref[rows, :]
        per = ref.shape[1] * ref.shape[2]
        return ref[rows.start // per:rows.stop // per].reshape(rows.stop - rows.start, GROUP_COLS)

    for blk in range(max(tm // PERM_BLOCK, 1)):
        rows = slice(blk * PERM_BLOCK, min((blk + 1) * PERM_BLOCK, tm))
        og = []
        for g in range(N_DIL):
            if permuted and g > 0:
                og.append(jnp.dot(pt_ref[g - 1], o_rows(o_refs[g], rows), preferred_element_type=F32))
            else:
                og.append(o_rows(o_refs[g], rows).astype(F32))
        for h in range(HEADS_PER_GROUP):
            cols = slice(h * HEAD_DIM, (h + 1) * HEAD_DIM)
            ob = sum(ws[g][rows, h:h + 1] * og[g][:, cols] for g in range(N_DIL))
            ob_ref[rows, cols] = ob.astype(BF16)

    ya = jnp.dot(sg_ref[...], wa_ref[...], preferred_element_type=F32)
    yb = jnp.dot(ob_ref[...], wb_ref[...], preferred_element_type=F32)
    m = (ga_ref[...].astype(F32) * ya + gb_ref[...].astype(F32) * yb).astype(BF16)
    _oproj_rows(m, x_ref, wo_ref, g_ref, wr_ref, br_ref, h_ref, hn_ref, ri_ref, rw_ref)


def _mixer_out(x, z, sg, os_, lses, wts, tm, perms_t=None):
    m = z.shape[0]
    row = lambda width: pl.BlockSpec((tm, width), lambda i: (i, 0))
    whole = lambda a: pl.BlockSpec(a.shape, lambda i: (0,) * a.ndim, pipeline_mode=pl.Buffered(1))

    def o_spec(o):
        if o.ndim == 2:
            return row(GROUP_COLS)
        _, blocks, dil, rpc, _ = o.shape
        per_tile, tiles = tm // (dil * rpc), blocks * dil * rpc // tm
        return pl.BlockSpec((None, per_tile, dil, rpc, GROUP_COLS), lambda i: (i // tiles, i % tiles, 0, 0, 0))

    consts = [wts["wa"], wts["wb"]]
    tail_consts = [wts["wo"], wts["norm_ffn"], wts["wr"], wts["br"]]
    in_specs = [row(A_WIDTH)] + [o_spec(o) for o in os_] + [row(HEADS_PER_GROUP)] * 3 + [
        pl.BlockSpec((tm, D_MODEL), lambda i: (i, COL_GA // D_MODEL)),
        pl.BlockSpec((tm, D_MODEL), lambda i: (i, COL_GB // D_MODEL)),
    ] + [whole(a) for a in consts] + [row(D_MODEL)] + [whole(a) for a in tail_consts]
    args = [sg, *os_, *lses, z, z, *consts, x, *tail_consts]
    if perms_t is not None:
        assert tm % PERM_BLOCK == 0
        in_specs.append(whole(perms_t))
        args.append(perms_t)
    return pl.pallas_call(
        functools.partial(_mixer_out_kernel, permuted=perms_t is not None),
        out_shape=[jax.ShapeDtypeStruct((m, D_MODEL), F32), jax.ShapeDtypeStruct((m * ROW_TILE, LANES), BF16),
                   jax.ShapeDtypeStruct((m, LANES), jnp.int32), jax.ShapeDtypeStruct((m, LANES), F32)],
        grid=(m // tm,),
        in_specs=in_specs,
        out_specs=[row(D_MODEL), pl.BlockSpec((tm * ROW_TILE, LANES), lambda i: (i, 0)), row(LANES), row(LANES)],
        scratch_shapes=[pltpu.VMEM((tm, GROUP_COLS), BF16)],
        compiler_params=_cparams("parallel"),
        name="mixer_out",
    )(*args)


def _oproj_rows(m, x_ref, wo_ref, g_ref, wr_ref, br_ref, h_ref, hn_ref, ri_ref, rw_ref):
    h = x_ref[...] + jnp.dot(m, wo_ref[...], preferred_element_type=F32)
    h_ref[...] = h
    ms = jnp.mean(h * h, axis=-1, keepdims=True)
    hn = h * lax.rsqrt(ms + RMS_EPS) * g_ref[...]
    _store_token_tiles(hn_ref, hn)
    logits = jnp.dot(hn.astype(BF16), wr_ref[...], preferred_element_type=F32) + br_ref[...]
    lane = lax.broadcasted_iota(jnp.int32, logits.shape, 1).astype(F32)
    big = float(LANES)

    def first_argmax(vals, vmax):
        return jnp.min(jnp.where(vals == vmax, lane, big), axis=-1, keepdims=True)

    lg = jnp.where(lane < N_EXPERT_GROUPS, logits, NEG)
    gmax = jnp.max(lg, axis=-1, keepdims=True)
    gsel = first_argmax(lg, gmax)
    p_sel = 1.0 / jnp.sum(jnp.exp(lg - gmax), axis=-1, keepdims=True)
    lo = N_EXPERT_GROUPS + EXPERTS_PER_GROUP * gsel
    le = jnp.where(jnp.logical_and(lane >= lo, lane < lo + EXPERTS_PER_GROUP), logits, NEG)
    v1 = jnp.max(le, axis=-1, keepdims=True)
    i1 = first_argmax(le, v1)
    le2 = jnp.where(lane == i1, NEG, le)
    v2 = jnp.max(le2, axis=-1, keepdims=True)
    i2 = first_argmax(le2, v2)
    e2 = jnp.exp(v2 - v1)
    w1 = p_sel / (1.0 + e2)
    w2 = p_sel * e2 / (1.0 + e2)
    ri = jnp.where(lane == 0, i1 - N_EXPERT_GROUPS, jnp.where(lane == 1, i2 - N_EXPERT_GROUPS, 0.0))
    ri_ref[...] = ri.astype(jnp.int32)
    rw_ref[...] = jnp.where(lane == 0, w1, jnp.where(lane == 1, w2, 0.0))


def _dispatch_kernel(slot_ref, pad_start_ref, pad_len_ref, hn_ref, hs_ref, xs_hbm, zbuf, sem, zsem,
                     *, tm, moe_tm, n_tiles):
    i = pl.program_id(0)
    n_main = pl.num_programs(0) - 1

    def zero_copy(dst_row, rows):
        return pltpu.make_async_copy(zbuf.at[pl.ds(0, rows * ROW_TILE)],
                                     xs_hbm.at[pl.ds(dst_row * ROW_TILE, rows * ROW_TILE)], zsem)

    def zero_fill(act):
        def per_expert(e, carry):
            off, length = pad_start_ref[e], pad_len_ref[e]
            rows = moe_tm // 2
            while rows >= 1:
                @pl.when((length & rows) != 0)
                def _(off=off, rows=rows):
                    act(zero_copy(off, rows))
                off = off + (length & rows)
                rows //= 2
            return carry

        lax.fori_loop(0, N_EXPERTS, per_expert, 0)

        def per_tile(t, carry):
            act(zero_copy(t * moe_tm, moe_tm))
            return carry

        lax.fori_loop(pad_start_ref[N_EXPERTS], n_tiles, per_tile, 0)

    @pl.when(i == 0)
    def _():
        zbuf[...] = jnp.zeros_like(zbuf)
        zero_fill(lambda c: c.start())

    def scatter_rows(ref, first_token):
        rows = ref.shape[0] // ROW_TILE

        def issue(r, carry):
            for k in range(TOP_K):
                slot = slot_ref[(first_token + r) * TOP_K + k]
                pltpu.make_async_copy(ref.at[pl.ds(r * ROW_TILE, ROW_TILE)],
                                      xs_hbm.at[pl.ds(slot * ROW_TILE, ROW_TILE)], sem).start(priority=k % 2)
            return carry

        lax.fori_loop(0, rows, issue, 0, unroll=8)
        for k in range(TOP_K):
            pltpu.make_async_copy(ref, ref, sem).wait()

    @pl.when(i < n_main)
    def _():
        scatter_rows(hn_ref, i * tm)

    @pl.when(i == n_main)
    def _():
        scatter_rows(hs_ref, n_main * tm)
        zero_fill(lambda c: c.wait())


def _dispatch(hn_tiles, hs_tiles, slot, pad_start, pad_len, n_tiles, moe_tm, tm):
    m = hn_tiles.shape[0] // ROW_TILE
    assert m % tm == 0 and moe_tm & (moe_tm - 1) == 0
    assert slot.shape[0] == (m + hs_tiles.shape[0] // ROW_TILE) * TOP_K
    n_main = m // tm
    grid_spec = pltpu.PrefetchScalarGridSpec(
        num_scalar_prefetch=3,
        grid=(n_main + 1,),
        in_specs=[pl.BlockSpec((tm * ROW_TILE, LANES), lambda i, s, ps, pn: (jnp.minimum(i, n_main - 1), 0)),
                  pl.BlockSpec(hs_tiles.shape, lambda i, s, ps, pn: (0, 0))],
        out_specs=pl.BlockSpec(memory_space=pl.ANY),
        scratch_shapes=[pltpu.VMEM((moe_tm * ROW_TILE, LANES), BF16),
                        pltpu.SemaphoreType.DMA(()), pltpu.SemaphoreType.DMA(())],
    )
    return pl.pallas_call(
        functools.partial(_dispatch_kernel, tm=tm, moe_tm=moe_tm, n_tiles=n_tiles),
        out_shape=jax.ShapeDtypeStruct((n_tiles * moe_tm * ROW_TILE, LANES), BF16),
        grid_spec=grid_spec,
        compiler_params=_cparams("arbitrary"),
        name="moe_dispatch",
    )(slot, pad_start, pad_len, hn_tiles, hs_tiles)


def _moe_kernel(texp_ref, valid_ref, xs_ref, wg_ref, wu_ref, wd_ref, ys_ref, wg_bf, wu_bf, wd_bf):
    i = pl.program_id(0)
    tm = xs_ref.shape[0] // ROW_TILE
    valid = valid_ref[i]
    new_expert = jnp.logical_or(i == 0, texp_ref[i] != texp_ref[jnp.maximum(i - 1, 0)])

    @pl.when(jnp.logical_and(valid > 0, new_expert))
    def _():
        wg_bf[...] = wg_ref[...].astype(BF16)
        wu_bf[...] = wu_ref[...].astype(BF16)
        wd_bf[...] = wd_ref[...].astype(BF16)

    @pl.when(valid > 0)
    def _():
        x = jnp.concatenate(_load_token_tile_chunks(xs_ref, (), tm), axis=1)
        gate = jnp.dot(x, wg_bf[...], preferred_element_type=F32)
        up = jnp.dot(x, wu_bf[...], preferred_element_type=F32)
        hid = (gate * _sigmoid(gate) * up).astype(BF16)
        _store_token_tiles(ys_ref, jnp.dot(hid, wd_bf[...], preferred_element_type=F32))

    @pl.when(valid == 0)
    def _():
        ys_ref[...] = jnp.zeros_like(ys_ref)


def _moe(xs, wg, wu, wd, tile_expert, tile_valid, tm):
    n_tiles = tile_expert.shape[0]
    grid_spec = pltpu.PrefetchScalarGridSpec(
        num_scalar_prefetch=2,
        grid=(n_tiles,),
        in_specs=[
            pl.BlockSpec((tm * ROW_TILE, LANES), lambda i, te, tv: (i, 0)),
            pl.BlockSpec((None, D_MODEL, D_EXPERT), lambda i, te, tv: (te[i], 0, 0)),
            pl.BlockSpec((None, D_MODEL, D_EXPERT), lambda i, te, tv: (te[i], 0, 0)),
            pl.BlockSpec((None, D_EXPERT, D_MODEL), lambda i, te, tv: (te[i], 0, 0)),
        ],
        out_specs=pl.BlockSpec((tm * ROW_TILE, LANES), lambda i, te, tv: (i, 0)),
        scratch_shapes=[pltpu.VMEM((D_MODEL, D_EXPERT), BF16), pltpu.VMEM((D_MODEL, D_EXPERT), BF16),
                        pltpu.VMEM((D_EXPERT, D_MODEL), BF16)],
    )
    return pl.pallas_call(
        _moe_kernel,
        out_shape=jax.ShapeDtypeStruct((n_tiles * tm * ROW_TILE, LANES), BF16),
        grid_spec=grid_spec,
        compiler_params=_cparams("arbitrary"),
        name="moe_experts",
    )(tile_expert, tile_valid, xs, wg, wu, wd)


def _combine_kernel(slot_ref, h_ref, rw_ref, g_ref, ys_hbm, y_ref, ybuf, sem, *, tm):
    i = pl.program_id(0)
    n_steps = pl.num_programs(0)

    def issue(tile, buf):
        def body(r, carry):
            for k in range(TOP_K):
                slot = slot_ref[(tile * tm + r) * TOP_K + k]
                pltpu.make_async_copy(ys_hbm.at[pl.ds(slot * ROW_TILE, ROW_TILE)],
                                      ybuf.at[buf, k, pl.ds(r * ROW_TILE, ROW_TILE)], sem.at[buf]).start(priority=k % 2)
            return carry

        lax.fori_loop(0, tm, body, 0, unroll=8)

    @pl.when(i == 0)
    def _():
        issue(0, 0)

    @pl.when(i + 1 < n_steps)
    def _():
        issue(i + 1, (i + 1) % 2)

    cur = i % 2
    for k in range(TOP_K):
        pltpu.make_async_copy(ybuf.at[cur, k], ybuf.at[cur, k], sem.at[cur]).wait()
    w0, w1 = rw_ref[:, 0:1], rw_ref[:, 1:2]
    e0 = _load_token_tile_chunks(ybuf, (cur, 0), tm)
    e1 = _load_token_tile_chunks(ybuf, (cur, 1), tm)
    chunks, ssq = [], 0.0
    for s in range(ROW_TILE):
        y = h_ref[:, s * LANES:(s + 1) * LANES] + w0 * e0[s].astype(F32) + w1 * e1[s].astype(F32)
        chunks.append(y)
        ssq = ssq + jnp.sum(y * y, axis=-1, keepdims=True)
    scale = lax.rsqrt(ssq * (1.0 / D_MODEL) + RMS_EPS)
    for s, y in enumerate(chunks):
        cols = slice(s * LANES, (s + 1) * LANES)
        y_ref[:, cols] = y * scale * g_ref[:, cols]


def _combine(h, route_w, norm_g, ys, slot, tm):
    m = h.shape[0]
    grid_spec = pltpu.PrefetchScalarGridSpec(
        num_scalar_prefetch=1,
        grid=(m // tm,),
        in_specs=[
            pl.BlockSpec((tm, D_MODEL), lambda i, s: (i, 0)),
            pl.BlockSpec((tm, LANES), lambda i, s: (i, 0)),
            pl.BlockSpec((1, D_MODEL), lambda i, s: (0, 0)),
            pl.BlockSpec(memory_space=pl.ANY),
        ],
        out_specs=pl.BlockSpec((tm, D_MODEL), lambda i, s: (i, 0)),
        scratch_shapes=[pltpu.VMEM((2, TOP_K, tm * ROW_TILE, LANES), BF16), pltpu.SemaphoreType.DMA((2,))],
    )
    return pl.pallas_call(
        functools.partial(_combine_kernel, tm=tm),
        out_shape=jax.ShapeDtypeStruct((m, D_MODEL), F32),
        grid_spec=grid_spec,
        compiler_params=_cparams("arbitrary"),
        name="moe_combine",
    )(slot, h, route_w, norm_g, ys)


def _routing_tables(route_i, m, tm):
    n = m * TOP_K
    n_tiles = pl.cdiv(n, tm) + N_EXPERTS - 1
    e_flat = route_i[:, :TOP_K].reshape(n)
    onehot = (e_flat[:, None] == jnp.arange(N_EXPERTS, dtype=jnp.int32)[None, :]).astype(jnp.int32)
    csum = jnp.cumsum(onehot, axis=0)
    rank = jnp.take_along_axis(csum, e_flat[:, None], axis=1)[:, 0] - 1
    counts = csum[-1]
    tiles_e = (counts + tm - 1) // tm
    tile_end = jnp.cumsum(tiles_e)
    tile_start = tile_end - tiles_e
    slot = (tile_start[e_flat] * tm + rank).astype(jnp.int32)
    tile_ids = jnp.arange(n_tiles, dtype=jnp.int32)
    tile_expert = jnp.sum((tile_end[None, :] <= tile_ids[:, None]).astype(jnp.int32), axis=1)
    tile_expert = jnp.minimum(tile_expert, N_EXPERTS - 1)
    tile_valid = jnp.clip(counts[tile_expert] - (tile_ids - tile_start[tile_expert]) * tm, 0, tm)
    tile_valid = jnp.where(tile_ids < tile_end[-1], tile_valid, 0).astype(jnp.int32)
    pad_start = jnp.concatenate([tile_start * tm + counts, tile_end[-1:]]).astype(jnp.int32)
    pad_len = (tiles_e * tm - counts).astype(jnp.int32)
    return slot, tile_expert.astype(jnp.int32), tile_valid, pad_start, pad_len


def _moe_ffn(prompt, sample, wts):
    (h_p, hn_p, ri_p, rw_p), (h_s, hn_s, ri_s, rw_s) = prompt, sample
    m_p, m_s = h_p.shape[0], h_s.shape[0]
    route_i = jnp.concatenate([ri_p, ri_s], axis=0)
    slot, tile_expert, tile_valid, pad_start, pad_len = _routing_tables(route_i, m_p + m_s, MOE_TM)
    xs = _dispatch(hn_p, hn_s, slot, pad_start, pad_len, tile_expert.shape[0], MOE_TM, tm=512)
    ys = _moe(xs, wts["wg"], wts["wu"], wts["wd"], tile_expert, tile_valid, MOE_TM)
    y_p = _combine(h_p, rw_p, wts["norm_final"], ys, slot[:m_p * TOP_K], tm=512)
    y_s = _combine(h_s, rw_s, wts["norm_final"], ys, slot[m_p * TOP_K:], tm=m_s)
    return y_p, y_s


def _kv_window_kernel(k0_ref, v0_ref, k1_ref, v1_ref, k2_ref, v2_ref, pt_ref, out0_ref, out1_ref, out2_ref,
                      *, first_step):
    i = pl.program_id(1)

    def emit(out_ref, refs, group, rows_in, rows_out):
        for part, ref in enumerate(refs):
            if group > 0:
                val = jnp.dot(pt_ref[group - 1], ref[...], preferred_element_type=F32)
            else:
                val = ref[...].astype(F32)
            for h in range(HEADS_PER_GROUP):
                out_ref[rows_out, part, h, :] = val[rows_in, h * HEAD_DIM:(h + 1) * HEAD_DIM]

    whole = slice(0, PERM_BLOCK)
    emit(out2_ref, (k2_ref, v2_ref), 2, whole, whole)

    @pl.when(i >= first_step[1])
    def _():
        emit(out1_ref, (k1_ref, v1_ref), 1, whole, whole)

    @pl.when(i >= first_step[0])
    def _():
        keep = out0_ref.shape[0]
        emit(out0_ref, (k0_ref, v0_ref), 0, slice(PERM_BLOCK - keep, PERM_BLOCK), slice(0, keep))


def _kv_windows(z_b, batch, seq, perms_t):
    wins = [min(win, seq) for win, _ in DILATED_GROUPS]
    assert wins[2] % PERM_BLOCK == 0 and wins[1] % PERM_BLOCK == 0 and wins[0] <= PERM_BLOCK and seq % PERM_BLOCK == 0
    steps = wins[2] // PERM_BLOCK
    blocks_per_seq = seq // PERM_BLOCK
    first_step = [steps - max(w // PERM_BLOCK, 1) for w in wins]

    def col_spec(group, part):
        cb = (group * QKV_COLS + (1 + part) * GROUP_COLS) // GROUP_COLS

        def index(b, i):
            return (b * blocks_per_seq + blocks_per_seq - steps + jnp.maximum(i, first_step[group]), cb)

        return pl.BlockSpec((PERM_BLOCK, GROUP_COLS), index)

    def out_spec(group):
        rows = min(wins[group], PERM_BLOCK)
        return pl.BlockSpec((None, None, rows, 2, HEADS_PER_GROUP, HEAD_DIM),
                            lambda b, i: (0, b, jnp.maximum(i - first_step[group], 0), 0, 0, 0))

    in_specs = [col_spec(g, part) for g in range(N_DIL) for part in range(2)]
    in_specs.append(pl.BlockSpec(perms_t.shape, lambda b, i: (0, 0, 0)))
    return pl.pallas_call(
        functools.partial(_kv_window_kernel, first_step=tuple(first_step)),
        out_shape=[jax.ShapeDtypeStruct((1, batch, w, 2, HEADS_PER_GROUP, HEAD_DIM), F32) for w in wins],
        grid=(batch, steps),
        in_specs=in_specs,
        out_specs=[out_spec(g) for g in range(N_DIL)],
        compiler_params=_cparams("arbitrary", "arbitrary"),
        name="kv_windows",
    )(*([z_b] * 6), perms_t)


def _kv_rows(z, lead, group):
    zz = z.reshape(lead + (z.shape[-1],))
    base = group * QKV_COLS
    k = zz[..., base + GROUP_COLS: base + 2 * GROUP_COLS]
    v = zz[..., base + 2 * GROUP_COLS: base + 3 * GROUP_COLS]
    kv = jnp.stack([k, v], axis=2).astype(F32)
    return kv.reshape(kv.shape[:3] + (HEADS_PER_GROUP, HEAD_DIM))


def kernel(x_prompt, x_sample, cache_kv_w128, cache_kv_w512, cache_kv_w2048, norm_mix, w_in, ln_v_g, ln_v_b, w_s, b_s, w_a_out, w_b_out, w_o, norm_ffn, w_route_group, b_route_group, w_route_expert, b_route_expert, w_gate_e, w_up_e, w_down_e, norm_final):
    assert norm_mix.shape[0] == 1, "single-layer trunk"
    batch, seq, _ = x_prompt.shape
    bd, n_new, _ = x_sample.shape
    caches = (cache_kv_w128, cache_kv_w512, cache_kv_w2048)

    pad = LANES - N_EXPERT_GROUPS - N_EXPERTS
    wr = jnp.concatenate([w_route_group[0], w_route_expert[0], jnp.zeros((D_MODEL, pad), F32)], axis=1)
    br = jnp.concatenate([b_route_group[0], b_route_expert[0], jnp.zeros((pad,), F32)])[None, :]
    wts = dict(
        wa=w_a_out[0].astype(BF16), wb=w_b_out[0].astype(BF16), wo=w_o[0].astype(BF16),
        wg=w_gate_e[0], wu=w_up_e[0], wd=w_down_e[0],
        wr=wr.astype(BF16), br=br, norm_ffn=norm_ffn[0][None, :], norm_final=norm_final[None, :],
    )
    w_in_b = w_in[0].astype(BF16)
    norm_g = norm_mix[0][None, :]
    ln_g, ln_b = ln_v_g[0][None, :], ln_v_b[0][None, :]

    xp = x_prompt.reshape(batch * seq, D_MODEL)
    perms = jnp.stack([_class_major_perm(dil) for _, dil in DILATED_GROUPS[1:]])
    zp_a, zp_b = _inproj_all(xp, norm_g, w_in_b, tm=1024, perms=perms)
    (sgp,) = _sgu(zp_a, ln_g, ln_b, w_s[0], b_s[0].T, chunks=8, emit_vn=False)
    os_p, lses_p = zip(*[_attn_group(zp_b, batch, seq, g) for g in range(N_DIL)])
    perms_t = perms.transpose(0, 2, 1)
    mixed_p = _mixer_out(xp, zp_a, sgp, os_p, lses_p, wts, tm=256, perms_t=perms_t)
    kv_prompt = _kv_windows(zp_b, batch, seq, perms_t)

    ms = bd * n_new
    assert ms == CHUNK and n_new <= CHUNK
    xs = x_sample.reshape(ms, D_MODEL)
    zs_a, zs_b = _inproj_all(xs, norm_g, w_in_b, tm=ms)
    eye = jnp.eye(bd, dtype=F32)
    ws_s = jnp.einsum("ab,gts->gatbs", eye, w_s[0][:, :n_new, :n_new]).reshape(A_GROUPS, ms, ms)
    bst_s = jnp.tile(b_s[0][:, :n_new].T, (bd, 1))
    sgs, vns = _sgu(zs_a, ln_g, ln_b, ws_s, bst_s, chunks=1, emit_vn=True)
    zs5 = zs_b.astype(F32).reshape(bd, n_new, N_DIL, 3, HEADS_PER_GROUP, HEAD_DIM)
    qkv = [zs5[:, :, :, part].reshape(bd, n_new, N_DIL * HEADS_PER_GROUP, HEAD_DIM) for part in range(3)]
    o_s, lse_s = _attn_sample(*qkv, caches)
    os_s = [o_s[g].reshape(ms, GROUP_COLS).astype(BF16) for g in range(N_DIL)]
    lses_s = [lse_s[g, ..., 0].reshape(ms, HEADS_PER_GROUP) for g in range(N_DIL)]
    mixed_s = _mixer_out(xs, zs_a, sgs, os_s, lses_s, wts, tm=ms)
    y_prompt, y_sample = _moe_ffn(mixed_p, mixed_s, wts)
    y_prompt = y_prompt.reshape(batch, seq, D_MODEL)
    y_sample = y_sample.reshape(bd, n_new, D_MODEL)
    kv_sample = [_kv_rows(zs_b, (bd, n_new), g)[None] for g in range(N_DIL)]
    chunk_v = vns.reshape(1, bd, n_new, A_WIDTH)

    return (y_prompt, y_sample, *kv_prompt, *kv_sample, chunk_v)
```

```python
import functools

import jax
import jax.numpy as jnp
from jax import lax
from jax.experimental import pallas as pl
from jax.experimental.pallas import tpu as pltpu

F32 = jnp.float32
BF16 = jnp.bfloat16

D_MODEL = 2048
CHUNK = 128
A_GROUPS = 16
A_GROUP_DIM = 128
A_WIDTH = A_GROUPS * A_GROUP_DIM
HEAD_DIM = 128
HEADS_PER_GROUP = 4
GROUP_COLS = HEADS_PER_GROUP * HEAD_DIM
DILATED_GROUPS = ((128, 1), (512, 4), (2048, 16))
N_DIL = len(DILATED_GROUPS)
B_WIDTH = N_DIL * GROUP_COLS
COL_U, COL_V = 0, A_WIDTH
COL_GA = 2 * A_WIDTH
COL_GB = COL_GA + D_MODEL
QKV_COLS = 3 * GROUP_COLS
N_EXPERT_GROUPS = 4
EXPERTS_PER_GROUP = 8
N_EXPERTS = N_EXPERT_GROUPS * EXPERTS_PER_GROUP
TOP_K = 2
D_EXPERT = 256
RMS_EPS = 1e-6
LN_EPS = 1e-5
NEG = -1e30
ATTN_SCALE = HEAD_DIM ** -0.5

LANES = 128
VMEM_LIMIT_BYTES = 56 * 1024 * 1024
IN_TN = 512
MOE_TM = 512
PERM_BLOCK = 256
ATTN_UNITS_PER_BATCH = 8
ROW_TILE = D_MODEL // LANES


def _cparams(*sem):
    return pltpu.CompilerParams(dimension_semantics=sem, vmem_limit_bytes=VMEM_LIMIT_BYTES)


def _gelu_tanh(x):
    return 0.5 * x * (1.0 + jnp.tanh(0.7978845608028654 * (x + 0.044715 * (x * x * x))))


def _sigmoid(x):
    return 1.0 / (1.0 + jnp.exp(-x))


def _store_token_tiles(ref, val):
    rows = val.shape[0]
    chunks = jnp.stack([val[:, s * LANES:(s + 1) * LANES].astype(BF16) for s in range(ROW_TILE)], axis=0)
    ref[...] = pltpu.einshape("stl->tsl", chunks).reshape(rows * ROW_TILE, LANES)


def _load_token_tile_chunks(ref, lead, rows):
    tiles = ref[lead + (slice(None), slice(None))].reshape(rows, ROW_TILE, LANES)
    by_chunk = pltpu.einshape("tsl->stl", tiles)
    return [by_chunk[s] for s in range(ROW_TILE)]


def _class_major_perm(dil):
    out_row = jnp.arange(PERM_BLOCK, dtype=jnp.int32)
    src = (out_row % (PERM_BLOCK // dil)) * dil + out_row // (PERM_BLOCK // dil)
    return (src[:, None] == jnp.arange(PERM_BLOCK, dtype=jnp.int32)[None, :]).astype(BF16)


def _inproj_kernel(x_ref, g_ref, *rest, strips, plan, permute):
    w_refs, rest = rest[:strips], rest[strips:]
    if permute:
        perm_ref, z_ref, xn_ref = rest
    else:
        z_ref, xn_ref = rest
    j = pl.program_id(1)

    @pl.when(j == 0)
    def _():
        x = x_ref[...]
        ms = jnp.mean(x * x, axis=-1, keepdims=True)
        xn_ref[...] = (x * lax.rsqrt(ms + RMS_EPS) * g_ref[...]).astype(BF16)

    tm = xn_ref.shape[0]
    sub = min(tm, PERM_BLOCK)

    def emit(epilogue):
        for blk in range(tm // sub):
            rows = slice(blk * sub, (blk + 1) * sub)
            for s, w_ref in enumerate(w_refs):
                z = jnp.dot(xn_ref[rows, :], w_ref[...], preferred_element_type=F32)
                z_ref[rows, s * IN_TN:(s + 1) * IN_TN] = epilogue(z).astype(BF16)

    def epilogue_of(kind):
        if kind == "gelu":
            return _gelu_tanh
        if kind == "gate":
            return _sigmoid
        if kind == "plain":
            return lambda z: z
        k = int(kind[len("perm"):])
        return lambda z: jnp.dot(perm_ref[k], z.astype(BF16), preferred_element_type=F32)

    for kind in sorted(set(plan)):
        tiles = [t for t, p in enumerate(plan) if p == kind]
        hit = functools.reduce(jnp.logical_or, [j == t for t in tiles])

        @pl.when(hit)
        def _():
            emit(epilogue_of(kind))


def _inproj(x, norm_g, w_in_bf16, tm, strips, plan, ref_tile, perms=None):
    m = x.shape[0]
    tn = strips * IN_TN
    in_specs = [pl.BlockSpec((tm, D_MODEL), lambda i, j: (i, 0)), pl.BlockSpec((1, D_MODEL), lambda i, j: (0, 0))]
    in_specs += [pl.BlockSpec((D_MODEL, IN_TN), functools.partial(lambda i, j, s: (0, ref_tile(j, s)), s=s))
                 for s in range(strips)]
    args = [x, norm_g] + [w_in_bf16] * strips
    if perms is not None:
        in_specs.append(pl.BlockSpec(perms.shape, lambda i, j: (0, 0, 0)))
        args.append(perms)
    return pl.pallas_call(
        functools.partial(_inproj_kernel, strips=strips, plan=plan, permute=perms is not None),
        out_shape=jax.ShapeDtypeStruct((m, len(plan) * tn), BF16),
        grid=(m // tm, len(plan)),
        in_specs=in_specs,
        out_specs=pl.BlockSpec((tm, tn), lambda i, j: (i, j)),
        scratch_shapes=[pltpu.VMEM((tm, D_MODEL), BF16)],
        compiler_params=_cparams("parallel", "arbitrary"),
        name="inproj",
    )(*args)


def _inproj_all(x, norm_g, w_in_bf16, tm, perms=None):
    n_gelu = 2 * A_WIDTH // IN_TN
    n_qkv = 3 * B_WIDTH // IN_TN
    strips_a = 2
    plan_a = ("gelu",) * (n_gelu // strips_a) + ("gate",) * (2 * D_MODEL // IN_TN // strips_a)

    def ref_a(j, s):
        t = j * strips_a + s
        return jnp.where(t < n_gelu, t, t + n_qkv)

    z_a = _inproj(x, norm_g, w_in_bf16, tm, strips_a, plan_a, ref_a)
    plan_b = ("plain",) + tuple(f"perm{k}" for k in range(N_DIL - 1)) if perms is not None else ("plain",) * N_DIL
    z_b = _inproj(x, norm_g, w_in_bf16, tm, 3, plan_b, lambda j, s: n_gelu + s * N_DIL + j, perms)
    return z_a, z_b


def _sgu_kernel(u_ref, v_ref, lng_ref, lnb_ref, ws_ref, bst_ref, sg_ref, *vn_out, chunks):
    row = lax.broadcasted_iota(jnp.int32, (CHUNK, CHUNK), 0)
    col = lax.broadcasted_iota(jnp.int32, (CHUNK, CHUNK), 1)
    tri = row >= col
    ws = [jnp.where(tri, ws_ref[g], 0.0).astype(BF16) for g in range(A_GROUPS)]
    for c in range(chunks):
        rows = slice(c * CHUNK, (c + 1) * CHUNK)
        v = v_ref[rows, :].astype(F32)
        mu = jnp.mean(v, axis=-1, keepdims=True)
        vc = v - mu
        var = jnp.mean(vc * vc, axis=-1, keepdims=True)
        vn = vc * lax.rsqrt(var + LN_EPS) * lng_ref[...] + lnb_ref[...]
        if vn_out:
            vn_out[0][rows, :] = vn
        vnb = vn.astype(BF16)
        for g in range(A_GROUPS):
            cols = slice(g * A_GROUP_DIM, (g + 1) * A_GROUP_DIM)
            s = jnp.dot(ws[g], vnb[:, cols], preferred_element_type=F32) + bst_ref[:, g:g + 1]
            sg_ref[rows, cols] = (u_ref[rows, cols].astype(F32) * s).astype(BF16)


def _sgu(z, ln_g, ln_b, ws, bst, chunks, emit_vn):
    m = z.shape[0]
    tm = chunks * CHUNK
    out_shape = [jax.ShapeDtypeStruct((m, A_WIDTH), BF16)]
    out_specs = [pl.BlockSpec((tm, A_WIDTH), lambda i: (i, 0))]
    if emit_vn:
        out_shape.append(jax.ShapeDtypeStruct((m, A_WIDTH), F32))
        out_specs.append(pl.BlockSpec((tm, A_WIDTH), lambda i: (i, 0)))
    return pl.pallas_call(
        functools.partial(_sgu_kernel, chunks=chunks),
        out_shape=out_shape,
        grid=(m // tm,),
        in_specs=[
            pl.BlockSpec((tm, A_WIDTH), lambda i: (i, COL_U // A_WIDTH)),
            pl.BlockSpec((tm, A_WIDTH), lambda i: (i, COL_V // A_WIDTH)),
            pl.BlockSpec((1, A_WIDTH), lambda i: (0, 0)),
            pl.BlockSpec((1, A_WIDTH), lambda i: (0, 0)),
            pl.BlockSpec((A_GROUPS, CHUNK, CHUNK), lambda i: (0, 0, 0)),
            pl.BlockSpec((CHUNK, A_GROUPS), lambda i: (0, 0)),
        ],
        out_specs=out_specs,
        compiler_params=_cparams("parallel"),
        name="sgu",
    )(z, z, ln_g, ln_b, ws, bst)


def _attn_kernel(q_ref, kc_ref, kp_ref, vc_ref, vp_ref, o_ref, lse_ref, *, qblocks):
    i = pl.program_id(2)
    qi = lax.broadcasted_iota(jnp.int32, (CHUNK, CHUNK), 0)
    ki = lax.broadcasted_iota(jnp.int32, (CHUNK, CHUNK), 1)
    cur_mask = ki <= qi
    no_prev = jnp.where(i == 0, CHUNK, 0)
    dn = (((1,), (1,)), ((), ()))
    rpc = q_ref.shape[1]
    nbq = CHUNK // rpc

    def load(ref, a, cols):
        return ref[a * nbq:(a + 1) * nbq, :, cols].reshape(CHUNK, HEAD_DIM)

    units = [(a, h) for a in range(qblocks) for h in range(HEADS_PER_GROUP)]
    for start in range(0, len(units), ATTN_UNITS_PER_BATCH):
        batch_units = units[start:start + ATTN_UNITS_PER_BATCH]
        scores = []
        for a, h in batch_units:
            cols = slice(h * HEAD_DIM, (h + 1) * HEAD_DIM)
            q = load(q_ref, a, cols)
            if a == 0:
                kp, prev_mask = load(kp_ref, 0, cols), ki >= qi + no_prev
            else:
                kp, prev_mask = load(kc_ref, a - 1, cols), ki >= qi
            s_c = lax.dot_general(q, load(kc_ref, a, cols), dn, preferred_element_type=F32) * ATTN_SCALE
            s_p = lax.dot_general(q, kp, dn, preferred_element_type=F32) * ATTN_SCALE
            scores.append((jnp.where(cur_mask, s_c, NEG), jnp.where(prev_mask, s_p, NEG)))
        probs = []
        for s_c, s_p in scores:
            mx = jnp.maximum(jnp.max(s_c, axis=-1, keepdims=True), jnp.max(s_p, axis=-1, keepdims=True))
            p_c = jnp.exp(s_c - mx)
            p_p = jnp.exp(s_p - mx)
            l = jnp.sum(p_c, axis=-1, keepdims=True) + jnp.sum(p_p, axis=-1, keepdims=True)
            probs.append((p_c.astype(BF16), p_p.astype(BF16), mx, l))
        for (a, h), (p_c, p_p, mx, l) in zip(batch_units, probs):
            cols = slice(h * HEAD_DIM, (h + 1) * HEAD_DIM)
            vp = load(vp_ref, 0, cols) if a == 0 else load(vc_ref, a - 1, cols)
            acc = jnp.dot(p_c, load(vc_ref, a, cols), preferred_element_type=F32)
            acc = acc + jnp.dot(p_p, vp, preferred_element_type=F32)
            o_ref[a * nbq:(a + 1) * nbq, :, cols] = (acc / l).astype(BF16).reshape(nbq, rpc, HEAD_DIM)
            lse_ref[a * CHUNK:(a + 1) * CHUNK, h:h + 1] = mx + jnp.log(l)


def _rows_per_class(dil):
    return min(PERM_BLOCK // dil, CHUNK)


def _attn_group(z, batch, seq, group):
    _, dil = DILATED_GROUPS[group]
    sub = seq // dil
    rpc = _rows_per_class(dil)
    qblocks = max(b for b in (4, 2, 1) if sub % (b * CHUNK) == 0)
    tq = qblocks * CHUNK
    cq, ck, cv = (group * QKV_COLS // GROUP_COLS + part for part in range(3))
    zv = z.reshape(batch, sub // rpc, dil, rpc, z.shape[-1])

    def cur(cb):
        return pl.BlockSpec((None, tq // rpc, None, rpc, GROUP_COLS), lambda b, r, i: (b, i, r, 0, cb))

    def prev(cb):
        return pl.BlockSpec((None, CHUNK // rpc, None, rpc, GROUP_COLS),
                            lambda b, r, i: (b, jnp.maximum(i * qblocks - 1, 0), r, 0, cb))

    o, lse = pl.pallas_call(
        functools.partial(_attn_kernel, qblocks=qblocks),
        out_shape=[jax.ShapeDtypeStruct((batch, sub // rpc, dil, rpc, GROUP_COLS), BF16),
                   jax.ShapeDtypeStruct((batch, dil, sub, HEADS_PER_GROUP), F32)],
        grid=(batch, dil, sub // tq),
        in_specs=[cur(cq), cur(ck), prev(ck), cur(cv), prev(cv)],
        out_specs=[pl.BlockSpec((None, tq // rpc, None, rpc, GROUP_COLS), lambda b, r, i: (b, i, r, 0, 0)),
                   pl.BlockSpec((None, None, tq, HEADS_PER_GROUP), lambda b, r, i: (b, r, i, 0))],
        compiler_params=_cparams("parallel", "parallel", "arbitrary"),
        name=f"attn_prompt_g{group}",
    )(zv, zv, zv, zv, zv)
    lse = lse.transpose(0, 2, 1, 3).reshape(batch * seq, HEADS_PER_GROUP)
    return o, lse


def _attn_sample_kernel(q_ref, k_ref, v_ref, c0_ref, c1_ref, c2_ref, o_ref, lse_ref, *, n_new):
    caches = (c0_ref, c1_ref, c2_ref)
    rowid = lax.broadcasted_iota(jnp.int32, (CHUNK, HEADS_PER_GROUP, 1), 0)
    for g, (win, dil) in enumerate(DILATED_GROUPS):
        cref = caches[g]
        hs = slice(g * HEADS_PER_GROUP, (g + 1) * HEADS_PER_GROUP)
        for t in range(n_new):
            res, first = t % dil, t // dil
            kc = cref[:, res, 0]
            vc = cref[:, res, 1]
            q = q_ref[t, hs, :]
            s = jnp.sum(kc * q[None], axis=-1, keepdims=True) * ATTN_SCALE
            if first > 0:
                s = jnp.where(rowid >= first, s, NEG)
            new_rows = [tn for tn in range(t + 1) if (t - tn) % dil == 0 and (t - tn) // dil <= win // dil]
            s_new = [jnp.sum(q * k_ref[tn, hs, :], axis=-1, keepdims=True) * ATTN_SCALE for tn in new_rows]
            mx = jnp.max(s, axis=0)
            for sn in s_new:
                mx = jnp.maximum(mx, sn)
            p = jnp.exp(s - mx[None])
            l = jnp.sum(p, axis=0)
            acc = jnp.sum(p * vc, axis=0)
            for tn, sn in zip(new_rows, s_new):
                pn = jnp.exp(sn - mx)
                l = l + pn
                acc = acc + pn * v_ref[tn, hs, :]
            o_ref[g, t] = acc / l
            lse_ref[g, t] = jnp.broadcast_to(mx + jnp.log(l), (HEADS_PER_GROUP, HEAD_DIM))


def _attn_sample(q, k, v, caches):
    bd, n_new = q.shape[:2]
    views, specs = [], []
    for (win, dil), c in zip(DILATED_GROUPS, caches):
        assert c.shape[2] == win and win == CHUNK * dil, "cache must hold exactly one full window"
        used = min(dil, n_new)
        views.append(c.reshape(1, bd, CHUNK, dil, 2, HEADS_PER_GROUP, HEAD_DIM))
        specs.append(pl.BlockSpec((None, None, CHUNK, used, 2, HEADS_PER_GROUP, HEAD_DIM),
                                  lambda b: (0, b, 0, 0, 0, 0, 0)))
    new_spec = pl.BlockSpec((None, n_new, N_DIL * HEADS_PER_GROUP, HEAD_DIM), lambda b: (b, 0, 0, 0))
    out_sds = jax.ShapeDtypeStruct((N_DIL, bd, n_new, HEADS_PER_GROUP, HEAD_DIM), F32)
    out_spec = pl.BlockSpec((N_DIL, None, n_new, HEADS_PER_GROUP, HEAD_DIM), lambda b: (0, b, 0, 0, 0))
    return pl.pallas_call(
        functools.partial(_attn_sample_kernel, n_new=n_new),
        out_shape=[out_sds, out_sds],
        grid=(bd,),
        in_specs=[new_spec, new_spec, new_spec] + specs,
        out_specs=[out_spec, out_spec],
        compiler_params=_cparams("parallel"),
        name="attn_sample",
    )(q, k, v, *views)


def _mixer_out_kernel(sg_ref, o0_ref, o1_ref, o2_ref, l0_ref, l1_ref, l2_ref, ga_ref, gb_ref,
                      wa_ref, wb_ref, x_ref, wo_ref, g_ref, wr_ref, br_ref, *rest, permuted):
    if permuted:
        pt_ref, h_ref, hn_ref, ri_ref, rw_ref, ob_ref = rest
    else:
        h_ref, hn_ref, ri_ref, rw_ref, ob_ref = rest

    l0, l1, l2 = l0_ref[...], l1_ref[...], l2_ref[...]
    mx = jnp.maximum(jnp.maximum(l0, l1), l2)
    e0, e1, e2 = jnp.exp(l0 - mx), jnp.exp(l1 - mx), jnp.exp(l2 - mx)
    inv = 1.0 / (e0 + e1 + e2)
    ws = (e0 * inv, e1 * inv, e2 * inv)
    o_refs = (o0_ref, o1_ref, o2_ref)
    tm = ob_ref.shape[0]

    def o_rows(ref, rows):
        if len(ref.shape) == 2:
            return ref[rows, :]
        per = ref.shape[1] * ref.shape[2]
        return ref[rows.start // per:rows.stop // per].reshape(rows.stop - rows.start, GROUP_COLS)

    for blk in range(max(tm // PERM_BLOCK, 1)):
        rows = slice(blk * PERM_BLOCK, min((blk + 1) * PERM_BLOCK, tm))
        og = []
        for g in range(N_DIL):
            if permuted and g > 0:
                og.append(jnp.dot(pt_ref[g - 1], o_rows(o_refs[g], rows), preferred_element_type=F32))
            else:
                og.append(o_rows(o_refs[g], rows).astype(F32))
        for h in range(HEADS_PER_GROUP):
            cols = slice(h * HEAD_DIM, (h + 1) * HEAD_DIM)
            ob = sum(ws[g][rows, h:h + 1] * og[g][:, cols] for g in range(N_DIL))
            ob_ref[rows, cols] = ob.astype(BF16)

    ya = jnp.dot(sg_ref[...], wa_ref[...], preferred_element_type=F32)
    yb = jnp.dot(ob_ref[...], wb_ref[...], preferred_element_type=F32)
    m = (ga_ref[...].astype(F32) * ya + gb_ref[...].astype(F32) * yb).astype(BF16)
    _oproj_rows(m, x_ref, wo_ref, g_ref, wr_ref, br_ref, h_ref, hn_ref, ri_ref, rw_ref)


def _mixer_out(x, z, sg, os_, lses, wts, tm, perms_t=None):
    m = z.shape[0]
    row = lambda width: pl.BlockSpec((tm, width), lambda i: (i, 0))
    whole = lambda a: pl.BlockSpec(a.shape, lambda i: (0,) * a.ndim, pipeline_mode=pl.Buffered(1))

    def o_spec(o):
        if o.ndim == 2:
            return row(GROUP_COLS)
        _, blocks, dil, rpc, _ = o.shape
        per_tile, tiles = tm // (dil * rpc), blocks * dil * rpc // tm
        return pl.BlockSpec((None, per_tile, dil, rpc, GROUP_COLS), lambda i: (i // tiles, i % tiles, 0, 0, 0))

    consts = [wts["wa"], wts["wb"]]
    tail_consts = [wts["wo"], wts["norm_ffn"], wts["wr"], wts["br"]]
    in_specs = [row(A_WIDTH)] + [o_spec(o) for o in os_] + [row(HEADS_PER_GROUP)] * 3 + [
        pl.BlockSpec((tm, D_MODEL), lambda i: (i, COL_GA // D_MODEL)),
        pl.BlockSpec((tm, D_MODEL), lambda i: (i, COL_GB // D_MODEL)),
    ] + [whole(a) for a in consts] + [row(D_MODEL)] + [whole(a) for a in tail_consts]
    args = [sg, *os_, *lses, z, z, *consts, x, *tail_consts]
    if perms_t is not None:
        assert tm % PERM_BLOCK == 0
        in_specs.append(whole(perms_t))
        args.append(perms_t)
    return pl.pallas_call(
        functools.partial(_mixer_out_kernel, permuted=perms_t is not None),
        out_shape=[jax.ShapeDtypeStruct((m, D_MODEL), F32), jax.ShapeDtypeStruct((m * ROW_TILE, LANES), BF16),
                   jax.ShapeDtypeStruct((m, LANES), jnp.int32), jax.ShapeDtypeStruct((m, LANES), F32)],
        grid=(m // tm,),
        in_specs=in_specs,
        out_specs=[row(D_MODEL), pl.BlockSpec((tm * ROW_TILE, LANES), lambda i: (i, 0)), row(LANES), row(LANES)],
        scratch_shapes=[pltpu.VMEM((tm, GROUP_COLS), BF16)],
        compiler_params=_cparams("parallel"),
        name="mixer_out",
    )(*args)


def _oproj_rows(m, x_ref, wo_ref, g_ref, wr_ref, br_ref, h_ref, hn_ref, ri_ref, rw_ref):
    h = x_ref[...] + jnp.dot(m, wo_ref[...], preferred_element_type=F32)
    h_ref[...] = h
    ms = jnp.mean(h * h, axis=-1, keepdims=True)
    hn = h * lax.rsqrt(ms + RMS_EPS) * g_ref[...]
    _store_token_tiles(hn_ref, hn)
    logits = jnp.dot(hn.astype(BF16), wr_ref[...], preferred_element_type=F32) + br_ref[...]
    lane = lax.broadcasted_iota(jnp.int32, logits.shape, 1).astype(F32)
    big = float(LANES)

    def first_argmax(vals, vmax):
        return jnp.min(jnp.where(vals == vmax, lane, big), axis=-1, keepdims=True)

    lg = jnp.where(lane < N_EXPERT_GROUPS, logits, NEG)
    gmax = jnp.max(lg, axis=-1, keepdims=True)
    gsel = first_argmax(lg, gmax)
    p_sel = 1.0 / jnp.sum(jnp.exp(lg - gmax), axis=-1, keepdims=True)
    lo = N_EXPERT_GROUPS + EXPERTS_PER_GROUP * gsel
    le = jnp.where(jnp.logical_and(lane >= lo, lane < lo + EXPERTS_PER_GROUP), logits, NEG)
    v1 = jnp.max(le, axis=-1, keepdims=True)
    i1 = first_argmax(le, v1)
    le2 = jnp.where(lane == i1, NEG, le)
    v2 = jnp.max(le2, axis=-1, keepdims=True)
    i2 = first_argmax(le2, v2)
    e2 = jnp.exp(v2 - v1)
    w1 = p_sel / (1.0 + e2)
    w2 = p_sel * e2 / (1.0 + e2)
    ri = jnp.where(lane == 0, i1 - N_EXPERT_GROUPS, jnp.where(lane == 1, i2 - N_EXPERT_GROUPS, 0.0))
    ri_ref[...] = ri.astype(jnp.int32)
    rw_ref[...] = jnp.where(lane == 0, w1, jnp.where(lane == 1, w2, 0.0))


def _dispatch_kernel(slot_ref, pad_start_ref, pad_len_ref, hn_ref, hs_ref, xs_hbm, zbuf, sem, zsem,
                     *, tm, moe_tm, n_tiles):
    i = pl.program_id(0)
    n_main = pl.num_programs(0) - 1

    def zero_copy(dst_row, rows):
        return pltpu.make_async_copy(zbuf.at[pl.ds(0, rows * ROW_TILE)],
                                     xs_hbm.at[pl.ds(dst_row * ROW_TILE, rows * ROW_TILE)], zsem)

    def zero_fill(act):
        def per_expert(e, carry):
            off, length = pad_start_ref[e], pad_len_ref[e]
            rows = moe_tm // 2
            while rows >= 1:
                @pl.when((length & rows) != 0)
                def _(off=off, rows=rows):
                    act(zero_copy(off, rows))
                off = off + (length & rows)
                rows //= 2
            return carry

        lax.fori_loop(0, N_EXPERTS, per_expert, 0)

        def per_tile(t, carry):
            act(zero_copy(t * moe_tm, moe_tm))
            return carry

        lax.fori_loop(pad_start_ref[N_EXPERTS], n_tiles, per_tile, 0)

    @pl.when(i == 0)
    def _():
        zbuf[...] = jnp.zeros_like(zbuf)
        zero_fill(lambda c: c.start())

    def scatter_rows(ref, first_token):
        rows = ref.shape[0] // ROW_TILE

        def issue(r, carry):
            for k in range(TOP_K):
                slot = slot_ref[(first_token + r) * TOP_K + k]
                pltpu.make_async_copy(ref.at[pl.ds(r * ROW_TILE, ROW_TILE)],
                                      xs_hbm.at[pl.ds(slot * ROW_TILE, ROW_TILE)], sem).start(priority=k % 2)
            return carry

        lax.fori_loop(0, rows, issue, 0, unroll=8)
        for k in range(TOP_K):
            pltpu.make_async_copy(ref, ref, sem).wait()

    @pl.when(i < n_main)
    def _():
        scatter_rows(hn_ref, i * tm)

    @pl.when(i == n_main)
    def _():
        scatter_rows(hs_ref, n_main * tm)
        zero_fill(lambda c: c.wait())


def _dispatch(hn_tiles, hs_tiles, slot, pad_start, pad_len, n_tiles, moe_tm, tm):
    m = hn_tiles.shape[0] // ROW_TILE
    assert m % tm == 0 and moe_tm & (moe_tm - 1) == 0
    assert slot.shape[0] == (m + hs_tiles.shape[0] // ROW_TILE) * TOP_K
    n_main = m // tm
    grid_spec = pltpu.PrefetchScalarGridSpec(
        num_scalar_prefetch=3,
        grid=(n_main + 1,),
        in_specs=[pl.BlockSpec((tm * ROW_TILE, LANES), lambda i, s, ps, pn: (jnp.minimum(i, n_main - 1), 0)),
                  pl.BlockSpec(hs_tiles.shape, lambda i, s, ps, pn: (0, 0))],
        out_specs=pl.BlockSpec(memory_space=pl.ANY),
        scratch_shapes=[pltpu.VMEM((moe_tm * ROW_TILE, LANES), BF16),
                        pltpu.SemaphoreType.DMA(()), pltpu.SemaphoreType.DMA(())],
    )
    return pl.pallas_call(
        functools.partial(_dispatch_kernel, tm=tm, moe_tm=moe_tm, n_tiles=n_tiles),
        out_shape=jax.ShapeDtypeStruct((n_tiles * moe_tm * ROW_TILE, LANES), BF16),
        grid_spec=grid_spec,
        compiler_params=_cparams("arbitrary"),
        name="moe_dispatch",
    )(slot, pad_start, pad_len, hn_tiles, hs_tiles)


def _moe_kernel(texp_ref, valid_ref, xs_ref, wg_ref, wu_ref, wd_ref, ys_ref, wg_bf, wu_bf, wd_bf):
    i = pl.program_id(0)
    tm = xs_ref.shape[0] // ROW_TILE
    valid = valid_ref[i]
    new_expert = jnp.logical_or(i == 0, texp_ref[i] != texp_ref[jnp.maximum(i - 1, 0)])

    @pl.when(jnp.logical_and(valid > 0, new_expert))
    def _():
        wg_bf[...] = wg_ref[...].astype(BF16)
        wu_bf[...] = wu_ref[...].astype(BF16)
        wd_bf[...] = wd_ref[...].astype(BF16)

    @pl.when(valid > 0)
    def _():
        x = jnp.concatenate(_load_token_tile_chunks(xs_ref, (), tm), axis=1)
        gate = jnp.dot(x, wg_bf[...], preferred_element_type=F32)
        up = jnp.dot(x, wu_bf[...], preferred_element_type=F32)
        hid = (gate * _sigmoid(gate) * up).astype(BF16)
        _store_token_tiles(ys_ref, jnp.dot(hid, wd_bf[...], preferred_element_type=F32))

    @pl.when(valid == 0)
    def _():
        ys_ref[...] = jnp.zeros_like(ys_ref)


def _moe(xs, wg, wu, wd, tile_expert, tile_valid, tm):
    n_tiles = tile_expert.shape[0]
    grid_spec = pltpu.PrefetchScalarGridSpec(
        num_scalar_prefetch=2,
        grid=(n_tiles,),
        in_specs=[
            pl.BlockSpec((tm * ROW_TILE, LANES), lambda i, te, tv: (i, 0)),
            pl.BlockSpec((None, D_MODEL, D_EXPERT), lambda i, te, tv: (te[i], 0, 0)),
            pl.BlockSpec((None, D_MODEL, D_EXPERT), lambda i, te, tv: (te[i], 0, 0)),
            pl.BlockSpec((None, D_EXPERT, D_MODEL), lambda i, te, tv: (te[i], 0, 0)),
        ],
        out_specs=pl.BlockSpec((tm * ROW_TILE, LANES), lambda i, te, tv: (i, 0)),
        scratch_shapes=[pltpu.VMEM((D_MODEL, D_EXPERT), BF16), pltpu.VMEM((D_MODEL, D_EXPERT), BF16),
                        pltpu.VMEM((D_EXPERT, D_MODEL), BF16)],
    )
    return pl.pallas_call(
        _moe_kernel,
        out_shape=jax.ShapeDtypeStruct((n_tiles * tm * ROW_TILE, LANES), BF16),
        grid_spec=grid_spec,
        compiler_params=_cparams("arbitrary"),
        name="moe_experts",
    )(tile_expert, tile_valid, xs, wg, wu, wd)


def _combine_kernel(slot_ref, h_ref, rw_ref, g_ref, ys_hbm, y_ref, ybuf, sem, *, tm):
    i = pl.program_id(0)
    n_steps = pl.num_programs(0)

    def issue(tile, buf):
        def body(r, carry):
            for k in range(TOP_K):
                slot = slot_ref[(tile * tm + r) * TOP_K + k]
                pltpu.make_async_copy(ys_hbm.at[pl.ds(slot * ROW_TILE, ROW_TILE)],
                                      ybuf.at[buf, k, pl.ds(r * ROW_TILE, ROW_TILE)], sem.at[buf]).start(priority=k % 2)
            return carry

        lax.fori_loop(0, tm, body, 0, unroll=8)

    @pl.when(i == 0)
    def _():
        issue(0, 0)

    @pl.when(i + 1 < n_steps)
    def _():
        issue(i + 1, (i + 1) % 2)

    cur = i % 2
    for k in range(TOP_K):
        pltpu.make_async_copy(ybuf.at[cur, k], ybuf.at[cur, k], sem.at[cur]).wait()
    w0, w1 = rw_ref[:, 0:1], rw_ref[:, 1:2]
    e0 = _load_token_tile_chunks(ybuf, (cur, 0), tm)
    e1 = _load_token_tile_chunks(ybuf, (cur, 1), tm)
    chunks, ssq = [], 0.0
    for s in range(ROW_TILE):
        y = h_ref[:, s * LANES:(s + 1) * LANES] + w0 * e0[s].astype(F32) + w1 * e1[s].astype(F32)
        chunks.append(y)
        ssq = ssq + jnp.sum(y * y, axis=-1, keepdims=True)
    scale = lax.rsqrt(ssq * (1.0 / D_MODEL) + RMS_EPS)
    for s, y in enumerate(chunks):
        cols = slice(s * LANES, (s + 1) * LANES)
        y_ref[:, cols] = y * scale * g_ref[:, cols]


def _combine(h, route_w, norm_g, ys, slot, tm):
    m = h.shape[0]
    grid_spec = pltpu.PrefetchScalarGridSpec(
        num_scalar_prefetch=1,
        grid=(m // tm,),
        in_specs=[
            pl.BlockSpec((tm, D_MODEL), lambda i, s: (i, 0)),
            pl.BlockSpec((tm, LANES), lambda i, s: (i, 0)),
            pl.BlockSpec((1, D_MODEL), lambda i, s: (0, 0)),
            pl.BlockSpec(memory_space=pl.ANY),
        ],
        out_specs=pl.BlockSpec((tm, D_MODEL), lambda i, s: (i, 0)),
        scratch_shapes=[pltpu.VMEM((2, TOP_K, tm * ROW_TILE, LANES), BF16), pltpu.SemaphoreType.DMA((2,))],
    )
    return pl.pallas_call(
        functools.partial(_combine_kernel, tm=tm),
        out_shape=jax.ShapeDtypeStruct((m, D_MODEL), F32),
        grid_spec=grid_spec,
        compiler_params=_cparams("arbitrary"),
        name="moe_combine",
    )(slot, h, route_w, norm_g, ys)


def _routing_tables(route_i, m, tm):
    n = m * TOP_K
    n_tiles = pl.cdiv(n, tm) + N_EXPERTS - 1
    e_flat = route_i[:, :TOP_K].reshape(n)
    onehot = (e_flat[:, None] == jnp.arange(N_EXPERTS, dtype=jnp.int32)[None, :]).astype(jnp.int32)
    csum = jnp.cumsum(onehot, axis=0)
    rank = jnp.take_along_axis(csum, e_flat[:, None], axis=1)[:, 0] - 1
    counts = csum[-1]
    tiles_e = (counts + tm - 1) // tm
    tile_end = jnp.cumsum(tiles_e)
    tile_start = tile_end - tiles_e
    slot = (tile_start[e_flat] * tm + rank).astype(jnp.int32)
    tile_ids = jnp.arange(n_tiles, dtype=jnp.int32)
    tile_expert = jnp.sum((tile_end[None, :] <= tile_ids[:, None]).astype(jnp.int32), axis=1)
    tile_expert = jnp.minimum(tile_expert, N_EXPERTS - 1)
    tile_valid = jnp.clip(counts[tile_expert] - (tile_ids - tile_start[tile_expert]) * tm, 0, tm)
    tile_valid = jnp.where(tile_ids < tile_end[-1], tile_valid, 0).astype(jnp.int32)
    pad_start = jnp.concatenate([tile_start * tm + counts, tile_end[-1:]]).astype(jnp.int32)
    pad_len = (tiles_e * tm - counts).astype(jnp.int32)
    return slot, tile_expert.astype(jnp.int32), tile_valid, pad_start, pad_len


def _moe_ffn(prompt, sample, wts):
    (h_p, hn_p, ri_p, rw_p), (h_s, hn_s, ri_s, rw_s) = prompt, sample
    m_p, m_s = h_p.shape[0], h_s.shape[0]
    route_i = jnp.concatenate([ri_p, ri_s], axis=0)
    slot, tile_expert, tile_valid, pad_start, pad_len = _routing_tables(route_i, m_p + m_s, MOE_TM)
    xs = _dispatch(hn_p, hn_s, slot, pad_start, pad_len, tile_expert.shape[0], MOE_TM, tm=512)
    ys = _moe(xs, wts["wg"], wts["wu"], wts["wd"], tile_expert, tile_valid, MOE_TM)
    y_p = _combine(h_p, rw_p, wts["norm_final"], ys, slot[:m_p * TOP_K], tm=512)
    y_s = _combine(h_s, rw_s, wts["norm_final"], ys, slot[m_p * TOP_K:], tm=m_s)
    return y_p, y_s


def _kv_window_kernel(k0_ref, v0_ref, k1_ref, v1_ref, k2_ref, v2_ref, pt_ref, out0_ref, out1_ref, out2_ref,
                      *, first_step):
    i = pl.program_id(1)

    def emit(out_ref, refs, group, rows_in, rows_out):
        for part, ref in enumerate(refs):
            if group > 0:
                val = jnp.dot(pt_ref[group - 1], ref[...], preferred_element_type=F32)
            else:
                val = ref[...].astype(F32)
            for h in range(HEADS_PER_GROUP):
                out_ref[rows_out, part, h, :] = val[rows_in, h * HEAD_DIM:(h + 1) * HEAD_DIM]

    whole = slice(0, PERM_BLOCK)
    emit(out2_ref, (k2_ref, v2_ref), 2, whole, whole)

    @pl.when(i >= first_step[1])
    def _():
        emit(out1_ref, (k1_ref, v1_ref), 1, whole, whole)

    @pl.when(i >= first_step[0])
    def _():
        keep = out0_ref.shape[0]
        emit(out0_ref, (k0_ref, v0_ref), 0, slice(PERM_BLOCK - keep, PERM_BLOCK), slice(0, keep))


def _kv_windows(z_b, batch, seq, perms_t):
    wins = [min(win, seq) for win, _ in DILATED_GROUPS]
    assert wins[2] % PERM_BLOCK == 0 and wins[1] % PERM_BLOCK == 0 and wins[0] <= PERM_BLOCK and seq % PERM_BLOCK == 0
    steps = wins[2] // PERM_BLOCK
    blocks_per_seq = seq // PERM_BLOCK
    first_step = [steps - max(w // PERM_BLOCK, 1) for w in wins]

    def col_spec(group, part):
        cb = (group * QKV_COLS + (1 + part) * GROUP_COLS) // GROUP_COLS

        def index(b, i):
            return (b * blocks_per_seq + blocks_per_seq - steps + jnp.maximum(i, first_step[group]), cb)

        return pl.BlockSpec((PERM_BLOCK, GROUP_COLS), index)

    def out_spec(group):
        rows = min(wins[group], PERM_BLOCK)
        return pl.BlockSpec((None, None, rows, 2, HEADS_PER_GROUP, HEAD_DIM),
                            lambda b, i: (0, b, jnp.maximum(i - first_step[group], 0), 0, 0, 0))

    in_specs = [col_spec(g, part) for g in range(N_DIL) for part in range(2)]
    in_specs.append(pl.BlockSpec(perms_t.shape, lambda b, i: (0, 0, 0)))
    return pl.pallas_call(
        functools.partial(_kv_window_kernel, first_step=tuple(first_step)),
        out_shape=[jax.ShapeDtypeStruct((1, batch, w, 2, HEADS_PER_GROUP, HEAD_DIM), F32) for w in wins],
        grid=(batch, steps),
        in_specs=in_specs,
        out_specs=[out_spec(g) for g in range(N_DIL)],
        compiler_params=_cparams("arbitrary", "arbitrary"),
        name="kv_windows",
    )(*([z_b] * 6), perms_t)


def _kv_rows(z, lead, group):
    zz = z.reshape(lead + (z.shape[-1],))
    base = group * QKV_COLS
    k = zz[..., base + GROUP_COLS: base + 2 * GROUP_COLS]
    v = zz[..., base + 2 * GROUP_COLS: base + 3 * GROUP_COLS]
    kv = jnp.stack([k, v], axis=2).astype(F32)
    return kv.reshape(kv.shape[:3] + (HEADS_PER_GROUP, HEAD_DIM))


def kernel(x_prompt, x_sample, cache_kv_w128, cache_kv_w512, cache_kv_w2048, norm_mix, w_in, ln_v_g, ln_v_b, w_s, b_s, w_a_out, w_b_out, w_o, norm_ffn, w_route_group, b_route_group, w_route_expert, b_route_expert, w_gate_e, w_up_e, w_down_e, norm_final):
    assert norm_mix.shape[0] == 1, "single-layer trunk"
    batch, seq, _ = x_prompt.shape
    bd, n_new, _ = x_sample.shape
    caches = (cache_kv_w128, cache_kv_w512, cache_kv_w2048)

    pad = LANES - N_EXPERT_GROUPS - N_EXPERTS
    wr = jnp.concatenate([w_route_group[0], w_route_expert[0], jnp.zeros((D_MODEL, pad), F32)], axis=1)
    br = jnp.concatenate([b_route_group[0], b_route_expert[0], jnp.zeros((pad,), F32)])[None, :]
    wts = dict(
        wa=w_a_out[0].astype(BF16), wb=w_b_out[0].astype(BF16), wo=w_o[0].astype(BF16),
        wg=w_gate_e[0], wu=w_up_e[0], wd=w_down_e[0],
        wr=wr.astype(BF16), br=br, norm_ffn=norm_ffn[0][None, :], norm_final=norm_final[None, :],
    )
    w_in_b = w_in[0].astype(BF16)
    norm_g = norm_mix[0][None, :]
    ln_g, ln_b = ln_v_g[0][None, :], ln_v_b[0][None, :]

    xp = x_prompt.reshape(batch * seq, D_MODEL)
    perms = jnp.stack([_class_major_perm(dil) for _, dil in DILATED_GROUPS[1:]])
    zp_a, zp_b = _inproj_all(xp, norm_g, w_in_b, tm=1024, perms=perms)
    (sgp,) = _sgu(zp_a, ln_g, ln_b, w_s[0], b_s[0].T, chunks=8, emit_vn=False)
    os_p, lses_p = zip(*[_attn_group(zp_b, batch, seq, g) for g in range(N_DIL)])
    perms_t = perms.transpose(0, 2, 1)
    mixed_p = _mixer_out(xp, zp_a, sgp, os_p, lses_p, wts, tm=256, perms_t=perms_t)
    kv_prompt = _kv_windows(zp_b, batch, seq, perms_t)

    ms = bd * n_new
    assert ms == CHUNK and n_new <= CHUNK
    xs = x_sample.reshape(ms, D_MODEL)
    zs_a, zs_b = _inproj_all(xs, norm_g, w_in_b, tm=ms)
    eye = jnp.eye(bd, dtype=F32)
    ws_s = jnp.einsum("ab,gts->gatbs", eye, w_s[0][:, :n_new, :n_new]).reshape(A_GROUPS, ms, ms)
    bst_s = jnp.tile(b_s[0][:, :n_new].T, (bd, 1))
    sgs, vns = _sgu(zs_a, ln_g, ln_b, ws_s, bst_s, chunks=1, emit_vn=True)
    zs5 = zs_b.astype(F32).reshape(bd, n_new, N_DIL, 3, HEADS_PER_GROUP, HEAD_DIM)
    qkv = [zs5[:, :, :, part].reshape(bd, n_new, N_DIL * HEADS_PER_GROUP, HEAD_DIM) for part in range(3)]
    o_s, lse_s = _attn_sample(*qkv, caches)
    os_s = [o_s[g].reshape(ms, GROUP_COLS).astype(BF16) for g in range(N_DIL)]
    lses_s = [lse_s[g, ..., 0].reshape(ms, HEADS_PER_GROUP) for g in range(N_DIL)]
    mixed_s = _mixer_out(xs, zs_a, sgs, os_s, lses_s, wts, tm=ms)
    y_prompt, y_sample = _moe_ffn(mixed_p, mixed_s, wts)
    y_prompt = y_prompt.reshape(batch, seq, D_MODEL)
    y_sample = y_sample.reshape(bd, n_new, D_MODEL)
    kv_sample = [_kv_rows(zs_b, (bd, n_new), g)[None] for g in range(N_DIL)]
    chunk_v = vns.reshape(1, bd, n_new, A_WIDTH)

    return (y_prompt, y_sample, *kv_prompt, *kv_sample, chunk_v)
```

```python
import functools

import jax
import jax.numpy as jnp
from jax import lax
from jax.experimental import pallas as pl
from jax.experimental.pallas import tpu as pltpu

F32 = jnp.float32
BF16 = jnp.bfloat16

D_MODEL = 2048
CHUNK = 128
A_GROUPS = 16
A_GROUP_DIM = 128
A_WIDTH = A_GROUPS * A_GROUP_DIM
HEAD_DIM = 128
HEADS_PER_GROUP = 4
GROUP_COLS = HEADS_PER_GROUP * HEAD_DIM
DILATED_GROUPS = ((128, 1), (512, 4), (2048, 16))
N_DIL = len(DILATED_GROUPS)
B_WIDTH = N_DIL * GROUP_COLS
COL_U, COL_V = 0, A_WIDTH
COL_GA = 2 * A_WIDTH
COL_GB = COL_GA + D_MODEL
QKV_COLS = 3 * GROUP_COLS
N_EXPERT_GROUPS = 4
EXPERTS_PER_GROUP = 8
N_EXPERTS = N_EXPERT_GROUPS * EXPERTS_PER_GROUP
TOP_K = 2
D_EXPERT = 256
RMS_EPS = 1e-6
LN_EPS = 1e-5
NEG = -1e30
ATTN_SCALE = HEAD_DIM ** -0.5

LANES = 128
VMEM_LIMIT_BYTES = 56 * 1024 * 1024
IN_TN = 512
MOE_TM = 512
PERM_BLOCK = 256
ATTN_UNITS_PER_BATCH = 8
ROW_TILE = D_MODEL // LANES


def _cparams(*sem):
    return pltpu.CompilerParams(dimension_semantics=sem, vmem_limit_bytes=VMEM_LIMIT_BYTES)


def _gelu_tanh(x):
    return 0.5 * x * (1.0 + jnp.tanh(0.7978845608028654 * (x + 0.044715 * (x * x * x))))


def _sigmoid(x):
    return 1.0 / (1.0 + jnp.exp(-x))


def _store_token_tiles(ref, val):
    rows = val.shape[0]
    chunks = jnp.stack([val[:, s * LANES:(s + 1) * LANES].astype(BF16) for s in range(ROW_TILE)], axis=0)
    ref[...] = pltpu.einshape("stl->tsl", chunks).reshape(rows * ROW_TILE, LANES)


def _load_token_tile_chunks(ref, lead, rows):
    tiles = ref[lead + (slice(None), slice(None))].reshape(rows, ROW_TILE, LANES)
    by_chunk = pltpu.einshape("tsl->stl", tiles)
    return [by_chunk[s] for s in range(ROW_TILE)]


def _class_major_perm(dil):
    out_row = jnp.arange(PERM_BLOCK, dtype=jnp.int32)
    src = (out_row % (PERM_BLOCK // dil)) * dil + out_row // (PERM_BLOCK // dil)
    return (src[:, None] == jnp.arange(PERM_BLOCK, dtype=jnp.int32)[None, :]).astype(BF16)


def _inproj_kernel(x_ref, g_ref, *rest, strips, plan, permute):
    w_refs, rest = rest[:strips], rest[strips:]
    if permute:
        perm_ref, z_ref, xn_ref = rest
    else:
        z_ref, xn_ref = rest
    j = pl.program_id(1)

    @pl.when(j == 0)
    def _():
        x = x_ref[...]
        ms = jnp.mean(x * x, axis=-1, keepdims=True)
        xn_ref[...] = (x * lax.rsqrt(ms + RMS_EPS) * g_ref[...]).astype(BF16)

    tm = xn_ref.shape[0]
    sub = min(tm, PERM_BLOCK)

    def emit(epilogue):
        for blk in range(tm // sub):
            rows = slice(blk * sub, (blk + 1) * sub)
            for s, w_ref in enumerate(w_refs):
                z = jnp.dot(xn_ref[rows, :], w_ref[...], preferred_element_type=F32)
                z_ref[rows, s * IN_TN:(s + 1) * IN_TN] = epilogue(z).astype(BF16)

    def epilogue_of(kind):
        if kind == "gelu":
            return _gelu_tanh
        if kind == "gate":
            return _sigmoid
        if kind == "plain":
            return lambda z: z
        k = int(kind[len("perm"):])
        return lambda z: jnp.dot(perm_ref[k], z.astype(BF16), preferred_element_type=F32)

    for kind in sorted(set(plan)):
        tiles = [t for t, p in enumerate(plan) if p == kind]
        hit = functools.reduce(jnp.logical_or, [j == t for t in tiles])

        @pl.when(hit)
        def _():
            emit(epilogue_of(kind))


def _inproj(x, norm_g, w_in_bf16, tm, strips, plan, ref_tile, perms=None):
    m = x.shape[0]
    tn = strips * IN_TN
    in_specs = [pl.BlockSpec((tm, D_MODEL), lambda i, j: (i, 0)), pl.BlockSpec((1, D_MODEL), lambda i, j: (0, 0))]
    in_specs += [pl.BlockSpec((D_MODEL, IN_TN), functools.partial(lambda i, j, s: (0, ref_tile(j, s)), s=s))
                 for s in range(strips)]
    args = [x, norm_g] + [w_in_bf16] * strips
    if perms is not None:
        in_specs.append(pl.BlockSpec(perms.shape, lambda i, j: (0, 0, 0)))
        args.append(perms)
    return pl.pallas_call(
        functools.partial(_inproj_kernel, strips=strips, plan=plan, permute=perms is not None),
        out_shape=jax.ShapeDtypeStruct((m, len(plan) * tn), BF16),
        grid=(m // tm, len(plan)),
        in_specs=in_specs,
        out_specs=pl.BlockSpec((tm, tn), lambda i, j: (i, j)),
        scratch_shapes=[pltpu.VMEM((tm, D_MODEL), BF16)],
        compiler_params=_cparams("parallel", "arbitrary"),
        name="inproj",
    )(*args)


def _inproj_all(x, norm_g, w_in_bf16, tm, perms=None):
    n_gelu = 2 * A_WIDTH // IN_TN
    n_qkv = 3 * B_WIDTH // IN_TN
    strips_a = 2
    plan_a = ("gelu",) * (n_gelu // strips_a) + ("gate",) * (2 * D_MODEL // IN_TN // strips_a)

    def ref_a(j, s):
        t = j * strips_a + s
        return jnp.where(t < n_gelu, t, t + n_qkv)

    z_a = _inproj(x, norm_g, w_in_bf16, tm, strips_a, plan_a, ref_a)
    plan_b = ("plain",) + tuple(f"perm{k}" for k in range(N_DIL - 1)) if perms is not None else ("plain",) * N_DIL
    z_b = _inproj(x, norm_g, w_in_bf16, tm, 3, plan_b, lambda j, s: n_gelu + s * N_DIL + j, perms)
    return z_a, z_b


def _sgu_kernel(u_ref, v_ref, lng_ref, lnb_ref, ws_ref, bst_ref, sg_ref, *vn_out, chunks):
    row = lax.broadcasted_iota(jnp.int32, (CHUNK, CHUNK), 0)
    col = lax.broadcasted_iota(jnp.int32, (CHUNK, CHUNK), 1)
    tri = row >= col
    ws = [jnp.where(tri, ws_ref[g], 0.0).astype(BF16) for g in range(A_GROUPS)]

    def normed(c):
        rows = slice(c * CHUNK, (c + 1) * CHUNK)
        v = v_ref[rows, :].astype(F32)
        mu = jnp.mean(v, axis=-1, keepdims=True)
        vc = v - mu
        var = jnp.mean(vc * vc, axis=-1, keepdims=True)
        vn = vc * lax.rsqrt(var + LN_EPS) * lng_ref[...] + lnb_ref[...]
        if vn_out:
            vn_out[0][rows, :] = vn
        return vn.astype(BF16)

    for c0 in range(0, chunks, 2):
        pair = list(range(c0, min(c0 + 2, chunks)))
        vnb = [normed(c) for c in pair]
        for g in range(A_GROUPS):
            cols = slice(g * A_GROUP_DIM, (g + 1) * A_GROUP_DIM)
            rhs = vnb[0][:, cols] if len(pair) == 1 else jnp.concatenate([b[:, cols] for b in vnb], axis=1)
            s = jnp.dot(ws[g], rhs, preferred_element_type=F32) + bst_ref[:, g:g + 1]
            for k, c in enumerate(pair):
                rows = slice(c * CHUNK, (c + 1) * CHUNK)
                s_c = s[:, k * A_GROUP_DIM:(k + 1) * A_GROUP_DIM]
                sg_ref[rows, cols] = (u_ref[rows, cols].astype(F32) * s_c).astype(BF16)


def _sgu(z, ln_g, ln_b, ws, bst, chunks, emit_vn):
    m = z.shape[0]
    tm = chunks * CHUNK
    out_shape = [jax.ShapeDtypeStruct((m, A_WIDTH), BF16)]
    out_specs = [pl.BlockSpec((tm, A_WIDTH), lambda i: (i, 0))]
    if emit_vn:
        out_shape.append(jax.ShapeDtypeStruct((m, A_WIDTH), F32))
        out_specs.append(pl.BlockSpec((tm, A_WIDTH), lambda i: (i, 0)))
    return pl.pallas_call(
        functools.partial(_sgu_kernel, chunks=chunks),
        out_shape=out_shape,
        grid=(m // tm,),
        in_specs=[
            pl.BlockSpec((tm, A_WIDTH), lambda i: (i, COL_U // A_WIDTH)),
            pl.BlockSpec((tm, A_WIDTH), lambda i: (i, COL_V // A_WIDTH)),
            pl.BlockSpec((1, A_WIDTH), lambda i: (0, 0)),
            pl.BlockSpec((1, A_WIDTH), lambda i: (0, 0)),
            pl.BlockSpec((A_GROUPS, CHUNK, CHUNK), lambda i: (0, 0, 0)),
            pl.BlockSpec((CHUNK, A_GROUPS), lambda i: (0, 0)),
        ],
        out_specs=out_specs,
        compiler_params=_cparams("parallel"),
        name="sgu",
    )(z, z, ln_g, ln_b, ws, bst)


def _attn_kernel(q_ref, kc_ref, kp_ref, vc_ref, vp_ref, o_ref, lse_ref, *, qblocks):
    i = pl.program_id(2)
    qi = lax.broadcasted_iota(jnp.int32, (CHUNK, CHUNK), 0)
    ki = lax.broadcasted_iota(jnp.int32, (CHUNK, CHUNK), 1)
    cur_mask = ki <= qi
    no_prev = jnp.where(i == 0, CHUNK, 0)
    dn = (((1,), (1,)), ((), ()))
    rpc = q_ref.shape[1]
    nbq = CHUNK // rpc

    def load(ref, a, cols):
        return ref[a * nbq:(a + 1) * nbq, :, cols].reshape(CHUNK, HEAD_DIM)

    units = [(a, h) for a in range(qblocks) for h in range(HEADS_PER_GROUP)]
    for start in range(0, len(units), ATTN_UNITS_PER_BATCH):
        batch_units = units[start:start + ATTN_UNITS_PER_BATCH]
        scores = []
        for a, h in batch_units:
            cols = slice(h * HEAD_DIM, (h + 1) * HEAD_DIM)
            q = load(q_ref, a, cols)
            if a == 0:
                kp, prev_mask = load(kp_ref, 0, cols), ki >= qi + no_prev
            else:
                kp, prev_mask = load(kc_ref, a - 1, cols), ki >= qi
            s_c = lax.dot_general(q, load(kc_ref, a, cols), dn, preferred_element_type=F32) * ATTN_SCALE
            s_p = lax.dot_general(q, kp, dn, preferred_element_type=F32) * ATTN_SCALE
            scores.append((jnp.where(cur_mask, s_c, NEG), jnp.where(prev_mask, s_p, NEG)))
        probs = []
        for s_c, s_p in scores:
            mx = jnp.maximum(jnp.max(s_c, axis=-1, keepdims=True), jnp.max(s_p, axis=-1, keepdims=True))
            p_c = jnp.exp(s_c - mx)
            p_p = jnp.exp(s_p - mx)
            l = jnp.sum(p_c, axis=-1, keepdims=True) + jnp.sum(p_p, axis=-1, keepdims=True)
            probs.append((p_c.astype(BF16), p_p.astype(BF16), mx, l))
        for (a, h), (p_c, p_p, mx, l) in zip(batch_units, probs):
            cols = slice(h * HEAD_DIM, (h + 1) * HEAD_DIM)
            vp = load(vp_ref, 0, cols) if a == 0 else load(vc_ref, a - 1, cols)
            acc = jnp.dot(p_c, load(vc_ref, a, cols), preferred_element_type=F32)
            acc = acc + jnp.dot(p_p, vp, preferred_element_type=F32)
            o_ref[a * nbq:(a + 1) * nbq, :, cols] = (acc / l).astype(BF16).reshape(nbq, rpc, HEAD_DIM)
            lse_ref[a * CHUNK:(a + 1) * CHUNK, h:h + 1] = mx + jnp.log(l)


def _rows_per_class(dil):
    return min(PERM_BLOCK // dil, CHUNK)


def _attn_group(z, batch, seq, group):
    _, dil = DILATED_GROUPS[group]
    sub = seq // dil
    rpc = _rows_per_class(dil)
    qblocks = max(b for b in (4, 2, 1) if sub % (b * CHUNK) == 0)
    tq = qblocks * CHUNK
    cq, ck, cv = (group * QKV_COLS // GROUP_COLS + part for part in range(3))
    zv = z.reshape(batch, sub // rpc, dil, rpc, z.shape[-1])

    def cur(cb):
        return pl.BlockSpec((None, tq // rpc, None, rpc, GROUP_COLS), lambda b, r, i: (b, i, r, 0, cb))

    def prev(cb):
        return pl.BlockSpec((None, CHUNK // rpc, None, rpc, GROUP_COLS),
                            lambda b, r, i: (b, jnp.maximum(i * qblocks - 1, 0), r, 0, cb))

    o, lse = pl.pallas_call(
        functools.partial(_attn_kernel, qblocks=qblocks),
        out_shape=[jax.ShapeDtypeStruct((batch, sub // rpc, dil, rpc, GROUP_COLS), BF16),
                   jax.ShapeDtypeStruct((batch, dil, sub, HEADS_PER_GROUP), F32)],
        grid=(batch, dil, sub // tq),
        in_specs=[cur(cq), cur(ck), prev(ck), cur(cv), prev(cv)],
        out_specs=[pl.BlockSpec((None, tq // rpc, None, rpc, GROUP_COLS), lambda b, r, i: (b, i, r, 0, 0)),
                   pl.BlockSpec((None, None, tq, HEADS_PER_GROUP), lambda b, r, i: (b, r, i, 0))],
        compiler_params=_cparams("parallel", "parallel", "arbitrary"),
        name=f"attn_prompt_g{group}",
    )(zv, zv, zv, zv, zv)
    lse = lse.transpose(0, 2, 1, 3).reshape(batch * seq, HEADS_PER_GROUP)
    return o, lse


def _attn_sample_kernel(q_ref, k_ref, v_ref, c0_ref, c1_ref, c2_ref, o_ref, lse_ref, *, n_new):
    caches = (c0_ref, c1_ref, c2_ref)
    rowid = lax.broadcasted_iota(jnp.int32, (CHUNK, HEADS_PER_GROUP, 1), 0)
    for g, (win, dil) in enumerate(DILATED_GROUPS):
        cref = caches[g]
        hs = slice(g * HEADS_PER_GROUP, (g + 1) * HEADS_PER_GROUP)
        for t in range(n_new):
            res, first = t % dil, t // dil
            kc = cref[:, res, 0]
            vc = cref[:, res, 1]
            q = q_ref[t, hs, :]
            s = jnp.sum(kc * q[None], axis=-1, keepdims=True) * ATTN_SCALE
            if first > 0:
                s = jnp.where(rowid >= first, s, NEG)
            new_rows = [tn for tn in range(t + 1) if (t - tn) % dil == 0 and (t - tn) // dil <= win // dil]
            s_new = [jnp.sum(q * k_ref[tn, hs, :], axis=-1, keepdims=True) * ATTN_SCALE for tn in new_rows]
            mx = jnp.max(s, axis=0)
            for sn in s_new:
                mx = jnp.maximum(mx, sn)
            p = jnp.exp(s - mx[None])
            l = jnp.sum(p, axis=0)
            acc = jnp.sum(p * vc, axis=0)
            for tn, sn in zip(new_rows, s_new):
                pn = jnp.exp(sn - mx)
                l = l + pn
                acc = acc + pn * v_ref[tn, hs, :]
            o_ref[g, t] = acc / l
            lse_ref[g, t] = jnp.broadcast_to(mx + jnp.log(l), (HEADS_PER_GROUP, HEAD_DIM))


def _attn_sample(q, k, v, caches):
    bd, n_new = q.shape[:2]
    views, specs = [], []
    for (win, dil), c in zip(DILATED_GROUPS, caches):
        assert c.shape[2] == win and win == CHUNK * dil, "cache must hold exactly one full window"
        used = min(dil, n_new)
        views.append(c.reshape(1, bd, CHUNK, dil, 2, HEADS_PER_GROUP, HEAD_DIM))
        specs.append(pl.BlockSpec((None, None, CHUNK, used, 2, HEADS_PER_GROUP, HEAD_DIM),
                                  lambda b: (0, b, 0, 0, 0, 0, 0)))
    new_spec = pl.BlockSpec((None, n_new, N_DIL * HEADS_PER_GROUP, HEAD_DIM), lambda b: (b, 0, 0, 0))
    out_sds = jax.ShapeDtypeStruct((N_DIL, bd, n_new, HEADS_PER_GROUP, HEAD_DIM), F32)
    out_spec = pl.BlockSpec((N_DIL, None, n_new, HEADS_PER_GROUP, HEAD_DIM), lambda b: (0, b, 0, 0, 0))
    return pl.pallas_call(
        functools.partial(_attn_sample_kernel, n_new=n_new),
        out_shape=[out_sds, out_sds],
        grid=(bd,),
        in_specs=[new_spec, new_spec, new_spec] + specs,
        out_specs=[out_spec, out_spec],
        compiler_params=_cparams("parallel"),
        name="attn_sample",
    )(q, k, v, *views)


def _mixer_out_kernel(sg_ref, o0_ref, o1_ref, o2_ref, l0_ref, l1_ref, l2_ref, ga_ref, gb_ref,
                      wa_ref, wb_ref, x_ref, wo_ref, g_ref, wr_ref, br_ref, *rest, permuted):
    if permuted:
        pt_ref, h_ref, hn_ref, ri_ref, rw_ref, ob_ref = rest
    else:
        h_ref, hn_ref, ri_ref, rw_ref, ob_ref = rest

    l0, l1, l2 = l0_ref[...], l1_ref[...], l2_ref[...]
    mx = jnp.maximum(jnp.maximum(l0, l1), l2)
    e0, e1, e2 = jnp.exp(l0 - mx), jnp.exp(l1 - mx), jnp.exp(l2 - mx)
    inv = 1.0 / (e0 + e1 + e2)
    ws = (e0 * inv, e1 * inv, e2 * inv)
    o_refs = (o0_ref, o1_ref, o2_ref)
    tm = ob_ref.shape[0]

    def o_rows(ref, rows):
        if len(ref.shape) == 2:
            return ref[rows, :]
        per = ref.shape[1] * ref.shape[2]
        return ref[rows.start // per:rows.stop // per].reshape(rows.stop - rows.start, GROUP_COLS)

    for blk in range(max(tm // PERM_BLOCK, 1)):
        rows = slice(blk * PERM_BLOCK, min((blk + 1) * PERM_BLOCK, tm))
        og = []
        for g in range(N_DIL):
            if permuted and g > 0:
                og.append(jnp.dot(pt_ref[g - 1], o_rows(o_refs[g], rows), preferred_element_type=F32))
            else:
                og.append(o_rows(o_refs[g], rows).astype(F32))
        for h in range(HEADS_PER_GROUP):
            cols = slice(h * HEAD_DIM, (h + 1) * HEAD_DIM)
            ob = sum(ws[g][rows, h:h + 1] * og[g][:, cols] for g in range(N_DIL))
            ob_ref[rows, cols] = ob.astype(BF16)

    ya = jnp.dot(sg_ref[...], wa_ref[...], preferred_element_type=F32)
    yb = jnp.dot(ob_ref[...], wb_ref[...], preferred_element_type=F32)
    m = (ga_ref[...].astype(F32) * ya + gb_ref[...].astype(F32) * yb).astype(BF16)
    _oproj_rows(m, x_ref, wo_ref, g_ref, wr_ref, br_ref, h_ref, hn_ref, ri_ref, rw_ref)


def _mixer_out(x, z, sg, os_, lses, wts, tm, perms_t=None):
    m = z.shape[0]
    row = lambda width: pl.BlockSpec((tm, width), lambda i: (i, 0))
    whole = lambda a: pl.BlockSpec(a.shape, lambda i: (0,) * a.ndim, pipeline_mode=pl.Buffered(1))

    def o_spec(o):
        if o.ndim == 2:
            return row(GROUP_COLS)
        _, blocks, dil, rpc, _ = o.shape
        per_tile, tiles = tm // (dil * rpc), blocks * dil * rpc // tm
        return pl.BlockSpec((None, per_tile, dil, rpc, GROUP_COLS), lambda i: (i // tiles, i % tiles, 0, 0, 0))

    consts = [wts["wa"], wts["wb"]]
    tail_consts = [wts["wo"], wts["norm_ffn"], wts["wr"], wts["br"]]
    in_specs = [row(A_WIDTH)] + [o_spec(o) for o in os_] + [row(HEADS_PER_GROUP)] * 3 + [
        pl.BlockSpec((tm, D_MODEL), lambda i: (i, COL_GA // D_MODEL)),
        pl.BlockSpec((tm, D_MODEL), lambda i: (i, COL_GB // D_MODEL)),
    ] + [whole(a) for a in consts] + [row(D_MODEL)] + [whole(a) for a in tail_consts]
    args = [sg, *os_, *lses, z, z, *consts, x, *tail_consts]
    if perms_t is not None:
        assert tm % PERM_BLOCK == 0
        in_specs.append(whole(perms_t))
        args.append(perms_t)
    return pl.pallas_call(
        functools.partial(_mixer_out_kernel, permuted=perms_t is not None),
        out_shape=[jax.ShapeDtypeStruct((m, D_MODEL), F32), jax.ShapeDtypeStruct((m * ROW_TILE, LANES), BF16),
                   jax.ShapeDtypeStruct((m, LANES), jnp.int32), jax.ShapeDtypeStruct((m, LANES), F32)],
        grid=(m // tm,),
        in_specs=in_specs,
        out_specs=[row(D_MODEL), pl.BlockSpec((tm * ROW_TILE, LANES), lambda i: (i, 0)), row(LANES), row(LANES)],
        scratch_shapes=[pltpu.VMEM((tm, GROUP_COLS), BF16)],
        compiler_params=_cparams("parallel"),
        name="mixer_out",
    )(*args)


def _oproj_rows(m, x_ref, wo_ref, g_ref, wr_ref, br_ref, h_ref, hn_ref, ri_ref, rw_ref):
    h = x_ref[...] + jnp.dot(m, wo_ref[...], preferred_element_type=F32)
    h_ref[...] = h
    ms = jnp.mean(h * h, axis=-1, keepdims=True)
    hn = h * lax.rsqrt(ms + RMS_EPS) * g_ref[...]
    _store_token_tiles(hn_ref, hn)
    logits = jnp.dot(hn.astype(BF16), wr_ref[...], preferred_element_type=F32) + br_ref[...]
    lane = lax.broadcasted_iota(jnp.int32, logits.shape, 1).astype(F32)
    big = float(LANES)

    def first_argmax(vals, vmax):
        return jnp.min(jnp.where(vals == vmax, lane, big), axis=-1, keepdims=True)

    lg = jnp.where(lane < N_EXPERT_GROUPS, logits, NEG)
    gmax = jnp.max(lg, axis=-1, keepdims=True)
    gsel = first_argmax(lg, gmax)
    p_sel = 1.0 / jnp.sum(jnp.exp(lg - gmax), axis=-1, keepdims=True)
    lo = N_EXPERT_GROUPS + EXPERTS_PER_GROUP * gsel
    le = jnp.where(jnp.logical_and(lane >= lo, lane < lo + EXPERTS_PER_GROUP), logits, NEG)
    v1 = jnp.max(le, axis=-1, keepdims=True)
    i1 = first_argmax(le, v1)
    le2 = jnp.where(lane == i1, NEG, le)
    v2 = jnp.max(le2, axis=-1, keepdims=True)
    i2 = first_argmax(le2, v2)
    e2 = jnp.exp(v2 - v1)
    w1 = p_sel / (1.0 + e2)
    w2 = p_sel * e2 / (1.0 + e2)
    ri = jnp.where(lane == 0, i1 - N_EXPERT_GROUPS, jnp.where(lane == 1, i2 - N_EXPERT_GROUPS, 0.0))
    ri_ref[...] = ri.astype(jnp.int32)
    rw_ref[...] = jnp.where(lane == 0, w1, jnp.where(lane == 1, w2, 0.0))


def _dispatch_kernel(slot_ref, pad_start_ref, pad_len_ref, hn_ref, hs_ref, xs_hbm, zbuf, sem, zsem,
                     *, tm, moe_tm, n_tiles):
    i = pl.program_id(0)
    n_main = pl.num_programs(0) - 1

    def zero_copy(dst_row, rows):
        return pltpu.make_async_copy(zbuf.at[pl.ds(0, rows * ROW_TILE)],
                                     xs_hbm.at[pl.ds(dst_row * ROW_TILE, rows * ROW_TILE)], zsem)

    def zero_fill(act):
        def per_expert(e, carry):
            off, length = pad_start_ref[e], pad_len_ref[e]
            rows = moe_tm // 2
            while rows >= 1:
                @pl.when((length & rows) != 0)
                def _(off=off, rows=rows):
                    act(zero_copy(off, rows))
                off = off + (length & rows)
                rows //= 2
            return carry

        lax.fori_loop(0, N_EXPERTS, per_expert, 0)

        def per_tile(t, carry):
            act(zero_copy(t * moe_tm, moe_tm))
            return carry

        lax.fori_loop(pad_start_ref[N_EXPERTS], n_tiles, per_tile, 0)

    @pl.when(i == 0)
    def _():
        zbuf[...] = jnp.zeros_like(zbuf)
        zero_fill(lambda c: c.start())

    def scatter_rows(ref, first_token):
        rows = ref.shape[0] // ROW_TILE

        def issue(r, carry):
            for k in range(TOP_K):
                slot = slot_ref[(first_token + r) * TOP_K + k]
                pltpu.make_async_copy(ref.at[pl.ds(r * ROW_TILE, ROW_TILE)],
                                      xs_hbm.at[pl.ds(slot * ROW_TILE, ROW_TILE)], sem).start(priority=k % 2)
            return carry

        lax.fori_loop(0, rows, issue, 0, unroll=8)
        for k in range(TOP_K):
            pltpu.make_async_copy(ref, ref, sem).wait()

    @pl.when(i < n_main)
    def _():
        scatter_rows(hn_ref, i * tm)

    @pl.when(i == n_main)
    def _():
        scatter_rows(hs_ref, n_main * tm)
        zero_fill(lambda c: c.wait())


def _dispatch(hn_tiles, hs_tiles, slot, pad_start, pad_len, n_tiles, moe_tm, tm):
    m = hn_tiles.shape[0] // ROW_TILE
    assert m % tm == 0 and moe_tm & (moe_tm - 1) == 0
    assert slot.shape[0] == (m + hs_tiles.shape[0] // ROW_TILE) * TOP_K
    n_main = m // tm
    grid_spec = pltpu.PrefetchScalarGridSpec(
        num_scalar_prefetch=3,
        grid=(n_main + 1,),
        in_specs=[pl.BlockSpec((tm * ROW_TILE, LANES), lambda i, s, ps, pn: (jnp.minimum(i, n_main - 1), 0)),
                  pl.BlockSpec(hs_tiles.shape, lambda i, s, ps, pn: (0, 0))],
        out_specs=pl.BlockSpec(memory_space=pl.ANY),
        scratch_shapes=[pltpu.VMEM((moe_tm * ROW_TILE, LANES), BF16),
                        pltpu.SemaphoreType.DMA(()), pltpu.SemaphoreType.DMA(())],
    )
    return pl.pallas_call(
        functools.partial(_dispatch_kernel, tm=tm, moe_tm=moe_tm, n_tiles=n_tiles),
        out_shape=jax.ShapeDtypeStruct((n_tiles * moe_tm * ROW_TILE, LANES), BF16),
        grid_spec=grid_spec,
        compiler_params=_cparams("arbitrary"),
        name="moe_dispatch",
    )(slot, pad_start, pad_len, hn_tiles, hs_tiles)


def _moe_kernel(texp_ref, valid_ref, xs_ref, wg_ref, wu_ref, wd_ref, ys_ref, wg_bf, wu_bf, wd_bf):
    i = pl.program_id(0)
    tm = xs_ref.shape[0] // ROW_TILE
    valid = valid_ref[i]
    new_expert = jnp.logical_or(i == 0, texp_ref[i] != texp_ref[jnp.maximum(i - 1, 0)])

    @pl.when(jnp.logical_and(valid > 0, new_expert))
    def _():
        wg_bf[...] = wg_ref[...].astype(BF16)
        wu_bf[...] = wu_ref[...].astype(BF16)
        wd_bf[...] = wd_ref[...].astype(BF16)

    @pl.when(valid > 0)
    def _():
        x = jnp.concatenate(_load_token_tile_chunks(xs_ref, (), tm), axis=1)
        gate = jnp.dot(x, wg_bf[...], preferred_element_type=F32)
        up = jnp.dot(x, wu_bf[...], preferred_element_type=F32)
        hid = (gate * _sigmoid(gate) * up).astype(BF16)
        _store_token_tiles(ys_ref, jnp.dot(hid, wd_bf[...], preferred_element_type=F32))

    @pl.when(valid == 0)
    def _():
        ys_ref[...] = jnp.zeros_like(ys_ref)


def _moe(xs, wg, wu, wd, tile_expert, tile_valid, tm):
    n_tiles = tile_expert.shape[0]
    grid_spec = pltpu.PrefetchScalarGridSpec(
        num_scalar_prefetch=2,
        grid=(n_tiles,),
        in_specs=[
            pl.BlockSpec((tm * ROW_TILE, LANES), lambda i, te, tv: (i, 0)),
            pl.BlockSpec((None, D_MODEL, D_EXPERT), lambda i, te, tv: (te[i], 0, 0)),
            pl.BlockSpec((None, D_MODEL, D_EXPERT), lambda i, te, tv: (te[i], 0, 0)),
            pl.BlockSpec((None, D_EXPERT, D_MODEL), lambda i, te, tv: (te[i], 0, 0)),
        ],
        out_specs=pl.BlockSpec((tm * ROW_TILE, LANES), lambda i, te, tv: (i, 0)),
        scratch_shapes=[pltpu.VMEM((D_MODEL, D_EXPERT), BF16), pltpu.VMEM((D_MODEL, D_EXPERT), BF16),
                        pltpu.VMEM((D_EXPERT, D_MODEL), BF16)],
    )
    return pl.pallas_call(
        _moe_kernel,
        out_shape=jax.ShapeDtypeStruct((n_tiles * tm * ROW_TILE, LANES), BF16),
        grid_spec=grid_spec,
        compiler_params=_cparams("arbitrary"),
        name="moe_experts",
    )(tile_expert, tile_valid, xs, wg, wu, wd)


def _combine_kernel(slot_ref, h_ref, rw_ref, g_ref, ys_hbm, y_ref, ybuf, sem, *, tm):
    i = pl.program_id(0)
    n_steps = pl.num_programs(0)

    def issue(tile, buf):
        def body(r, carry):
            for k in range(TOP_K):
                slot = slot_ref[(tile * tm + r) * TOP_K + k]
                pltpu.make_async_copy(ys_hbm.at[pl.ds(slot * ROW_TILE, ROW_TILE)],
                                      ybuf.at[buf, k, pl.ds(r * ROW_TILE, ROW_TILE)], sem.at[buf]).start(priority=k % 2)
            return carry

        lax.fori_loop(0, tm, body, 0, unroll=8)

    @pl.when(i == 0)
    def _():
        issue(0, 0)

    @pl.when(i + 1 < n_steps)
    def _():
        issue(i + 1, (i + 1) % 2)

    cur = i % 2
    for k in range(TOP_K):
        pltpu.make_async_copy(ybuf.at[cur, k], ybuf.at[cur, k], sem.at[cur]).wait()
    w0, w1 = rw_ref[:, 0:1], rw_ref[:, 1:2]
    e0 = _load_token_tile_chunks(ybuf, (cur, 0), tm)
    e1 = _load_token_tile_chunks(ybuf, (cur, 1), tm)
    chunks, ssq = [], 0.0
    for s in range(ROW_TILE):
        y = h_ref[:, s * LANES:(s + 1) * LANES] + w0 * e0[s].astype(F32) + w1 * e1[s].astype(F32)
        chunks.append(y)
        ssq = ssq + jnp.sum(y * y, axis=-1, keepdims=True)
    scale = lax.rsqrt(ssq * (1.0 / D_MODEL) + RMS_EPS)
    for s, y in enumerate(chunks):
        cols = slice(s * LANES, (s + 1) * LANES)
        y_ref[:, cols] = y * scale * g_ref[:, cols]


def _combine(h, route_w, norm_g, ys, slot, tm):
    m = h.shape[0]
    grid_spec = pltpu.PrefetchScalarGridSpec(
        num_scalar_prefetch=1,
        grid=(m // tm,),
        in_specs=[
            pl.BlockSpec((tm, D_MODEL), lambda i, s: (i, 0)),
            pl.BlockSpec((tm, LANES), lambda i, s: (i, 0)),
            pl.BlockSpec((1, D_MODEL), lambda i, s: (0, 0)),
            pl.BlockSpec(memory_space=pl.ANY),
        ],
        out_specs=pl.BlockSpec((tm, D_MODEL), lambda i, s: (i, 0)),
        scratch_shapes=[pltpu.VMEM((2, TOP_K, tm * ROW_TILE, LANES), BF16), pltpu.SemaphoreType.DMA((2,))],
    )
    return pl.pallas_call(
        functools.partial(_combine_kernel, tm=tm),
        out_shape=jax.ShapeDtypeStruct((m, D_MODEL), F32),
        grid_spec=grid_spec,
        compiler_params=_cparams("arbitrary"),
        name="moe_combine",
    )(slot, h, route_w, norm_g, ys)


def _routing_tables(route_i, m, tm):
    n = m * TOP_K
    n_tiles = pl.cdiv(n, tm) + N_EXPERTS - 1
    e_flat = route_i[:, :TOP_K].reshape(n)
    onehot = (e_flat[:, None] == jnp.arange(N_EXPERTS, dtype=jnp.int32)[None, :]).astype(jnp.int32)
    csum = jnp.cumsum(onehot, axis=0)
    rank = jnp.take_along_axis(csum, e_flat[:, None], axis=1)[:, 0] - 1
    counts = csum[-1]
    tiles_e = (counts + tm - 1) // tm
    tile_end = jnp.cumsum(tiles_e)
    tile_start = tile_end - tiles_e
    slot = (tile_start[e_flat] * tm + rank).astype(jnp.int32)
    tile_ids = jnp.arange(n_tiles, dtype=jnp.int32)
    tile_expert = jnp.sum((tile_end[None, :] <= tile_ids[:, None]).astype(jnp.int32), axis=1)
    tile_expert = jnp.minimum(tile_expert, N_EXPERTS - 1)
    tile_valid = jnp.clip(counts[tile_expert] - (tile_ids - tile_start[tile_expert]) * tm, 0, tm)
    tile_valid = jnp.where(tile_ids < tile_end[-1], tile_valid, 0).astype(jnp.int32)
    pad_start = jnp.concatenate([tile_start * tm + counts, tile_end[-1:]]).astype(jnp.int32)
    pad_len = (tiles_e * tm - counts).astype(jnp.int32)
    return slot, tile_expert.astype(jnp.int32), tile_valid, pad_start, pad_len


def _moe_ffn(prompt, sample, wts):
    (h_p, hn_p, ri_p, rw_p), (h_s, hn_s, ri_s, rw_s) = prompt, sample
    m_p, m_s = h_p.shape[0], h_s.shape[0]
    route_i = jnp.concatenate([ri_p, ri_s], axis=0)
    slot, tile_expert, tile_valid, pad_start, pad_len = _routing_tables(route_i, m_p + m_s, MOE_TM)
    xs = _dispatch(hn_p, hn_s, slot, pad_start, pad_len, tile_expert.shape[0], MOE_TM, tm=512)
    ys = _moe(xs, wts["wg"], wts["wu"], wts["wd"], tile_expert, tile_valid, MOE_TM)
    y_p = _combine(h_p, rw_p, wts["norm_final"], ys, slot[:m_p * TOP_K], tm=512)
    y_s = _combine(h_s, rw_s, wts["norm_final"], ys, slot[m_p * TOP_K:], tm=m_s)
    return y_p, y_s


def _kv_window_kernel(k0_ref, v0_ref, k1_ref, v1_ref, k2_ref, v2_ref, pt_ref, out0_ref, out1_ref, out2_ref,
                      *, first_step):
    i = pl.program_id(1)

    def emit(out_ref, refs, group, rows_in, rows_out):
        for part, ref in enumerate(refs):
            if group > 0:
                val = jnp.dot(pt_ref[group - 1], ref[...], preferred_element_type=F32)
            else:
                val = ref[...].astype(F32)
            for h in range(HEADS_PER_GROUP):
                out_ref[rows_out, part, h, :] = val[rows_in, h * HEAD_DIM:(h + 1) * HEAD_DIM]

    whole = slice(0, PERM_BLOCK)
    emit(out2_ref, (k2_ref, v2_ref), 2, whole, whole)

    @pl.when(i >= first_step[1])
    def _():
        emit(out1_ref, (k1_ref, v1_ref), 1, whole, whole)

    @pl.when(i >= first_step[0])
    def _():
        keep = out0_ref.shape[0]
        emit(out0_ref, (k0_ref, v0_ref), 0, slice(PERM_BLOCK - keep, PERM_BLOCK), slice(0, keep))


def _kv_windows(z_b, batch, seq, perms_t):
    wins = [min(win, seq) for win, _ in DILATED_GROUPS]
    assert wins[2] % PERM_BLOCK == 0 and wins[1] % PERM_BLOCK == 0 and wins[0] <= PERM_BLOCK and seq % PERM_BLOCK == 0
    steps = wins[2] // PERM_BLOCK
    blocks_per_seq = seq // PERM_BLOCK
    first_step = [steps - max(w // PERM_BLOCK, 1) for w in wins]

    def col_spec(group, part):
        cb = (group * QKV_COLS + (1 + part) * GROUP_COLS) // GROUP_COLS

        def index(b, i):
            return (b * blocks_per_seq + blocks_per_seq - steps + jnp.maximum(i, first_step[group]), cb)

        return pl.BlockSpec((PERM_BLOCK, GROUP_COLS), index)

    def out_spec(group):
        rows = min(wins[group], PERM_BLOCK)
        return pl.BlockSpec((None, None, rows, 2, HEADS_PER_GROUP, HEAD_DIM),
                            lambda b, i: (0, b, jnp.maximum(i - first_step[group], 0), 0, 0, 0))

    in_specs = [col_spec(g, part) for g in range(N_DIL) for part in range(2)]
    in_specs.append(pl.BlockSpec(perms_t.shape, lambda b, i: (0, 0, 0)))
    return pl.pallas_call(
        functools.partial(_kv_window_kernel, first_step=tuple(first_step)),
        out_shape=[jax.ShapeDtypeStruct((1, batch, w, 2, HEADS_PER_GROUP, HEAD_DIM), F32) for w in wins],
        grid=(batch, steps),
        in_specs=in_specs,
        out_specs=[out_spec(g) for g in range(N_DIL)],
        compiler_params=_cparams("arbitrary", "arbitrary"),
        name="kv_windows",
    )(*([z_b] * 6), perms_t)


def _kv_rows(z, lead, group):
    zz = z.reshape(lead + (z.shape[-1],))
    base = group * QKV_COLS
    k = zz[..., base + GROUP_COLS: base + 2 * GROUP_COLS]
    v = zz[..., base + 2 * GROUP_COLS: base + 3 * GROUP_COLS]
    kv = jnp.stack([k, v], axis=2).astype(F32)
    return kv.reshape(kv.shape[:3] + (HEADS_PER_GROUP, HEAD_DIM))


def kernel(x_prompt, x_sample, cache_kv_w128, cache_kv_w512, cache_kv_w2048, norm_mix, w_in, ln_v_g, ln_v_b, w_s, b_s, w_a_out, w_b_out, w_o, norm_ffn, w_route_group, b_route_group, w_route_expert, b_route_expert, w_gate_e, w_up_e, w_down_e, norm_final):
    assert norm_mix.shape[0] == 1, "single-layer trunk"
    batch, seq, _ = x_prompt.shape
    bd, n_new, _ = x_sample.shape
    caches = (cache_kv_w128, cache_kv_w512, cache_kv_w2048)

    pad = LANES - N_EXPERT_GROUPS - N_EXPERTS
    wr = jnp.concatenate([w_route_group[0], w_route_expert[0], jnp.zeros((D_MODEL, pad), F32)], axis=1)
    br = jnp.concatenate([b_route_group[0], b_route_expert[0], jnp.zeros((pad,), F32)])[None, :]
    wts = dict(
        wa=w_a_out[0].astype(BF16), wb=w_b_out[0].astype(BF16), wo=w_o[0].astype(BF16),
        wg=w_gate_e[0], wu=w_up_e[0], wd=w_down_e[0],
        wr=wr.astype(BF16), br=br, norm_ffn=norm_ffn[0][None, :], norm_final=norm_final[None, :],
    )
    w_in_b = w_in[0].astype(BF16)
    norm_g = norm_mix[0][None, :]
    ln_g, ln_b = ln_v_g[0][None, :], ln_v_b[0][None, :]

    xp = x_prompt.reshape(batch * seq, D_MODEL)
    perms = jnp.stack([_class_major_perm(dil) for _, dil in DILATED_GROUPS[1:]])
    zp_a, zp_b = _inproj_all(xp, norm_g, w_in_b, tm=1024, perms=perms)
    (sgp,) = _sgu(zp_a, ln_g, ln_b, w_s[0], b_s[0].T, chunks=8, emit_vn=False)
    os_p, lses_p = zip(*[_attn_group(zp_b, batch, seq, g) for g in range(N_DIL)])
    perms_t = perms.transpose(0, 2, 1)
    mixed_p = _mixer_out(xp, zp_a, sgp, os_p, lses_p, wts, tm=256, perms_t=perms_t)
    kv_prompt = _kv_windows(zp_b, batch, seq, perms_t)

    ms = bd * n_new
    assert ms == CHUNK and n_new <= CHUNK
    xs = x_sample.reshape(ms, D_MODEL)
    zs_a, zs_b = _inproj_all(xs, norm_g, w_in_b, tm=ms)
    eye = jnp.eye(bd, dtype=F32)
    ws_s = jnp.einsum("ab,gts->gatbs", eye, w_s[0][:, :n_new, :n_new]).reshape(A_GROUPS, ms, ms)
    bst_s = jnp.tile(b_s[0][:, :n_new].T, (bd, 1))
    sgs, vns = _sgu(zs_a, ln_g, ln_b, ws_s, bst_s, chunks=1, emit_vn=True)
    zs5 = zs_b.astype(F32).reshape(bd, n_new, N_DIL, 3, HEADS_PER_GROUP, HEAD_DIM)
    qkv = [zs5[:, :, :, part].reshape(bd, n_new, N_DIL * HEADS_PER_GROUP, HEAD_DIM) for part in range(3)]
    o_s, lse_s = _attn_sample(*qkv, caches)
    os_s = [o_s[g].reshape(ms, GROUP_COLS).astype(BF16) for g in range(N_DIL)]
    lses_s = [lse_s[g, ..., 0].reshape(ms, HEADS_PER_GROUP) for g in range(N_DIL)]
    mixed_s = _mixer_out(xs, zs_a, sgs, os_s, lses_s, wts, tm=ms)
    y_prompt, y_sample = _moe_ffn(mixed_p, mixed_s, wts)
    y_prompt = y_prompt.reshape(batch, seq, D_MODEL)
    y_sample = y_sample.reshape(bd, n_new, D_MODEL)
    kv_sample = [_kv_rows(zs_b, (bd, n_new), g)[None] for g in range(N_DIL)]
    chunk_v = vns.reshape(1, bd, n_new, A_WIDTH)

    return (y_prompt, y_sample, *kv_prompt, *kv_sample, chunk_v)
```

```python
import functools

import jax
import jax.numpy as jnp
from jax import lax
from jax.experimental import pallas as pl
from jax.experimental.pallas import tpu as pltpu

F32 = jnp.float32
BF16 = jnp.bfloat16

D_MODEL = 2048
CHUNK = 128
A_GROUPS = 16
A_GROUP_DIM = 128
A_WIDTH = A_GROUPS * A_GROUP_DIM
HEAD_DIM = 128
HEADS_PER_GROUP = 4
GROUP_COLS = HEADS_PER_GROUP * HEAD_DIM
DILATED_GROUPS = ((128, 1), (512, 4), (2048, 16))
N_DIL = len(DILATED_GROUPS)
B_WIDTH = N_DIL * GROUP_COLS
COL_U, COL_V = 0, A_WIDTH
COL_GA = 2 * A_WIDTH
COL_GB = COL_GA + D_MODEL
QKV_COLS = 3 * GROUP_COLS
N_EXPERT_GROUPS = 4
EXPERTS_PER_GROUP = 8
N_EXPERTS = N_EXPERT_GROUPS * EXPERTS_PER_GROUP
TOP_K = 2
D_EXPERT = 256
RMS_EPS = 1e-6
LN_EPS = 1e-5
NEG = -1e30
ATTN_SCALE = HEAD_DIM ** -0.5

LANES = 128
VMEM_LIMIT_BYTES = 56 * 1024 * 1024
IN_TN = 512
MOE_TM = 512
PERM_BLOCK = 256
ATTN_UNITS_PER_BATCH = 8
ROW_TILE = D_MODEL // LANES
W_SLOTS = 3


def _cparams(*sem):
    return pltpu.CompilerParams(dimension_semantics=sem, vmem_limit_bytes=VMEM_LIMIT_BYTES)


def _gelu_tanh(x):
    return 0.5 * x * (1.0 + jnp.tanh(0.7978845608028654 * (x + 0.044715 * (x * x * x))))


def _sigmoid(x):
    return 1.0 / (1.0 + jnp.exp(-x))


def _store_token_tiles(ref, val):
    rows = val.shape[0]
    chunks = jnp.stack([val[:, s * LANES:(s + 1) * LANES].astype(BF16) for s in range(ROW_TILE)], axis=0)
    ref[...] = pltpu.einshape("stl->tsl", chunks).reshape(rows * ROW_TILE, LANES)


def _load_token_tile_chunks(ref, lead, rows):
    tiles = ref[lead + (slice(None), slice(None))].reshape(rows, ROW_TILE, LANES)
    by_chunk = pltpu.einshape("tsl->stl", tiles)
    return [by_chunk[s] for s in range(ROW_TILE)]


def _class_major_perm(dil):
    out_row = jnp.arange(PERM_BLOCK, dtype=jnp.int32)
    src = (out_row % (PERM_BLOCK // dil)) * dil + out_row // (PERM_BLOCK // dil)
    return (src[:, None] == jnp.arange(PERM_BLOCK, dtype=jnp.int32)[None, :]).astype(BF16)


def _inproj_kernel(x_ref, g_ref, *rest, strips, plan, permute, ref_tile):
    w_hbm, rest = rest[0], rest[1:]
    if permute:
        perm_ref, z_ref, xn_ref, wbuf, wsem = rest
    else:
        z_ref, xn_ref, wbuf, wsem = rest
    j = pl.program_id(1)
    n_cols = len(plan)

    step = pl.program_id(0) * n_cols + j
    n_steps = pl.num_programs(0) * n_cols

    def strip_copy(t, s):
        col = pl.multiple_of(ref_tile(lax.rem(t, n_cols), s) * IN_TN, IN_TN)
        slot = lax.rem(t, W_SLOTS)
        return pltpu.make_async_copy(w_hbm.at[:, pl.ds(col, IN_TN)], wbuf.at[slot, s], wsem.at[slot])

    def fetch(t):
        for s in range(strips):
            strip_copy(t, s).start()

    @pl.when(step == 0)
    def _():
        for t in range(W_SLOTS - 1):
            @pl.when(jnp.int32(t) < n_steps)
            def _():
                fetch(jnp.int32(t))

    @pl.when(step + W_SLOTS - 1 < n_steps)
    def _():
        fetch(step + W_SLOTS - 1)

    for s in range(strips):
        strip_copy(step, s).wait()
    slot = lax.rem(step, W_SLOTS)
    w_refs = [wbuf.at[slot, s] for s in range(strips)]

    @pl.when(j == 0)
    def _():
        x = x_ref[...]
        ms = jnp.mean(x * x, axis=-1, keepdims=True)
        xn_ref[...] = (x * lax.rsqrt(ms + RMS_EPS) * g_ref[...]).astype(BF16)

    tm = xn_ref.shape[0]
    sub = min(tm, PERM_BLOCK)

    def emit(epilogue):
        for blk in range(tm // sub):
            rows = slice(blk * sub, (blk + 1) * sub)
            for s, w_ref in enumerate(w_refs):
                z = jnp.dot(xn_ref[rows, :], w_ref[...], preferred_element_type=F32)
                z_ref[rows, s * IN_TN:(s + 1) * IN_TN] = epilogue(z).astype(BF16)

    def epilogue_of(kind):
        if kind == "gelu":
            return _gelu_tanh
        if kind == "gate":
            return _sigmoid
        if kind == "plain":
            return lambda z: z
        k = int(kind[len("perm"):])
        return lambda z: jnp.dot(perm_ref[k], z.astype(BF16), preferred_element_type=F32)

    for kind in sorted(set(plan)):
        tiles = [t for t, p in enumerate(plan) if p == kind]
        hit = functools.reduce(jnp.logical_or, [j == t for t in tiles])

        @pl.when(hit)
        def _():
            emit(epilogue_of(kind))


def _inproj(x, norm_g, w_in_bf16, tm, strips, plan, ref_tile, perms=None):
    m = x.shape[0]
    tn = strips * IN_TN
    in_specs = [pl.BlockSpec((tm, D_MODEL), lambda i, j: (i, 0)), pl.BlockSpec((1, D_MODEL), lambda i, j: (0, 0)),
                pl.BlockSpec(memory_space=pl.ANY)]
    args = [x, norm_g, w_in_bf16]
    if perms is not None:
        in_specs.append(pl.BlockSpec(perms.shape, lambda i, j: (0, 0, 0)))
        args.append(perms)
    return pl.pallas_call(
        functools.partial(_inproj_kernel, strips=strips, plan=plan, permute=perms is not None, ref_tile=ref_tile),
        out_shape=jax.ShapeDtypeStruct((m, len(plan) * tn), BF16),
        grid=(m // tm, len(plan)),
        in_specs=in_specs,
        out_specs=pl.BlockSpec((tm, tn), lambda i, j: (i, j)),
        scratch_shapes=[pltpu.VMEM((tm, D_MODEL), BF16), pltpu.VMEM((W_SLOTS, strips, D_MODEL, IN_TN), BF16),
                        pltpu.SemaphoreType.DMA((W_SLOTS,))],
        compiler_params=_cparams("arbitrary", "arbitrary"),
        name="inproj",
    )(*args)


def _inproj_all(x, norm_g, w_in_bf16, tm, perms=None):
    n_gelu = 2 * A_WIDTH // IN_TN
    n_qkv = 3 * B_WIDTH // IN_TN
    strips_a = 2
    plan_a = ("gelu",) * (n_gelu // strips_a) + ("gate",) * (2 * D_MODEL // IN_TN // strips_a)

    def ref_a(j, s):
        t = j * strips_a + s
        return jnp.where(t < n_gelu, t, t + n_qkv)

    z_a = _inproj(x, norm_g, w_in_bf16, tm, strips_a, plan_a, ref_a)
    plan_b = ("plain",) + tuple(f"perm{k}" for k in range(N_DIL - 1)) if perms is not None else ("plain",) * N_DIL
    z_b = _inproj(x, norm_g, w_in_bf16, tm, 3, plan_b, lambda j, s: n_gelu + s * N_DIL + j, perms)
    return z_a, z_b


def _sgu_kernel(u_ref, v_ref, lng_ref, lnb_ref, ws_ref, bst_ref, sg_ref, *vn_out, chunks):
    row = lax.broadcasted_iota(jnp.int32, (CHUNK, CHUNK), 0)
    col = lax.broadcasted_iota(jnp.int32, (CHUNK, CHUNK), 1)
    tri = row >= col
    ws = [jnp.where(tri, ws_ref[g], 0.0).astype(BF16) for g in range(A_GROUPS)]

    def normed(c):
        rows = slice(c * CHUNK, (c + 1) * CHUNK)
        v = v_ref[rows, :].astype(F32)
        mu = jnp.mean(v, axis=-1, keepdims=True)
        vc = v - mu
        var = jnp.mean(vc * vc, axis=-1, keepdims=True)
        vn = vc * lax.rsqrt(var + LN_EPS) * lng_ref[...] + lnb_ref[...]
        if vn_out:
            vn_out[0][rows, :] = vn
        return vn.astype(BF16)

    for c0 in range(0, chunks, 2):
        pair = list(range(c0, min(c0 + 2, chunks)))
        vnb = [normed(c) for c in pair]
        for g in range(A_GROUPS):
            cols = slice(g * A_GROUP_DIM, (g + 1) * A_GROUP_DIM)
            rhs = vnb[0][:, cols] if len(pair) == 1 else jnp.concatenate([b[:, cols] for b in vnb], axis=1)
            s = jnp.dot(ws[g], rhs, preferred_element_type=F32) + bst_ref[:, g:g + 1]
            for k, c in enumerate(pair):
                rows = slice(c * CHUNK, (c + 1) * CHUNK)
                s_c = s[:, k * A_GROUP_DIM:(k + 1) * A_GROUP_DIM]
                sg_ref[rows, cols] = (u_ref[rows, cols].astype(F32) * s_c).astype(BF16)


def _sgu(z, ln_g, ln_b, ws, bst, chunks, emit_vn):
    m = z.shape[0]
    tm = chunks * CHUNK
    out_shape = [jax.ShapeDtypeStruct((m, A_WIDTH), BF16)]
    out_specs = [pl.BlockSpec((tm, A_WIDTH), lambda i: (i, 0))]
    if emit_vn:
        out_shape.append(jax.ShapeDtypeStruct((m, A_WIDTH), F32))
        out_specs.append(pl.BlockSpec((tm, A_WIDTH), lambda i: (i, 0)))
    return pl.pallas_call(
        functools.partial(_sgu_kernel, chunks=chunks),
        out_shape=out_shape,
        grid=(m // tm,),
        in_specs=[
            pl.BlockSpec((tm, A_WIDTH), lambda i: (i, COL_U // A_WIDTH)),
            pl.BlockSpec((tm, A_WIDTH), lambda i: (i, COL_V // A_WIDTH)),
            pl.BlockSpec((1, A_WIDTH), lambda i: (0, 0)),
            pl.BlockSpec((1, A_WIDTH), lambda i: (0, 0)),
            pl.BlockSpec((A_GROUPS, CHUNK, CHUNK), lambda i: (0, 0, 0)),
            pl.BlockSpec((CHUNK, A_GROUPS), lambda i: (0, 0)),
        ],
        out_specs=out_specs,
        compiler_params=_cparams("parallel"),
        name="sgu",
    )(z, z, ln_g, ln_b, ws, bst)


def _attn_kernel(q_ref, kc_ref, kp_ref, vc_ref, vp_ref, o_ref, lse_ref, *, qblocks):
    i = pl.program_id(2)
    qi = lax.broadcasted_iota(jnp.int32, (CHUNK, CHUNK), 0)
    ki = lax.broadcasted_iota(jnp.int32, (CHUNK, CHUNK), 1)
    cur_mask = ki <= qi
    no_prev = jnp.where(i == 0, CHUNK, 0)
    dn = (((1,), (1,)), ((), ()))
    rpc = q_ref.shape[1]
    nbq = CHUNK // rpc

    def load(ref, a, cols):
        return ref[a * nbq:(a + 1) * nbq, :, cols].reshape(CHUNK, HEAD_DIM)

    units = [(a, h) for a in range(qblocks) for h in range(HEADS_PER_GROUP)]
    for start in range(0, len(units), ATTN_UNITS_PER_BATCH):
        batch_units = units[start:start + ATTN_UNITS_PER_BATCH]
        scores = []
        for a, h in batch_units:
            cols = slice(h * HEAD_DIM, (h + 1) * HEAD_DIM)
            q = load(q_ref, a, cols)
            if a == 0:
                kp, prev_mask = load(kp_ref, 0, cols), ki >= qi + no_prev
            else:
                kp, prev_mask = load(kc_ref, a - 1, cols), ki >= qi
            s_c = lax.dot_general(q, load(kc_ref, a, cols), dn, preferred_element_type=F32) * ATTN_SCALE
            s_p = lax.dot_general(q, kp, dn, preferred_element_type=F32) * ATTN_SCALE
            scores.append((jnp.where(cur_mask, s_c, NEG), jnp.where(prev_mask, s_p, NEG)))
        probs = []
        for s_c, s_p in scores:
            mx = jnp.maximum(jnp.max(s_c, axis=-1, keepdims=True), jnp.max(s_p, axis=-1, keepdims=True))
            p_c = jnp.exp(s_c - mx)
            p_p = jnp.exp(s_p - mx)
            l = jnp.sum(p_c, axis=-1, keepdims=True) + jnp.sum(p_p, axis=-1, keepdims=True)
            probs.append((p_c.astype(BF16), p_p.astype(BF16), mx, l))
        for (a, h), (p_c, p_p, mx, l) in zip(batch_units, probs):
            cols = slice(h * HEAD_DIM, (h + 1) * HEAD_DIM)
            vp = load(vp_ref, 0, cols) if a == 0 else load(vc_ref, a - 1, cols)
            acc = jnp.dot(p_c, load(vc_ref, a, cols), preferred_element_type=F32)
            acc = acc + jnp.dot(p_p, vp, preferred_element_type=F32)
            o_ref[a * nbq:(a + 1) * nbq, :, cols] = (acc / l).astype(BF16).reshape(nbq, rpc, HEAD_DIM)
            lse_ref[a * CHUNK:(a + 1) * CHUNK, h:h + 1] = mx + jnp.log(l)


def _rows_per_class(dil):
    return min(PERM_BLOCK // dil, CHUNK)


def _attn_group(z, batch, seq, group):
    _, dil = DILATED_GROUPS[group]
    sub = seq // dil
    rpc = _rows_per_class(dil)
    qblocks = max(b for b in (4, 2, 1) if sub % (b * CHUNK) == 0)
    tq = qblocks * CHUNK
    cq, ck, cv = (group * QKV_COLS // GROUP_COLS + part for part in range(3))
    zv = z.reshape(batch, sub // rpc, dil, rpc, z.shape[-1])

    def cur(cb):
        return pl.BlockSpec((None, tq // rpc, None, rpc, GROUP_COLS), lambda b, r, i: (b, i, r, 0, cb))

    def prev(cb):
        return pl.BlockSpec((None, CHUNK // rpc, None, rpc, GROUP_COLS),
                            lambda b, r, i: (b, jnp.maximum(i * qblocks - 1, 0), r, 0, cb))

    o, lse = pl.pallas_call(
        functools.partial(_attn_kernel, qblocks=qblocks),
        out_shape=[jax.ShapeDtypeStruct((batch, sub // rpc, dil, rpc, GROUP_COLS), BF16),
                   jax.ShapeDtypeStruct((batch, dil, sub, HEADS_PER_GROUP), F32)],
        grid=(batch, dil, sub // tq),
        in_specs=[cur(cq), cur(ck), prev(ck), cur(cv), prev(cv)],
        out_specs=[pl.BlockSpec((None, tq // rpc, None, rpc, GROUP_COLS), lambda b, r, i: (b, i, r, 0, 0)),
                   pl.BlockSpec((None, None, tq, HEADS_PER_GROUP), lambda b, r, i: (b, r, i, 0))],
        compiler_params=_cparams("parallel", "parallel", "arbitrary"),
        name=f"attn_prompt_g{group}",
    )(zv, zv, zv, zv, zv)
    lse = lse.transpose(0, 2, 1, 3).reshape(batch * seq, HEADS_PER_GROUP)
    return o, lse


def _attn_sample_kernel(q_ref, k_ref, v_ref, c0_ref, c1_ref, c2_ref, o_ref, lse_ref, *, n_new):
    caches = (c0_ref, c1_ref, c2_ref)
    rowid = lax.broadcasted_iota(jnp.int32, (CHUNK, HEADS_PER_GROUP, 1), 0)
    for g, (win, dil) in enumerate(DILATED_GROUPS):
        cref = caches[g]
        hs = slice(g * HEADS_PER_GROUP, (g + 1) * HEADS_PER_GROUP)
        for t in range(n_new):
            res, first = t % dil, t // dil
            kc = cref[:, res, 0]
            vc = cref[:, res, 1]
            q = q_ref[t, hs, :]
            s = jnp.sum(kc * q[None], axis=-1, keepdims=True) * ATTN_SCALE
            if first > 0:
                s = jnp.where(rowid >= first, s, NEG)
            new_rows = [tn for tn in range(t + 1) if (t - tn) % dil == 0 and (t - tn) // dil <= win // dil]
            s_new = [jnp.sum(q * k_ref[tn, hs, :], axis=-1, keepdims=True) * ATTN_SCALE for tn in new_rows]
            mx = jnp.max(s, axis=0)
            for sn in s_new:
                mx = jnp.maximum(mx, sn)
            p = jnp.exp(s - mx[None])
            l = jnp.sum(p, axis=0)
            acc = jnp.sum(p * vc, axis=0)
            for tn, sn in zip(new_rows, s_new):
                pn = jnp.exp(sn - mx)
                l = l + pn
                acc = acc + pn * v_ref[tn, hs, :]
            o_ref[g, t] = acc / l
            lse_ref[g, t] = jnp.broadcast_to(mx + jnp.log(l), (HEADS_PER_GROUP, HEAD_DIM))


def _attn_sample(q, k, v, caches):
    bd, n_new = q.shape[:2]
    views, specs = [], []
    for (win, dil), c in zip(DILATED_GROUPS, caches):
        assert c.shape[2] == win and win == CHUNK * dil, "cache must hold exactly one full window"
        used = min(dil, n_new)
        views.append(c.reshape(1, bd, CHUNK, dil, 2, HEADS_PER_GROUP, HEAD_DIM))
        specs.append(pl.BlockSpec((None, None, CHUNK, used, 2, HEADS_PER_GROUP, HEAD_DIM),
                                  lambda b: (0, b, 0, 0, 0, 0, 0)))
    new_spec = pl.BlockSpec((None, n_new, N_DIL * HEADS_PER_GROUP, HEAD_DIM), lambda b: (b, 0, 0, 0))
    out_sds = jax.ShapeDtypeStruct((N_DIL, bd, n_new, HEADS_PER_GROUP, HEAD_DIM), F32)
    out_spec = pl.BlockSpec((N_DIL, None, n_new, HEADS_PER_GROUP, HEAD_DIM), lambda b: (0, b, 0, 0, 0))
    return pl.pallas_call(
        functools.partial(_attn_sample_kernel, n_new=n_new),
        out_shape=[out_sds, out_sds],
        grid=(bd,),
        in_specs=[new_spec, new_spec, new_spec] + specs,
        out_specs=[out_spec, out_spec],
        compiler_params=_cparams("parallel"),
        name="attn_sample",
    )(q, k, v, *views)


def _mixer_out_kernel(sg_ref, o0_ref, o1_ref, o2_ref, l0_ref, l1_ref, l2_ref, ga_ref, gb_ref,
                      wa_ref, wb_ref, x_ref, wo_ref, g_ref, wr_ref, br_ref, *rest, permuted):
    if permuted:
        pt_ref, h_ref, hn_ref, ri_ref, rw_ref, ob_ref = rest
    else:
        h_ref, hn_ref, ri_ref, rw_ref, ob_ref = rest

    l0, l1, l2 = l0_ref[...], l1_ref[...], l2_ref[...]
    mx = jnp.maximum(jnp.maximum(l0, l1), l2)
    e0, e1, e2 = jnp.exp(l0 - mx), jnp.exp(l1 - mx), jnp.exp(l2 - mx)
    inv = 1.0 / (e0 + e1 + e2)
    ws = (e0 * inv, e1 * inv, e2 * inv)
    o_refs = (o0_ref, o1_ref, o2_ref)
    tm = ob_ref.shape[0]

    def o_rows(ref, rows):
        if len(ref.shape) == 2:
            return ref[rows, :]
        per = ref.shape[1] * ref.shape[2]
        return ref[rows.start // per:rows.stop // per].reshape(rows.stop - rows.start, GROUP_COLS)

    for blk in range(max(tm // PERM_BLOCK, 1)):
        rows = slice(blk * PERM_BLOCK, min((blk + 1) * PERM_BLOCK, tm))
        og = []
        for g in range(N_DIL):
            if permuted and g > 0:
                og.append(jnp.dot(pt_ref[g - 1], o_rows(o_refs[g], rows), preferred_element_type=F32))
            else:
                og.append(o_rows(o_refs[g], rows).astype(F32))
        for h in range(HEADS_PER_GROUP):
            cols = slice(h * HEAD_DIM, (h + 1) * HEAD_DIM)
            ob = sum(ws[g][rows, h:h + 1] * og[g][:, cols] for g in range(N_DIL))
            ob_ref[rows, cols] = ob.astype(BF16)

    ya = jnp.dot(sg_ref[...], wa_ref[...], preferred_element_type=F32)
    yb = jnp.dot(ob_ref[...], wb_ref[...], preferred_element_type=F32)
    m = (ga_ref[...].astype(F32) * ya + gb_ref[...].astype(F32) * yb).astype(BF16)
    _oproj_rows(m, x_ref, wo_ref, g_ref, wr_ref, br_ref, h_ref, hn_ref, ri_ref, rw_ref)


def _mixer_out(x, z, sg, os_, lses, wts, tm, perms_t=None):
    m = z.shape[0]
    row = lambda width: pl.BlockSpec((tm, width), lambda i: (i, 0))
    whole = lambda a: pl.BlockSpec(a.shape, lambda i: (0,) * a.ndim, pipeline_mode=pl.Buffered(1))

    def o_spec(o):
        if o.ndim == 2:
            return row(GROUP_COLS)
        _, blocks, dil, rpc, _ = o.shape
        per_tile, tiles = tm // (dil * rpc), blocks * dil * rpc // tm
        return pl.BlockSpec((None, per_tile, dil, rpc, GROUP_COLS), lambda i: (i // tiles, i % tiles, 0, 0, 0))

    consts = [wts["wa"], wts["wb"]]
    tail_consts = [wts["wo"], wts["norm_ffn"], wts["wr"], wts["br"]]
    in_specs = [row(A_WIDTH)] + [o_spec(o) for o in os_] + [row(HEADS_PER_GROUP)] * 3 + [
        pl.BlockSpec((tm, D_MODEL), lambda i: (i, COL_GA // D_MODEL)),
        pl.BlockSpec((tm, D_MODEL), lambda i: (i, COL_GB // D_MODEL)),
    ] + [whole(a) for a in consts] + [row(D_MODEL)] + [whole(a) for a in tail_consts]
    args = [sg, *os_, *lses, z, z, *consts, x, *tail_consts]
    if perms_t is not None:
        assert tm % PERM_BLOCK == 0
        in_specs.append(whole(perms_t))
        args.append(perms_t)
    return pl.pallas_call(
        functools.partial(_mixer_out_kernel, permuted=perms_t is not None),
        out_shape=[jax.ShapeDtypeStruct((m, D_MODEL), F32), jax.ShapeDtypeStruct((m * ROW_TILE, LANES), BF16),
                   jax.ShapeDtypeStruct((m, LANES), jnp.int32), jax.ShapeDtypeStruct((m, LANES), F32)],
        grid=(m // tm,),
        in_specs=in_specs,
        out_specs=[row(D_MODEL), pl.BlockSpec((tm * ROW_TILE, LANES), lambda i: (i, 0)), row(LANES), row(LANES)],
        scratch_shapes=[pltpu.VMEM((tm, GROUP_COLS), BF16)],
        compiler_params=_cparams("parallel"),
        name="mixer_out",
    )(*args)


def _oproj_rows(m, x_ref, wo_ref, g_ref, wr_ref, br_ref, h_ref, hn_ref, ri_ref, rw_ref):
    h = x_ref[...] + jnp.dot(m, wo_ref[...], preferred_element_type=F32)
    h_ref[...] = h
    ms = jnp.mean(h * h, axis=-1, keepdims=True)
    hn = h * lax.rsqrt(ms + RMS_EPS) * g_ref[...]
    _store_token_tiles(hn_ref, hn)
    logits = jnp.dot(hn.astype(BF16), wr_ref[...], preferred_element_type=F32) + br_ref[...]
    lane = lax.broadcasted_iota(jnp.int32, logits.shape, 1).astype(F32)
    big = float(LANES)

    def first_argmax(vals, vmax):
        return jnp.min(jnp.where(vals == vmax, lane, big), axis=-1, keepdims=True)

    lg = jnp.where(lane < N_EXPERT_GROUPS, logits, NEG)
    gmax = jnp.max(lg, axis=-1, keepdims=True)
    gsel = first_argmax(lg, gmax)
    p_sel = 1.0 / jnp.sum(jnp.exp(lg - gmax), axis=-1, keepdims=True)
    lo = N_EXPERT_GROUPS + EXPERTS_PER_GROUP * gsel
    le = jnp.where(jnp.logical_and(lane >= lo, lane < lo + EXPERTS_PER_GROUP), logits, NEG)
    v1 = jnp.max(le, axis=-1, keepdims=True)
    i1 = first_argmax(le, v1)
    le2 = jnp.where(lane == i1, NEG, le)
    v2 = jnp.max(le2, axis=-1, keepdims=True)
    i2 = first_argmax(le2, v2)
    e2 = jnp.exp(v2 - v1)
    w1 = p_sel / (1.0 + e2)
    w2 = p_sel * e2 / (1.0 + e2)
    ri = jnp.where(lane == 0, i1 - N_EXPERT_GROUPS, jnp.where(lane == 1, i2 - N_EXPERT_GROUPS, 0.0))
    ri_ref[...] = ri.astype(jnp.int32)
    rw_ref[...] = jnp.where(lane == 0, w1, jnp.where(lane == 1, w2, 0.0))


def _dispatch_kernel(slot_ref, pad_start_ref, pad_len_ref, hn_ref, hs_ref, xs_hbm, zbuf, sem, zsem,
                     *, tm, moe_tm, n_tiles):
    i = pl.program_id(0)
    n_main = pl.num_programs(0) - 1

    def zero_copy(dst_row, rows):
        return pltpu.make_async_copy(zbuf.at[pl.ds(0, rows * ROW_TILE)],
                                     xs_hbm.at[pl.ds(dst_row * ROW_TILE, rows * ROW_TILE)], zsem)

    def zero_fill(act):
        def per_expert(e, carry):
            off, length = pad_start_ref[e], pad_len_ref[e]
            rows = moe_tm // 2
            while rows >= 1:
                @pl.when((length & rows) != 0)
                def _(off=off, rows=rows):
                    act(zero_copy(off, rows))
                off = off + (length & rows)
                rows //= 2
            return carry

        lax.fori_loop(0, N_EXPERTS, per_expert, 0)

        def per_tile(t, carry):
            act(zero_copy(t * moe_tm, moe_tm))
            return carry

        lax.fori_loop(pad_start_ref[N_EXPERTS], n_tiles, per_tile, 0)

    @pl.when(i == 0)
    def _():
        zbuf[...] = jnp.zeros_like(zbuf)
        zero_fill(lambda c: c.start())

    def scatter_rows(ref, first_token):
        rows = ref.shape[0] // ROW_TILE

        def issue(r, carry):
            for k in range(TOP_K):
                slot = slot_ref[(first_token + r) * TOP_K + k]
                pltpu.make_async_copy(ref.at[pl.ds(r * ROW_TILE, ROW_TILE)],
                                      xs_hbm.at[pl.ds(slot * ROW_TILE, ROW_TILE)], sem).start(priority=k % 2)
            return carry

        lax.fori_loop(0, rows, issue, 0, unroll=8)
        for k in range(TOP_K):
            pltpu.make_async_copy(ref, ref, sem).wait()

    @pl.when(i < n_main)
    def _():
        scatter_rows(hn_ref, i * tm)

    @pl.when(i == n_main)
    def _():
        scatter_rows(hs_ref, n_main * tm)
        zero_fill(lambda c: c.wait())


def _dispatch(hn_tiles, hs_tiles, slot, pad_start, pad_len, n_tiles, moe_tm, tm):
    m = hn_tiles.shape[0] // ROW_TILE
    assert m % tm == 0 and moe_tm & (moe_tm - 1) == 0
    assert slot.shape[0] == (m + hs_tiles.shape[0] // ROW_TILE) * TOP_K
    n_main = m // tm
    grid_spec = pltpu.PrefetchScalarGridSpec(
        num_scalar_prefetch=3,
        grid=(n_main + 1,),
        in_specs=[pl.BlockSpec((tm * ROW_TILE, LANES), lambda i, s, ps, pn: (jnp.minimum(i, n_main - 1), 0)),
                  pl.BlockSpec(hs_tiles.shape, lambda i, s, ps, pn: (0, 0))],
        out_specs=pl.BlockSpec(memory_space=pl.ANY),
        scratch_shapes=[pltpu.VMEM((moe_tm * ROW_TILE, LANES), BF16),
                        pltpu.SemaphoreType.DMA(()), pltpu.SemaphoreType.DMA(())],
    )
    return pl.pallas_call(
        functools.partial(_dispatch_kernel, tm=tm, moe_tm=moe_tm, n_tiles=n_tiles),
        out_shape=jax.ShapeDtypeStruct((n_tiles * moe_tm * ROW_TILE, LANES), BF16),
        grid_spec=grid_spec,
        compiler_params=_cparams("arbitrary"),
        name="moe_dispatch",
    )(slot, pad_start, pad_len, hn_tiles, hs_tiles)


def _moe_kernel(texp_ref, valid_ref, xs_ref, wg_ref, wu_ref, wd_ref, ys_ref, wg_bf, wu_bf, wd_bf):
    i = pl.program_id(0)
    tm = xs_ref.shape[0] // ROW_TILE
    valid = valid_ref[i]
    new_expert = jnp.logical_or(i == 0, texp_ref[i] != texp_ref[jnp.maximum(i - 1, 0)])

    @pl.when(jnp.logical_and(valid > 0, new_expert))
    def _():
        wg_bf[...] = wg_ref[...].astype(BF16)
        wu_bf[...] = wu_ref[...].astype(BF16)
        wd_bf[...] = wd_ref[...].astype(BF16)

    @pl.when(valid > 0)
    def _():
        x = jnp.concatenate(_load_token_tile_chunks(xs_ref, (), tm), axis=1)
        gate = jnp.dot(x, wg_bf[...], preferred_element_type=F32)
        up = jnp.dot(x, wu_bf[...], preferred_element_type=F32)
        hid = (gate * _sigmoid(gate) * up).astype(BF16)
        _store_token_tiles(ys_ref, jnp.dot(hid, wd_bf[...], preferred_element_type=F32))

    @pl.when(valid == 0)
    def _():
        ys_ref[...] = jnp.zeros_like(ys_ref)


def _moe(xs, wg, wu, wd, tile_expert, tile_valid, tm):
    n_tiles = tile_expert.shape[0]
    grid_spec = pltpu.PrefetchScalarGridSpec(
        num_scalar_prefetch=2,
        grid=(n_tiles,),
        in_specs=[
            pl.BlockSpec((tm * ROW_TILE, LANES), lambda i, te, tv: (i, 0)),
            pl.BlockSpec((None, D_MODEL, D_EXPERT), lambda i, te, tv: (te[i], 0, 0)),
            pl.BlockSpec((None, D_MODEL, D_EXPERT), lambda i, te, tv: (te[i], 0, 0)),
            pl.BlockSpec((None, D_EXPERT, D_MODEL), lambda i, te, tv: (te[i], 0, 0)),
        ],
        out_specs=pl.BlockSpec((tm * ROW_TILE, LANES), lambda i, te, tv: (i, 0)),
        scratch_shapes=[pltpu.VMEM((D_MODEL, D_EXPERT), BF16), pltpu.VMEM((D_MODEL, D_EXPERT), BF16),
                        pltpu.VMEM((D_EXPERT, D_MODEL), BF16)],
    )
    return pl.pallas_call(
        _moe_kernel,
        out_shape=jax.ShapeDtypeStruct((n_tiles * tm * ROW_TILE, LANES), BF16),
        grid_spec=grid_spec,
        compiler_params=_cparams("arbitrary"),
        name="moe_experts",
    )(tile_expert, tile_valid, xs, wg, wu, wd)


def _combine_kernel(slot_ref, h_ref, rw_ref, g_ref, ys_hbm, y_ref, ybuf, sem, *, tm):
    i = pl.program_id(0)
    n_steps = pl.num_programs(0)

    def issue(tile, buf):
        def body(r, carry):
            for k in range(TOP_K):
                slot = slot_ref[(tile * tm + r) * TOP_K + k]
                pltpu.make_async_copy(ys_hbm.at[pl.ds(slot * ROW_TILE, ROW_TILE)],
                                      ybuf.at[buf, k, pl.ds(r * ROW_TILE, ROW_TILE)], sem.at[buf]).start(priority=k % 2)
            return carry

        lax.fori_loop(0, tm, body, 0, unroll=8)

    @pl.when(i == 0)
    def _():
        issue(0, 0)

    @pl.when(i + 1 < n_steps)
    def _():
        issue(i + 1, (i + 1) % 2)

    cur = i % 2
    for k in range(TOP_K):
        pltpu.make_async_copy(ybuf.at[cur, k], ybuf.at[cur, k], sem.at[cur]).wait()
    w0, w1 = rw_ref[:, 0:1], rw_ref[:, 1:2]
    e0 = _load_token_tile_chunks(ybuf, (cur, 0), tm)
    e1 = _load_token_tile_chunks(ybuf, (cur, 1), tm)
    chunks, ssq = [], 0.0
    for s in range(ROW_TILE):
        y = h_ref[:, s * LANES:(s + 1) * LANES] + w0 * e0[s].astype(F32) + w1 * e1[s].astype(F32)
        chunks.append(y)
        ssq = ssq + jnp.sum(y * y, axis=-1, keepdims=True)
    scale = lax.rsqrt(ssq * (1.0 / D_MODEL) + RMS_EPS)
    for s, y in enumerate(chunks):
        cols = slice(s * LANES, (s + 1) * LANES)
        y_ref[:, cols] = y * scale * g_ref[:, cols]


def _combine(h, route_w, norm_g, ys, slot, tm):
    m = h.shape[0]
    grid_spec = pltpu.PrefetchScalarGridSpec(
        num_scalar_prefetch=1,
        grid=(m // tm,),
        in_specs=[
            pl.BlockSpec((tm, D_MODEL), lambda i, s: (i, 0)),
            pl.BlockSpec((tm, LANES), lambda i, s: (i, 0)),
            pl.BlockSpec((1, D_MODEL), lambda i, s: (0, 0)),
            pl.BlockSpec(memory_space=pl.ANY),
        ],
        out_specs=pl.BlockSpec((tm, D_MODEL), lambda i, s: (i, 0)),
        scratch_shapes=[pltpu.VMEM((2, TOP_K, tm * ROW_TILE, LANES), BF16), pltpu.SemaphoreType.DMA((2,))],
    )
    return pl.pallas_call(
        functools.partial(_combine_kernel, tm=tm),
        out_shape=jax.ShapeDtypeStruct((m, D_MODEL), F32),
        grid_spec=grid_spec,
        compiler_params=_cparams("arbitrary"),
        name="moe_combine",
    )(slot, h, route_w, norm_g, ys)


def _routing_tables(route_i, m, tm):
    n = m * TOP_K
    n_tiles = pl.cdiv(n, tm) + N_EXPERTS - 1
    e_flat = route_i[:, :TOP_K].reshape(n)
    onehot = (e_flat[:, None] == jnp.arange(N_EXPERTS, dtype=jnp.int32)[None, :]).astype(jnp.int32)
    csum = jnp.cumsum(onehot, axis=0)
    rank = jnp.take_along_axis(csum, e_flat[:, None], axis=1)[:, 0] - 1
    counts = csum[-1]
    tiles_e = (counts + tm - 1) // tm
    tile_end = jnp.cumsum(tiles_e)
    tile_start = tile_end - tiles_e
    slot = (tile_start[e_flat] * tm + rank).astype(jnp.int32)
    tile_ids = jnp.arange(n_tiles, dtype=jnp.int32)
    tile_expert = jnp.sum((tile_end[None, :] <= tile_ids[:, None]).astype(jnp.int32), axis=1)
    tile_expert = jnp.minimum(tile_expert, N_EXPERTS - 1)
    tile_valid = jnp.clip(counts[tile_expert] - (tile_ids - tile_start[tile_expert]) * tm, 0, tm)
    tile_valid = jnp.where(tile_ids < tile_end[-1], tile_valid, 0).astype(jnp.int32)
    pad_start = jnp.concatenate([tile_start * tm + counts, tile_end[-1:]]).astype(jnp.int32)
    pad_len = (tiles_e * tm - counts).astype(jnp.int32)
    return slot, tile_expert.astype(jnp.int32), tile_valid, pad_start, pad_len


def _moe_ffn(prompt, sample, wts):
    (h_p, hn_p, ri_p, rw_p), (h_s, hn_s, ri_s, rw_s) = prompt, sample
    m_p, m_s = h_p.shape[0], h_s.shape[0]
    route_i = jnp.concatenate([ri_p, ri_s], axis=0)
    slot, tile_expert, tile_valid, pad_start, pad_len = _routing_tables(route_i, m_p + m_s, MOE_TM)
    xs = _dispatch(hn_p, hn_s, slot, pad_start, pad_len, tile_expert.shape[0], MOE_TM, tm=512)
    ys = _moe(xs, wts["wg"], wts["wu"], wts["wd"], tile_expert, tile_valid, MOE_TM)
    y_p = _combine(h_p, rw_p, wts["norm_final"], ys, slot[:m_p * TOP_K], tm=512)
    y_s = _combine(h_s, rw_s, wts["norm_final"], ys, slot[m_p * TOP_K:], tm=m_s)
    return y_p, y_s


def _kv_window_kernel(k0_ref, v0_ref, k1_ref, v1_ref, k2_ref, v2_ref, pt_ref, out0_ref, out1_ref, out2_ref,
                      *, first_step):
    i = pl.program_id(1)

    def emit(out_ref, refs, group, rows_in, rows_out):
        for part, ref in enumerate(refs):
            if group > 0:
                val = jnp.dot(pt_ref[group - 1], ref[...], preferred_element_type=F32)
            else:
                val = ref[...].astype(F32)
            for h in range(HEADS_PER_GROUP):
                out_ref[rows_out, part, h, :] = val[rows_in, h * HEAD_DIM:(h + 1) * HEAD_DIM]

    whole = slice(0, PERM_BLOCK)
    emit(out2_ref, (k2_ref, v2_ref), 2, whole, whole)

    @pl.when(i >= first_step[1])
    def _():
        emit(out1_ref, (k1_ref, v1_ref), 1, whole, whole)

    @pl.when(i >= first_step[0])
    def _():
        keep = out0_ref.shape[0]
        emit(out0_ref, (k0_ref, v0_ref), 0, slice(PERM_BLOCK - keep, PERM_BLOCK), slice(0, keep))


def _kv_windows(z_b, batch, seq, perms_t):
    wins = [min(win, seq) for win, _ in DILATED_GROUPS]
    assert wins[2] % PERM_BLOCK == 0 and wins[1] % PERM_BLOCK == 0 and wins[0] <= PERM_BLOCK and seq % PERM_BLOCK == 0
    steps = wins[2] // PERM_BLOCK
    blocks_per_seq = seq // PERM_BLOCK
    first_step = [steps - max(w // PERM_BLOCK, 1) for w in wins]

    def col_spec(group, part):
        cb = (group * QKV_COLS + (1 + part) * GROUP_COLS) // GROUP_COLS

        def index(b, i):
            return (b * blocks_per_seq + blocks_per_seq - steps + jnp.maximum(i, first_step[group]), cb)

        return pl.BlockSpec((PERM_BLOCK, GROUP_COLS), index)

    def out_spec(group):
        rows = min(wins[group], PERM_BLOCK)
        return pl.BlockSpec((None, None, rows, 2, HEADS_PER_GROUP, HEAD_DIM),
                            lambda b, i: (0, b, jnp.maximum(i - first_step[group], 0), 0, 0, 0))

    in_specs = [col_spec(g, part) for g in range(N_DIL) for part in range(2)]
    in_specs.append(pl.BlockSpec(perms_t.shape, lambda b, i: (0, 0, 0)))
    return pl.pallas_call(
        functools.partial(_kv_window_kernel, first_step=tuple(first_step)),
        out_shape=[jax.ShapeDtypeStruct((1, batch, w, 2, HEADS_PER_GROUP, HEAD_DIM), F32) for w in wins],
        grid=(batch, steps),
        in_specs=in_specs,
        out_specs=[out_spec(g) for g in range(N_DIL)],
        compiler_params=_cparams("arbitrary", "arbitrary"),
        name="kv_windows",
    )(*([z_b] * 6), perms_t)


def _kv_rows(z, lead, group):
    zz = z.reshape(lead + (z.shape[-1],))
    base = group * QKV_COLS
    k = zz[..., base + GROUP_COLS: base + 2 * GROUP_COLS]
    v = zz[..., base + 2 * GROUP_COLS: base + 3 * GROUP_COLS]
    kv = jnp.stack([k, v], axis=2).astype(F32)
    return kv.reshape(kv.shape[:3] + (HEADS_PER_GROUP, HEAD_DIM))


def kernel(x_prompt, x_sample, cache_kv_w128, cache_kv_w512, cache_kv_w2048, norm_mix, w_in, ln_v_g, ln_v_b, w_s, b_s, w_a_out, w_b_out, w_o, norm_ffn, w_route_group, b_route_group, w_route_expert, b_route_expert, w_gate_e, w_up_e, w_down_e, norm_final):
    assert norm_mix.shape[0] == 1, "single-layer trunk"
    batch, seq, _ = x_prompt.shape
    bd, n_new, _ = x_sample.shape
    caches = (cache_kv_w128, cache_kv_w512, cache_kv_w2048)

    pad = LANES - N_EXPERT_GROUPS - N_EXPERTS
    wr = jnp.concatenate([w_route_group[0], w_route_expert[0], jnp.zeros((D_MODEL, pad), F32)], axis=1)
    br = jnp.concatenate([b_route_group[0], b_route_expert[0], jnp.zeros((pad,), F32)])[None, :]
    wts = dict(
        wa=w_a_out[0].astype(BF16), wb=w_b_out[0].astype(BF16), wo=w_o[0].astype(BF16),
        wg=w_gate_e[0], wu=w_up_e[0], wd=w_down_e[0],
        wr=wr.astype(BF16), br=br, norm_ffn=norm_ffn[0][None, :], norm_final=norm_final[None, :],
    )
    w_in_b = w_in[0].astype(BF16)
    norm_g = norm_mix[0][None, :]
    ln_g, ln_b = ln_v_g[0][None, :], ln_v_b[0][None, :]

    xp = x_prompt.reshape(batch * seq, D_MODEL)
    perms = jnp.stack([_class_major_perm(dil) for _, dil in DILATED_GROUPS[1:]])
    zp_a, zp_b = _inproj_all(xp, norm_g, w_in_b, tm=1024, perms=perms)
    (sgp,) = _sgu(zp_a, ln_g, ln_b, w_s[0], b_s[0].T, chunks=8, emit_vn=False)
    os_p, lses_p = zip(*[_attn_group(zp_b, batch, seq, g) for g in range(N_DIL)])
    perms_t = perms.transpose(0, 2, 1)
    mixed_p = _mixer_out(xp, zp_a, sgp, os_p, lses_p, wts, tm=256, perms_t=perms_t)
    kv_prompt = _kv_windows(zp_b, batch, seq, perms_t)

    ms = bd * n_new
    assert ms == CHUNK and n_new <= CHUNK
    xs = x_sample.reshape(ms, D_MODEL)
    zs_a, zs_b = _inproj_all(xs, norm_g, w_in_b, tm=ms)
    eye = jnp.eye(bd, dtype=F32)
    ws_s = jnp.einsum("ab,gts->gatbs", eye, w_s[0][:, :n_new, :n_new]).reshape(A_GROUPS, ms, ms)
    bst_s = jnp.tile(b_s[0][:, :n_new].T, (bd, 1))
    sgs, vns = _sgu(zs_a, ln_g, ln_b, ws_s, bst_s, chunks=1, emit_vn=True)
    zs5 = zs_b.astype(F32).reshape(bd, n_new, N_DIL, 3, HEADS_PER_GROUP, HEAD_DIM)
    qkv = [zs5[:, :, :, part].reshape(bd, n_new, N_DIL * HEADS_PER_GROUP, HEAD_DIM) for part in range(3)]
    o_s, lse_s = _attn_sample(*qkv, caches)
    os_s = [o_s[g].reshape(ms, GROUP_COLS).astype(BF16) for g in range(N_DIL)]
    lses_s = [lse_s[g, ..., 0].reshape(ms, HEADS_PER_GROUP) for g in range(N_DIL)]
    mixed_s = _mixer_out(xs, zs_a, sgs, os_s, lses_s, wts, tm=ms)
    y_prompt, y_sample = _moe_ffn(mixed_p, mixed_s, wts)
    y_prompt = y_prompt.reshape(batch, seq, D_MODEL)
    y_sample = y_sample.reshape(bd, n_new, D_MODEL)
    kv_sample = [_kv_rows(zs_b, (bd, n_new), g)[None] for g in range(N_DIL)]
    chunk_v = vns.reshape(1, bd, n_new, A_WIDTH)

    return (y_prompt, y_sample, *kv_prompt, *kv_sample, chunk_v)
```

```python
import functools

import jax
import jax.numpy as jnp
from jax import lax
from jax.experimental import pallas as pl
from jax.experimental.pallas import tpu as pltpu

F32 = jnp.float32
BF16 = jnp.bfloat16

D_MODEL = 2048
CHUNK = 128
A_GROUPS = 16
A_GROUP_DIM = 128
A_WIDTH = A_GROUPS * A_GROUP_DIM
HEAD_DIM = 128
HEADS_PER_GROUP = 4
GROUP_COLS = HEADS_PER_GROUP * HEAD_DIM
DILATED_GROUPS = ((128, 1), (512, 4), (2048, 16))
N_DIL = len(DILATED_GROUPS)
B_WIDTH = N_DIL * GROUP_COLS
COL_U, COL_V = 0, A_WIDTH
COL_GA = 2 * A_WIDTH
COL_GB = COL_GA + D_MODEL
QKV_COLS = 3 * GROUP_COLS
N_EXPERT_GROUPS = 4
EXPERTS_PER_GROUP = 8
N_EXPERTS = N_EXPERT_GROUPS * EXPERTS_PER_GROUP
TOP_K = 2
D_EXPERT = 256
RMS_EPS = 1e-6
LN_EPS = 1e-5
NEG = -1e30
ATTN_SCALE = HEAD_DIM ** -0.5

LANES = 128
VMEM_LIMIT_BYTES = 56 * 1024 * 1024
IN_TN = 512
MOE_TM = 512
PERM_BLOCK = 256
ATTN_UNITS_PER_BATCH = 8
ROW_TILE = D_MODEL // LANES


def _cparams(*sem):
    return pltpu.CompilerParams(dimension_semantics=sem, vmem_limit_bytes=VMEM_LIMIT_BYTES)


def _gelu_tanh(x):
    return 0.5 * x * (1.0 + jnp.tanh(0.7978845608028654 * (x + 0.044715 * (x * x * x))))


def _sigmoid(x):
    return 1.0 / (1.0 + jnp.exp(-x))


def _store_token_tiles(ref, val):
    rows = val.shape[0]
    chunks = jnp.stack([val[:, s * LANES:(s + 1) * LANES].astype(BF16) for s in range(ROW_TILE)], axis=0)
    ref[...] = pltpu.einshape("stl->tsl", chunks).reshape(rows * ROW_TILE, LANES)


def _load_token_tile_chunks(ref, lead, rows):
    tiles = ref[lead + (slice(None), slice(None))].reshape(rows, ROW_TILE, LANES)
    by_chunk = pltpu.einshape("tsl->stl", tiles)
    return [by_chunk[s] for s in range(ROW_TILE)]


def _class_major_perm(dil):
    out_row = jnp.arange(PERM_BLOCK, dtype=jnp.int32)
    src = (out_row % (PERM_BLOCK // dil)) * dil + out_row // (PERM_BLOCK // dil)
    return (src[:, None] == jnp.arange(PERM_BLOCK, dtype=jnp.int32)[None, :]).astype(BF16)


def _inproj_kernel(x_ref, g_ref, *rest, strips, plan, permute):
    w_refs, rest = rest[:strips], rest[strips:]
    if permute:
        perm_ref, z_ref, xn_ref = rest
    else:
        z_ref, xn_ref = rest
    j = pl.program_id(1)

    @pl.when(j == 0)
    def _():
        x = x_ref[...]
        ms = jnp.mean(x * x, axis=-1, keepdims=True)
        xn_ref[...] = (x * lax.rsqrt(ms + RMS_EPS) * g_ref[...]).astype(BF16)

    tm = xn_ref.shape[0]
    sub = min(tm, PERM_BLOCK)

    def emit(epilogue):
        for blk in range(tm // sub):
            rows = slice(blk * sub, (blk + 1) * sub)
            for s, w_ref in enumerate(w_refs):
                z = jnp.dot(xn_ref[rows, :], w_ref[...], preferred_element_type=F32)
                z_ref[rows, s * IN_TN:(s + 1) * IN_TN] = epilogue(z).astype(BF16)

    def epilogue_of(kind):
        if kind == "gelu":
            return _gelu_tanh
        if kind == "gate":
            return _sigmoid
        if kind == "plain":
            return lambda z: z
        k = int(kind[len("perm"):])
        dil = DILATED_GROUPS[k + 1][1]

        def class_major(z):
            zt = pltpu.einshape("lrc->rlc", z.astype(BF16).reshape(PERM_BLOCK // dil, dil, IN_TN))
            return zt.reshape(PERM_BLOCK, IN_TN)

        return class_major

    for kind in sorted(set(plan)):
        tiles = [t for t, p in enumerate(plan) if p == kind]
        hit = functools.reduce(jnp.logical_or, [j == t for t in tiles])

        @pl.when(hit)
        def _():
            emit(epilogue_of(kind))


def _inproj(x, norm_g, w_in_bf16, tm, strips, plan, ref_tile, perms=None):
    m = x.shape[0]
    tn = strips * IN_TN
    in_specs = [pl.BlockSpec((tm, D_MODEL), lambda i, j: (i, 0)), pl.BlockSpec((1, D_MODEL), lambda i, j: (0, 0))]
    in_specs += [pl.BlockSpec((D_MODEL, IN_TN), functools.partial(lambda i, j, s: (0, ref_tile(j, s)), s=s))
                 for s in range(strips)]
    args = [x, norm_g] + [w_in_bf16] * strips
    if perms is not None:
        in_specs.append(pl.BlockSpec(perms.shape, lambda i, j: (0, 0, 0)))
        args.append(perms)
    return pl.pallas_call(
        functools.partial(_inproj_kernel, strips=strips, plan=plan, permute=perms is not None),
        out_shape=jax.ShapeDtypeStruct((m, len(plan) * tn), BF16),
        grid=(m // tm, len(plan)),
        in_specs=in_specs,
        out_specs=pl.BlockSpec((tm, tn), lambda i, j: (i, j)),
        scratch_shapes=[pltpu.VMEM((tm, D_MODEL), BF16)],
        compiler_params=_cparams("parallel", "arbitrary"),
        name="inproj",
    )(*args)


def _inproj_all(x, norm_g, w_in_bf16, tm, perms=None):
    n_gelu = 2 * A_WIDTH // IN_TN
    n_qkv = 3 * B_WIDTH // IN_TN
    strips_a = 2
    plan_a = ("gelu",) * (n_gelu // strips_a) + ("gate",) * (2 * D_MODEL // IN_TN // strips_a)

    def ref_a(j, s):
        t = j * strips_a + s
        return jnp.where(t < n_gelu, t, t + n_qkv)

    z_a = _inproj(x, norm_g, w_in_bf16, tm, strips_a, plan_a, ref_a)
    plan_b = ("plain",) + tuple(f"perm{k}" for k in range(N_DIL - 1)) if perms is not None else ("plain",) * N_DIL
    z_b = _inproj(x, norm_g, w_in_bf16, tm, 3, plan_b, lambda j, s: n_gelu + s * N_DIL + j, perms)
    return z_a, z_b


def _sgu_kernel(u_ref, v_ref, lng_ref, lnb_ref, ws_ref, bst_ref, sg_ref, *vn_out, chunks):
    row = lax.broadcasted_iota(jnp.int32, (CHUNK, CHUNK), 0)
    col = lax.broadcasted_iota(jnp.int32, (CHUNK, CHUNK), 1)
    tri = row >= col
    ws = [jnp.where(tri, ws_ref[g], 0.0).astype(BF16) for g in range(A_GROUPS)]

    def normed(c):
        rows = slice(c * CHUNK, (c + 1) * CHUNK)
        v = v_ref[rows, :].astype(F32)
        mu = jnp.mean(v, axis=-1, keepdims=True)
        vc = v - mu
        var = jnp.mean(vc * vc, axis=-1, keepdims=True)
        vn = vc * lax.rsqrt(var + LN_EPS) * lng_ref[...] + lnb_ref[...]
        if vn_out:
            vn_out[0][rows, :] = vn
        return vn.astype(BF16)

    for c0 in range(0, chunks, 2):
        pair = list(range(c0, min(c0 + 2, chunks)))
        vnb = [normed(c) for c in pair]
        for g in range(A_GROUPS):
            cols = slice(g * A_GROUP_DIM, (g + 1) * A_GROUP_DIM)
            rhs = vnb[0][:, cols] if len(pair) == 1 else jnp.concatenate([b[:, cols] for b in vnb], axis=1)
            s = jnp.dot(ws[g], rhs, preferred_element_type=F32) + bst_ref[:, g:g + 1]
            for k, c in enumerate(pair):
                rows = slice(c * CHUNK, (c + 1) * CHUNK)
                s_c = s[:, k * A_GROUP_DIM:(k + 1) * A_GROUP_DIM]
                sg_ref[rows, cols] = (u_ref[rows, cols].astype(F32) * s_c).astype(BF16)


def _sgu(z, ln_g, ln_b, ws, bst, chunks, emit_vn):
    m = z.shape[0]
    tm = chunks * CHUNK
    out_shape = [jax.ShapeDtypeStruct((m, A_WIDTH), BF16)]
    out_specs = [pl.BlockSpec((tm, A_WIDTH), lambda i: (i, 0))]
    if emit_vn:
        out_shape.append(jax.ShapeDtypeStruct((m, A_WIDTH), F32))
        out_specs.append(pl.BlockSpec((tm, A_WIDTH), lambda i: (i, 0)))
    return pl.pallas_call(
        functools.partial(_sgu_kernel, chunks=chunks),
        out_shape=out_shape,
        grid=(m // tm,),
        in_specs=[
            pl.BlockSpec((tm, A_WIDTH), lambda i: (i, COL_U // A_WIDTH)),
            pl.BlockSpec((tm, A_WIDTH), lambda i: (i, COL_V // A_WIDTH)),
            pl.BlockSpec((1, A_WIDTH), lambda i: (0, 0)),
            pl.BlockSpec((1, A_WIDTH), lambda i: (0, 0)),
            pl.BlockSpec((A_GROUPS, CHUNK, CHUNK), lambda i: (0, 0, 0)),
            pl.BlockSpec((CHUNK, A_GROUPS), lambda i: (0, 0)),
        ],
        out_specs=out_specs,
        compiler_params=_cparams("parallel"),
        name="sgu",
    )(z, z, ln_g, ln_b, ws, bst)


def _attn_kernel(q_ref, kc_ref, kp_ref, vc_ref, vp_ref, o_ref, lse_ref, *, qblocks):
    i = pl.program_id(2)
    qi = lax.broadcasted_iota(jnp.int32, (CHUNK, CHUNK), 0)
    ki = lax.broadcasted_iota(jnp.int32, (CHUNK, CHUNK), 1)
    cur_mask = ki <= qi
    no_prev = jnp.where(i == 0, CHUNK, 0)
    dn = (((1,), (1,)), ((), ()))
    rpc = q_ref.shape[1]
    nbq = CHUNK // rpc

    def load(ref, a, cols):
        return ref[a * nbq:(a + 1) * nbq, :, cols].reshape(CHUNK, HEAD_DIM)

    units = [(a, h) for a in range(qblocks) for h in range(HEADS_PER_GROUP)]
    for start in range(0, len(units), ATTN_UNITS_PER_BATCH):
        batch_units = units[start:start + ATTN_UNITS_PER_BATCH]
        scores = []
        for a, h in batch_units:
            cols = slice(h * HEAD_DIM, (h + 1) * HEAD_DIM)
            q = load(q_ref, a, cols)
            if a == 0:
                kp, prev_mask = load(kp_ref, 0, cols), ki >= qi + no_prev
            else:
                kp, prev_mask = load(kc_ref, a - 1, cols), ki >= qi
            s_c = lax.dot_general(q, load(kc_ref, a, cols), dn, preferred_element_type=F32) * ATTN_SCALE
            s_p = lax.dot_general(q, kp, dn, preferred_element_type=F32) * ATTN_SCALE
            scores.append((jnp.where(cur_mask, s_c, NEG), jnp.where(prev_mask, s_p, NEG)))
        probs = []
        for s_c, s_p in scores:
            mx = jnp.maximum(jnp.max(s_c, axis=-1, keepdims=True), jnp.max(s_p, axis=-1, keepdims=True))
            p_c = jnp.exp(s_c - mx)
            p_p = jnp.exp(s_p - mx)
            l = jnp.sum(p_c, axis=-1, keepdims=True) + jnp.sum(p_p, axis=-1, keepdims=True)
            probs.append((p_c.astype(BF16), p_p.astype(BF16), mx, l))
        for (a, h), (p_c, p_p, mx, l) in zip(batch_units, probs):
            cols = slice(h * HEAD_DIM, (h + 1) * HEAD_DIM)
            vp = load(vp_ref, 0, cols) if a == 0 else load(vc_ref, a - 1, cols)
            acc = jnp.dot(p_c, load(vc_ref, a, cols), preferred_element_type=F32)
            acc = acc + jnp.dot(p_p, vp, preferred_element_type=F32)
            o_ref[a * nbq:(a + 1) * nbq, :, cols] = (acc / l).astype(BF16).reshape(nbq, rpc, HEAD_DIM)
            lse_ref[a * CHUNK:(a + 1) * CHUNK, h:h + 1] = mx + jnp.log(l)


def _rows_per_class(dil):
    return min(PERM_BLOCK // dil, CHUNK)


def _attn_group(z, batch, seq, group):
    _, dil = DILATED_GROUPS[group]
    sub = seq // dil
    rpc = _rows_per_class(dil)
    qblocks = max(b for b in (4, 2, 1) if sub % (b * CHUNK) == 0)
    tq = qblocks * CHUNK
    cq, ck, cv = (group * QKV_COLS // GROUP_COLS + part for part in range(3))
    zv = z.reshape(batch, sub // rpc, dil, rpc, z.shape[-1])

    def cur(cb):
        return pl.BlockSpec((None, tq // rpc, None, rpc, GROUP_COLS), lambda b, r, i: (b, i, r, 0, cb))

    def prev(cb):
        return pl.BlockSpec((None, CHUNK // rpc, None, rpc, GROUP_COLS),
                            lambda b, r, i: (b, jnp.maximum(i * qblocks - 1, 0), r, 0, cb))

    o, lse = pl.pallas_call(
        functools.partial(_attn_kernel, qblocks=qblocks),
        out_shape=[jax.ShapeDtypeStruct((batch, sub // rpc, dil, rpc, GROUP_COLS), BF16),
                   jax.ShapeDtypeStruct((batch, dil, sub, HEADS_PER_GROUP), F32)],
        grid=(batch, dil, sub // tq),
        in_specs=[cur(cq), cur(ck), prev(ck), cur(cv), prev(cv)],
        out_specs=[pl.BlockSpec((None, tq // rpc, None, rpc, GROUP_COLS), lambda b, r, i: (b, i, r, 0, 0)),
                   pl.BlockSpec((None, None, tq, HEADS_PER_GROUP), lambda b, r, i: (b, r, i, 0))],
        compiler_params=_cparams("parallel", "parallel", "arbitrary"),
        name=f"attn_prompt_g{group}",
    )(zv, zv, zv, zv, zv)
    lse = lse.transpose(0, 2, 1, 3).reshape(batch * seq, HEADS_PER_GROUP)
    return o, lse


def _attn_sample_kernel(q_ref, k_ref, v_ref, c0_ref, c1_ref, c2_ref, o_ref, lse_ref, *, n_new):
    caches = (c0_ref, c1_ref, c2_ref)
    rowid = lax.broadcasted_iota(jnp.int32, (CHUNK, HEADS_PER_GROUP, 1), 0)
    for g, (win, dil) in enumerate(DILATED_GROUPS):
        cref = caches[g]
        hs = slice(g * HEADS_PER_GROUP, (g + 1) * HEADS_PER_GROUP)
        for t in range(n_new):
            res, first = t % dil, t // dil
            kc = cref[:, res, 0]
            vc = cref[:, res, 1]
            q = q_ref[t, hs, :]
            s = jnp.sum(kc * q[None], axis=-1, keepdims=True) * ATTN_SCALE
            if first > 0:
                s = jnp.where(rowid >= first, s, NEG)
            new_rows = [tn for tn in range(t + 1) if (t - tn) % dil == 0 and (t - tn) // dil <= win // dil]
            s_new = [jnp.sum(q * k_ref[tn, hs, :], axis=-1, keepdims=True) * ATTN_SCALE for tn in new_rows]
            mx = jnp.max(s, axis=0)
            for sn in s_new:
                mx = jnp.maximum(mx, sn)
            p = jnp.exp(s - mx[None])
            l = jnp.sum(p, axis=0)
            acc = jnp.sum(p * vc, axis=0)
            for tn, sn in zip(new_rows, s_new):
                pn = jnp.exp(sn - mx)
                l = l + pn
                acc = acc + pn * v_ref[tn, hs, :]
            o_ref[g, t] = acc / l
            lse_ref[g, t] = jnp.broadcast_to(mx + jnp.log(l), (HEADS_PER_GROUP, HEAD_DIM))


def _attn_sample(q, k, v, caches):
    bd, n_new = q.shape[:2]
    views, specs = [], []
    for (win, dil), c in zip(DILATED_GROUPS, caches):
        assert c.shape[2] == win and win == CHUNK * dil, "cache must hold exactly one full window"
        used = min(dil, n_new)
        views.append(c.reshape(1, bd, CHUNK, dil, 2, HEADS_PER_GROUP, HEAD_DIM))
        specs.append(pl.BlockSpec((None, None, CHUNK, used, 2, HEADS_PER_GROUP, HEAD_DIM),
                                  lambda b: (0, b, 0, 0, 0, 0, 0)))
    new_spec = pl.BlockSpec((None, n_new, N_DIL * HEADS_PER_GROUP, HEAD_DIM), lambda b: (b, 0, 0, 0))
    out_sds = jax.ShapeDtypeStruct((N_DIL, bd, n_new, HEADS_PER_GROUP, HEAD_DIM), F32)
    out_spec = pl.BlockSpec((N_DIL, None, n_new, HEADS_PER_GROUP, HEAD_DIM), lambda b: (0, b, 0, 0, 0))
    return pl.pallas_call(
        functools.partial(_attn_sample_kernel, n_new=n_new),
        out_shape=[out_sds, out_sds],
        grid=(bd,),
        in_specs=[new_spec, new_spec, new_spec] + specs,
        out_specs=[out_spec, out_spec],
        compiler_params=_cparams("parallel"),
        name="attn_sample",
    )(q, k, v, *views)


def _mixer_out_kernel(sg_ref, o0_ref, o1_ref, o2_ref, l0_ref, l1_ref, l2_ref, ga_ref, gb_ref,
                      wa_ref, wb_ref, x_ref, wo_ref, g_ref, wr_ref, br_ref, *rest, permuted):
    if permuted:
        pt_ref, h_ref, hn_ref, ri_ref, rw_ref, ob_ref = rest
    else:
        h_ref, hn_ref, ri_ref, rw_ref, ob_ref = rest

    l0, l1, l2 = l0_ref[...], l1_ref[...], l2_ref[...]
    mx = jnp.maximum(jnp.maximum(l0, l1), l2)
    e0, e1, e2 = jnp.exp(l0 - mx), jnp.exp(l1 - mx), jnp.exp(l2 - mx)
    inv = 1.0 / (e0 + e1 + e2)
    ws = (e0 * inv, e1 * inv, e2 * inv)
    o_refs = (o0_ref, o1_ref, o2_ref)
    tm = ob_ref.shape[0]

    def o_rows(ref, rows):
        if len(ref.shape) == 2:
            return ref[rows, :]
        per = ref.shape[1] * ref.shape[2]
        return ref[rows.start // per:rows.stop // per].reshape(rows.stop - rows.start, GROUP_COLS)

    for blk in range(max(tm // PERM_BLOCK, 1)):
        rows = slice(blk * PERM_BLOCK, min((blk + 1) * PERM_BLOCK, tm))
        og = []
        for g in range(N_DIL):
            if permuted and g > 0:
                og.append(jnp.dot(pt_ref[g - 1], o_rows(o_refs[g], rows), preferred_element_type=F32))
            else:
                og.append(o_rows(o_refs[g], rows).astype(F32))
        for h in range(HEADS_PER_GROUP):
            cols = slice(h * HEAD_DIM, (h + 1) * HEAD_DIM)
            ob = sum(ws[g][rows, h:h + 1] * og[g][:, cols] for g in range(N_DIL))
            ob_ref[rows, cols] = ob.astype(BF16)

    ya = jnp.dot(sg_ref[...], wa_ref[...], preferred_element_type=F32)
    yb = jnp.dot(ob_ref[...], wb_ref[...], preferred_element_type=F32)
    m = (ga_ref[...].astype(F32) * ya + gb_ref[...].astype(F32) * yb).astype(BF16)
    _oproj_rows(m, x_ref, wo_ref, g_ref, wr_ref, br_ref, h_ref, hn_ref, ri_ref, rw_ref)


def _mixer_out(x, z, sg, os_, lses, wts, tm, perms_t=None):
    m = z.shape[0]
    row = lambda width: pl.BlockSpec((tm, width), lambda i: (i, 0))
    whole = lambda a: pl.BlockSpec(a.shape, lambda i: (0,) * a.ndim, pipeline_mode=pl.Buffered(1))

    def o_spec(o):
        if o.ndim == 2:
            return row(GROUP_COLS)
        _, blocks, dil, rpc, _ = o.shape
        per_tile, tiles = tm // (dil * rpc), blocks * dil * rpc // tm
        return pl.BlockSpec((None, per_tile, dil, rpc, GROUP_COLS), lambda i: (i // tiles, i % tiles, 0, 0, 0))

    consts = [wts["wa"], wts["wb"]]
    tail_consts = [wts["wo"], wts["norm_ffn"], wts["wr"], wts["br"]]
    in_specs = [row(A_WIDTH)] + [o_spec(o) for o in os_] + [row(HEADS_PER_GROUP)] * 3 + [
        pl.BlockSpec((tm, D_MODEL), lambda i: (i, COL_GA // D_MODEL)),
        pl.BlockSpec((tm, D_MODEL), lambda i: (i, COL_GB // D_MODEL)),
    ] + [whole(a) for a in consts] + [row(D_MODEL)] + [whole(a) for a in tail_consts]
    args = [sg, *os_, *lses, z, z, *consts, x, *tail_consts]
    if perms_t is not None:
        assert tm % PERM_BLOCK == 0
        in_specs.append(whole(perms_t))
        args.append(perms_t)
    return pl.pallas_call(
        functools.partial(_mixer_out_kernel, permuted=perms_t is not None),
        out_shape=[jax.ShapeDtypeStruct((m, D_MODEL), F32), jax.ShapeDtypeStruct((m * ROW_TILE, LANES), BF16),
                   jax.ShapeDtypeStruct((m, LANES), jnp.int32), jax.ShapeDtypeStruct((m, LANES), F32)],
        grid=(m // tm,),
        in_specs=in_specs,
        out_specs=[row(D_MODEL), pl.BlockSpec((tm * ROW_TILE, LANES), lambda i: (i, 0)), row(LANES), row(LANES)],
        scratch_shapes=[pltpu.VMEM((tm, GROUP_COLS), BF16)],
        compiler_params=_cparams("parallel"),
        name="mixer_out",
    )(*args)


def _oproj_rows(m, x_ref, wo_ref, g_ref, wr_ref, br_ref, h_ref, hn_ref, ri_ref, rw_ref):
    h = x_ref[...] + jnp.dot(m, wo_ref[...], preferred_element_type=F32)
    h_ref[...] = h
    ms = jnp.mean(h * h, axis=-1, keepdims=True)
    hn = h * lax.rsqrt(ms + RMS_EPS) * g_ref[...]
    _store_token_tiles(hn_ref, hn)
    logits = jnp.dot(hn.astype(BF16), wr_ref[...], preferred_element_type=F32) + br_ref[...]
    lane = lax.broadcasted_iota(jnp.int32, logits.shape, 1).astype(F32)
    big = float(LANES)

    def first_argmax(vals, vmax):
        return jnp.min(jnp.where(vals == vmax, lane, big), axis=-1, keepdims=True)

    lg = jnp.where(lane < N_EXPERT_GROUPS, logits, NEG)
    gmax = jnp.max(lg, axis=-1, keepdims=True)
    gsel = first_argmax(lg, gmax)
    p_sel = 1.0 / jnp.sum(jnp.exp(lg - gmax), axis=-1, keepdims=True)
    lo = N_EXPERT_GROUPS + EXPERTS_PER_GROUP * gsel
    le = jnp.where(jnp.logical_and(lane >= lo, lane < lo + EXPERTS_PER_GROUP), logits, NEG)
    v1 = jnp.max(le, axis=-1, keepdims=True)
    i1 = first_argmax(le, v1)
    le2 = jnp.where(lane == i1, NEG, le)
    v2 = jnp.max(le2, axis=-1, keepdims=True)
    i2 = first_argmax(le2, v2)
    e2 = jnp.exp(v2 - v1)
    w1 = p_sel / (1.0 + e2)
    w2 = p_sel * e2 / (1.0 + e2)
    ri = jnp.where(lane == 0, i1 - N_EXPERT_GROUPS, jnp.where(lane == 1, i2 - N_EXPERT_GROUPS, 0.0))
    ri_ref[...] = ri.astype(jnp.int32)
    rw_ref[...] = jnp.where(lane == 0, w1, jnp.where(lane == 1, w2, 0.0))


def _dispatch_kernel(slot_ref, pad_start_ref, pad_len_ref, hn_ref, hs_ref, xs_hbm, zbuf, sem, zsem,
                     *, tm, moe_tm, n_tiles):
    i = pl.program_id(0)
    n_main = pl.num_programs(0) - 1

    def zero_copy(dst_row, rows):
        return pltpu.make_async_copy(zbuf.at[pl.ds(0, rows * ROW_TILE)],
                                     xs_hbm.at[pl.ds(dst_row * ROW_TILE, rows * ROW_TILE)], zsem)

    def zero_fill(act):
        def per_expert(e, carry):
            off, length = pad_start_ref[e], pad_len_ref[e]
            rows = moe_tm // 2
            while rows >= 1:
                @pl.when((length & rows) != 0)
                def _(off=off, rows=rows):
                    act(zero_copy(off, rows))
                off = off + (length & rows)
                rows //= 2
            return carry

        lax.fori_loop(0, N_EXPERTS, per_expert, 0)

        def per_tile(t, carry):
            act(zero_copy(t * moe_tm, moe_tm))
            return carry

        lax.fori_loop(pad_start_ref[N_EXPERTS], n_tiles, per_tile, 0)

    @pl.when(i == 0)
    def _():
        zbuf[...] = jnp.zeros_like(zbuf)
        zero_fill(lambda c: c.start())

    def scatter_rows(ref, first_token):
        rows = ref.shape[0] // ROW_TILE

        def issue(r, carry):
            for k in range(TOP_K):
                slot = slot_ref[(first_token + r) * TOP_K + k]
                pltpu.make_async_copy(ref.at[pl.ds(r * ROW_TILE, ROW_TILE)],
                                      xs_hbm.at[pl.ds(slot * ROW_TILE, ROW_TILE)], sem).start(priority=k % 2)
            return carry

        lax.fori_loop(0, rows, issue, 0, unroll=8)
        for k in range(TOP_K):
            pltpu.make_async_copy(ref, ref, sem).wait()

    @pl.when(i < n_main)
    def _():
        scatter_rows(hn_ref, i * tm)

    @pl.when(i == n_main)
    def _():
        scatter_rows(hs_ref, n_main * tm)
        zero_fill(lambda c: c.wait())


def _dispatch(hn_tiles, hs_tiles, slot, pad_start, pad_len, n_tiles, moe_tm, tm):
    m = hn_tiles.shape[0] // ROW_TILE
    assert m % tm == 0 and moe_tm & (moe_tm - 1) == 0
    assert slot.shape[0] == (m + hs_tiles.shape[0] // ROW_TILE) * TOP_K
    n_main = m // tm
    grid_spec = pltpu.PrefetchScalarGridSpec(
        num_scalar_prefetch=3,
        grid=(n_main + 1,),
        in_specs=[pl.BlockSpec((tm * ROW_TILE, LANES), lambda i, s, ps, pn: (jnp.minimum(i, n_main - 1), 0)),
                  pl.BlockSpec(hs_tiles.shape, lambda i, s, ps, pn: (0, 0))],
        out_specs=pl.BlockSpec(memory_space=pl.ANY),
        scratch_shapes=[pltpu.VMEM((moe_tm * ROW_TILE, LANES), BF16),
                        pltpu.SemaphoreType.DMA(()), pltpu.SemaphoreType.DMA(())],
    )
    return pl.pallas_call(
        functools.partial(_dispatch_kernel, tm=tm, moe_tm=moe_tm, n_tiles=n_tiles),
        out_shape=jax.ShapeDtypeStruct((n_tiles * moe_tm * ROW_TILE, LANES), BF16),
        grid_spec=grid_spec,
        compiler_params=_cparams("arbitrary"),
        name="moe_dispatch",
    )(slot, pad_start, pad_len, hn_tiles, hs_tiles)


def _moe_kernel(texp_ref, valid_ref, xs_ref, wg_ref, wu_ref, wd_ref, ys_ref, wg_bf, wu_bf, wd_bf):
    i = pl.program_id(0)
    tm = xs_ref.shape[0] // ROW_TILE
    valid = valid_ref[i]
    new_expert = jnp.logical_or(i == 0, texp_ref[i] != texp_ref[jnp.maximum(i - 1, 0)])

    @pl.when(jnp.logical_and(valid > 0, new_expert))
    def _():
        wg_bf[...] = wg_ref[...].astype(BF16)
        wu_bf[...] = wu_ref[...].astype(BF16)
        wd_bf[...] = wd_ref[...].astype(BF16)

    @pl.when(valid > 0)
    def _():
        x = jnp.concatenate(_load_token_tile_chunks(xs_ref, (), tm), axis=1)
        gate = jnp.dot(x, wg_bf[...], preferred_element_type=F32)
        up = jnp.dot(x, wu_bf[...], preferred_element_type=F32)
        hid = (gate * _sigmoid(gate) * up).astype(BF16)
        _store_token_tiles(ys_ref, jnp.dot(hid, wd_bf[...], preferred_element_type=F32))

    @pl.when(valid == 0)
    def _():
        ys_ref[...] = jnp.zeros_like(ys_ref)


def _moe(xs, wg, wu, wd, tile_expert, tile_valid, tm):
    n_tiles = tile_expert.shape[0]
    grid_spec = pltpu.PrefetchScalarGridSpec(
        num_scalar_prefetch=2,
        grid=(n_tiles,),
        in_specs=[
            pl.BlockSpec((tm * ROW_TILE, LANES), lambda i, te, tv: (i, 0)),
            pl.BlockSpec((None, D_MODEL, D_EXPERT), lambda i, te, tv: (te[i], 0, 0)),
            pl.BlockSpec((None, D_MODEL, D_EXPERT), lambda i, te, tv: (te[i], 0, 0)),
            pl.BlockSpec((None, D_EXPERT, D_MODEL), lambda i, te, tv: (te[i], 0, 0)),
        ],
        out_specs=pl.BlockSpec((tm * ROW_TILE, LANES), lambda i, te, tv: (i, 0)),
        scratch_shapes=[pltpu.VMEM((D_MODEL, D_EXPERT), BF16), pltpu.VMEM((D_MODEL, D_EXPERT), BF16),
                        pltpu.VMEM((D_EXPERT, D_MODEL), BF16)],
    )
    return pl.pallas_call(
        _moe_kernel,
        out_shape=jax.ShapeDtypeStruct((n_tiles * tm * ROW_TILE, LANES), BF16),
        grid_spec=grid_spec,
        compiler_params=_cparams("arbitrary"),
        name="moe_experts",
    )(tile_expert, tile_valid, xs, wg, wu, wd)


def _combine_kernel(slot_ref, h_ref, rw_ref, g_ref, ys_hbm, y_ref, ybuf, sem, *, tm):
    i = pl.program_id(0)
    n_steps = pl.num_programs(0)

    def issue(tile, buf):
        def body(r, carry):
            for k in range(TOP_K):
                slot = slot_ref[(tile * tm + r) * TOP_K + k]
                pltpu.make_async_copy(ys_hbm.at[pl.ds(slot * ROW_TILE, ROW_TILE)],
                                      ybuf.at[buf, k, pl.ds(r * ROW_TILE, ROW_TILE)], sem.at[buf]).start(priority=k % 2)
            return carry

        lax.fori_loop(0, tm, body, 0, unroll=8)

    @pl.when(i == 0)
    def _():
        issue(0, 0)

    @pl.when(i + 1 < n_steps)
    def _():
        issue(i + 1, (i + 1) % 2)

    cur = i % 2
    for k in range(TOP_K):
        pltpu.make_async_copy(ybuf.at[cur, k], ybuf.at[cur, k], sem.at[cur]).wait()
    w0, w1 = rw_ref[:, 0:1], rw_ref[:, 1:2]
    e0 = _load_token_tile_chunks(ybuf, (cur, 0), tm)
    e1 = _load_token_tile_chunks(ybuf, (cur, 1), tm)
    chunks, ssq = [], 0.0
    for s in range(ROW_TILE):
        y = h_ref[:, s * LANES:(s + 1) * LANES] + w0 * e0[s].astype(F32) + w1 * e1[s].astype(F32)
        chunks.append(y)
        ssq = ssq + jnp.sum(y * y, axis=-1, keepdims=True)
    scale = lax.rsqrt(ssq * (1.0 / D_MODEL) + RMS_EPS)
    for s, y in enumerate(chunks):
        cols = slice(s * LANES, (s + 1) * LANES)
        y_ref[:, cols] = y * scale * g_ref[:, cols]


def _combine(h, route_w, norm_g, ys, slot, tm):
    m = h.shape[0]
    grid_spec = pltpu.PrefetchScalarGridSpec(
        num_scalar_prefetch=1,
        grid=(m // tm,),
        in_specs=[
            pl.BlockSpec((tm, D_MODEL), lambda i, s: (i, 0)),
            pl.BlockSpec((tm, LANES), lambda i, s: (i, 0)),
            pl.BlockSpec((1, D_MODEL), lambda i, s: (0, 0)),
            pl.BlockSpec(memory_space=pl.ANY),
        ],
        out_specs=pl.BlockSpec((tm, D_MODEL), lambda i, s: (i, 0)),
        scratch_shapes=[pltpu.VMEM((2, TOP_K, tm * ROW_TILE, LANES), BF16), pltpu.SemaphoreType.DMA((2,))],
    )
    return pl.pallas_call(
        functools.partial(_combine_kernel, tm=tm),
        out_shape=jax.ShapeDtypeStruct((m, D_MODEL), F32),
        grid_spec=grid_spec,
        compiler_params=_cparams("arbitrary"),
        name="moe_combine",
    )(slot, h, route_w, norm_g, ys)


def _routing_tables(route_i, m, tm):
    n = m * TOP_K
    n_tiles = pl.cdiv(n, tm) + N_EXPERTS - 1
    e_flat = route_i[:, :TOP_K].reshape(n)
    onehot = (e_flat[:, None] == jnp.arange(N_EXPERTS, dtype=jnp.int32)[None, :]).astype(jnp.int32)
    csum = jnp.cumsum(onehot, axis=0)
    rank = jnp.take_along_axis(csum, e_flat[:, None], axis=1)[:, 0] - 1
    counts = csum[-1]
    tiles_e = (counts + tm - 1) // tm
    tile_end = jnp.cumsum(tiles_e)
    tile_start = tile_end - tiles_e
    slot = (tile_start[e_flat] * tm + rank).astype(jnp.int32)
    tile_ids = jnp.arange(n_tiles, dtype=jnp.int32)
    tile_expert = jnp.sum((tile_end[None, :] <= tile_ids[:, None]).astype(jnp.int32), axis=1)
    tile_expert = jnp.minimum(tile_expert, N_EXPERTS - 1)
    tile_valid = jnp.clip(counts[tile_expert] - (tile_ids - tile_start[tile_expert]) * tm, 0, tm)
    tile_valid = jnp.where(tile_ids < tile_end[-1], tile_valid, 0).astype(jnp.int32)
    pad_start = jnp.concatenate([tile_start * tm + counts, tile_end[-1:]]).astype(jnp.int32)
    pad_len = (tiles_e * tm - counts).astype(jnp.int32)
    return slot, tile_expert.astype(jnp.int32), tile_valid, pad_start, pad_len


def _moe_ffn(prompt, sample, wts):
    (h_p, hn_p, ri_p, rw_p), (h_s, hn_s, ri_s, rw_s) = prompt, sample
    m_p, m_s = h_p.shape[0], h_s.shape[0]
    route_i = jnp.concatenate([ri_p, ri_s], axis=0)
    slot, tile_expert, tile_valid, pad_start, pad_len = _routing_tables(route_i, m_p + m_s, MOE_TM)
    xs = _dispatch(hn_p, hn_s, slot, pad_start, pad_len, tile_expert.shape[0], MOE_TM, tm=512)
    ys = _moe(xs, wts["wg"], wts["wu"], wts["wd"], tile_expert, tile_valid, MOE_TM)
    y_p = _combine(h_p, rw_p, wts["norm_final"], ys, slot[:m_p * TOP_K], tm=512)
    y_s = _combine(h_s, rw_s, wts["norm_final"], ys, slot[m_p * TOP_K:], tm=m_s)
    return y_p, y_s


def _kv_window_kernel(k0_ref, v0_ref, k1_ref, v1_ref, k2_ref, v2_ref, pt_ref, out0_ref, out1_ref, out2_ref,
                      *, first_step):
    i = pl.program_id(1)

    def emit(out_ref, refs, group, rows_in, rows_out):
        for part, ref in enumerate(refs):
            if group > 0:
                val = jnp.dot(pt_ref[group - 1], ref[...], preferred_element_type=F32)
            else:
                val = ref[...].astype(F32)
            for h in range(HEADS_PER_GROUP):
                out_ref[rows_out, part, h, :] = val[rows_in, h * HEAD_DIM:(h + 1) * HEAD_DIM]

    whole = slice(0, PERM_BLOCK)
    emit(out2_ref, (k2_ref, v2_ref), 2, whole, whole)

    @pl.when(i >= first_step[1])
    def _():
        emit(out1_ref, (k1_ref, v1_ref), 1, whole, whole)

    @pl.when(i >= first_step[0])
    def _():
        keep = out0_ref.shape[0]
        emit(out0_ref, (k0_ref, v0_ref), 0, slice(PERM_BLOCK - keep, PERM_BLOCK), slice(0, keep))


def _kv_windows(z_b, batch, seq, perms_t):
    wins = [min(win, seq) for win, _ in DILATED_GROUPS]
    assert wins[2] % PERM_BLOCK == 0 and wins[1] % PERM_BLOCK == 0 and wins[0] <= PERM_BLOCK and seq % PERM_BLOCK == 0
    steps = wins[2] // PERM_BLOCK
    blocks_per_seq = seq // PERM_BLOCK
    first_step = [steps - max(w // PERM_BLOCK, 1) for w in wins]

    def col_spec(group, part):
        cb = (group * QKV_COLS + (1 + part) * GROUP_COLS) // GROUP_COLS

        def index(b, i):
            return (b * blocks_per_seq + blocks_per_seq - steps + jnp.maximum(i, first_step[group]), cb)

        return pl.BlockSpec((PERM_BLOCK, GROUP_COLS), index)

    def out_spec(group):
        rows = min(wins[group], PERM_BLOCK)
        return pl.BlockSpec((None, None, rows, 2, HEADS_PER_GROUP, HEAD_DIM),
                            lambda b, i: (0, b, jnp.maximum(i - first_step[group], 0), 0, 0, 0))

    in_specs = [col_spec(g, part) for g in range(N_DIL) for part in range(2)]
    in_specs.append(pl.BlockSpec(perms_t.shape, lambda b, i: (0, 0, 0)))
    return pl.pallas_call(
        functools.partial(_kv_window_kernel, first_step=tuple(first_step)),
        out_shape=[jax.ShapeDtypeStruct((1, batch, w, 2, HEADS_PER_GROUP, HEAD_DIM), F32) for w in wins],
        grid=(batch, steps),
        in_specs=in_specs,
        out_specs=[out_spec(g) for g in range(N_DIL)],
        compiler_params=_cparams("arbitrary", "arbitrary"),
        name="kv_windows",
    )(*([z_b] * 6), perms_t)


def _kv_rows(z, lead, group):
    zz = z.reshape(lead + (z.shape[-1],))
    base = group * QKV_COLS
    k = zz[..., base + GROUP_COLS: base + 2 * GROUP_COLS]
    v = zz[..., base + 2 * GROUP_COLS: base + 3 * GROUP_COLS]
    kv = jnp.stack([k, v], axis=2).astype(F32)
    return kv.reshape(kv.shape[:3] + (HEADS_PER_GROUP, HEAD_DIM))


def kernel(x_prompt, x_sample, cache_kv_w128, cache_kv_w512, cache_kv_w2048, norm_mix, w_in, ln_v_g, ln_v_b, w_s, b_s, w_a_out, w_b_out, w_o, norm_ffn, w_route_group, b_route_group, w_route_expert, b_route_expert, w_gate_e, w_up_e, w_down_e, norm_final):
    assert norm_mix.shape[0] == 1, "single-layer trunk"
    batch, seq, _ = x_prompt.shape
    bd, n_new, _ = x_sample.shape
    caches = (cache_kv_w128, cache_kv_w512, cache_kv_w2048)

    pad = LANES - N_EXPERT_GROUPS - N_EXPERTS
    wr = jnp.concatenate([w_route_group[0], w_route_expert[0], jnp.zeros((D_MODEL, pad), F32)], axis=1)
    br = jnp.concatenate([b_route_group[0], b_route_expert[0], jnp.zeros((pad,), F32)])[None, :]
    wts = dict(
        wa=w_a_out[0].astype(BF16), wb=w_b_out[0].astype(BF16), wo=w_o[0].astype(BF16),
        wg=w_gate_e[0], wu=w_up_e[0], wd=w_down_e[0],
        wr=wr.astype(BF16), br=br, norm_ffn=norm_ffn[0][None, :], norm_final=norm_final[None, :],
    )
    w_in_b = w_in[0].astype(BF16)
    norm_g = norm_mix[0][None, :]
    ln_g, ln_b = ln_v_g[0][None, :], ln_v_b[0][None, :]

    xp = x_prompt.reshape(batch * seq, D_MODEL)
    perms = jnp.stack([_class_major_perm(dil) for _, dil in DILATED_GROUPS[1:]])
    zp_a, zp_b = _inproj_all(xp, norm_g, w_in_b, tm=1024, perms=perms)
    (sgp,) = _sgu(zp_a, ln_g, ln_b, w_s[0], b_s[0].T, chunks=8, emit_vn=False)
    os_p, lses_p = zip(*[_attn_group(zp_b, batch, seq, g) for g in range(N_DIL)])
    perms_t = perms.transpose(0, 2, 1)
    mixed_p = _mixer_out(xp, zp_a, sgp, os_p, lses_p, wts, tm=256, perms_t=perms_t)
    kv_prompt = _kv_windows(zp_b, batch, seq, perms_t)

    ms = bd * n_new
    assert ms == CHUNK and n_new <= CHUNK
    xs = x_sample.reshape(ms, D_MODEL)
    zs_a, zs_b = _inproj_all(xs, norm_g, w_in_b, tm=ms)
    eye = jnp.eye(bd, dtype=F32)
    ws_s = jnp.einsum("ab,gts->gatbs", eye, w_s[0][:, :n_new, :n_new]).reshape(A_GROUPS, ms, ms)
    bst_s = jnp.tile(b_s[0][:, :n_new].T, (bd, 1))
    sgs, vns = _sgu(zs_a, ln_g, ln_b, ws_s, bst_s, chunks=1, emit_vn=True)
    zs5 = zs_b.astype(F32).reshape(bd, n_new, N_DIL, 3, HEADS_PER_GROUP, HEAD_DIM)
    qkv = [zs5[:, :, :, part].reshape(bd, n_new, N_DIL * HEADS_PER_GROUP, HEAD_DIM) for part in range(3)]
    o_s, lse_s = _attn_sample(*qkv, caches)
    os_s = [o_s[g].reshape(ms, GROUP_COLS).astype(BF16) for g in range(N_DIL)]
    lses_s = [lse_s[g, ..., 0].reshape(ms, HEADS_PER_GROUP) for g in range(N_DIL)]
    mixed_s = _mixer_out(xs, zs_a, sgs, os_s, lses_s, wts, tm=ms)
    y_prompt, y_sample = _moe_ffn(mixed_p, mixed_s, wts)
    y_prompt = y_prompt.reshape(batch, seq, D_MODEL)
    y_sample = y_sample.reshape(bd, n_new, D_MODEL)
    kv_sample = [_kv_rows(zs_b, (bd, n_new), g)[None] for g in range(N_DIL)]
    chunk_v = vns.reshape(1, bd, n_new, A_WIDTH)

    return (y_prompt, y_sample, *kv_prompt, *kv_sample, chunk_v)
```
